```python
import jax, jax.numpy as jnp
from jax import lax
import numpy as np

D_MODEL = 2048
BATCH = 1
SEQ = 16384
DEPTH = 2
DEC_BATCH = 8
DEC_SEQ = 64
PAST_LEN = 2048

CHUNK = 64
HEAD_A = 64
N_HEADS_A = 16
WIDTH_A = N_HEADS_A * HEAD_A
LORA_W = 64
LORA_A = 64
LORA_G = 128
GN_EPS = 64e-5
SPLIT_A = [WIDTH_A, 2 * WIDTH_A, 3 * WIDTH_A, 3 * WIDTH_A + LORA_W, 3 * WIDTH_A + LORA_W + LORA_A]
C_RWKV = 3 * WIDTH_A + LORA_W + LORA_A + LORA_G
HEAD_B = 64
N_HEADS_B = 16
N_KV_B = 4
GROUP_B = N_HEADS_B // N_KV_B
WIDTH_B = N_HEADS_B * HEAD_B
KV_WIDTH_B = N_KV_B * HEAD_B
WINDOW = 128
WIN_CHUNKS = WINDOW // CHUNK
C_SWA = WIDTH_B + 2 * KV_WIDTH_B
C_GATE = 2 * D_MODEL
C_IN = C_RWKV + C_SWA + C_GATE
N_MEM = 256
N_HEADS_M = 4
HEAD_M = 128
N_KEYS = 128
N_EXPERTS = N_KEYS * N_KEYS
PEER_HEADS = 8
D_KEY = 256
TOPK = 16
PEER_BLOCK = 128
ALPHA = (2.0 * DEPTH) ** 0.25
BETA = (8.0 * DEPTH) ** -0.25
NEG_INF = -1e30

kernel_name = 'hybrid_rwkv7_swa_peer_stream_step'


def layer_norm(x, g, b, eps=1e-5):
    xf = x.astype(jnp.float32)
    mu = jnp.mean(xf, -1, keepdims=True)
    var = jnp.mean(jnp.square(xf - mu), -1, keepdims=True)
    return ((xf - mu) * lax.rsqrt(var + eps) * g + b).astype(x.dtype)


def rwkv7_scan(r, w, k, v, kk, a, state0):
    def step(S, inp):
        r_t, w_t, k_t, v_t, kk_t, a_t = inp
        sa = jnp.einsum('bhij,bhj->bhi', S, -kk_t)
        S = (S * w_t[:, :, None, :] + sa[..., None] * (kk_t * a_t)[:, :, None, :]
             + v_t[..., None] * k_t[:, :, None, :])
        y = jnp.einsum('bhij,bhj->bhi', S, r_t)
        return S, y
    xs = tuple(jnp.moveaxis(t, 1, 0) for t in (r, w, k, v, kk, a))
    S, ys = lax.scan(step, state0.astype(jnp.float32), xs)
    return jnp.moveaxis(ys, 0, 1), S


def rwkv7_branch(seg, prev_row, state0, mu, w0, w_up, a0, a_up, g_up, k_k, k_a, r_k, lnx_g, lnx_b):
    B, T, _ = seg.shape
    f32 = jnp.float32
    shifted = jnp.concatenate([prev_row.astype(seg.dtype), seg[:, :-1]], axis=1)
    xm = seg + mu * (shifted - seg)
    r, k, v, wl, al, gl = jnp.split(xm, SPLIT_A, axis=-1)
    w_log = -jax.nn.softplus(-(w0 + jnp.tanh(wl) @ w_up).astype(f32)) - 0.5
    decay = jnp.exp(-jnp.exp(w_log))
    a = jax.nn.sigmoid((a0 + al @ a_up).astype(f32))
    g = jax.nn.sigmoid(gl) @ g_up
    heads = lambda t: t.reshape(B, T, N_HEADS_A, HEAD_A)
    kk = heads((k * k_k).astype(f32))
    kk = kk * lax.rsqrt(jnp.maximum(jnp.sum(kk * kk, -1, keepdims=True), 1e-24))
    kf = k.astype(f32) * (1.0 + (a - 1.0) * k_a.astype(f32))
    rh, kh, vh, ah, wh = heads(r.astype(f32)), heads(kf), heads(v.astype(f32)), heads(a), heads(decay)
    y, s_final = rwkv7_scan(rh, wh, kh, vh, kk, ah, state0)
    mean = jnp.mean(y, -1, keepdims=True)
    var = jnp.mean(jnp.square(y - mean), -1, keepdims=True)
    yn = ((y - mean) * lax.rsqrt(var + GN_EPS)).reshape(B, T, WIDTH_A) * lnx_g + lnx_b
    bonus = jnp.sum(rh * kh * r_k.astype(f32), -1, keepdims=True) * vh
    out = (yn + bonus.reshape(B, T, WIDTH_A)) * g
    return out.astype(seg.dtype), s_final, seg[:, -1:]


def alibi_bias(q_pos, k_pos):
    slopes = 2.0 ** (-8.0 * jnp.arange(1, N_HEADS_B + 1, dtype=jnp.float32) / N_HEADS_B)
    dist = jnp.abs(q_pos[:, None] - k_pos[None, :]).astype(jnp.float32)
    return -slopes.reshape(N_KV_B, GROUP_B, 1, 1) * dist


def sink_attention(q, k, v, bias, mask, sinks):
    s = jnp.einsum('bnqhgd,bnjhd->bnhgqj', q, k).astype(jnp.float32) * (HEAD_B ** -0.5) + bias
    if mask is not None:
        s = jnp.where(mask, s, NEG_INF)
    sink = sinks.astype(jnp.float32).reshape(N_KV_B, GROUP_B, 1, 1)
    m = jnp.maximum(jnp.max(s, -1, keepdims=True), sink)
    p = jnp.exp(s - m)
    p = p / (jnp.sum(p, -1, keepdims=True) + jnp.exp(sink - m))
    return jnp.einsum('bnhgqj,bnjhd->bnqhgd', p.astype(v.dtype), v)


def swa_prompt(q, k, v, sinks):
    B, S = q.shape[:2]
    NC = S // CHUNK
    band = (WIN_CHUNKS + 1) * CHUNK
    qc = q.reshape(B, NC, CHUNK, N_KV_B, GROUP_B, HEAD_B)
    pad = ((0, 0), (WIN_CHUNKS, 0), (0, 0), (0, 0), (0, 0))
    kp = jnp.pad(k.reshape(B, NC, CHUNK, N_KV_B, HEAD_B), pad)
    vp = jnp.pad(v.reshape(B, NC, CHUNK, N_KV_B, HEAD_B), pad)
    kb = jnp.concatenate([kp[:, o:o + NC] for o in range(WIN_CHUNKS + 1)], axis=2)
    vb = jnp.concatenate([vp[:, o:o + NC] for o in range(WIN_CHUNKS + 1)], axis=2)
    bias = alibi_bias(jnp.arange(CHUNK), jnp.arange(band) - WIN_CHUNKS * CHUNK)
    key_chunk = jnp.arange(NC)[:, None] - WIN_CHUNKS + (jnp.arange(band) // CHUNK)[None, :]
    mask = (key_chunk >= 0)[None, :, None, None, None, :]
    o = sink_attention(qc, kb, vb, bias, mask, sinks)
    return o.reshape(B, S, WIDTH_B)


def swa_sample(q, k, v, k_cache, v_cache, sinks):
    B, T = q.shape[:2]
    Lc = k_cache.shape[1]
    k_all = jnp.concatenate([k_cache.astype(k.dtype), k], axis=1)
    v_all = jnp.concatenate([v_cache.astype(v.dtype), v], axis=1)
    bias = alibi_bias(jnp.arange(T), jnp.arange(Lc + T) - Lc)
    o = sink_attention(q[:, None], k_all[:, None], v_all[:, None], bias, None, sinks)
    return o.reshape(B, T, WIDTH_B), k_all[:, -Lc:], v_all[:, -Lc:]


def mem_kv(mem, wk, wv):
    B, M, _ = mem.shape
    return ((mem @ wk).reshape(B, M, N_HEADS_M, HEAD_M), (mem @ wv).reshape(B, M, N_HEADS_M, HEAD_M))


def mem_attention(x, mk, mv, wq, wo):
    B, T, _ = x.shape
    q = (x @ wq).reshape(B, T, N_HEADS_M, HEAD_M)
    s = jnp.einsum('bthd,bmhd->bhtm', q, mk.astype(q.dtype)).astype(jnp.float32) * (HEAD_M ** -0.5)
    p = jax.nn.softmax(s, axis=-1).astype(x.dtype)
    o = jnp.einsum('bhtm,bmhd->bthd', p, mv.astype(x.dtype)).reshape(B, T, N_HEADS_M * HEAD_M)
    return o @ wo


def peer_ffn(x, w_query, sub_keys, u_tab, v_tab):
    B, T, D = x.shape
    n = B * T
    nblk = -(-n // PEER_BLOCK)
    xt = jnp.pad(x.reshape(n, D), ((0, nblk * PEER_BLOCK - n), (0, 0))).reshape(nblk, PEER_BLOCK, D)

    def peer_block(xb):
        q = (xb @ w_query).reshape(PEER_BLOCK, PEER_HEADS, 2, D_KEY // 2)
        s = jnp.einsum('thpd,hpnd->thpn', q, sub_keys).astype(jnp.float32)
        s1, i1 = lax.top_k(s[:, :, 0], TOPK)
        s2, i2 = lax.top_k(s[:, :, 1], TOPK)
        cand = (s1[..., :, None] + s2[..., None, :]).reshape(PEER_BLOCK, PEER_HEADS, TOPK * TOPK)
        cidx = (i1[..., :, None] * N_KEYS + i2[..., None, :]).reshape(PEER_BLOCK, PEER_HEADS, TOPK * TOPK)
        sc, pos = lax.top_k(cand, TOPK)
        e = jnp.take_along_axis(cidx, pos, axis=-1)
        g = jax.nn.softmax(sc, axis=-1)
        act = jax.nn.gelu(jnp.einsum('td,thkd->thk', xb, u_tab[e]).astype(jnp.float32), approximate=False)
        return jnp.einsum('thk,thkd->td', (g * act).astype(xb.dtype), v_tab[e])

    y = lax.map(peer_block, xt).reshape(nblk * PEER_BLOCK, D)[:n]
    return y.reshape(B, T, D)


def trunk_layer(x, p, mem_k, mem_v, rwkv_state, shift_row, swa_k_cache, swa_v_cache):
    B, T, _ = x.shape
    proj = x @ p['w_in']
    seg_a, seg_b, gates = jnp.split(proj, [C_RWKV, C_RWKV + C_SWA], axis=-1)
    o_a, rwkv_new, shift_new = rwkv7_branch(
        seg_a, shift_row, rwkv_state, p['shift_mu'], p['w0'], p['w_lora_up'], p['a0'], p['a_lora_up'],
        p['g_lora_up'], p['k_k'], p['k_a'], p['r_k'], p['lnx_g'], p['lnx_b'])
    q, k, v = jnp.split(seg_b, [WIDTH_B, WIDTH_B + KV_WIDTH_B], axis=-1)
    q = q.reshape(B, T, N_KV_B, GROUP_B, HEAD_B)
    k = k.reshape(B, T, N_KV_B, HEAD_B)
    v = v.reshape(B, T, N_KV_B, HEAD_B)
    if swa_k_cache is None:
        o_b = swa_prompt(q, k, v, p['attn_sinks'])
        swa_k_new, swa_v_new = k[:, -WINDOW:], v[:, -WINDOW:]
    else:
        o_b, swa_k_new, swa_v_new = swa_sample(q, k, v, swa_k_cache, swa_v_cache, p['attn_sinks'])
    g_a, g_b = jnp.split(jax.nn.sigmoid(gates), 2, axis=-1)
    merged = g_a * (o_a @ p['w_branch_a']) + g_b * (o_b @ p['w_branch_b'])
    x = layer_norm(ALPHA * x + merged @ p['w_out'], p['ln1_g'], p['ln1_b'])
    x = layer_norm(ALPHA * x + mem_attention(x, mem_k, mem_v, p['wq_mem'], p['wo_mem']), p['ln2_g'], p['ln2_b'])
    x = layer_norm(ALPHA * x + peer_ffn(x, p['peer_wq'], p['peer_sub_keys'], p['peer_u'], p['peer_v']),
                   p['ln3_g'], p['ln3_b'])
    return x, rwkv_new, shift_new, swa_k_new, swa_v_new


def setup_inputs(seed: int = 0) -> dict:
    key = jax.random.key(seed)
    ks = iter(jax.random.split(key, 48))
    f32 = jnp.float32

    def nrm(shape, scale):
        return jax.random.normal(next(ks), shape, f32) * scale

    def gain(shape):
        return 1.0 + nrm(shape, 0.02)

    L, D = DEPTH, D_MODEL
    return {
        'x_prompt': nrm((BATCH, SEQ, D), 1.0),
        'x_sample': nrm((DEC_BATCH, DEC_SEQ, D), 1.0),
        'state_rwkv': nrm((L, DEC_BATCH, N_HEADS_A, HEAD_A, HEAD_A), 0.3),
        'state_shift': nrm((L, DEC_BATCH, 1, C_RWKV), 1.0),
        'cache_swa_k': nrm((L, DEC_BATCH, min(WINDOW, PAST_LEN), N_KV_B, HEAD_B), 1.0),
        'cache_swa_v': nrm((L, DEC_BATCH, min(WINDOW, PAST_LEN), N_KV_B, HEAD_B), 1.0),
        'cache_mem_k': nrm((L, DEC_BATCH, N_MEM, N_HEADS_M, HEAD_M), 1.0),
        'cache_mem_v': nrm((L, DEC_BATCH, N_MEM, N_HEADS_M, HEAD_M), BETA),
        'mem_prompt': nrm((BATCH, N_MEM, D), 1.0),
        'w_in': nrm((L, D, C_IN), D ** -0.5),
        'shift_mu': jax.random.uniform(next(ks), (L, C_RWKV), f32),
        'w0': nrm((L, WIDTH_A), 1.0),
        'w_lora_up': nrm((L, LORA_W, WIDTH_A), LORA_W ** -0.5),
        'a0': nrm((L, WIDTH_A), 0.1),
        'a_lora_up': nrm((L, LORA_A, WIDTH_A), LORA_A ** -0.5),
        'g_lora_up': nrm((L, LORA_G, WIDTH_A), LORA_G ** -0.5),
        'k_k': 0.85 + nrm((L, WIDTH_A), 0.05),
        'k_a': 1.0 + nrm((L, WIDTH_A), 0.05),
        'r_k': nrm((L, N_HEADS_A, HEAD_A), 0.1),
        'lnx_g': gain((L, WIDTH_A)),
        'lnx_b': nrm((L, WIDTH_A), 0.02),
        'attn_sinks': nrm((L, N_HEADS_B), 0.5),
        'w_branch_a': nrm((L, WIDTH_A, D), WIDTH_A ** -0.5),
        'w_branch_b': nrm((L, WIDTH_B, D), WIDTH_B ** -0.5),
        'w_out': nrm((L, D, D), BETA * D ** -0.5),
        'ln1_g': gain((L, D)),
        'ln1_b': nrm((L, D), 0.02),
        'wq_mem': nrm((L, D, N_HEADS_M * HEAD_M), D ** -0.5),
        'wk_mem': nrm((L, D, N_HEADS_M * HEAD_M), D ** -0.5),
        'wv_mem': nrm((L, D, N_HEADS_M * HEAD_M), BETA * D ** -0.5),
        'wo_mem': nrm((L, N_HEADS_M * HEAD_M, D), BETA * (N_HEADS_M * HEAD_M) ** -0.5),
        'ln2_g': gain((L, D)),
        'ln2_b': nrm((L, D), 0.02),
        'peer_wq': nrm((L, D, PEER_HEADS * D_KEY), D ** -0.5),
        'peer_sub_keys': nrm((L, PEER_HEADS, 2, N_KEYS, D_KEY // 2), (D_KEY // 2) ** -0.5),
        'peer_u': nrm((L, N_EXPERTS, D), D ** -0.5),
        'peer_v': nrm((L, N_EXPERTS, D), BETA),
        'ln3_g': gain((L, D)),
        'ln3_b': nrm((L, D), 0.02),
    }


def reference(x_prompt, x_sample, state_rwkv, state_shift, cache_swa_k, cache_swa_v, cache_mem_k, cache_mem_v,
              mem_prompt, w_in, shift_mu, w0, w_lora_up, a0, a_lora_up, g_lora_up, k_k, k_a, r_k, lnx_g, lnx_b,
              attn_sinks, w_branch_a, w_branch_b, w_out, ln1_g, ln1_b, wq_mem, wk_mem, wv_mem, wo_mem,
              ln2_g, ln2_b, peer_wq, peer_sub_keys, peer_u, peer_v, ln3_g, ln3_b):
    params = {
        'w_in': w_in, 'shift_mu': shift_mu, 'w0': w0, 'w_lora_up': w_lora_up, 'a0': a0,
        'a_lora_up': a_lora_up, 'g_lora_up': g_lora_up, 'k_k': k_k, 'k_a': k_a, 'r_k': r_k,
        'lnx_g': lnx_g, 'lnx_b': lnx_b, 'attn_sinks': attn_sinks, 'w_branch_a': w_branch_a,
        'w_branch_b': w_branch_b, 'w_out': w_out, 'ln1_g': ln1_g, 'ln1_b': ln1_b, 'wq_mem': wq_mem,
        'wk_mem': wk_mem, 'wv_mem': wv_mem, 'wo_mem': wo_mem, 'ln2_g': ln2_g, 'ln2_b': ln2_b,
        'peer_wq': peer_wq, 'peer_sub_keys': peer_sub_keys, 'peer_u': peer_u, 'peer_v': peer_v,
        'ln3_g': ln3_g, 'ln3_b': ln3_b,
    }
    B = x_prompt.shape[0]
    rwkv0 = jnp.zeros((B, N_HEADS_A, HEAD_A, HEAD_A), jnp.float32)
    shift0 = jnp.zeros((B, 1, C_RWKV), x_prompt.dtype)
    xp, xs = x_prompt, x_sample
    p_rw, p_sh, p_k, p_v, p_mk, p_mv = [], [], [], [], [], []
    s_rw, s_sh, s_k, s_v = [], [], [], []
    for l in range(DEPTH):
        p = {name: arr[l] for name, arr in params.items()}
        mk, mv = mem_kv(mem_prompt, p['wk_mem'], p['wv_mem'])
        xp, rw, sh, kn, vn = trunk_layer(xp, p, mk, mv, rwkv0, shift0, None, None)
        p_rw.append(rw); p_sh.append(sh); p_k.append(kn); p_v.append(vn); p_mk.append(mk); p_mv.append(mv)
        xs, rw, sh, kn, vn = trunk_layer(xs, p, cache_mem_k[l], cache_mem_v[l], state_rwkv[l], state_shift[l],
                                         cache_swa_k[l], cache_swa_v[l])
        s_rw.append(rw); s_sh.append(sh); s_k.append(kn); s_v.append(vn)
    return (xp, xs,
            jnp.stack(p_rw), jnp.stack(p_sh), jnp.stack(p_k), jnp.stack(p_v), jnp.stack(p_mk), jnp.stack(p_mv),
            jnp.stack(s_rw), jnp.stack(s_sh), jnp.stack(s_k), jnp.stack(s_v))
```

```python
import functools

import jax
import jax.numpy as jnp
from jax import lax
from jax.experimental import pallas as pl
from jax.experimental.pallas import tpu as pltpu

D_MODEL = 2048
DEPTH = 2
CHUNK = 64
HEAD_A = 64
N_HEADS_A = 16
WIDTH_A = N_HEADS_A * HEAD_A
LORA_W = 64
LORA_A = 64
LORA_G = 128
GN_EPS = 64e-5
SPLIT_A = [WIDTH_A, 2 * WIDTH_A, 3 * WIDTH_A, 3 * WIDTH_A + LORA_W, 3 * WIDTH_A + LORA_W + LORA_A]
C_RWKV = 3 * WIDTH_A + LORA_W + LORA_A + LORA_G
HEAD_B = 64
N_HEADS_B = 16
N_KV_B = 4
GROUP_B = N_HEADS_B // N_KV_B
WIDTH_B = N_HEADS_B * HEAD_B
KV_WIDTH_B = N_KV_B * HEAD_B
WINDOW = 128
WIN_CHUNKS = WINDOW // CHUNK
C_SWA = WIDTH_B + 2 * KV_WIDTH_B
C_GATE = 2 * D_MODEL
C_IN = C_RWKV + C_SWA + C_GATE
N_MEM = 256
N_HEADS_M = 4
HEAD_M = 128
N_KEYS = 128
PEER_HEADS = 8
D_KEY = 256
TOPK = 16
PEER_BLOCK = 128
ALPHA = (2.0 * DEPTH) ** 0.25
NEG_INF = -1e30

VMEM_LIMIT = 48 * 1024 * 1024


def _matmul_kernel(a_ref, b_ref, o_ref, a16_ref):
    @pl.when(pl.program_id(1) == 0)
    def _():
        a16_ref[...] = a_ref[...].astype(jnp.bfloat16)

    o_ref[...] = jnp.dot(a16_ref[...], b_ref[...], preferred_element_type=jnp.float32)


def _pick(n, cands):
    for c in cands:
        if n % c == 0:
            return c
    return n


def matmul(a, b16):
    m, k = a.shape
    n = b16.shape[1]
    tm = _pick(m, (512, 256, 128))
    tn = _pick(n, (1280, 1024, 512, 256, 128))
    return pl.pallas_call(
        _matmul_kernel,
        out_shape=jax.ShapeDtypeStruct((m, n), jnp.float32),
        grid=(m // tm, n // tn),
        in_specs=[pl.BlockSpec((tm, k), lambda i, j: (i, 0)),
                  pl.BlockSpec((k, tn), lambda i, j: (0, j))],
        out_specs=pl.BlockSpec((tm, tn), lambda i, j: (i, j)),
        scratch_shapes=[pltpu.VMEM((tm, k), jnp.bfloat16)],
        compiler_params=pltpu.CompilerParams(
            dimension_semantics=("arbitrary", "arbitrary"),
            vmem_limit_bytes=VMEM_LIMIT),
        name="matmul",
    )(a, b16)


def mm(x, w16):
    lead = x.shape[:-1]
    return matmul(x.reshape(-1, x.shape[-1]), w16).reshape(*lead, w16.shape[1])


def layer_norm(x, g, b, eps=1e-5):
    xf = x.astype(jnp.float32)
    mu = jnp.mean(xf, -1, keepdims=True)
    var = jnp.mean(jnp.square(xf - mu), -1, keepdims=True)
    return ((xf - mu) * lax.rsqrt(var + eps) * g + b).astype(x.dtype)


def rwkv7_scan(r, w, k, v, kk, a, state0):
    def step(S, inp):
        r_t, w_t, k_t, v_t, kk_t, a_t = inp
        sa = jnp.einsum('bhij,bhj->bhi', S, -kk_t)
        S = (S * w_t[:, :, None, :] + sa[..., None] * (kk_t * a_t)[:, :, None, :]
             + v_t[..., None] * k_t[:, :, None, :])
        y = jnp.einsum('bhij,bhj->bhi', S, r_t)
        return S, y
    xs = tuple(jnp.moveaxis(t, 1, 0) for t in (r, w, k, v, kk, a))
    S, ys = lax.scan(step, state0.astype(jnp.float32), xs)
    return jnp.moveaxis(ys, 0, 1), S


def rwkv7_branch(seg, prev_row, state0, p):
    B, T, _ = seg.shape
    f32 = jnp.float32
    shifted = jnp.concatenate([prev_row.astype(seg.dtype), seg[:, :-1]], axis=1)
    xm = seg + p['shift_mu'] * (shifted - seg)
    r, k, v, wl, al, gl = jnp.split(xm, SPLIT_A, axis=-1)
    w_log = -jax.nn.softplus(-(p['w0'] + jnp.tanh(wl) @ p['w_lora_up']).astype(f32)) - 0.5
    decay = jnp.exp(-jnp.exp(w_log))
    a = jax.nn.sigmoid((p['a0'] + al @ p['a_lora_up']).astype(f32))
    g = jax.nn.sigmoid(gl) @ p['g_lora_up']
    heads = lambda t: t.reshape(B, T, N_HEADS_A, HEAD_A)
    kk = heads((k * p['k_k']).astype(f32))
    kk = kk * lax.rsqrt(jnp.maximum(jnp.sum(kk * kk, -1, keepdims=True), 1e-24))
    kf = k.astype(f32) * (1.0 + (a - 1.0) * p['k_a'].astype(f32))
    rh, kh, vh, ah, wh = heads(r.astype(f32)), heads(kf), heads(v.astype(f32)), heads(a), heads(decay)
    y, s_final = rwkv7_scan(rh, wh, kh, vh, kk, ah, state0)
    mean = jnp.mean(y, -1, keepdims=True)
    var = jnp.mean(jnp.square(y - mean), -1, keepdims=True)
    yn = ((y - mean) * lax.rsqrt(var + GN_EPS)).reshape(B, T, WIDTH_A) * p['lnx_g'] + p['lnx_b']
    bonus = jnp.sum(rh * kh * p['r_k'].astype(f32), -1, keepdims=True) * vh
    out = (yn + bonus.reshape(B, T, WIDTH_A)) * g
    return out.astype(seg.dtype), s_final, seg[:, -1:]


def alibi_bias(q_pos, k_pos):
    slopes = 2.0 ** (-8.0 * jnp.arange(1, N_HEADS_B + 1, dtype=jnp.float32) / N_HEADS_B)
    dist = jnp.abs(q_pos[:, None] - k_pos[None, :]).astype(jnp.float32)
    return -slopes.reshape(N_KV_B, GROUP_B, 1, 1) * dist


def sink_attention(q, k, v, bias, mask, sinks):
    s = jnp.einsum('bnqhgd,bnjhd->bnhgqj', q, k).astype(jnp.float32) * (HEAD_B ** -0.5) + bias
    if mask is not None:
        s = jnp.where(mask, s, NEG_INF)
    sink = sinks.astype(jnp.float32).reshape(N_KV_B, GROUP_B, 1, 1)
    m = jnp.maximum(jnp.max(s, -1, keepdims=True), sink)
    p = jnp.exp(s - m)
    p = p / (jnp.sum(p, -1, keepdims=True) + jnp.exp(sink - m))
    return jnp.einsum('bnhgqj,bnjhd->bnqhgd', p.astype(v.dtype), v)


def swa_prompt(q, k, v, sinks):
    B, S = q.shape[:2]
    NC = S // CHUNK
    band = (WIN_CHUNKS + 1) * CHUNK
    qc = q.reshape(B, NC, CHUNK, N_KV_B, GROUP_B, HEAD_B)
    pad = ((0, 0), (WIN_CHUNKS, 0), (0, 0), (0, 0), (0, 0))
    kp = jnp.pad(k.reshape(B, NC, CHUNK, N_KV_B, HEAD_B), pad)
    vp = jnp.pad(v.reshape(B, NC, CHUNK, N_KV_B, HEAD_B), pad)
    kb = jnp.concatenate([kp[:, o:o + NC] for o in range(WIN_CHUNKS + 1)], axis=2)
    vb = jnp.concatenate([vp[:, o:o + NC] for o in range(WIN_CHUNKS + 1)], axis=2)
    bias = alibi_bias(jnp.arange(CHUNK), jnp.arange(band) - WIN_CHUNKS * CHUNK)
    key_chunk = jnp.arange(NC)[:, None] - WIN_CHUNKS + (jnp.arange(band) // CHUNK)[None, :]
    mask = (key_chunk >= 0)[None, :, None, None, None, :]
    o = sink_attention(qc, kb, vb, bias, mask, sinks)
    return o.reshape(B, S, WIDTH_B)


def swa_sample(q, k, v, k_cache, v_cache, sinks):
    B, T = q.shape[:2]
    Lc = k_cache.shape[1]
    k_all = jnp.concatenate([k_cache.astype(k.dtype), k], axis=1)
    v_all = jnp.concatenate([v_cache.astype(v.dtype), v], axis=1)
    bias = alibi_bias(jnp.arange(T), jnp.arange(Lc + T) - Lc)
    o = sink_attention(q[:, None], k_all[:, None], v_all[:, None], bias, None, sinks)
    return o.reshape(B, T, WIDTH_B), k_all[:, -Lc:], v_all[:, -Lc:]


def mem_attention(x, mk, mv, p):
    B, T, _ = x.shape
    q = mm(x, p['wq_mem16']).reshape(B, T, N_HEADS_M, HEAD_M)
    s = jnp.einsum('bthd,bmhd->bhtm', q, mk.astype(q.dtype)).astype(jnp.float32) * (HEAD_M ** -0.5)
    pr = jax.nn.softmax(s, axis=-1).astype(x.dtype)
    o = jnp.einsum('bhtm,bmhd->bthd', pr, mv.astype(x.dtype)).reshape(B, T, N_HEADS_M * HEAD_M)
    return mm(o, p['wo_mem16'])


def peer_ffn(x, p):
    B, T, D = x.shape
    n = B * T
    nblk = -(-n // PEER_BLOCK)
    xt = x.reshape(nblk, PEER_BLOCK, D)
    qa = mm(x, p['peer_wq16']).reshape(nblk, PEER_BLOCK, PEER_HEADS * D_KEY)
    sub_keys, u_tab, v_tab = p['peer_sub_keys'], p['peer_u'], p['peer_v']

    def peer_block(args):
        xb, qb = args
        q = qb.reshape(PEER_BLOCK, PEER_HEADS, 2, D_KEY // 2)
        s = jnp.einsum('thpd,hpnd->thpn', q, sub_keys).astype(jnp.float32)
        s1, i1 = lax.top_k(s[:, :, 0], TOPK)
        s2, i2 = lax.top_k(s[:, :, 1], TOPK)
        cand = (s1[..., :, None] + s2[..., None, :]).reshape(PEER_BLOCK, PEER_HEADS, TOPK * TOPK)
        cidx = (i1[..., :, None] * N_KEYS + i2[..., None, :]).reshape(PEER_BLOCK, PEER_HEADS, TOPK * TOPK)
        sc, pos = lax.top_k(cand, TOPK)
        e = jnp.take_along_axis(cidx, pos, axis=-1)
        g = jax.nn.softmax(sc, axis=-1)
        act = jax.nn.gelu(jnp.einsum('td,thkd->thk', xb, u_tab[e]).astype(jnp.float32), approximate=False)
        return jnp.einsum('thk,thkd->td', (g * act).astype(xb.dtype), v_tab[e])

    y = lax.map(peer_block, (xt, qa)).reshape(nblk * PEER_BLOCK, D)
    return y.reshape(B, T, D)


def trunk_layer(x, p, mem_k, mem_v, rwkv_state, shift_row, swa_k_cache, swa_v_cache):
    B, T, _ = x.shape
    proj = mm(x, p['w_in16'])
    seg_a, seg_b, gates = jnp.split(proj, [C_RWKV, C_RWKV + C_SWA], axis=-1)
    o_a, rwkv_new, shift_new = rwkv7_branch(seg_a, shift_row, rwkv_state, p)
    q, k, v = jnp.split(seg_b, [WIDTH_B, WIDTH_B + KV_WIDTH_B], axis=-1)
    q = q.reshape(B, T, N_KV_B, GROUP_B, HEAD_B)
    k = k.reshape(B, T, N_KV_B, HEAD_B)
    v = v.reshape(B, T, N_KV_B, HEAD_B)
    if swa_k_cache is None:
        o_b = swa_prompt(q, k, v, p['attn_sinks'])
        swa_k_new, swa_v_new = k[:, -WINDOW:], v[:, -WINDOW:]
    else:
        o_b, swa_k_new, swa_v_new = swa_sample(q, k, v, swa_k_cache, swa_v_cache, p['attn_sinks'])
    g_a, g_b = jnp.split(jax.nn.sigmoid(gates), 2, axis=-1)
    merged = g_a * mm(o_a, p['w_branch_a16']) + g_b * mm(o_b, p['w_branch_b16'])
    x = layer_norm(ALPHA * x + mm(merged, p['w_out16']), p['ln1_g'], p['ln1_b'])
    x = layer_norm(ALPHA * x + mem_attention(x, mem_k, mem_v, p), p['ln2_g'], p['ln2_b'])
    x = layer_norm(ALPHA * x + peer_ffn(x, p), p['ln3_g'], p['ln3_b'])
    return x, rwkv_new, shift_new, swa_k_new, swa_v_new


_MM_WEIGHTS = ('w_in', 'w_branch_a', 'w_branch_b', 'w_out', 'wq_mem', 'wk_mem', 'wv_mem', 'wo_mem', 'peer_wq')


def kernel(x_prompt, x_sample, state_rwkv, state_shift, cache_swa_k, cache_swa_v, cache_mem_k, cache_mem_v, mem_prompt, w_in, shift_mu, w0, w_lora_up, a0, a_lora_up, g_lora_up, k_k, k_a, r_k, lnx_g, lnx_b, attn_sinks, w_branch_a, w_branch_b, w_out, ln1_g, ln1_b, wq_mem, wk_mem, wv_mem, wo_mem, ln2_g, ln2_b, peer_wq, peer_sub_keys, peer_u, peer_v, ln3_g, ln3_b):
    params = {
        'w_in': w_in, 'shift_mu': shift_mu, 'w0': w0, 'w_lora_up': w_lora_up, 'a0': a0,
        'a_lora_up': a_lora_up, 'g_lora_up': g_lora_up, 'k_k': k_k, 'k_a': k_a, 'r_k': r_k,
        'lnx_g': lnx_g, 'lnx_b': lnx_b, 'attn_sinks': attn_sinks, 'w_branch_a': w_branch_a,
        'w_branch_b': w_branch_b, 'w_out': w_out, 'ln1_g': ln1_g, 'ln1_b': ln1_b, 'wq_mem': wq_mem,
        'wk_mem': wk_mem, 'wv_mem': wv_mem, 'wo_mem': wo_mem, 'ln2_g': ln2_g, 'ln2_b': ln2_b,
        'peer_wq': peer_wq, 'peer_sub_keys': peer_sub_keys, 'peer_u': peer_u, 'peer_v': peer_v,
        'ln3_g': ln3_g, 'ln3_b': ln3_b,
    }
    B = x_prompt.shape[0]
    rwkv0 = jnp.zeros((B, N_HEADS_A, HEAD_A, HEAD_A), jnp.float32)
    shift0 = jnp.zeros((B, 1, C_RWKV), x_prompt.dtype)
    xp, xs = x_prompt, x_sample
    p_rw, p_sh, p_k, p_v, p_mk, p_mv = [], [], [], [], [], []
    s_rw, s_sh, s_k, s_v = [], [], [], []
    for l in range(DEPTH):
        p = {name: arr[l] for name, arr in params.items()}
        for name in _MM_WEIGHTS:
            p[name + '16'] = p[name].astype(jnp.bfloat16)
        mk = mm(mem_prompt, p['wk_mem16']).reshape(B, N_MEM, N_HEADS_M, HEAD_M)
        mv = mm(mem_prompt, p['wv_mem16']).reshape(B, N_MEM, N_HEADS_M, HEAD_M)
        xp, rw, sh, kn, vn = trunk_layer(xp, p, mk, mv, rwkv0, shift0, None, None)
        p_rw.append(rw); p_sh.append(sh); p_k.append(kn); p_v.append(vn); p_mk.append(mk); p_mv.append(mv)
        xs, rw, sh, kn, vn = trunk_layer(xs, p, cache_mem_k[l], cache_mem_v[l], state_rwkv[l], state_shift[l],
                                         cache_swa_k[l], cache_swa_v[l])
        s_rw.append(rw); s_sh.append(sh); s_k.append(kn); s_v.append(vn)
    return (xp, xs,
            jnp.stack(p_rw), jnp.stack(p_sh), jnp.stack(p_k), jnp.stack(p_v), jnp.stack(p_mk), jnp.stack(p_mv),
            jnp.stack(s_rw), jnp.stack(s_sh), jnp.stack(s_k), jnp.stack(s_v))
```

```python
import functools

import jax
import jax.numpy as jnp
from jax import lax
from jax.experimental import pallas as pl
from jax.experimental.pallas import tpu as pltpu

D_MODEL = 2048
DEPTH = 2
CHUNK = 64
HEAD_A = 64
N_HEADS_A = 16
WIDTH_A = N_HEADS_A * HEAD_A
LORA_W = 64
LORA_A = 64
LORA_G = 128
GN_EPS = 64e-5
SPLIT_A = [WIDTH_A, 2 * WIDTH_A, 3 * WIDTH_A, 3 * WIDTH_A + LORA_W, 3 * WIDTH_A + LORA_W + LORA_A]
C_RWKV = 3 * WIDTH_A + LORA_W + LORA_A + LORA_G
HEAD_B = 64
N_HEADS_B = 16
N_KV_B = 4
GROUP_B = N_HEADS_B // N_KV_B
WIDTH_B = N_HEADS_B * HEAD_B
KV_WIDTH_B = N_KV_B * HEAD_B
WINDOW = 128
WIN_CHUNKS = WINDOW // CHUNK
C_SWA = WIDTH_B + 2 * KV_WIDTH_B
C_GATE = 2 * D_MODEL
C_IN = C_RWKV + C_SWA + C_GATE
N_MEM = 256
N_HEADS_M = 4
HEAD_M = 128
N_KEYS = 128
PEER_HEADS = 8
D_KEY = 256
TOPK = 16
PEER_BLOCK = 128
ALPHA = (2.0 * DEPTH) ** 0.25
NEG_INF = -1e30

VMEM_LIMIT = 56 * 1024 * 1024


def _matmul_kernel(a_ref, b_ref, o_ref, a16_ref):
    @pl.when(pl.program_id(1) == 0)
    def _():
        a16_ref[...] = a_ref[...].astype(jnp.bfloat16)

    o_ref[...] = jnp.dot(a16_ref[...], b_ref[...], preferred_element_type=jnp.float32)


def _pick(n, cands):
    for c in cands:
        if n % c == 0:
            return c
    return n


def matmul(a, b16):
    m, k = a.shape
    n = b16.shape[1]
    tm = _pick(m, (512, 256, 128))
    tn = _pick(n, (1280, 1024, 512, 256, 128))
    return pl.pallas_call(
        _matmul_kernel,
        out_shape=jax.ShapeDtypeStruct((m, n), jnp.float32),
        grid=(m // tm, n // tn),
        in_specs=[pl.BlockSpec((tm, k), lambda i, j: (i, 0)),
                  pl.BlockSpec((k, tn), lambda i, j: (0, j))],
        out_specs=pl.BlockSpec((tm, tn), lambda i, j: (i, j)),
        scratch_shapes=[pltpu.VMEM((tm, k), jnp.bfloat16)],
        compiler_params=pltpu.CompilerParams(
            dimension_semantics=("arbitrary", "arbitrary"),
            vmem_limit_bytes=VMEM_LIMIT),
        name="matmul",
    )(a, b16)


def mm(x, w16):
    lead = x.shape[:-1]
    return matmul(x.reshape(-1, x.shape[-1]), w16).reshape(*lead, w16.shape[1])


N_PAIR = N_HEADS_A // 2
PAIR_W = 2 * HEAD_A


def _split2(x):
    hi = x.astype(jnp.bfloat16)
    lo = (x - hi.astype(jnp.float32)).astype(jnp.bfloat16)
    return jnp.concatenate([hi, lo], axis=1)


def _rwkv_scan_kernel(r_ref, w_ref, k_ref, v_ref, kk_ref, a_ref, s0_ref, ones_ref, eye_ref,
                      y_ref, sT_ref, s_ref, *, tb):
    tblk = pl.program_id(1)

    @pl.when(tblk == 0)
    def _():
        s_ref[...] = s0_ref[0]

    ones2 = ones_ref[...]
    eye2 = eye_ref[...]

    def segsum(tiles):
        lhs = jnp.concatenate([_split2(m) for m in tiles], axis=0)
        return jnp.dot(lhs, ones2, preferred_element_type=jnp.float32)

    def step(g, carry):
        rows = pl.ds(pl.multiple_of(g * 8, 8), 8)
        pairs = range(N_PAIR)
        sls = [pl.ds(p * PAIR_W, PAIR_W) for p in pairs]
        kk8 = [kk_ref[0, rows, sl] for sl in sls]
        kka8 = [kk8[p] * a_ref[0, rows, sls[p]] for p in pairs]
        w8 = [w_ref[0, rows, sl] for sl in sls]
        k8 = [k_ref[0, rows, sl] for sl in sls]
        v8 = [v_ref[0, rows, sl] for sl in sls]
        r8 = [r_ref[0, rows, sl] for sl in sls]
        s = [s_ref[p] for p in pairs]
        ys = [[] for _ in pairs]
        tile = lambda x, p: x[p * HEAD_A:(p + 1) * HEAD_A]
        for j in range(8):
            row = slice(j, j + 1)
            sa = segsum([s[p] * (-kk8[p][row]) for p in pairs])
            vb = segsum([eye2 * v8[p][row] for p in pairs])
            s = [s[p] * w8[p][row] + tile(sa, p) * kka8[p][row] + tile(vb, p) * k8[p][row] for p in pairs]
            yb = segsum([s[p] * r8[p][row] for p in pairs])
            for p in pairs:
                ys[p].append(jnp.sum(tile(yb, p) * eye2, axis=0, keepdims=True))
        for p in pairs:
            s_ref[p] = s[p]
            y_ref[0, rows, sls[p]] = jnp.concatenate(ys[p], axis=0)
        return carry

    lax.fori_loop(0, tb // 8, step, 0)

    @pl.when(tblk == pl.num_programs(1) - 1)
    def _():
        sT_ref[0] = s_ref[...]


def rwkv7_scan_pallas(r, w, k, v, kk, a, state0, tb=256):
    B, T, _ = r.shape
    tb = min(tb, T)
    s0 = state0.astype(jnp.float32).reshape(B, N_PAIR, 2, HEAD_A, HEAD_A)
    s0 = s0.transpose(0, 1, 3, 2, 4).reshape(B, N_PAIR, HEAD_A, PAIR_W)
    lane_head = jnp.arange(PAIR_W) // HEAD_A
    ones2 = (jnp.tile(lane_head, 2)[:, None] == lane_head[None, :]).astype(jnp.bfloat16)
    eye2 = (jnp.arange(HEAD_A)[:, None] == (jnp.arange(PAIR_W) % HEAD_A)[None, :]).astype(jnp.float32)
    seq = pl.BlockSpec((1, tb, WIDTH_A), lambda b, t: (b, t, 0))
    st = pl.BlockSpec((1, N_PAIR, HEAD_A, PAIR_W), lambda b, t: (b, 0, 0, 0))
    y, sT = pl.pallas_call(
        functools.partial(_rwkv_scan_kernel, tb=tb),
        out_shape=(jax.ShapeDtypeStruct((B, T, WIDTH_A), jnp.float32),
                   jax.ShapeDtypeStruct((B, N_PAIR, HEAD_A, PAIR_W), jnp.float32)),
        grid=(B, T // tb),
        in_specs=[seq] * 6 + [st,
                              pl.BlockSpec((2 * PAIR_W, PAIR_W), lambda b, t: (0, 0)),
                              pl.BlockSpec((HEAD_A, PAIR_W), lambda b, t: (0, 0))],
        out_specs=(seq, st),
        scratch_shapes=[pltpu.VMEM((N_PAIR, HEAD_A, PAIR_W), jnp.float32)],
        compiler_params=pltpu.CompilerParams(
            dimension_semantics=("arbitrary", "arbitrary"),
            vmem_limit_bytes=VMEM_LIMIT),
        name="rwkv7_scan",
    )(r, w, k, v, kk, a, s0, ones2, eye2)
    sT = sT.reshape(B, N_PAIR, HEAD_A, 2, HEAD_A).transpose(0, 1, 3, 2, 4)
    return y, sT.reshape(B, N_HEADS_A, HEAD_A, HEAD_A)


ROUTE_TQ = 256


def _top_values(s, k):
    n_rows = s.shape[0]
    iota = lax.broadcasted_iota(jnp.int32, s.shape, 0).astype(jnp.float32)
    out = []
    for _ in range(k):
        m = jnp.max(s, axis=0, keepdims=True)
        first = jnp.min(jnp.where(s == m, iota, float(n_rows)), axis=0, keepdims=True)
        s = jnp.where(iota == first, -jnp.inf, s)
        out.append(m)
    return out


def _peer_route_kernel(q_ref, keys_ref, s1_ref, c1_ref, s2_ref, e2_ref, th_ref):
    half = D_KEY // 2
    thetas = []
    for h in range(PEER_HEADS):
        tops, scores = [], []
        for p in range(2):
            c0 = (2 * h + p) * half
            qs = q_ref[:, c0:c0 + half].astype(jnp.bfloat16)
            s = lax.dot_general(keys_ref[2 * h + p], qs, (((1,), (1,)), ((), ())),
                                preferred_element_type=jnp.float32)
            scores.append(s)
            tops.append(_top_values(s, TOPK))
        t1, t2 = tops
        t2all = jnp.concatenate(t2, axis=0)
        cand = jnp.concatenate([t1[a] + t2all for a in range(TOPK)], axis=0)
        sc = _top_values(cand, TOPK)
        z = jnp.zeros_like(sc[0])
        for kq in range(TOPK):
            z = z + jnp.exp(sc[kq] - sc[0])
        s1_ref[h] = scores[0]
        s2_ref[h] = scores[1]
        c1_ref[h] = jnp.exp(scores[0] - t1[0]) / z
        e2_ref[h] = jnp.exp(scores[1] - t2[0])
        thetas.append(sc[TOPK - 1])
    th_ref[...] = jnp.concatenate(thetas, axis=0)


def peer_route(q, keys16):
    n = q.shape[0]
    tq = _pick(n, (ROUTE_TQ, 128))
    big = jax.ShapeDtypeStruct((PEER_HEADS, N_KEYS, n), jnp.float32)
    bspec = pl.BlockSpec((PEER_HEADS, N_KEYS, tq), lambda i: (0, 0, i))
    return pl.pallas_call(
        _peer_route_kernel,
        out_shape=(big, big, big, big, jax.ShapeDtypeStruct((PEER_HEADS, n), jnp.float32)),
        grid=(n // tq,),
        in_specs=[pl.BlockSpec((tq, PEER_HEADS * D_KEY), lambda i: (i, 0)),
                  pl.BlockSpec((2 * PEER_HEADS, N_KEYS, D_KEY // 2), lambda i: (0, 0, 0))],
        out_specs=(bspec, bspec, bspec, bspec, pl.BlockSpec((PEER_HEADS, tq), lambda i: (0, i))),
        compiler_params=pltpu.CompilerParams(
            dimension_semantics=("arbitrary",), vmem_limit_bytes=VMEM_LIMIT),
        name="peer_route",
    )(q, keys16)


PEER_TM = 512
PEER_ROWS = 8
PEER_TE = PEER_ROWS * N_KEYS


def _gelu(x):
    return 0.5 * x * (1.0 + lax.erf(x * (2.0 ** -0.5)))


def _peer_mix_kernel(x_ref, u_ref, vt_ref, s1_ref, c1_ref, s2_ref, e2_ref, th_ref, o_ref, x16_ref, h_ref):
    j = pl.program_id(1)

    @pl.when(j == 0)
    def _():
        x16_ref[...] = x_ref[...].astype(jnp.bfloat16)
        o_ref[...] = jnp.zeros_like(o_ref)

    a_t = lax.dot_general(u_ref[...], x16_ref[...], (((1,), (1,)), ((), ())),
                          preferred_element_type=jnp.float32)
    for r in range(PEER_ROWS):
        gate = None
        for h in range(PEER_HEADS):
            ssum = s1_ref[h, r:r + 1, :] + s2_ref[h]
            w = jnp.where(ssum >= th_ref[h:h + 1, :], e2_ref[h], 0.0) * c1_ref[h, r:r + 1, :]
            gate = w if gate is None else gate + w
        rows = a_t[r * N_KEYS:(r + 1) * N_KEYS]
        h_ref[r * N_KEYS:(r + 1) * N_KEYS, :] = (gate * _gelu(rows)).astype(jnp.bfloat16)
    o_ref[...] += jnp.dot(vt_ref[...], h_ref[...], preferred_element_type=jnp.float32)


def peer_mix(x, u16, vt16, s1t, c1t, s2t, e2t, theta):
    n, d = x.shape
    tm = _pick(n, (PEER_TM, 256, 128))
    n_exp = u16.shape[0]
    row_spec = pl.BlockSpec((PEER_HEADS, PEER_ROWS, tm), lambda i, j: (0, j, i))
    all_spec = pl.BlockSpec((PEER_HEADS, N_KEYS, tm), lambda i, j: (0, 0, i))
    return pl.pallas_call(
        _peer_mix_kernel,
        out_shape=jax.ShapeDtypeStruct((d, n), jnp.float32),
        grid=(n // tm, n_exp // PEER_TE),
        in_specs=[pl.BlockSpec((tm, d), lambda i, j: (i, 0)),
                  pl.BlockSpec((PEER_TE, d), lambda i, j: (j, 0)),
                  pl.BlockSpec((d, PEER_TE), lambda i, j: (0, j)),
                  row_spec, row_spec, all_spec, all_spec,
                  pl.BlockSpec((PEER_HEADS, tm), lambda i, j: (0, i))],
        out_specs=pl.BlockSpec((d, tm), lambda i, j: (0, i)),
        scratch_shapes=[pltpu.VMEM((tm, d), jnp.bfloat16), pltpu.VMEM((PEER_TE, tm), jnp.bfloat16)],
        compiler_params=pltpu.CompilerParams(
            dimension_semantics=("arbitrary", "arbitrary"), vmem_limit_bytes=VMEM_LIMIT),
        name="peer_mix",
    )(x, u16, vt16, s1t, c1t, s2t, e2t, theta)


def peer_ffn_pallas(x, p):
    q = matmul(x, p['peer_wq16'])
    stats = peer_route(q, p['peer_keys16'])
    return peer_mix(x, p['peer_u16'], p['peer_vt16'], *stats).T


def layer_norm(x, g, b, eps=1e-5):
    xf = x.astype(jnp.float32)
    mu = jnp.mean(xf, -1, keepdims=True)
    var = jnp.mean(jnp.square(xf - mu), -1, keepdims=True)
    return ((xf - mu) * lax.rsqrt(var + eps) * g + b).astype(x.dtype)


def rwkv7_scan(r, w, k, v, kk, a, state0):
    def step(S, inp):
        r_t, w_t, k_t, v_t, kk_t, a_t = inp
        sa = jnp.einsum('bhij,bhj->bhi', S, -kk_t)
        S = (S * w_t[:, :, None, :] + sa[..., None] * (kk_t * a_t)[:, :, None, :]
             + v_t[..., None] * k_t[:, :, None, :])
        y = jnp.einsum('bhij,bhj->bhi', S, r_t)
        return S, y
    xs = tuple(jnp.moveaxis(t, 1, 0) for t in (r, w, k, v, kk, a))
    S, ys = lax.scan(step, state0.astype(jnp.float32), xs)
    return jnp.moveaxis(ys, 0, 1), S


def rwkv7_branch(seg, prev_row, state0, p):
    B, T, _ = seg.shape
    f32 = jnp.float32
    shifted = jnp.concatenate([prev_row.astype(seg.dtype), seg[:, :-1]], axis=1)
    xm = seg + p['shift_mu'] * (shifted - seg)
    r, k, v, wl, al, gl = jnp.split(xm, SPLIT_A, axis=-1)
    w_log = -jax.nn.softplus(-(p['w0'] + jnp.tanh(wl) @ p['w_lora_up']).astype(f32)) - 0.5
    decay = jnp.exp(-jnp.exp(w_log))
    a = jax.nn.sigmoid((p['a0'] + al @ p['a_lora_up']).astype(f32))
    g = jax.nn.sigmoid(gl) @ p['g_lora_up']
    heads = lambda t: t.reshape(B, T, N_HEADS_A, HEAD_A)
    kk = heads((k * p['k_k']).astype(f32))
    kk = kk * lax.rsqrt(jnp.maximum(jnp.sum(kk * kk, -1, keepdims=True), 1e-24))
    kf = k.astype(f32) * (1.0 + (a - 1.0) * p['k_a'].astype(f32))
    rh, kh, vh, ah, wh = heads(r.astype(f32)), heads(kf), heads(v.astype(f32)), heads(a), heads(decay)
    flat = lambda t: t.reshape(B, T, WIDTH_A)
    y, s_final = rwkv7_scan_pallas(flat(rh), flat(wh), flat(kh), flat(vh), flat(kk), flat(ah), state0)
    y = heads(y)
    mean = jnp.mean(y, -1, keepdims=True)
    var = jnp.mean(jnp.square(y - mean), -1, keepdims=True)
    yn = ((y - mean) * lax.rsqrt(var + GN_EPS)).reshape(B, T, WIDTH_A) * p['lnx_g'] + p['lnx_b']
    bonus = jnp.sum(rh * kh * p['r_k'].astype(f32), -1, keepdims=True) * vh
    out = (yn + bonus.reshape(B, T, WIDTH_A)) * g
    return out.astype(seg.dtype), s_final, seg[:, -1:]


def alibi_bias(q_pos, k_pos):
    slopes = 2.0 ** (-8.0 * jnp.arange(1, N_HEADS_B + 1, dtype=jnp.float32) / N_HEADS_B)
    dist = jnp.abs(q_pos[:, None] - k_pos[None, :]).astype(jnp.float32)
    return -slopes.reshape(N_KV_B, GROUP_B, 1, 1) * dist


def sink_attention(q, k, v, bias, mask, sinks):
    s = jnp.einsum('bnqhgd,bnjhd->bnhgqj', q, k).astype(jnp.float32) * (HEAD_B ** -0.5) + bias
    if mask is not None:
        s = jnp.where(mask, s, NEG_INF)
    sink = sinks.astype(jnp.float32).reshape(N_KV_B, GROUP_B, 1, 1)
    m = jnp.maximum(jnp.max(s, -1, keepdims=True), sink)
    p = jnp.exp(s - m)
    p = p / (jnp.sum(p, -1, keepdims=True) + jnp.exp(sink - m))
    return jnp.einsum('bnhgqj,bnjhd->bnqhgd', p.astype(v.dtype), v)


def swa_prompt(q, k, v, sinks):
    B, S = q.shape[:2]
    NC = S // CHUNK
    band = (WIN_CHUNKS + 1) * CHUNK
    qc = q.reshape(B, NC, CHUNK, N_KV_B, GROUP_B, HEAD_B)
    pad = ((0, 0), (WIN_CHUNKS, 0), (0, 0), (0, 0), (0, 0))
    kp = jnp.pad(k.reshape(B, NC, CHUNK, N_KV_B, HEAD_B), pad)
    vp = jnp.pad(v.reshape(B, NC, CHUNK, N_KV_B, HEAD_B), pad)
    kb = jnp.concatenate([kp[:, o:o + NC] for o in range(WIN_CHUNKS + 1)], axis=2)
    vb = jnp.concatenate([vp[:, o:o + NC] for o in range(WIN_CHUNKS + 1)], axis=2)
    bias = alibi_bias(jnp.arange(CHUNK), jnp.arange(band) - WIN_CHUNKS * CHUNK)
    key_chunk = jnp.arange(NC)[:, None] - WIN_CHUNKS + (jnp.arange(band) // CHUNK)[None, :]
    mask = (key_chunk >= 0)[None, :, None, None, None, :]
    o = sink_attention(qc, kb, vb, bias, mask, sinks)
    return o.reshape(B, S, WIDTH_B)


def swa_sample(q, k, v, k_cache, v_cache, sinks):
    B, T = q.shape[:2]
    Lc = k_cache.shape[1]
    k_all = jnp.concatenate([k_cache.astype(k.dtype), k], axis=1)
    v_all = jnp.concatenate([v_cache.astype(v.dtype), v], axis=1)
    bias = alibi_bias(jnp.arange(T), jnp.arange(Lc + T) - Lc)
    o = sink_attention(q[:, None], k_all[:, None], v_all[:, None], bias, None, sinks)
    return o.reshape(B, T, WIDTH_B), k_all[:, -Lc:], v_all[:, -Lc:]


def mem_attention(x, mk, mv, p):
    B, T, _ = x.shape
    q = mm(x, p['wq_mem16']).reshape(B, T, N_HEADS_M, HEAD_M)
    s = jnp.einsum('bthd,bmhd->bhtm', q, mk.astype(q.dtype)).astype(jnp.float32) * (HEAD_M ** -0.5)
    pr = jax.nn.softmax(s, axis=-1).astype(x.dtype)
    o = jnp.einsum('bhtm,bmhd->bthd', pr, mv.astype(x.dtype)).reshape(B, T, N_HEADS_M * HEAD_M)
    return mm(o, p['wo_mem16'])


def peer_ffn(x, p):
    B, T, D = x.shape
    n = B * T
    nblk = -(-n // PEER_BLOCK)
    xt = x.reshape(nblk, PEER_BLOCK, D)
    qa = mm(x, p['peer_wq16']).reshape(nblk, PEER_BLOCK, PEER_HEADS * D_KEY)
    sub_keys, u_tab, v_tab = p['peer_sub_keys'], p['peer_u'], p['peer_v']

    def peer_block(args):
        xb, qb = args
        q = qb.reshape(PEER_BLOCK, PEER_HEADS, 2, D_KEY // 2)
        s = jnp.einsum('thpd,hpnd->thpn', q, sub_keys).astype(jnp.float32)
        s1, i1 = lax.top_k(s[:, :, 0], TOPK)
        s2, i2 = lax.top_k(s[:, :, 1], TOPK)
        cand = (s1[..., :, None] + s2[..., None, :]).reshape(PEER_BLOCK, PEER_HEADS, TOPK * TOPK)
        cidx = (i1[..., :, None] * N_KEYS + i2[..., None, :]).reshape(PEER_BLOCK, PEER_HEADS, TOPK * TOPK)
        sc, pos = lax.top_k(cand, TOPK)
        e = jnp.take_along_axis(cidx, pos, axis=-1)
        g = jax.nn.softmax(sc, axis=-1)
        act = jax.nn.gelu(jnp.einsum('td,thkd->thk', xb, u_tab[e]).astype(jnp.float32), approximate=False)
        return jnp.einsum('thk,thkd->td', (g * act).astype(xb.dtype), v_tab[e])

    y = lax.map(peer_block, (xt, qa)).reshape(nblk * PEER_BLOCK, D)
    return y.reshape(B, T, D)


def trunk_layer(x, p, mem_k, mem_v, rwkv_state, shift_row, swa_k_cache, swa_v_cache):
    B, T, _ = x.shape
    proj = mm(x, p['w_in16'])
    seg_a, seg_b, gates = jnp.split(proj, [C_RWKV, C_RWKV + C_SWA], axis=-1)
    o_a, rwkv_new, shift_new = rwkv7_branch(seg_a, shift_row, rwkv_state, p)
    q, k, v = jnp.split(seg_b, [WIDTH_B, WIDTH_B + KV_WIDTH_B], axis=-1)
    q = q.reshape(B, T, N_KV_B, GROUP_B, HEAD_B)
    k = k.reshape(B, T, N_KV_B, HEAD_B)
    v = v.reshape(B, T, N_KV_B, HEAD_B)
    if swa_k_cache is None:
        o_b = swa_prompt(q, k, v, p['attn_sinks'])
        swa_k_new, swa_v_new = k[:, -WINDOW:], v[:, -WINDOW:]
    else:
        o_b, swa_k_new, swa_v_new = swa_sample(q, k, v, swa_k_cache, swa_v_cache, p['attn_sinks'])
    g_a, g_b = jnp.split(jax.nn.sigmoid(gates), 2, axis=-1)
    merged = g_a * mm(o_a, p['w_branch_a16']) + g_b * mm(o_b, p['w_branch_b16'])
    x = layer_norm(ALPHA * x + mm(merged, p['w_out16']), p['ln1_g'], p['ln1_b'])
    x = layer_norm(ALPHA * x + mem_attention(x, mem_k, mem_v, p), p['ln2_g'], p['ln2_b'])
    peer = peer_ffn_pallas(x.reshape(B * T, D_MODEL), p).reshape(B, T, D_MODEL)
    x = layer_norm(ALPHA * x + peer, p['ln3_g'], p['ln3_b'])
    return x, rwkv_new, shift_new, swa_k_new, swa_v_new


_MM_WEIGHTS = ('w_in', 'w_branch_a', 'w_branch_b', 'w_out', 'wq_mem', 'wk_mem', 'wv_mem', 'wo_mem', 'peer_wq')


def kernel(x_prompt, x_sample, state_rwkv, state_shift, cache_swa_k, cache_swa_v, cache_mem_k, cache_mem_v, mem_prompt, w_in, shift_mu, w0, w_lora_up, a0, a_lora_up, g_lora_up, k_k, k_a, r_k, lnx_g, lnx_b, attn_sinks, w_branch_a, w_branch_b, w_out, ln1_g, ln1_b, wq_mem, wk_mem, wv_mem, wo_mem, ln2_g, ln2_b, peer_wq, peer_sub_keys, peer_u, peer_v, ln3_g, ln3_b):
    params = {
        'w_in': w_in, 'shift_mu': shift_mu, 'w0': w0, 'w_lora_up': w_lora_up, 'a0': a0,
        'a_lora_up': a_lora_up, 'g_lora_up': g_lora_up, 'k_k': k_k, 'k_a': k_a, 'r_k': r_k,
        'lnx_g': lnx_g, 'lnx_b': lnx_b, 'attn_sinks': attn_sinks, 'w_branch_a': w_branch_a,
        'w_branch_b': w_branch_b, 'w_out': w_out, 'ln1_g': ln1_g, 'ln1_b': ln1_b, 'wq_mem': wq_mem,
        'wk_mem': wk_mem, 'wv_mem': wv_mem, 'wo_mem': wo_mem, 'ln2_g': ln2_g, 'ln2_b': ln2_b,
        'peer_wq': peer_wq, 'peer_sub_keys': peer_sub_keys, 'peer_u': peer_u, 'peer_v': peer_v,
        'ln3_g': ln3_g, 'ln3_b': ln3_b,
    }
    B = x_prompt.shape[0]
    rwkv0 = jnp.zeros((B, N_HEADS_A, HEAD_A, HEAD_A), jnp.float32)
    shift0 = jnp.zeros((B, 1, C_RWKV), x_prompt.dtype)
    xp, xs = x_prompt, x_sample
    p_rw, p_sh, p_k, p_v, p_mk, p_mv = [], [], [], [], [], []
    s_rw, s_sh, s_k, s_v = [], [], [], []
    for l in range(DEPTH):
        p = {name: arr[l] for name, arr in params.items()}
        for name in _MM_WEIGHTS:
            p[name + '16'] = p[name].astype(jnp.bfloat16)
        p['peer_keys16'] = p['peer_sub_keys'].reshape(2 * PEER_HEADS, N_KEYS, D_KEY // 2).astype(jnp.bfloat16)
        p['peer_u16'] = p['peer_u'].astype(jnp.bfloat16)
        p['peer_vt16'] = p['peer_v'].T.astype(jnp.bfloat16)
        mk = mm(mem_prompt, p['wk_mem16']).reshape(B, N_MEM, N_HEADS_M, HEAD_M)
        mv = mm(mem_prompt, p['wv_mem16']).reshape(B, N_MEM, N_HEADS_M, HEAD_M)
        xp, rw, sh, kn, vn = trunk_layer(xp, p, mk, mv, rwkv0, shift0, None, None)
        p_rw.append(rw); p_sh.append(sh); p_k.append(kn); p_v.append(vn); p_mk.append(mk); p_mv.append(mv)
        xs, rw, sh, kn, vn = trunk_layer(xs, p, cache_mem_k[l], cache_mem_v[l], state_rwkv[l], state_shift[l],
                                         cache_swa_k[l], cache_swa_v[l])
        s_rw.append(rw); s_sh.append(sh); s_k.append(kn); s_v.append(vn)
    return (xp, xs,
            jnp.stack(p_rw), jnp.stack(p_sh), jnp.stack(p_k), jnp.stack(p_v), jnp.stack(p_mk), jnp.stack(p_mv),
            jnp.stack(s_rw), jnp.stack(s_sh), jnp.stack(s_k), jnp.stack(s_v))
```

```python
import functools

import jax
import jax.numpy as jnp
from jax import lax
from jax.experimental import pallas as pl
from jax.experimental.pallas import tpu as pltpu

D_MODEL = 2048
DEPTH = 2
CHUNK = 64
HEAD_A = 64
N_HEADS_A = 16
WIDTH_A = N_HEADS_A * HEAD_A
LORA_W = 64
LORA_A = 64
LORA_G = 128
GN_EPS = 64e-5
SPLIT_A = [WIDTH_A, 2 * WIDTH_A, 3 * WIDTH_A, 3 * WIDTH_A + LORA_W, 3 * WIDTH_A + LORA_W + LORA_A]
C_RWKV = 3 * WIDTH_A + LORA_W + LORA_A + LORA_G
HEAD_B = 64
N_HEADS_B = 16
N_KV_B = 4
GROUP_B = N_HEADS_B // N_KV_B
WIDTH_B = N_HEADS_B * HEAD_B
KV_WIDTH_B = N_KV_B * HEAD_B
WINDOW = 128
WIN_CHUNKS = WINDOW // CHUNK
C_SWA = WIDTH_B + 2 * KV_WIDTH_B
C_GATE = 2 * D_MODEL
C_IN = C_RWKV + C_SWA + C_GATE
N_MEM = 256
N_HEADS_M = 4
HEAD_M = 128
N_KEYS = 128
PEER_HEADS = 8
D_KEY = 256
TOPK = 16
PEER_BLOCK = 128
ALPHA = (2.0 * DEPTH) ** 0.25
NEG_INF = -1e30

VMEM_LIMIT = 56 * 1024 * 1024


def _matmul_kernel(a_ref, b_ref, o_ref, a16_ref):
    @pl.when(pl.program_id(1) == 0)
    def _():
        a16_ref[...] = a_ref[...].astype(jnp.bfloat16)

    o_ref[...] = jnp.dot(a16_ref[...], b_ref[...], preferred_element_type=jnp.float32)


def _pick(n, cands):
    for c in cands:
        if n % c == 0:
            return c
    return n


def matmul(a, b16):
    m, k = a.shape
    n = b16.shape[1]
    tm = _pick(m, (512, 256, 128))
    tn = _pick(n, (2048, 1664, 1536, 1024, 512, 256, 128))
    return pl.pallas_call(
        _matmul_kernel,
        out_shape=jax.ShapeDtypeStruct((m, n), jnp.float32),
        grid=(m // tm, n // tn),
        in_specs=[pl.BlockSpec((tm, k), lambda i, j: (i, 0)),
                  pl.BlockSpec((k, tn), lambda i, j: (0, j))],
        out_specs=pl.BlockSpec((tm, tn), lambda i, j: (i, j)),
        scratch_shapes=[pltpu.VMEM((tm, k), jnp.bfloat16)],
        compiler_params=pltpu.CompilerParams(
            dimension_semantics=("arbitrary", "arbitrary"),
            vmem_limit_bytes=VMEM_LIMIT),
        name="matmul",
    )(a, b16)


def mm(x, w16):
    lead = x.shape[:-1]
    return matmul(x.reshape(-1, x.shape[-1]), w16).reshape(*lead, w16.shape[1])


N_PAIR = N_HEADS_A // 2
PAIR_W = 2 * HEAD_A


def _split2(x):
    hi = x.astype(jnp.bfloat16)
    lo = (x - hi.astype(jnp.float32)).astype(jnp.bfloat16)
    return jnp.concatenate([hi, lo], axis=1)


def _rwkv_scan_kernel(r_ref, w_ref, k_ref, v_ref, kk_ref, a_ref, s0_ref, ones_ref, eye_ref,
                      y_ref, sT_ref, s_ref, *, tb):
    tblk = pl.program_id(1)

    @pl.when(tblk == 0)
    def _():
        s_ref[...] = s0_ref[0]

    ones2 = ones_ref[...]
    eye2 = eye_ref[...]

    def segsum(tiles):
        lhs = jnp.concatenate([_split2(m) for m in tiles], axis=0)
        return jnp.dot(lhs, ones2, preferred_element_type=jnp.float32)

    def step(g, carry):
        rows = pl.ds(pl.multiple_of(g * 8, 8), 8)
        pairs = range(N_PAIR)
        sls = [pl.ds(p * PAIR_W, PAIR_W) for p in pairs]
        kk8 = [kk_ref[0, rows, sl] for sl in sls]
        kka8 = [kk8[p] * a_ref[0, rows, sls[p]] for p in pairs]
        w8 = [w_ref[0, rows, sl] for sl in sls]
        k8 = [k_ref[0, rows, sl] for sl in sls]
        v8 = [v_ref[0, rows, sl] for sl in sls]
        r8 = [r_ref[0, rows, sl] for sl in sls]
        s = [s_ref[p] for p in pairs]
        ys = [[] for _ in pairs]
        tile = lambda x, p: x[p * HEAD_A:(p + 1) * HEAD_A]
        for j in range(8):
            row = slice(j, j + 1)
            sa = segsum([s[p] * (-kk8[p][row]) for p in pairs])
            vb = segsum([eye2 * v8[p][row] for p in pairs])
            s = [s[p] * w8[p][row] + tile(sa, p) * kka8[p][row] + tile(vb, p) * k8[p][row] for p in pairs]
            yb = segsum([s[p] * r8[p][row] for p in pairs])
            for p in pairs:
                ys[p].append(jnp.sum(tile(yb, p) * eye2, axis=0, keepdims=True))
        for p in pairs:
            s_ref[p] = s[p]
            y_ref[0, rows, sls[p]] = jnp.concatenate(ys[p], axis=0)
        return carry

    lax.fori_loop(0, tb // 8, step, 0)

    @pl.when(tblk == pl.num_programs(1) - 1)
    def _():
        sT_ref[0] = s_ref[...]


def rwkv7_scan_pallas(r, w, k, v, kk, a, state0, tb=256):
    B, T, _ = r.shape
    tb = min(tb, T)
    s0 = state0.astype(jnp.float32).reshape(B, N_PAIR, 2, HEAD_A, HEAD_A)
    s0 = s0.transpose(0, 1, 3, 2, 4).reshape(B, N_PAIR, HEAD_A, PAIR_W)
    lane_head = jnp.arange(PAIR_W) // HEAD_A
    ones2 = (jnp.tile(lane_head, 2)[:, None] == lane_head[None, :]).astype(jnp.bfloat16)
    eye2 = (jnp.arange(HEAD_A)[:, None] == (jnp.arange(PAIR_W) % HEAD_A)[None, :]).astype(jnp.float32)
    seq = pl.BlockSpec((1, tb, WIDTH_A), lambda b, t: (b, t, 0))
    st = pl.BlockSpec((1, N_PAIR, HEAD_A, PAIR_W), lambda b, t: (b, 0, 0, 0))
    y, sT = pl.pallas_call(
        functools.partial(_rwkv_scan_kernel, tb=tb),
        out_shape=(jax.ShapeDtypeStruct((B, T, WIDTH_A), jnp.float32),
                   jax.ShapeDtypeStruct((B, N_PAIR, HEAD_A, PAIR_W), jnp.float32)),
        grid=(B, T // tb),
        in_specs=[seq] * 6 + [st,
                              pl.BlockSpec((2 * PAIR_W, PAIR_W), lambda b, t: (0, 0)),
                              pl.BlockSpec((HEAD_A, PAIR_W), lambda b, t: (0, 0))],
        out_specs=(seq, st),
        scratch_shapes=[pltpu.VMEM((N_PAIR, HEAD_A, PAIR_W), jnp.float32)],
        compiler_params=pltpu.CompilerParams(
            dimension_semantics=("arbitrary", "arbitrary"),
            vmem_limit_bytes=VMEM_LIMIT),
        name="rwkv7_scan",
    )(r, w, k, v, kk, a, s0, ones2, eye2)
    sT = sT.reshape(B, N_PAIR, HEAD_A, 2, HEAD_A).transpose(0, 1, 3, 2, 4)
    return y, sT.reshape(B, N_HEADS_A, HEAD_A, HEAD_A)


def _head_sums(x, ones2):
    tiles = [jnp.dot(_split2(x[:, c:c + PAIR_W]), ones2, preferred_element_type=jnp.float32)
             for c in range(0, WIDTH_A, PAIR_W)]
    return jnp.concatenate(tiles, axis=1)


def _softplus(z):
    return jnp.maximum(z, 0.0) + jnp.log1p(jnp.exp(-jnp.abs(z)))


def _rwkv_pre_kernel(seg_ref, prev_ref, shift_ref, mu_ref, vec_ref, wa_ref, gup_ref, ones_ref,
                     r_ref, w_ref, k_ref, v_ref, kk_ref, a_ref, g_ref, bonus_ref):
    seg = seg_ref[0]
    tb = seg.shape[0]
    before = jnp.where(pl.program_id(1) == 0, shift_ref[0], prev_ref[0, 7:8, :])
    row = lax.broadcasted_iota(jnp.int32, seg.shape, 0)
    shifted = jnp.where(row == 0, before, pltpu.roll(seg, 1, axis=0))
    xm = seg + mu_ref[...] * (shifted - seg)
    r = xm[:, :WIDTH_A]
    k = xm[:, WIDTH_A:2 * WIDTH_A]
    v = xm[:, 2 * WIDTH_A:3 * WIDTH_A]
    wa = xm[:, 3 * WIDTH_A:3 * WIDTH_A + LORA_W + LORA_A]
    gl = xm[:, 3 * WIDTH_A + LORA_W + LORA_A:]
    lane = lax.broadcasted_iota(jnp.int32, wa.shape, 1)
    wa = jnp.where(lane < LORA_W, jnp.tanh(wa), wa).astype(jnp.bfloat16)
    lora = jnp.dot(wa, wa_ref[...], preferred_element_type=jnp.float32)
    w0, a0, k_k, k_a, r_k = (vec_ref[i:i + 1, :] for i in range(5))
    w_log = -_softplus(-(w0 + lora[:, :WIDTH_A])) - 0.5
    decay = jnp.exp(-jnp.exp(w_log))
    a = jax.nn.sigmoid(a0 + lora[:, WIDTH_A:])
    g = jnp.dot(jax.nn.sigmoid(gl).astype(jnp.bfloat16), gup_ref[...], preferred_element_type=jnp.float32)
    ones2 = ones_ref[...]
    kk = k * k_k
    kk = kk * lax.rsqrt(jnp.maximum(_head_sums(kk * kk, ones2), 1e-24))
    kf = k * (1.0 + (a - 1.0) * k_a)
    r_ref[0], w_ref[0], k_ref[0], v_ref[0], kk_ref[0], a_ref[0], g_ref[0] = r, decay, kf, v, kk, a, g
    bonus_ref[0] = _head_sums(r * kf * r_k, ones2) * v


def _block_ones2():
    lane_head = jnp.arange(PAIR_W) // HEAD_A
    return (jnp.tile(lane_head, 2)[:, None] == lane_head[None, :]).astype(jnp.bfloat16)


def rwkv7_pre(seg, prev_row, p, tb=256):
    B, T, _ = seg.shape
    tb = min(tb, T)
    zeros = jnp.zeros((LORA_W, WIDTH_A), jnp.float32)
    wa_up = jnp.concatenate([jnp.concatenate([p['w_lora_up'], zeros], 1),
                             jnp.concatenate([zeros, p['a_lora_up']], 1)], 0).astype(jnp.bfloat16)
    vecs = jnp.stack([p['w0'], p['a0'], p['k_k'], p['k_a'], p['r_k'].reshape(WIDTH_A)])
    out = jax.ShapeDtypeStruct((B, T, WIDTH_A), jnp.float32)
    ospec = pl.BlockSpec((1, tb, WIDTH_A), lambda b, t: (b, t, 0))
    full = lambda shape: pl.BlockSpec(shape, lambda b, t: (0,) * len(shape))
    return pl.pallas_call(
        _rwkv_pre_kernel,
        out_shape=(out,) * 8,
        grid=(B, T // tb),
        in_specs=[pl.BlockSpec((1, tb, C_RWKV), lambda b, t: (b, t, 0)),
                  pl.BlockSpec((1, 8, C_RWKV), lambda b, t: (b, jnp.maximum(t * (tb // 8) - 1, 0), 0)),
                  pl.BlockSpec((1, 1, C_RWKV), lambda b, t: (b, 0, 0)),
                  full((1, C_RWKV)), full((5, WIDTH_A)), full((LORA_W + LORA_A, 2 * WIDTH_A)),
                  full((LORA_G, WIDTH_A)), full((2 * PAIR_W, PAIR_W))],
        out_specs=(ospec,) * 8,
        compiler_params=pltpu.CompilerParams(
            dimension_semantics=("arbitrary", "arbitrary"), vmem_limit_bytes=VMEM_LIMIT),
        name="rwkv7_pre",
    )(seg, seg, prev_row, p['shift_mu'].reshape(1, C_RWKV), vecs, wa_up,
      p['g_lora_up'].astype(jnp.bfloat16), _block_ones2())


SWA_BAND = WINDOW + CHUNK


def _swa_kernel(q_ref, k_ref, v_ref, kp_ref, vp_ref, bias_ref, sink_ref, o_ref, *, mask_start):
    n_chunks = q_ref.shape[1] // CHUNK
    k_all = jnp.concatenate([kp_ref[0], k_ref[0]], axis=0).astype(jnp.bfloat16)
    v_all = jnp.concatenate([vp_ref[0], v_ref[0]], axis=0).astype(jnp.bfloat16)
    first = pl.program_id(1) == 0
    key_chunk = lax.broadcasted_iota(jnp.int32, (CHUNK, SWA_BAND), 1) // CHUNK
    for c in range(n_chunks):
        q_c = q_ref[0, c * CHUNK:(c + 1) * CHUNK, :].astype(jnp.bfloat16)
        k_c = k_all[c * CHUNK:c * CHUNK + SWA_BAND]
        v_c = v_all[c * CHUNK:c * CHUNK + SWA_BAND]
        dead = jnp.logical_and(first, key_chunk + (c - WIN_CHUNKS) < 0) if (mask_start and c < WIN_CHUNKS) else None
        outs = []
        for head in range(N_HEADS_B):
            kv = head // GROUP_B
            s = lax.dot_general(q_c[:, head * HEAD_B:(head + 1) * HEAD_B], k_c[:, kv * HEAD_B:(kv + 1) * HEAD_B],
                                (((1,), (1,)), ((), ())), preferred_element_type=jnp.float32)
            s = s * (HEAD_B ** -0.5) + bias_ref[head]
            if dead is not None:
                s = jnp.where(dead, NEG_INF, s)
            sink = sink_ref[head:head + 1, 0:1]
            m = jnp.maximum(jnp.max(s, axis=-1, keepdims=True), sink)
            e = jnp.exp(s - m)
            pr = e / (jnp.sum(e, axis=-1, keepdims=True) + jnp.exp(sink - m))
            outs.append(jnp.dot(pr.astype(jnp.bfloat16), v_c[:, kv * HEAD_B:(kv + 1) * HEAD_B],
                                preferred_element_type=jnp.float32))
        o_ref[0, c * CHUNK:(c + 1) * CHUNK, :] = jnp.concatenate(outs, axis=1)


def swa_attention(seg_b, prev_k, prev_v, sinks, *, prev_is_seq, qb=512):
    B, T, _ = seg_b.shape
    qb = min(qb, T)
    slopes = 2.0 ** (-8.0 * jnp.arange(1, N_HEADS_B + 1, dtype=jnp.float32) / N_HEADS_B)
    dist = jnp.abs(jnp.arange(CHUNK)[:, None] - (jnp.arange(SWA_BAND) - WINDOW)[None, :]).astype(jnp.float32)
    bias = -slopes[:, None, None] * dist
    sink_tab = jnp.broadcast_to(sinks.astype(jnp.float32)[:, None], (N_HEADS_B, 128))
    kcol, vcol = WIDTH_B // KV_WIDTH_B, WIDTH_B // KV_WIDTH_B + 1
    if prev_is_seq:
        per = qb // WINDOW
        kp_spec = pl.BlockSpec((1, WINDOW, KV_WIDTH_B), lambda b, i: (b, jnp.maximum(i * per - 1, 0), kcol))
        vp_spec = pl.BlockSpec((1, WINDOW, KV_WIDTH_B), lambda b, i: (b, jnp.maximum(i * per - 1, 0), vcol))
    else:
        kp_spec = vp_spec = pl.BlockSpec((1, WINDOW, KV_WIDTH_B), lambda b, i: (b, 0, 0))
    return pl.pallas_call(
        functools.partial(_swa_kernel, mask_start=prev_is_seq),
        out_shape=jax.ShapeDtypeStruct((B, T, WIDTH_B), jnp.float32),
        grid=(B, T // qb),
        in_specs=[pl.BlockSpec((1, qb, WIDTH_B), lambda b, i: (b, i, 0)),
                  pl.BlockSpec((1, qb, KV_WIDTH_B), lambda b, i: (b, i, kcol)),
                  pl.BlockSpec((1, qb, KV_WIDTH_B), lambda b, i: (b, i, vcol)),
                  kp_spec, vp_spec,
                  pl.BlockSpec((N_HEADS_B, CHUNK, SWA_BAND), lambda b, i: (0, 0, 0)),
                  pl.BlockSpec((N_HEADS_B, 128), lambda b, i: (0, 0))],
        out_specs=pl.BlockSpec((1, qb, WIDTH_B), lambda b, i: (b, i, 0)),
        compiler_params=pltpu.CompilerParams(
            dimension_semantics=("arbitrary", "arbitrary"), vmem_limit_bytes=VMEM_LIMIT),
        name="swa_attention",
    )(seg_b, seg_b, seg_b, prev_k, prev_v, bias, sink_tab)


def _layer_norm_rows(h, g, b, eps=1e-5):
    mu = jnp.mean(h, axis=-1, keepdims=True)
    d = h - mu
    var = jnp.mean(d * d, axis=-1, keepdims=True)
    return d * lax.rsqrt(var + eps) * g + b


def _const_spec(shape):
    return pl.BlockSpec(shape, lambda *_: (0,) * len(shape), pipeline_mode=pl.Buffered(1))


def _merge_kernel(x_ref, y_ref, bonus_ref, g_ref, ob_ref, gates_ref, lnx_ref, ln1_ref, ones_ref,
                  pa_ref, pb_ref, wout_ref, o_ref):
    ones2 = ones_ref[...]
    y = y_ref[...]
    mean = _head_sums(y, ones2) * (1.0 / HEAD_A)
    d = y - mean
    var = _head_sums(d * d, ones2) * (1.0 / HEAD_A)
    yn = d * lax.rsqrt(var + GN_EPS) * lnx_ref[0:1, :] + lnx_ref[1:2, :]
    o_a = ((yn + bonus_ref[...]) * g_ref[...]).astype(jnp.bfloat16)
    br_a = jnp.dot(o_a, pa_ref[...], preferred_element_type=jnp.float32)
    br_b = jnp.dot(ob_ref[...].astype(jnp.bfloat16), pb_ref[...], preferred_element_type=jnp.float32)
    gates = jax.nn.sigmoid(gates_ref[...])
    merged = gates[:, :D_MODEL] * br_a + gates[:, D_MODEL:] * br_b
    h = ALPHA * x_ref[...] + jnp.dot(merged.astype(jnp.bfloat16), wout_ref[...], preferred_element_type=jnp.float32)
    o_ref[...] = _layer_norm_rows(h, ln1_ref[0:1, :], ln1_ref[1:2, :])


def branch_merge(x, y, bonus, g, o_b, gates, p, tm=256):
    n = x.shape[0]
    tm = _pick(n, (tm, 128, 64))
    rows = lambda w: pl.BlockSpec((tm, w), lambda i: (i, 0))
    return pl.pallas_call(
        _merge_kernel,
        out_shape=jax.ShapeDtypeStruct((n, D_MODEL), jnp.float32),
        grid=(n // tm,),
        in_specs=[rows(D_MODEL), rows(WIDTH_A), rows(WIDTH_A), rows(WIDTH_A), rows(WIDTH_B), rows(C_GATE),
                  _const_spec((2, WIDTH_A)), _const_spec((2, D_MODEL)), _const_spec((2 * PAIR_W, PAIR_W)),
                  _const_spec((WIDTH_A, D_MODEL)), _const_spec((WIDTH_B, D_MODEL)), _const_spec((D_MODEL, D_MODEL))],
        out_specs=rows(D_MODEL),
        compiler_params=pltpu.CompilerParams(dimension_semantics=("arbitrary",), vmem_limit_bytes=VMEM_LIMIT),
        name="branch_merge",
    )(x, y, bonus, g, o_b, gates, jnp.stack([p['lnx_g'], p['lnx_b']]), jnp.stack([p['ln1_g'], p['ln1_b']]),
      _block_ones2(), p['w_branch_a16'], p['w_branch_b16'], p['w_out16'])


def _mem_kernel(x_ref, mk_ref, mv_ref, ln2_ref, wq_ref, wo_ref, pwq_ref, o_ref, q_ref):
    x = x_ref[0]
    qm = jnp.dot(x.astype(jnp.bfloat16), wq_ref[...], preferred_element_type=jnp.float32).astype(jnp.bfloat16)
    mk = mk_ref[0].astype(jnp.bfloat16)
    mv = mv_ref[0].astype(jnp.bfloat16)
    outs = []
    for h in range(N_HEADS_M):
        cols = slice(h * HEAD_M, (h + 1) * HEAD_M)
        s = lax.dot_general(qm[:, cols], mk[:, cols], (((1,), (1,)), ((), ())),
                            preferred_element_type=jnp.float32) * (HEAD_M ** -0.5)
        e = jnp.exp(s - jnp.max(s, axis=-1, keepdims=True))
        pr = e / jnp.sum(e, axis=-1, keepdims=True)
        outs.append(jnp.dot(pr.astype(jnp.bfloat16), mv[:, cols], preferred_element_type=jnp.float32))
    o = jnp.concatenate(outs, axis=1).astype(jnp.bfloat16)
    h2 = ALPHA * x + jnp.dot(o, wo_ref[...], preferred_element_type=jnp.float32)
    x2 = _layer_norm_rows(h2, ln2_ref[0:1, :], ln2_ref[1:2, :])
    o_ref[0] = x2
    q_ref[0] = jnp.dot(x2.astype(jnp.bfloat16), pwq_ref[...], preferred_element_type=jnp.float32)


def mem_block(x, mk, mv, p, tm=256):
    B, T, _ = x.shape
    tm = _pick(T, (tm, 128, 64))
    wm = N_HEADS_M * HEAD_M
    rows = lambda w: pl.BlockSpec((1, tm, w), lambda b, i: (b, i, 0))
    mem = pl.BlockSpec((1, N_MEM, wm), lambda b, i: (b, 0, 0))
    out = jax.ShapeDtypeStruct((B, T, D_MODEL), jnp.float32)
    return pl.pallas_call(
        _mem_kernel,
        out_shape=(out, jax.ShapeDtypeStruct((B, T, PEER_HEADS * D_KEY), jnp.float32)),
        grid=(B, T // tm),
        in_specs=[rows(D_MODEL), mem, mem, _const_spec((2, D_MODEL)), _const_spec((D_MODEL, wm)),
                  _const_spec((wm, D_MODEL)), _const_spec((D_MODEL, PEER_HEADS * D_KEY))],
        out_specs=(rows(D_MODEL), rows(PEER_HEADS * D_KEY)),
        compiler_params=pltpu.CompilerParams(
            dimension_semantics=("arbitrary", "arbitrary"), vmem_limit_bytes=VMEM_LIMIT),
        name="mem_block",
    )(x, mk, mv, jnp.stack([p['ln2_g'], p['ln2_b']]), p['wq_mem16'], p['wo_mem16'], p['peer_wq16'])


ROUTE_TQ = 256


def _top_values(s, k):
    n_rows = s.shape[0]
    iota = lax.broadcasted_iota(jnp.int32, s.shape, 0).astype(jnp.float32)
    out = []
    for _ in range(k):
        m = jnp.max(s, axis=0, keepdims=True)
        first = jnp.min(jnp.where(s == m, iota, float(n_rows)), axis=0, keepdims=True)
        s = jnp.where(iota == first, -jnp.inf, s)
        out.append(m)
    return out


def _peer_route_kernel(q_ref, keys_ref, tau_ref, c1_ref, s2_ref, e2_ref):
    half = D_KEY // 2
    for h in range(PEER_HEADS):
        tops, scores = [], []
        for p in range(2):
            c0 = (2 * h + p) * half
            qs = q_ref[:, c0:c0 + half].astype(jnp.bfloat16)
            s = lax.dot_general(keys_ref[2 * h + p], qs, (((1,), (1,)), ((), ())),
                                preferred_element_type=jnp.float32)
            scores.append(s)
            tops.append(_top_values(s, TOPK))
        t1, t2 = tops
        t2all = jnp.concatenate(t2, axis=0)
        cand = jnp.concatenate([t1[a] + t2all for a in range(TOPK)], axis=0)
        sc = _top_values(cand, TOPK)
        z = jnp.zeros_like(sc[0])
        for kq in range(TOPK):
            z = z + jnp.exp(sc[kq] - sc[0])
        theta = sc[TOPK - 1]
        tau = jnp.full_like(scores[0], jnp.inf)
        for a in range(TOPK):
            tau_a = jnp.min(jnp.where(t1[a] + t2all >= theta, t2all, jnp.inf), axis=0, keepdims=True)
            tau = jnp.where(scores[0] == t1[a], tau_a, tau)
        tau_ref[h] = tau
        s2_ref[h] = scores[1]
        c1_ref[h] = jnp.exp(scores[0] - t1[0]) / z
        e2_ref[h] = jnp.exp(scores[1] - t2[0])


def peer_route(q, keys16):
    n = q.shape[0]
    tq = _pick(n, (ROUTE_TQ, 128))
    big = jax.ShapeDtypeStruct((PEER_HEADS, N_KEYS, n), jnp.float32)
    bspec = pl.BlockSpec((PEER_HEADS, N_KEYS, tq), lambda i: (0, 0, i))
    return pl.pallas_call(
        _peer_route_kernel,
        out_shape=(big, big, big, big),
        grid=(n // tq,),
        in_specs=[pl.BlockSpec((tq, PEER_HEADS * D_KEY), lambda i: (i, 0)),
                  pl.BlockSpec((2 * PEER_HEADS, N_KEYS, D_KEY // 2), lambda i: (0, 0, 0))],
        out_specs=(bspec, bspec, bspec, bspec),
        compiler_params=pltpu.CompilerParams(
            dimension_semantics=("arbitrary",), vmem_limit_bytes=VMEM_LIMIT),
        name="peer_route",
    )(q, keys16)


PEER_TM = 512
PEER_ROWS = 8
PEER_TE = PEER_ROWS * N_KEYS


def _gelu(x):
    return 0.5 * x * (1.0 + lax.erf(x * (2.0 ** -0.5)))


def _peer_mix_kernel(x_ref, u_ref, vt_ref, tau_ref, c1_ref, s2_ref, e2_ref, ln3_ref, o_ref,
                     x16_ref, h_ref, acc_ref):
    j = pl.program_id(1)

    @pl.when(j == 0)
    def _():
        x16_ref[...] = x_ref[...].astype(jnp.bfloat16)
        acc_ref[...] = jnp.zeros_like(acc_ref)

    a_t = lax.dot_general(u_ref[...], x16_ref[...], (((1,), (1,)), ((), ())),
                          preferred_element_type=jnp.float32)
    for r in range(PEER_ROWS):
        gate = None
        for h in range(PEER_HEADS):
            w = jnp.where(s2_ref[h] >= tau_ref[h, r:r + 1, :], e2_ref[h], 0.0) * c1_ref[h, r:r + 1, :]
            gate = w if gate is None else gate + w
        rows = a_t[r * N_KEYS:(r + 1) * N_KEYS]
        h_ref[r * N_KEYS:(r + 1) * N_KEYS, :] = (gate * _gelu(rows)).astype(jnp.bfloat16)
    acc_ref[...] += jnp.dot(vt_ref[...], h_ref[...], preferred_element_type=jnp.float32)

    @pl.when(j == pl.num_programs(1) - 1)
    def _():
        o_ref[...] = _layer_norm_rows(ALPHA * x_ref[...] + acc_ref[...].T, ln3_ref[0:1, :], ln3_ref[1:2, :])


def peer_mix(x, u16, vt16, tau, c1, s2, e2, ln3):
    n, d = x.shape
    tm = _pick(n, (PEER_TM, 256, 128))
    n_exp = u16.shape[0]
    row_spec = pl.BlockSpec((PEER_HEADS, PEER_ROWS, tm), lambda i, j: (0, j, i))
    all_spec = pl.BlockSpec((PEER_HEADS, N_KEYS, tm), lambda i, j: (0, 0, i), pipeline_mode=pl.Buffered(1))
    return pl.pallas_call(
        _peer_mix_kernel,
        out_shape=jax.ShapeDtypeStruct((n, d), jnp.float32),
        grid=(n // tm, n_exp // PEER_TE),
        in_specs=[pl.BlockSpec((tm, d), lambda i, j: (i, 0), pipeline_mode=pl.Buffered(1)),
                  pl.BlockSpec((PEER_TE, d), lambda i, j: (j, 0)),
                  pl.BlockSpec((d, PEER_TE), lambda i, j: (0, j)),
                  row_spec, row_spec, all_spec, all_spec, _const_spec((2, d))],
        out_specs=pl.BlockSpec((tm, d), lambda i, j: (i, 0)),
        scratch_shapes=[pltpu.VMEM((tm, d), jnp.bfloat16), pltpu.VMEM((PEER_TE, tm), jnp.bfloat16),
                        pltpu.VMEM((d, tm), jnp.float32)],
        compiler_params=pltpu.CompilerParams(
            dimension_semantics=("arbitrary", "arbitrary"), vmem_limit_bytes=VMEM_LIMIT),
        name="peer_mix",
    )(x, u16, vt16, tau, c1, s2, e2, ln3)


def peer_block(x, q, p):
    stats = peer_route(q, p['peer_keys16'])
    return peer_mix(x, p['peer_u16'], p['peer_vt16'], *stats, jnp.stack([p['ln3_g'], p['ln3_b']]))


def layer_norm(x, g, b, eps=1e-5):
    xf = x.astype(jnp.float32)
    mu = jnp.mean(xf, -1, keepdims=True)
    var = jnp.mean(jnp.square(xf - mu), -1, keepdims=True)
    return ((xf - mu) * lax.rsqrt(var + eps) * g + b).astype(x.dtype)


def rwkv7_scan(r, w, k, v, kk, a, state0):
    def step(S, inp):
        r_t, w_t, k_t, v_t, kk_t, a_t = inp
        sa = jnp.einsum('bhij,bhj->bhi', S, -kk_t)
        S = (S * w_t[:, :, None, :] + sa[..., None] * (kk_t * a_t)[:, :, None, :]
             + v_t[..., None] * k_t[:, :, None, :])
        y = jnp.einsum('bhij,bhj->bhi', S, r_t)
        return S, y
    xs = tuple(jnp.moveaxis(t, 1, 0) for t in (r, w, k, v, kk, a))
    S, ys = lax.scan(step, state0.astype(jnp.float32), xs)
    return jnp.moveaxis(ys, 0, 1), S


def rwkv7_branch(seg, prev_row, state0, p):
    B, T, _ = seg.shape
    f32 = jnp.float32
    shifted = jnp.concatenate([prev_row.astype(seg.dtype), seg[:, :-1]], axis=1)
    xm = seg + p['shift_mu'] * (shifted - seg)
    r, k, v, wl, al, gl = jnp.split(xm, SPLIT_A, axis=-1)
    w_log = -jax.nn.softplus(-(p['w0'] + jnp.tanh(wl) @ p['w_lora_up']).astype(f32)) - 0.5
    decay = jnp.exp(-jnp.exp(w_log))
    a = jax.nn.sigmoid((p['a0'] + al @ p['a_lora_up']).astype(f32))
    g = jax.nn.sigmoid(gl) @ p['g_lora_up']
    heads = lambda t: t.reshape(B, T, N_HEADS_A, HEAD_A)
    kk = heads((k * p['k_k']).astype(f32))
    kk = kk * lax.rsqrt(jnp.maximum(jnp.sum(kk * kk, -1, keepdims=True), 1e-24))
    kf = k.astype(f32) * (1.0 + (a - 1.0) * p['k_a'].astype(f32))
    rh, kh, vh, ah, wh = heads(r.astype(f32)), heads(kf), heads(v.astype(f32)), heads(a), heads(decay)
    flat = lambda t: t.reshape(B, T, WIDTH_A)
    y, s_final = rwkv7_scan_pallas(flat(rh), flat(wh), flat(kh), flat(vh), flat(kk), flat(ah), state0)
    y = heads(y)
    mean = jnp.mean(y, -1, keepdims=True)
    var = jnp.mean(jnp.square(y - mean), -1, keepdims=True)
    yn = ((y - mean) * lax.rsqrt(var + GN_EPS)).reshape(B, T, WIDTH_A) * p['lnx_g'] + p['lnx_b']
    bonus = jnp.sum(rh * kh * p['r_k'].astype(f32), -1, keepdims=True) * vh
    out = (yn + bonus.reshape(B, T, WIDTH_A)) * g
    return out.astype(seg.dtype), s_final, seg[:, -1:]


def alibi_bias(q_pos, k_pos):
    slopes = 2.0 ** (-8.0 * jnp.arange(1, N_HEADS_B + 1, dtype=jnp.float32) / N_HEADS_B)
    dist = jnp.abs(q_pos[:, None] - k_pos[None, :]).astype(jnp.float32)
    return -slopes.reshape(N_KV_B, GROUP_B, 1, 1) * dist


def sink_attention(q, k, v, bias, mask, sinks):
    s = jnp.einsum('bnqhgd,bnjhd->bnhgqj', q, k).astype(jnp.float32) * (HEAD_B ** -0.5) + bias
    if mask is not None:
        s = jnp.where(mask, s, NEG_INF)
    sink = sinks.astype(jnp.float32).reshape(N_KV_B, GROUP_B, 1, 1)
    m = jnp.maximum(jnp.max(s, -1, keepdims=True), sink)
    p = jnp.exp(s - m)
    p = p / (jnp.sum(p, -1, keepdims=True) + jnp.exp(sink - m))
    return jnp.einsum('bnhgqj,bnjhd->bnqhgd', p.astype(v.dtype), v)


def swa_prompt(q, k, v, sinks):
    B, S = q.shape[:2]
    NC = S // CHUNK
    band = (WIN_CHUNKS + 1) * CHUNK
    qc = q.reshape(B, NC, CHUNK, N_KV_B, GROUP_B, HEAD_B)
    pad = ((0, 0), (WIN_CHUNKS, 0), (0, 0), (0, 0), (0, 0))
    kp = jnp.pad(k.reshape(B, NC, CHUNK, N_KV_B, HEAD_B), pad)
    vp = jnp.pad(v.reshape(B, NC, CHUNK, N_KV_B, HEAD_B), pad)
    kb = jnp.concatenate([kp[:, o:o + NC] for o in range(WIN_CHUNKS + 1)], axis=2)
    vb = jnp.concatenate([vp[:, o:o + NC] for o in range(WIN_CHUNKS + 1)], axis=2)
    bias = alibi_bias(jnp.arange(CHUNK), jnp.arange(band) - WIN_CHUNKS * CHUNK)
    key_chunk = jnp.arange(NC)[:, None] - WIN_CHUNKS + (jnp.arange(band) // CHUNK)[None, :]
    mask = (key_chunk >= 0)[None, :, None, None, None, :]
    o = sink_attention(qc, kb, vb, bias, mask, sinks)
    return o.reshape(B, S, WIDTH_B)


def swa_sample(q, k, v, k_cache, v_cache, sinks):
    B, T = q.shape[:2]
    Lc = k_cache.shape[1]
    k_all = jnp.concatenate([k_cache.astype(k.dtype), k], axis=1)
    v_all = jnp.concatenate([v_cache.astype(v.dtype), v], axis=1)
    bias = alibi_bias(jnp.arange(T), jnp.arange(Lc + T) - Lc)
    o = sink_attention(q[:, None], k_all[:, None], v_all[:, None], bias, None, sinks)
    return o.reshape(B, T, WIDTH_B), k_all[:, -Lc:], v_all[:, -Lc:]


def mem_attention(x, mk, mv, p):
    B, T, _ = x.shape
    q = mm(x, p['wq_mem16']).reshape(B, T, N_HEADS_M, HEAD_M)
    s = jnp.einsum('bthd,bmhd->bhtm', q, mk.astype(q.dtype)).astype(jnp.float32) * (HEAD_M ** -0.5)
    pr = jax.nn.softmax(s, axis=-1).astype(x.dtype)
    o = jnp.einsum('bhtm,bmhd->bthd', pr, mv.astype(x.dtype)).reshape(B, T, N_HEADS_M * HEAD_M)
    return mm(o, p['wo_mem16'])


def peer_ffn(x, p):
    B, T, D = x.shape
    n = B * T
    nblk = -(-n // PEER_BLOCK)
    xt = x.reshape(nblk, PEER_BLOCK, D)
    qa = mm(x, p['peer_wq16']).reshape(nblk, PEER_BLOCK, PEER_HEADS * D_KEY)
    sub_keys, u_tab, v_tab = p['peer_sub_keys'], p['peer_u'], p['peer_v']

    def peer_block(args):
        xb, qb = args
        q = qb.reshape(PEER_BLOCK, PEER_HEADS, 2, D_KEY // 2)
        s = jnp.einsum('thpd,hpnd->thpn', q, sub_keys).astype(jnp.float32)
        s1, i1 = lax.top_k(s[:, :, 0], TOPK)
        s2, i2 = lax.top_k(s[:, :, 1], TOPK)
        cand = (s1[..., :, None] + s2[..., None, :]).reshape(PEER_BLOCK, PEER_HEADS, TOPK * TOPK)
        cidx = (i1[..., :, None] * N_KEYS + i2[..., None, :]).reshape(PEER_BLOCK, PEER_HEADS, TOPK * TOPK)
        sc, pos = lax.top_k(cand, TOPK)
        e = jnp.take_along_axis(cidx, pos, axis=-1)
        g = jax.nn.softmax(sc, axis=-1)
        act = jax.nn.gelu(jnp.einsum('td,thkd->thk', xb, u_tab[e]).astype(jnp.float32), approximate=False)
        return jnp.einsum('thk,thkd->td', (g * act).astype(xb.dtype), v_tab[e])

    y = lax.map(peer_block, (xt, qa)).reshape(nblk * PEER_BLOCK, D)
    return y.reshape(B, T, D)


def trunk_layer(x, p, mem_k, mem_v, rwkv_state, shift_row, swa_k_cache, swa_v_cache):
    B, T, _ = x.shape
    n = B * T
    x2d = x.reshape(n, D_MODEL)
    seg_a = matmul(x2d, p['w_in_a16']).reshape(B, T, C_RWKV)
    seg_b = matmul(x2d, p['w_in_b16']).reshape(B, T, C_SWA)
    gates = matmul(x2d, p['w_in_g16'])
    r, w, k, v, kk, a, g, bonus = rwkv7_pre(seg_a, shift_row.astype(jnp.float32), p)
    y, rwkv_new = rwkv7_scan_pallas(r, w, k, v, kk, a, rwkv_state)
    shift_new = seg_a[:, -1:]
    k_new = seg_b[:, :, WIDTH_B:WIDTH_B + KV_WIDTH_B].reshape(B, T, N_KV_B, HEAD_B)
    v_new = seg_b[:, :, WIDTH_B + KV_WIDTH_B:].reshape(B, T, N_KV_B, HEAD_B)
    if swa_k_cache is None:
        o_b = swa_attention(seg_b, seg_b, seg_b, p['attn_sinks'], prev_is_seq=True)
        swa_k_new, swa_v_new = k_new[:, -WINDOW:], v_new[:, -WINDOW:]
    else:
        o_b = swa_attention(seg_b, swa_k_cache.reshape(B, WINDOW, KV_WIDTH_B),
                            swa_v_cache.reshape(B, WINDOW, KV_WIDTH_B), p['attn_sinks'], prev_is_seq=False)
        swa_k_new = jnp.concatenate([swa_k_cache, k_new], axis=1)[:, -WINDOW:]
        swa_v_new = jnp.concatenate([swa_v_cache, v_new], axis=1)[:, -WINDOW:]
    flat = lambda t: t.reshape(n, t.shape[-1])
    x1 = branch_merge(x2d, flat(y), flat(bonus), flat(g), flat(o_b), gates, p)
    wm = N_HEADS_M * HEAD_M
    x2, q = mem_block(x1.reshape(B, T, D_MODEL), mem_k.reshape(-1, N_MEM, wm), mem_v.reshape(-1, N_MEM, wm), p)
    x3 = peer_block(flat(x2), flat(q), p)
    return x3.reshape(B, T, D_MODEL), rwkv_new, shift_new, swa_k_new, swa_v_new


_MM_WEIGHTS = ('w_branch_a', 'w_branch_b', 'w_out', 'wq_mem', 'wk_mem', 'wv_mem', 'wo_mem', 'peer_wq')


def kernel(x_prompt, x_sample, state_rwkv, state_shift, cache_swa_k, cache_swa_v, cache_mem_k, cache_mem_v, mem_prompt, w_in, shift_mu, w0, w_lora_up, a0, a_lora_up, g_lora_up, k_k, k_a, r_k, lnx_g, lnx_b, attn_sinks, w_branch_a, w_branch_b, w_out, ln1_g, ln1_b, wq_mem, wk_mem, wv_mem, wo_mem, ln2_g, ln2_b, peer_wq, peer_sub_keys, peer_u, peer_v, ln3_g, ln3_b):
    params = {
        'w_in': w_in, 'shift_mu': shift_mu, 'w0': w0, 'w_lora_up': w_lora_up, 'a0': a0,
        'a_lora_up': a_lora_up, 'g_lora_up': g_lora_up, 'k_k': k_k, 'k_a': k_a, 'r_k': r_k,
        'lnx_g': lnx_g, 'lnx_b': lnx_b, 'attn_sinks': attn_sinks, 'w_branch_a': w_branch_a,
        'w_branch_b': w_branch_b, 'w_out': w_out, 'ln1_g': ln1_g, 'ln1_b': ln1_b, 'wq_mem': wq_mem,
        'wk_mem': wk_mem, 'wv_mem': wv_mem, 'wo_mem': wo_mem, 'ln2_g': ln2_g, 'ln2_b': ln2_b,
        'peer_wq': peer_wq, 'peer_sub_keys': peer_sub_keys, 'peer_u': peer_u, 'peer_v': peer_v,
        'ln3_g': ln3_g, 'ln3_b': ln3_b,
    }
    B = x_prompt.shape[0]
    rwkv0 = jnp.zeros((B, N_HEADS_A, HEAD_A, HEAD_A), jnp.float32)
    shift0 = jnp.zeros((B, 1, C_RWKV), x_prompt.dtype)
    xp, xs = x_prompt, x_sample
    p_rw, p_sh, p_k, p_v, p_mk, p_mv = [], [], [], [], [], []
    s_rw, s_sh, s_k, s_v = [], [], [], []
    for l in range(DEPTH):
        p = {name: arr[l] for name, arr in params.items()}
        for name in _MM_WEIGHTS:
            p[name + '16'] = p[name].astype(jnp.bfloat16)
        w_in16 = p['w_in'].astype(jnp.bfloat16)
        p['w_in_a16'] = w_in16[:, :C_RWKV]
        p['w_in_b16'] = w_in16[:, C_RWKV:C_RWKV + C_SWA]
        p['w_in_g16'] = w_in16[:, C_RWKV + C_SWA:]
        p['peer_keys16'] = p['peer_sub_keys'].reshape(2 * PEER_HEADS, N_KEYS, D_KEY // 2).astype(jnp.bfloat16)
        p['peer_u16'] = p['peer_u'].astype(jnp.bfloat16)
        p['peer_vt16'] = p['peer_v'].T.astype(jnp.bfloat16)
        mk = mm(mem_prompt, p['wk_mem16']).reshape(B, N_MEM, N_HEADS_M, HEAD_M)
        mv = mm(mem_prompt, p['wv_mem16']).reshape(B, N_MEM, N_HEADS_M, HEAD_M)
        xp, rw, sh, kn, vn = trunk_layer(xp, p, mk, mv, rwkv0, shift0, None, None)
        p_rw.append(rw); p_sh.append(sh); p_k.append(kn); p_v.append(vn); p_mk.append(mk); p_mv.append(mv)
        xs, rw, sh, kn, vn = trunk_layer(xs, p, cache_mem_k[l], cache_mem_v[l], state_rwkv[l], state_shift[l],
                                         cache_swa_k[l], cache_swa_v[l])
        s_rw.append(rw); s_sh.append(sh); s_k.append(kn); s_v.append(vn)
    return (xp, xs,
            jnp.stack(p_rw), jnp.stack(p_sh), jnp.stack(p_k), jnp.stack(p_v), jnp.stack(p_mk), jnp.stack(p_mv),
            jnp.stack(s_rw), jnp.stack(s_sh), jnp.stack(s_k), jnp.stack(s_v))
```

```python
import functools

import jax
import jax.numpy as jnp
from jax import lax
from jax.experimental import pallas as pl
from jax.experimental.pallas import tpu as pltpu

D_MODEL = 2048
DEPTH = 2
CHUNK = 64
HEAD_A = 64
N_HEADS_A = 16
WIDTH_A = N_HEADS_A * HEAD_A
LORA_W = 64
LORA_A = 64
LORA_G = 128
GN_EPS = 64e-5
SPLIT_A = [WIDTH_A, 2 * WIDTH_A, 3 * WIDTH_A, 3 * WIDTH_A + LORA_W, 3 * WIDTH_A + LORA_W + LORA_A]
C_RWKV = 3 * WIDTH_A + LORA_W + LORA_A + LORA_G
HEAD_B = 64
N_HEADS_B = 16
N_KV_B = 4
GROUP_B = N_HEADS_B // N_KV_B
WIDTH_B = N_HEADS_B * HEAD_B
KV_WIDTH_B = N_KV_B * HEAD_B
WINDOW = 128
WIN_CHUNKS = WINDOW // CHUNK
C_SWA = WIDTH_B + 2 * KV_WIDTH_B
C_GATE = 2 * D_MODEL
C_IN = C_RWKV + C_SWA + C_GATE
N_MEM = 256
N_HEADS_M = 4
HEAD_M = 128
N_KEYS = 128
PEER_HEADS = 8
D_KEY = 256
TOPK = 16
PEER_BLOCK = 128
ALPHA = (2.0 * DEPTH) ** 0.25
NEG_INF = -1e30

VMEM_LIMIT = 56 * 1024 * 1024


def _matmul_kernel(a_ref, b_ref, o_ref, a16_ref):
    @pl.when(pl.program_id(1) == 0)
    def _():
        a16_ref[...] = a_ref[...].astype(jnp.bfloat16)

    o_ref[...] = jnp.dot(a16_ref[...], b_ref[...], preferred_element_type=jnp.float32)


def _pick(n, cands):
    for c in cands:
        if n % c == 0:
            return c
    return n


def matmul(a, b16):
    m, k = a.shape
    n = b16.shape[1]
    tm = _pick(m, (512, 256, 128))
    tn = _pick(n, (2048, 1664, 1536, 1024, 512, 256, 128))
    return pl.pallas_call(
        _matmul_kernel,
        out_shape=jax.ShapeDtypeStruct((m, n), jnp.float32),
        grid=(m // tm, n // tn),
        in_specs=[pl.BlockSpec((tm, k), lambda i, j: (i, 0)),
                  pl.BlockSpec((k, tn), lambda i, j: (0, j))],
        out_specs=pl.BlockSpec((tm, tn), lambda i, j: (i, j)),
        scratch_shapes=[pltpu.VMEM((tm, k), jnp.bfloat16)],
        compiler_params=pltpu.CompilerParams(
            dimension_semantics=("arbitrary", "arbitrary"),
            vmem_limit_bytes=VMEM_LIMIT),
        name="matmul",
    )(a, b16)


def mm(x, w16):
    lead = x.shape[:-1]
    return matmul(x.reshape(-1, x.shape[-1]), w16).reshape(*lead, w16.shape[1])


N_PAIR = N_HEADS_A // 2
PAIR_W = 2 * HEAD_A


def _split2(x):
    hi = x.astype(jnp.bfloat16)
    lo = (x - hi.astype(jnp.float32)).astype(jnp.bfloat16)
    return jnp.concatenate([hi, lo], axis=1)


def _rwkv_scan_kernel(r_ref, w_ref, k_ref, v_ref, kk_ref, a_ref, s0_ref, ones_ref, eye_ref,
                      y_ref, sT_ref, s_ref, *, tb):
    tblk = pl.program_id(1)

    @pl.when(tblk == 0)
    def _():
        s_ref[...] = s0_ref[0]

    ones2 = ones_ref[...]
    eye2 = eye_ref[...]

    def segsum(tiles):
        lhs = jnp.concatenate([_split2(m) for m in tiles], axis=0)
        return jnp.dot(lhs, ones2, preferred_element_type=jnp.float32)

    def step(g, carry):
        rows = pl.ds(pl.multiple_of(g * 8, 8), 8)
        pairs = range(N_PAIR)
        sls = [pl.ds(p * PAIR_W, PAIR_W) for p in pairs]
        kk8 = [kk_ref[0, rows, sl] for sl in sls]
        kka8 = [kk8[p] * a_ref[0, rows, sls[p]] for p in pairs]
        w8 = [w_ref[0, rows, sl] for sl in sls]
        k8 = [k_ref[0, rows, sl] for sl in sls]
        v8 = [v_ref[0, rows, sl] for sl in sls]
        r8 = [r_ref[0, rows, sl] for sl in sls]
        s = [s_ref[p] for p in pairs]
        ys = [[] for _ in pairs]
        tile = lambda x, p: x[p * HEAD_A:(p + 1) * HEAD_A]
        for j in range(8):
            row = slice(j, j + 1)
            sa = segsum([s[p] * (-kk8[p][row]) for p in pairs])
            vb = segsum([eye2 * v8[p][row] for p in pairs])
            s = [s[p] * w8[p][row] + tile(sa, p) * kka8[p][row] + tile(vb, p) * k8[p][row] for p in pairs]
            yb = segsum([s[p] * r8[p][row] for p in pairs])
            for p in pairs:
                ys[p].append(jnp.sum(tile(yb, p) * eye2, axis=0, keepdims=True))
        for p in pairs:
            s_ref[p] = s[p]
            y_ref[0, rows, sls[p]] = jnp.concatenate(ys[p], axis=0)
        return carry

    lax.fori_loop(0, tb // 8, step, 0)

    @pl.when(tblk == pl.num_programs(1) - 1)
    def _():
        sT_ref[0] = s_ref[...]


def rwkv7_scan_pallas(r, w, k, v, kk, a, state0, tb=256):
    B, T, _ = r.shape
    tb = min(tb, T)
    s0 = state0.astype(jnp.float32).reshape(B, N_PAIR, 2, HEAD_A, HEAD_A)
    s0 = s0.transpose(0, 1, 3, 2, 4).reshape(B, N_PAIR, HEAD_A, PAIR_W)
    lane_head = jnp.arange(PAIR_W) // HEAD_A
    ones2 = (jnp.tile(lane_head, 2)[:, None] == lane_head[None, :]).astype(jnp.bfloat16)
    eye2 = (jnp.arange(HEAD_A)[:, None] == (jnp.arange(PAIR_W) % HEAD_A)[None, :]).astype(jnp.float32)
    seq = pl.BlockSpec((1, tb, WIDTH_A), lambda b, t: (b, t, 0))
    st = pl.BlockSpec((1, N_PAIR, HEAD_A, PAIR_W), lambda b, t: (b, 0, 0, 0))
    y, sT = pl.pallas_call(
        functools.partial(_rwkv_scan_kernel, tb=tb),
        out_shape=(jax.ShapeDtypeStruct((B, T, WIDTH_A), jnp.float32),
                   jax.ShapeDtypeStruct((B, N_PAIR, HEAD_A, PAIR_W), jnp.float32)),
        grid=(B, T // tb),
        in_specs=[seq] * 6 + [st,
                              pl.BlockSpec((2 * PAIR_W, PAIR_W), lambda b, t: (0, 0)),
                              pl.BlockSpec((HEAD_A, PAIR_W), lambda b, t: (0, 0))],
        out_specs=(seq, st),
        scratch_shapes=[pltpu.VMEM((N_PAIR, HEAD_A, PAIR_W), jnp.float32)],
        compiler_params=pltpu.CompilerParams(
            dimension_semantics=("arbitrary", "arbitrary"),
            vmem_limit_bytes=VMEM_LIMIT),
        name="rwkv7_scan",
    )(r, w, k, v, kk, a, s0, ones2, eye2)
    sT = sT.reshape(B, N_PAIR, HEAD_A, 2, HEAD_A).transpose(0, 1, 3, 2, 4)
    return y, sT.reshape(B, N_HEADS_A, HEAD_A, HEAD_A)


def _hl(x):
    hi = x.astype(jnp.bfloat16)
    return hi, (x - hi.astype(jnp.float32)).astype(jnp.bfloat16)


def _dot3(a, b):
    ah, al = _hl(a)
    bh, bl = _hl(b)
    return jnp.dot(jnp.concatenate([ah, ah, al], axis=1), jnp.concatenate([bh, bl, bh], axis=0),
                   preferred_element_type=jnp.float32)


def _dot3_nt(a, b):
    ah, al = _hl(a)
    bh, bl = _hl(b)
    return lax.dot_general(jnp.concatenate([ah, ah, al], axis=1), jnp.concatenate([bh, bl, bh], axis=1),
                           (((1,), (1,)), ((), ())), preferred_element_type=jnp.float32)


def _rwkv_chunk_kernel(r_ref, w_ref, k_ref, v_ref, kk_ref, a_ref, s0_ref, ltri_ref, ones_ref, msk_ref,
                       y_ref, sT_ref, p_ref, *, n_chunks):
    tblk = pl.program_id(1)

    @pl.when(tblk == 0)
    def _():
        p_ref[...] = s0_ref[0]

    eye2, m0, m1, strict, incl = (msk_ref[i] for i in range(5))
    bd = lambda y: jnp.concatenate([y * m0, y * m1], axis=0)

    def split3(x, axis):
        t1 = x.astype(jnp.bfloat16)
        d = x - t1.astype(jnp.float32)
        t2 = d.astype(jnp.bfloat16)
        t3 = (d - t2.astype(jnp.float32)).astype(jnp.bfloat16)
        return jnp.concatenate([t1, t2, t3], axis=axis)

    def chunk(c, carry):
        rows = pl.ds(pl.multiple_of(c * CHUNK, CHUNK), CHUNK)
        pairs = range(N_PAIR)
        sls = [pl.ds(p * PAIR_W, PAIR_W) for p in pairs]
        lw = [w_ref[0, rows, sl] for sl in sls]
        cum = [jnp.dot(ltri_ref[...], split3(lw[p], 0), preferred_element_type=jnp.float32) for p in pairs]
        g = [jnp.exp(cum[p]) for p in pairs]
        ginv = [jnp.exp(-cum[p]) for p in pairs]
        kk = [kk_ref[0, rows, sl] for sl in sls]
        kh = [kk[p] * jnp.exp(cum[p] - lw[p]) for p in pairs]
        bh = [kk[p] * a_ref[0, rows, sls[p]] * ginv[p] for p in pairs]
        kf = [k_ref[0, rows, sls[p]] * ginv[p] for p in pairs]
        rh = [r_ref[0, rows, sls[p]] * g[p] for p in pairs]
        v = [v_ref[0, rows, sl] for sl in sls]
        g_last = [g[p][CHUNK - 1:CHUNK, :] for p in pairs]
        kr = [jnp.concatenate([kh[p], rh[p]], axis=0) for p in pairs]
        gram = [_dot3_nt(kr[p], jnp.concatenate([bd(bh[p]), bd(kf[p])], axis=0)) for p in pairs]
        a_b = [gram[p][:CHUNK, :PAIR_W] * strict for p in pairs]
        a_k = [gram[p][:CHUNK, PAIR_W:] * strict for p in pairs]
        a_r = [jnp.concatenate([gram[p][CHUNK:, :PAIR_W] * incl, gram[p][CHUNK:, PAIR_W:] * incl], axis=1)
               for p in pairs]
        t_inv = [eye2 - a_b[p] * msk_ref[5] for p in pairs]
        for lvl in range(1, 6):
            half = [_dot3(t_inv[p], bd(a_b[p] * msk_ref[5 + lvl])) for p in pairs]
            t_inv = [t_inv[p] - _dot3(half[p], bd(t_inv[p])) for p in pairs]
        akv = [_dot3(a_k[p], bd(v[p])) for p in pairs]
        g_col = [jnp.dot(split3(eye2 * g_last[p], 1), ones_ref[...], preferred_element_type=jnp.float32)
                 for p in pairs]
        xt = [jnp.concatenate([bh[p] * g_last[p], kf[p] * g_last[p]], axis=0).T for p in pairs]
        p0 = [p_ref[p] for p in pairs]
        zy = [_dot3(kr[p], bd(p0[p])) for p in pairs]
        u = [-_dot3(t_inv[p], bd(zy[p][:CHUNK] + akv[p])) for p in pairs]
        y = [zy[p][CHUNK:] + _dot3(a_r[p], jnp.concatenate([bd(u[p]), bd(v[p])], axis=0)) for p in pairs]
        delta = [_dot3(xt[p], jnp.concatenate([u[p], v[p]], axis=0)) for p in pairs]
        for p in pairs:
            p_ref[p] = g_col[p] * p0[p] + delta[p][:CHUNK] * m0 + delta[p][CHUNK:] * m1
            y_ref[0, rows, sls[p]] = y[p]
        return carry

    lax.fori_loop(0, n_chunks, chunk, 0)

    @pl.when(tblk == pl.num_programs(1) - 1)
    def _():
        sT_ref[0] = p_ref[...]


def rwkv7_chunked(r, logw, k, v, kk, a, state0, tb=256):
    B, T, _ = r.shape
    tb = min(tb, T)
    s0 = state0.astype(jnp.float32).reshape(B, N_PAIR, 2, HEAD_A, HEAD_A)
    s0 = s0.transpose(0, 1, 4, 2, 3).reshape(B, N_PAIR, HEAD_A, PAIR_W)
    t_i = jnp.arange(CHUNK)[:, None]
    s_i = (jnp.arange(PAIR_W) % HEAD_A)[None, :]
    lane_head = (jnp.arange(PAIR_W) // HEAD_A)[None, :]
    ones_row = jnp.ones((CHUNK, 1), jnp.int32)
    masks = [t_i == s_i, (lane_head == 0) * ones_row, (lane_head == 1) * ones_row, s_i < t_i, s_i <= t_i]
    for m in (1, 2, 4, 8, 16, 32):
        masks.append((t_i // (2 * m) == s_i // (2 * m)) & (t_i % (2 * m) >= m) & (s_i % (2 * m) < m))
    masks = jnp.stack([mk.astype(jnp.float32) for mk in masks])
    ltri = jnp.tile((jnp.arange(CHUNK)[None, :] <= jnp.arange(CHUNK)[:, None]), (1, 3)).astype(jnp.bfloat16)
    ones3 = jnp.tile((lane_head.T == lane_head), (3, 1)).astype(jnp.bfloat16)
    seq = pl.BlockSpec((1, tb, WIDTH_A), lambda b, t: (b, t, 0))
    st = pl.BlockSpec((1, N_PAIR, HEAD_A, PAIR_W), lambda b, t: (b, 0, 0, 0))
    y, sT = pl.pallas_call(
        functools.partial(_rwkv_chunk_kernel, n_chunks=tb // CHUNK),
        out_shape=(jax.ShapeDtypeStruct((B, T, WIDTH_A), jnp.float32),
                   jax.ShapeDtypeStruct((B, N_PAIR, HEAD_A, PAIR_W), jnp.float32)),
        grid=(B, T // tb),
        in_specs=[seq] * 6 + [st, _const_spec((CHUNK, 3 * CHUNK)), _const_spec((3 * PAIR_W, PAIR_W)),
                              _const_spec((11, CHUNK, PAIR_W))],
        out_specs=(seq, st),
        scratch_shapes=[pltpu.VMEM((N_PAIR, HEAD_A, PAIR_W), jnp.float32)],
        compiler_params=pltpu.CompilerParams(
            dimension_semantics=("arbitrary", "arbitrary"), vmem_limit_bytes=VMEM_LIMIT),
        name="rwkv7_chunked",
    )(r, logw, k, v, kk, a, s0, ltri, ones3, masks)
    sT = sT.reshape(B, N_PAIR, HEAD_A, 2, HEAD_A).transpose(0, 1, 3, 4, 2)
    return y, sT.reshape(B, N_HEADS_A, HEAD_A, HEAD_A)


def _head_sums(x, ones2):
    tiles = [jnp.dot(_split2(x[:, c:c + PAIR_W]), ones2, preferred_element_type=jnp.float32)
             for c in range(0, WIDTH_A, PAIR_W)]
    return jnp.concatenate(tiles, axis=1)


def _softplus(z):
    return jnp.maximum(z, 0.0) + jnp.log1p(jnp.exp(-jnp.abs(z)))


def _rwkv_pre_kernel(seg_ref, prev_ref, shift_ref, mu_ref, vec_ref, wa_ref, gup_ref, ones_ref,
                     r_ref, w_ref, k_ref, v_ref, kk_ref, a_ref, g_ref, bonus_ref):
    seg = seg_ref[0]
    tb = seg.shape[0]
    before = jnp.where(pl.program_id(1) == 0, shift_ref[0], prev_ref[0, 7:8, :])
    row = lax.broadcasted_iota(jnp.int32, seg.shape, 0)
    shifted = jnp.where(row == 0, before, pltpu.roll(seg, 1, axis=0))
    xm = seg + mu_ref[...] * (shifted - seg)
    r = xm[:, :WIDTH_A]
    k = xm[:, WIDTH_A:2 * WIDTH_A]
    v = xm[:, 2 * WIDTH_A:3 * WIDTH_A]
    wa = xm[:, 3 * WIDTH_A:3 * WIDTH_A + LORA_W + LORA_A]
    gl = xm[:, 3 * WIDTH_A + LORA_W + LORA_A:]
    lane = lax.broadcasted_iota(jnp.int32, wa.shape, 1)
    wa = jnp.where(lane < LORA_W, jnp.tanh(wa), wa).astype(jnp.bfloat16)
    lora = jnp.dot(wa, wa_ref[...], preferred_element_type=jnp.float32)
    w0, a0, k_k, k_a, r_k = (vec_ref[i:i + 1, :] for i in range(5))
    w_log = -_softplus(-(w0 + lora[:, :WIDTH_A])) - 0.5
    log_decay = -jnp.exp(w_log)
    a = jax.nn.sigmoid(a0 + lora[:, WIDTH_A:])
    g = jnp.dot(jax.nn.sigmoid(gl).astype(jnp.bfloat16), gup_ref[...], preferred_element_type=jnp.float32)
    ones2 = ones_ref[...]
    kk = k * k_k
    kk = kk * lax.rsqrt(jnp.maximum(_head_sums(kk * kk, ones2), 1e-24))
    kf = k * (1.0 + (a - 1.0) * k_a)
    r_ref[0], w_ref[0], k_ref[0], v_ref[0], kk_ref[0], a_ref[0], g_ref[0] = r, log_decay, kf, v, kk, a, g
    bonus_ref[0] = _head_sums(r * kf * r_k, ones2) * v


def _block_ones2():
    lane_head = jnp.arange(PAIR_W) // HEAD_A
    return (jnp.tile(lane_head, 2)[:, None] == lane_head[None, :]).astype(jnp.bfloat16)


def rwkv7_pre(seg, prev_row, p, tb=256):
    B, T, _ = seg.shape
    tb = min(tb, T)
    zeros = jnp.zeros((LORA_W, WIDTH_A), jnp.float32)
    wa_up = jnp.concatenate([jnp.concatenate([p['w_lora_up'], zeros], 1),
                             jnp.concatenate([zeros, p['a_lora_up']], 1)], 0).astype(jnp.bfloat16)
    vecs = jnp.stack([p['w0'], p['a0'], p['k_k'], p['k_a'], p['r_k'].reshape(WIDTH_A)])
    out = jax.ShapeDtypeStruct((B, T, WIDTH_A), jnp.float32)
    ospec = pl.BlockSpec((1, tb, WIDTH_A), lambda b, t: (b, t, 0))
    full = lambda shape: pl.BlockSpec(shape, lambda b, t: (0,) * len(shape))
    return pl.pallas_call(
        _rwkv_pre_kernel,
        out_shape=(out,) * 8,
        grid=(B, T // tb),
        in_specs=[pl.BlockSpec((1, tb, C_RWKV), lambda b, t: (b, t, 0)),
                  pl.BlockSpec((1, 8, C_RWKV), lambda b, t: (b, jnp.maximum(t * (tb // 8) - 1, 0), 0)),
                  pl.BlockSpec((1, 1, C_RWKV), lambda b, t: (b, 0, 0)),
                  full((1, C_RWKV)), full((5, WIDTH_A)), full((LORA_W + LORA_A, 2 * WIDTH_A)),
                  full((LORA_G, WIDTH_A)), full((2 * PAIR_W, PAIR_W))],
        out_specs=(ospec,) * 8,
        compiler_params=pltpu.CompilerParams(
            dimension_semantics=("arbitrary", "arbitrary"), vmem_limit_bytes=VMEM_LIMIT),
        name="rwkv7_pre",
    )(seg, seg, prev_row, p['shift_mu'].reshape(1, C_RWKV), vecs, wa_up,
      p['g_lora_up'].astype(jnp.bfloat16), _block_ones2())


SWA_BAND = WINDOW + CHUNK


def _swa_kernel(q_ref, k_ref, v_ref, kp_ref, vp_ref, bias_ref, sink_ref, o_ref, *, mask_start):
    n_chunks = q_ref.shape[1] // CHUNK
    k_all = jnp.concatenate([kp_ref[0], k_ref[0]], axis=0).astype(jnp.bfloat16)
    v_all = jnp.concatenate([vp_ref[0], v_ref[0]], axis=0).astype(jnp.bfloat16)
    first = pl.program_id(1) == 0
    key_chunk = lax.broadcasted_iota(jnp.int32, (GROUP_B * CHUNK, SWA_BAND), 1) // CHUNK
    for c in range(n_chunks):
        q_c = q_ref[0, c * CHUNK:(c + 1) * CHUNK, :].astype(jnp.bfloat16)
        k_c = k_all[c * CHUNK:c * CHUNK + SWA_BAND]
        v_c = v_all[c * CHUNK:c * CHUNK + SWA_BAND]
        dead = jnp.logical_and(first, key_chunk + (c - WIN_CHUNKS) < 0) if (mask_start and c < WIN_CHUNKS) else None
        outs = []
        for kv in range(N_KV_B):
            qg = jnp.concatenate([q_c[:, (kv * GROUP_B + g) * HEAD_B:(kv * GROUP_B + g + 1) * HEAD_B]
                                  for g in range(GROUP_B)], axis=0)
            s = lax.dot_general(qg, k_c[:, kv * HEAD_B:(kv + 1) * HEAD_B],
                                (((1,), (1,)), ((), ())), preferred_element_type=jnp.float32)
            s = s * (HEAD_B ** -0.5) + bias_ref[kv]
            if dead is not None:
                s = jnp.where(dead, NEG_INF, s)
            sink = sink_ref[kv * GROUP_B * CHUNK:(kv + 1) * GROUP_B * CHUNK, 0:1]
            m = jnp.maximum(jnp.max(s, axis=-1, keepdims=True), sink)
            e = jnp.exp(s - m)
            pr = e / (jnp.sum(e, axis=-1, keepdims=True) + jnp.exp(sink - m))
            og = jnp.dot(pr.astype(jnp.bfloat16), v_c[:, kv * HEAD_B:(kv + 1) * HEAD_B],
                         preferred_element_type=jnp.float32)
            outs.extend(og[g * CHUNK:(g + 1) * CHUNK] for g in range(GROUP_B))
        o_ref[0, c * CHUNK:(c + 1) * CHUNK, :] = jnp.concatenate(outs, axis=1)


def swa_attention(seg_b, prev_k, prev_v, sinks, *, prev_is_seq, qb=512):
    B, T, _ = seg_b.shape
    qb = min(qb, T)
    slopes = 2.0 ** (-8.0 * jnp.arange(1, N_HEADS_B + 1, dtype=jnp.float32) / N_HEADS_B)
    dist = jnp.abs(jnp.arange(CHUNK)[:, None] - (jnp.arange(SWA_BAND) - WINDOW)[None, :]).astype(jnp.float32)
    bias = (-slopes[:, None, None] * dist).reshape(N_KV_B, GROUP_B * CHUNK, SWA_BAND)
    sink_tab = jnp.broadcast_to(jnp.repeat(sinks.astype(jnp.float32), CHUNK)[:, None], (N_HEADS_B * CHUNK, 128))
    kcol, vcol = WIDTH_B // KV_WIDTH_B, WIDTH_B // KV_WIDTH_B + 1
    if prev_is_seq:
        per = qb // WINDOW
        kp_spec = pl.BlockSpec((1, WINDOW, KV_WIDTH_B), lambda b, i: (b, jnp.maximum(i * per - 1, 0), kcol))
        vp_spec = pl.BlockSpec((1, WINDOW, KV_WIDTH_B), lambda b, i: (b, jnp.maximum(i * per - 1, 0), vcol))
    else:
        kp_spec = vp_spec = pl.BlockSpec((1, WINDOW, KV_WIDTH_B), lambda b, i: (b, 0, 0))
    return pl.pallas_call(
        functools.partial(_swa_kernel, mask_start=prev_is_seq),
        out_shape=jax.ShapeDtypeStruct((B, T, WIDTH_B), jnp.float32),
        grid=(B, T // qb),
        in_specs=[pl.BlockSpec((1, qb, WIDTH_B), lambda b, i: (b, i, 0)),
                  pl.BlockSpec((1, qb, KV_WIDTH_B), lambda b, i: (b, i, kcol)),
                  pl.BlockSpec((1, qb, KV_WIDTH_B), lambda b, i: (b, i, vcol)),
                  kp_spec, vp_spec,
                  pl.BlockSpec((N_KV_B, GROUP_B * CHUNK, SWA_BAND), lambda b, i: (0, 0, 0)),
                  pl.BlockSpec((N_HEADS_B * CHUNK, 128), lambda b, i: (0, 0))],
        out_specs=pl.BlockSpec((1, qb, WIDTH_B), lambda b, i: (b, i, 0)),
        compiler_params=pltpu.CompilerParams(
            dimension_semantics=("arbitrary", "arbitrary"), vmem_limit_bytes=VMEM_LIMIT),
        name="swa_attention",
    )(seg_b, seg_b, seg_b, prev_k, prev_v, bias, sink_tab)


def _layer_norm_rows(h, g, b, eps=1e-5):
    mu = jnp.mean(h, axis=-1, keepdims=True)
    d = h - mu
    var = jnp.mean(d * d, axis=-1, keepdims=True)
    return d * lax.rsqrt(var + eps) * g + b


def _const_spec(shape):
    return pl.BlockSpec(shape, lambda *_: (0,) * len(shape), pipeline_mode=pl.Buffered(1))


def _merge_kernel(x_ref, y_ref, bonus_ref, g_ref, ob_ref, gates_ref, lnx_ref, ln1_ref, ones_ref,
                  pa_ref, pb_ref, wout_ref, o_ref):
    ones2 = ones_ref[...]
    y = y_ref[...]
    mean = _head_sums(y, ones2) * (1.0 / HEAD_A)
    d = y - mean
    var = _head_sums(d * d, ones2) * (1.0 / HEAD_A)
    yn = d * lax.rsqrt(var + GN_EPS) * lnx_ref[0:1, :] + lnx_ref[1:2, :]
    o_a = ((yn + bonus_ref[...]) * g_ref[...]).astype(jnp.bfloat16)
    br_a = jnp.dot(o_a, pa_ref[...], preferred_element_type=jnp.float32)
    br_b = jnp.dot(ob_ref[...].astype(jnp.bfloat16), pb_ref[...], preferred_element_type=jnp.float32)
    gates = jax.nn.sigmoid(gates_ref[...])
    merged = gates[:, :D_MODEL] * br_a + gates[:, D_MODEL:] * br_b
    h = ALPHA * x_ref[...] + jnp.dot(merged.astype(jnp.bfloat16), wout_ref[...], preferred_element_type=jnp.float32)
    o_ref[...] = _layer_norm_rows(h, ln1_ref[0:1, :], ln1_ref[1:2, :])


def branch_merge(x, y, bonus, g, o_b, gates, p, tm=256):
    n = x.shape[0]
    tm = _pick(n, (tm, 128, 64))
    rows = lambda w: pl.BlockSpec((tm, w), lambda i: (i, 0))
    return pl.pallas_call(
        _merge_kernel,
        out_shape=jax.ShapeDtypeStruct((n, D_MODEL), jnp.float32),
        grid=(n // tm,),
        in_specs=[rows(D_MODEL), rows(WIDTH_A), rows(WIDTH_A), rows(WIDTH_A), rows(WIDTH_B), rows(C_GATE),
                  _const_spec((2, WIDTH_A)), _const_spec((2, D_MODEL)), _const_spec((2 * PAIR_W, PAIR_W)),
                  _const_spec((WIDTH_A, D_MODEL)), _const_spec((WIDTH_B, D_MODEL)), _const_spec((D_MODEL, D_MODEL))],
        out_specs=rows(D_MODEL),
        compiler_params=pltpu.CompilerParams(dimension_semantics=("arbitrary",), vmem_limit_bytes=VMEM_LIMIT),
        name="branch_merge",
    )(x, y, bonus, g, o_b, gates, jnp.stack([p['lnx_g'], p['lnx_b']]), jnp.stack([p['ln1_g'], p['ln1_b']]),
      _block_ones2(), p['w_branch_a16'], p['w_branch_b16'], p['w_out16'])


def _mem_kernel(x_ref, mk_ref, mv_ref, ln2_ref, wq_ref, wo_ref, pwq_ref, o_ref, q_ref):
    x = x_ref[0]
    qm = jnp.dot(x.astype(jnp.bfloat16), wq_ref[...], preferred_element_type=jnp.float32).astype(jnp.bfloat16)
    mk = mk_ref[0].astype(jnp.bfloat16)
    mv = mv_ref[0].astype(jnp.bfloat16)
    outs = []
    for h in range(N_HEADS_M):
        cols = slice(h * HEAD_M, (h + 1) * HEAD_M)
        s = lax.dot_general(qm[:, cols], mk[:, cols], (((1,), (1,)), ((), ())),
                            preferred_element_type=jnp.float32) * (HEAD_M ** -0.5)
        e = jnp.exp(s - jnp.max(s, axis=-1, keepdims=True))
        pr = e / jnp.sum(e, axis=-1, keepdims=True)
        outs.append(jnp.dot(pr.astype(jnp.bfloat16), mv[:, cols], preferred_element_type=jnp.float32))
    o = jnp.concatenate(outs, axis=1).astype(jnp.bfloat16)
    h2 = ALPHA * x + jnp.dot(o, wo_ref[...], preferred_element_type=jnp.float32)
    x2 = _layer_norm_rows(h2, ln2_ref[0:1, :], ln2_ref[1:2, :])
    o_ref[0] = x2
    q_ref[0] = jnp.dot(x2.astype(jnp.bfloat16), pwq_ref[...], preferred_element_type=jnp.float32)


def mem_block(x, mk, mv, p, tm=256):
    B, T, _ = x.shape
    tm = _pick(T, (tm, 128, 64))
    wm = N_HEADS_M * HEAD_M
    rows = lambda w: pl.BlockSpec((1, tm, w), lambda b, i: (b, i, 0))
    mem = pl.BlockSpec((1, N_MEM, wm), lambda b, i: (b, 0, 0))
    out = jax.ShapeDtypeStruct((B, T, D_MODEL), jnp.float32)
    return pl.pallas_call(
        _mem_kernel,
        out_shape=(out, jax.ShapeDtypeStruct((B, T, PEER_HEADS * D_KEY), jnp.float32)),
        grid=(B, T // tm),
        in_specs=[rows(D_MODEL), mem, mem, _const_spec((2, D_MODEL)), _const_spec((D_MODEL, wm)),
                  _const_spec((wm, D_MODEL)), _const_spec((D_MODEL, PEER_HEADS * D_KEY))],
        out_specs=(rows(D_MODEL), rows(PEER_HEADS * D_KEY)),
        compiler_params=pltpu.CompilerParams(
            dimension_semantics=("arbitrary", "arbitrary"), vmem_limit_bytes=VMEM_LIMIT),
        name="mem_block",
    )(x, mk, mv, jnp.stack([p['ln2_g'], p['ln2_b']]), p['wq_mem16'], p['wo_mem16'], p['peer_wq16'])


ROUTE_TQ = 256


def _top_values(s, k):
    n_rows = s.shape[0]
    iota = lax.broadcasted_iota(jnp.int32, s.shape, 0).astype(jnp.float32)
    out = []
    for _ in range(k):
        m = jnp.max(s, axis=0, keepdims=True)
        first = jnp.min(jnp.where(s == m, iota, float(n_rows)), axis=0, keepdims=True)
        s = jnp.where(iota == first, -jnp.inf, s)
        out.append(m)
    return out


def _peer_route_kernel(q_ref, keys_ref, tau_ref, c1_ref, s2_ref, e2_ref):
    half = D_KEY // 2
    for h in range(PEER_HEADS):
        tops, scores = [], []
        for p in range(2):
            c0 = (2 * h + p) * half
            qs = q_ref[:, c0:c0 + half].astype(jnp.bfloat16)
            s = lax.dot_general(keys_ref[2 * h + p], qs, (((1,), (1,)), ((), ())),
                                preferred_element_type=jnp.float32)
            scores.append(s)
            tops.append(_top_values(s, TOPK))
        t1, t2 = tops
        t2all = jnp.concatenate(t2, axis=0)
        cand = jnp.concatenate([t1[a] + t2all for a in range(TOPK)], axis=0)
        sc = _top_values(cand, TOPK)
        z = jnp.zeros_like(sc[0])
        for kq in range(TOPK):
            z = z + jnp.exp(sc[kq] - sc[0])
        theta = sc[TOPK - 1]
        tau = jnp.full_like(scores[0], jnp.inf)
        for a in range(TOPK):
            tau_a = jnp.min(jnp.where(t1[a] + t2all >= theta, t2all, jnp.inf), axis=0, keepdims=True)
            tau = jnp.where(scores[0] == t1[a], tau_a, tau)
        tau_ref[h] = tau
        s2_ref[h] = scores[1]
        c1_ref[h] = jnp.exp(scores[0] - t1[0]) / z
        e2_ref[h] = jnp.exp(scores[1] - t2[0])


def peer_route(q, keys16):
    n = q.shape[0]
    tq = _pick(n, (ROUTE_TQ, 128))
    big = jax.ShapeDtypeStruct((PEER_HEADS, N_KEYS, n), jnp.float32)
    bspec = pl.BlockSpec((PEER_HEADS, N_KEYS, tq), lambda i: (0, 0, i))
    return pl.pallas_call(
        _peer_route_kernel,
        out_shape=(big, big, big, big),
        grid=(n // tq,),
        in_specs=[pl.BlockSpec((tq, PEER_HEADS * D_KEY), lambda i: (i, 0)),
                  pl.BlockSpec((2 * PEER_HEADS, N_KEYS, D_KEY // 2), lambda i: (0, 0, 0))],
        out_specs=(bspec, bspec, bspec, bspec),
        compiler_params=pltpu.CompilerParams(
            dimension_semantics=("arbitrary",), vmem_limit_bytes=VMEM_LIMIT),
        name="peer_route",
    )(q, keys16)


PEER_TM = 512
PEER_ROWS = 8
PEER_TE = PEER_ROWS * N_KEYS


def _gelu(x):
    return 0.5 * x * (1.0 + lax.erf(x * (2.0 ** -0.5)))


def _peer_mix_kernel(x_ref, u_ref, vt_ref, tau_ref, c1_ref, s2_ref, e2_ref, ln3_ref, o_ref,
                     x16_ref, h_ref, acc_ref):
    j = pl.program_id(1)

    @pl.when(j == 0)
    def _():
        x16_ref[...] = x_ref[...].astype(jnp.bfloat16)
        acc_ref[...] = jnp.zeros_like(acc_ref)

    a_t = lax.dot_general(u_ref[...], x16_ref[...], (((1,), (1,)), ((), ())),
                          preferred_element_type=jnp.float32)
    for r in range(PEER_ROWS):
        gate = None
        for h in range(PEER_HEADS):
            w = jnp.where(s2_ref[h] >= tau_ref[h, r:r + 1, :], e2_ref[h], 0.0) * c1_ref[h, r:r + 1, :]
            gate = w if gate is None else gate + w
        rows = a_t[r * N_KEYS:(r + 1) * N_KEYS]
        h_ref[r * N_KEYS:(r + 1) * N_KEYS, :] = (gate * _gelu(rows)).astype(jnp.bfloat16)
    acc_ref[...] += jnp.dot(vt_ref[...], h_ref[...], preferred_element_type=jnp.float32)

    @pl.when(j == pl.num_programs(1) - 1)
    def _():
        o_ref[...] = _layer_norm_rows(ALPHA * x_ref[...] + acc_ref[...].T, ln3_ref[0:1, :], ln3_ref[1:2, :])


def peer_mix(x, u16, vt16, tau, c1, s2, e2, ln3):
    n, d = x.shape
    tm = _pick(n, (PEER_TM, 256, 128))
    n_exp = u16.shape[0]
    row_spec = pl.BlockSpec((PEER_HEADS, PEER_ROWS, tm), lambda i, j: (0, j, i))
    all_spec = pl.BlockSpec((PEER_HEADS, N_KEYS, tm), lambda i, j: (0, 0, i), pipeline_mode=pl.Buffered(1))
    return pl.pallas_call(
        _peer_mix_kernel,
        out_shape=jax.ShapeDtypeStruct((n, d), jnp.float32),
        grid=(n // tm, n_exp // PEER_TE),
        in_specs=[pl.BlockSpec((tm, d), lambda i, j: (i, 0), pipeline_mode=pl.Buffered(1)),
                  pl.BlockSpec((PEER_TE, d), lambda i, j: (j, 0)),
                  pl.BlockSpec((d, PEER_TE), lambda i, j: (0, j)),
                  row_spec, row_spec, all_spec, all_spec, _const_spec((2, d))],
        out_specs=pl.BlockSpec((tm, d), lambda i, j: (i, 0)),
        scratch_shapes=[pltpu.VMEM((tm, d), jnp.bfloat16), pltpu.VMEM((PEER_TE, tm), jnp.bfloat16),
                        pltpu.VMEM((d, tm), jnp.float32)],
        compiler_params=pltpu.CompilerParams(
            dimension_semantics=("arbitrary", "arbitrary"), vmem_limit_bytes=VMEM_LIMIT),
        name="peer_mix",
    )(x, u16, vt16, tau, c1, s2, e2, ln3)


def peer_block(x, q, p):
    stats = peer_route(q, p['peer_keys16'])
    return peer_mix(x, p['peer_u16'], p['peer_vt16'], *stats, jnp.stack([p['ln3_g'], p['ln3_b']]))


def layer_norm(x, g, b, eps=1e-5):
    xf = x.astype(jnp.float32)
    mu = jnp.mean(xf, -1, keepdims=True)
    var = jnp.mean(jnp.square(xf - mu), -1, keepdims=True)
    return ((xf - mu) * lax.rsqrt(var + eps) * g + b).astype(x.dtype)


def rwkv7_scan(r, w, k, v, kk, a, state0):
    def step(S, inp):
        r_t, w_t, k_t, v_t, kk_t, a_t = inp
        sa = jnp.einsum('bhij,bhj->bhi', S, -kk_t)
        S = (S * w_t[:, :, None, :] + sa[..., None] * (kk_t * a_t)[:, :, None, :]
             + v_t[..., None] * k_t[:, :, None, :])
        y = jnp.einsum('bhij,bhj->bhi', S, r_t)
        return S, y
    xs = tuple(jnp.moveaxis(t, 1, 0) for t in (r, w, k, v, kk, a))
    S, ys = lax.scan(step, state0.astype(jnp.float32), xs)
    return jnp.moveaxis(ys, 0, 1), S


def rwkv7_branch(seg, prev_row, state0, p):
    B, T, _ = seg.shape
    f32 = jnp.float32
    shifted = jnp.concatenate([prev_row.astype(seg.dtype), seg[:, :-1]], axis=1)
    xm = seg + p['shift_mu'] * (shifted - seg)
    r, k, v, wl, al, gl = jnp.split(xm, SPLIT_A, axis=-1)
    w_log = -jax.nn.softplus(-(p['w0'] + jnp.tanh(wl) @ p['w_lora_up']).astype(f32)) - 0.5
    decay = jnp.exp(-jnp.exp(w_log))
    a = jax.nn.sigmoid((p['a0'] + al @ p['a_lora_up']).astype(f32))
    g = jax.nn.sigmoid(gl) @ p['g_lora_up']
    heads = lambda t: t.reshape(B, T, N_HEADS_A, HEAD_A)
    kk = heads((k * p['k_k']).astype(f32))
    kk = kk * lax.rsqrt(jnp.maximum(jnp.sum(kk * kk, -1, keepdims=True), 1e-24))
    kf = k.astype(f32) * (1.0 + (a - 1.0) * p['k_a'].astype(f32))
    rh, kh, vh, ah, wh = heads(r.astype(f32)), heads(kf), heads(v.astype(f32)), heads(a), heads(decay)
    flat = lambda t: t.reshape(B, T, WIDTH_A)
    y, s_final = rwkv7_scan_pallas(flat(rh), flat(wh), flat(kh), flat(vh), flat(kk), flat(ah), state0)
    y = heads(y)
    mean = jnp.mean(y, -1, keepdims=True)
    var = jnp.mean(jnp.square(y - mean), -1, keepdims=True)
    yn = ((y - mean) * lax.rsqrt(var + GN_EPS)).reshape(B, T, WIDTH_A) * p['lnx_g'] + p['lnx_b']
    bonus = jnp.sum(rh * kh * p['r_k'].astype(f32), -1, keepdims=True) * vh
    out = (yn + bonus.reshape(B, T, WIDTH_A)) * g
    return out.astype(seg.dtype), s_final, seg[:, -1:]


def alibi_bias(q_pos, k_pos):
    slopes = 2.0 ** (-8.0 * jnp.arange(1, N_HEADS_B + 1, dtype=jnp.float32) / N_HEADS_B)
    dist = jnp.abs(q_pos[:, None] - k_pos[None, :]).astype(jnp.float32)
    return -slopes.reshape(N_KV_B, GROUP_B, 1, 1) * dist


def sink_attention(q, k, v, bias, mask, sinks):
    s = jnp.einsum('bnqhgd,bnjhd->bnhgqj', q, k).astype(jnp.float32) * (HEAD_B ** -0.5) + bias
    if mask is not None:
        s = jnp.where(mask, s, NEG_INF)
    sink = sinks.astype(jnp.float32).reshape(N_KV_B, GROUP_B, 1, 1)
    m = jnp.maximum(jnp.max(s, -1, keepdims=True), sink)
    p = jnp.exp(s - m)
    p = p / (jnp.sum(p, -1, keepdims=True) + jnp.exp(sink - m))
    return jnp.einsum('bnhgqj,bnjhd->bnqhgd', p.astype(v.dtype), v)


def swa_prompt(q, k, v, sinks):
    B, S = q.shape[:2]
    NC = S // CHUNK
    band = (WIN_CHUNKS + 1) * CHUNK
    qc = q.reshape(B, NC, CHUNK, N_KV_B, GROUP_B, HEAD_B)
    pad = ((0, 0), (WIN_CHUNKS, 0), (0, 0), (0, 0), (0, 0))
    kp = jnp.pad(k.reshape(B, NC, CHUNK, N_KV_B, HEAD_B), pad)
    vp = jnp.pad(v.reshape(B, NC, CHUNK, N_KV_B, HEAD_B), pad)
    kb = jnp.concatenate([kp[:, o:o + NC] for o in range(WIN_CHUNKS + 1)], axis=2)
    vb = jnp.concatenate([vp[:, o:o + NC] for o in range(WIN_CHUNKS + 1)], axis=2)
    bias = alibi_bias(jnp.arange(CHUNK), jnp.arange(band) - WIN_CHUNKS * CHUNK)
    key_chunk = jnp.arange(NC)[:, None] - WIN_CHUNKS + (jnp.arange(band) // CHUNK)[None, :]
    mask = (key_chunk >= 0)[None, :, None, None, None, :]
    o = sink_attention(qc, kb, vb, bias, mask, sinks)
    return o.reshape(B, S, WIDTH_B)


def swa_sample(q, k, v, k_cache, v_cache, sinks):
    B, T = q.shape[:2]
    Lc = k_cache.shape[1]
    k_all = jnp.concatenate([k_cache.astype(k.dtype), k], axis=1)
    v_all = jnp.concatenate([v_cache.astype(v.dtype), v], axis=1)
    bias = alibi_bias(jnp.arange(T), jnp.arange(Lc + T) - Lc)
    o = sink_attention(q[:, None], k_all[:, None], v_all[:, None], bias, None, sinks)
    return o.reshape(B, T, WIDTH_B), k_all[:, -Lc:], v_all[:, -Lc:]


def mem_attention(x, mk, mv, p):
    B, T, _ = x.shape
    q = mm(x, p['wq_mem16']).reshape(B, T, N_HEADS_M, HEAD_M)
    s = jnp.einsum('bthd,bmhd->bhtm', q, mk.astype(q.dtype)).astype(jnp.float32) * (HEAD_M ** -0.5)
    pr = jax.nn.softmax(s, axis=-1).astype(x.dtype)
    o = jnp.einsum('bhtm,bmhd->bthd', pr, mv.astype(x.dtype)).reshape(B, T, N_HEADS_M * HEAD_M)
    return mm(o, p['wo_mem16'])


def peer_ffn(x, p):
    B, T, D = x.shape
    n = B * T
    nblk = -(-n // PEER_BLOCK)
    xt = x.reshape(nblk, PEER_BLOCK, D)
    qa = mm(x, p['peer_wq16']).reshape(nblk, PEER_BLOCK, PEER_HEADS * D_KEY)
    sub_keys, u_tab, v_tab = p['peer_sub_keys'], p['peer_u'], p['peer_v']

    def peer_block(args):
        xb, qb = args
        q = qb.reshape(PEER_BLOCK, PEER_HEADS, 2, D_KEY // 2)
        s = jnp.einsum('thpd,hpnd->thpn', q, sub_keys).astype(jnp.float32)
        s1, i1 = lax.top_k(s[:, :, 0], TOPK)
        s2, i2 = lax.top_k(s[:, :, 1], TOPK)
        cand = (s1[..., :, None] + s2[..., None, :]).reshape(PEER_BLOCK, PEER_HEADS, TOPK * TOPK)
        cidx = (i1[..., :, None] * N_KEYS + i2[..., None, :]).reshape(PEER_BLOCK, PEER_HEADS, TOPK * TOPK)
        sc, pos = lax.top_k(cand, TOPK)
        e = jnp.take_along_axis(cidx, pos, axis=-1)
        g = jax.nn.softmax(sc, axis=-1)
        act = jax.nn.gelu(jnp.einsum('td,thkd->thk', xb, u_tab[e]).astype(jnp.float32), approximate=False)
        return jnp.einsum('thk,thkd->td', (g * act).astype(xb.dtype), v_tab[e])

    y = lax.map(peer_block, (xt, qa)).reshape(nblk * PEER_BLOCK, D)
    return y.reshape(B, T, D)


def trunk_layer(x, p, mem_k, mem_v, rwkv_state, shift_row, swa_k_cache, swa_v_cache):
    B, T, _ = x.shape
    n = B * T
    x2d = x.reshape(n, D_MODEL)
    seg_a = matmul(x2d, p['w_in_a16']).reshape(B, T, C_RWKV)
    seg_b = matmul(x2d, p['w_in_b16']).reshape(B, T, C_SWA)
    gates = matmul(x2d, p['w_in_g16'])
    r, w, k, v, kk, a, g, bonus = rwkv7_pre(seg_a, shift_row.astype(jnp.float32), p)
    y, rwkv_new = rwkv7_chunked(r, w, k, v, kk, a, rwkv_state)
    shift_new = seg_a[:, -1:]
    k_new = seg_b[:, :, WIDTH_B:WIDTH_B + KV_WIDTH_B].reshape(B, T, N_KV_B, HEAD_B)
    v_new = seg_b[:, :, WIDTH_B + KV_WIDTH_B:].reshape(B, T, N_KV_B, HEAD_B)
    if swa_k_cache is None:
        o_b = swa_attention(seg_b, seg_b, seg_b, p['attn_sinks'], prev_is_seq=True)
        swa_k_new, swa_v_new = k_new[:, -WINDOW:], v_new[:, -WINDOW:]
    else:
        o_b = swa_attention(seg_b, swa_k_cache.reshape(B, WINDOW, KV_WIDTH_B),
                            swa_v_cache.reshape(B, WINDOW, KV_WIDTH_B), p['attn_sinks'], prev_is_seq=False)
        swa_k_new = jnp.concatenate([swa_k_cache, k_new], axis=1)[:, -WINDOW:]
        swa_v_new = jnp.concatenate([swa_v_cache, v_new], axis=1)[:, -WINDOW:]
    flat = lambda t: t.reshape(n, t.shape[-1])
    x1 = branch_merge(x2d, flat(y), flat(bonus), flat(g), flat(o_b), gates, p)
    wm = N_HEADS_M * HEAD_M
    x2, q = mem_block(x1.reshape(B, T, D_MODEL), mem_k.reshape(-1, N_MEM, wm), mem_v.reshape(-1, N_MEM, wm), p)
    x3 = peer_block(flat(x2), flat(q), p)
    return x3.reshape(B, T, D_MODEL), rwkv_new, shift_new, swa_k_new, swa_v_new


_MM_WEIGHTS = ('w_branch_a', 'w_branch_b', 'w_out', 'wq_mem', 'wk_mem', 'wv_mem', 'wo_mem', 'peer_wq')


def kernel(x_prompt, x_sample, state_rwkv, state_shift, cache_swa_k, cache_swa_v, cache_mem_k, cache_mem_v, mem_prompt, w_in, shift_mu, w0, w_lora_up, a0, a_lora_up, g_lora_up, k_k, k_a, r_k, lnx_g, lnx_b, attn_sinks, w_branch_a, w_branch_b, w_out, ln1_g, ln1_b, wq_mem, wk_mem, wv_mem, wo_mem, ln2_g, ln2_b, peer_wq, peer_sub_keys, peer_u, peer_v, ln3_g, ln3_b):
    params = {
        'w_in': w_in, 'shift_mu': shift_mu, 'w0': w0, 'w_lora_up': w_lora_up, 'a0': a0,
        'a_lora_up': a_lora_up, 'g_lora_up': g_lora_up, 'k_k': k_k, 'k_a': k_a, 'r_k': r_k,
        'lnx_g': lnx_g, 'lnx_b': lnx_b, 'attn_sinks': attn_sinks, 'w_branch_a': w_branch_a,
        'w_branch_b': w_branch_b, 'w_out': w_out, 'ln1_g': ln1_g, 'ln1_b': ln1_b, 'wq_mem': wq_mem,
        'wk_mem': wk_mem, 'wv_mem': wv_mem, 'wo_mem': wo_mem, 'ln2_g': ln2_g, 'ln2_b': ln2_b,
        'peer_wq': peer_wq, 'peer_sub_keys': peer_sub_keys, 'peer_u': peer_u, 'peer_v': peer_v,
        'ln3_g': ln3_g, 'ln3_b': ln3_b,
    }
    B = x_prompt.shape[0]
    rwkv0 = jnp.zeros((B, N_HEADS_A, HEAD_A, HEAD_A), jnp.float32)
    shift0 = jnp.zeros((B, 1, C_RWKV), x_prompt.dtype)
    xp, xs = x_prompt, x_sample
    p_rw, p_sh, p_k, p_v, p_mk, p_mv = [], [], [], [], [], []
    s_rw, s_sh, s_k, s_v = [], [], [], []
    for l in range(DEPTH):
        p = {name: arr[l] for name, arr in params.items()}
        for name in _MM_WEIGHTS:
            p[name + '16'] = p[name].astype(jnp.bfloat16)
        w_in16 = p['w_in'].astype(jnp.bfloat16)
        p['w_in_a16'] = w_in16[:, :C_RWKV]
        p['w_in_b16'] = w_in16[:, C_RWKV:C_RWKV + C_SWA]
        p['w_in_g16'] = w_in16[:, C_RWKV + C_SWA:]
        p['peer_keys16'] = p['peer_sub_keys'].reshape(2 * PEER_HEADS, N_KEYS, D_KEY // 2).astype(jnp.bfloat16)
        p['peer_u16'] = p['peer_u'].astype(jnp.bfloat16)
        p['peer_vt16'] = p['peer_v'].T.astype(jnp.bfloat16)
        mk = mm(mem_prompt, p['wk_mem16']).reshape(B, N_MEM, N_HEADS_M, HEAD_M)
        mv = mm(mem_prompt, p['wv_mem16']).reshape(B, N_MEM, N_HEADS_M, HEAD_M)
        xp, rw, sh, kn, vn = trunk_layer(xp, p, mk, mv, rwkv0, shift0, None, None)
        p_rw.append(rw); p_sh.append(sh); p_k.append(kn); p_v.append(vn); p_mk.append(mk); p_mv.append(mv)
        xs, rw, sh, kn, vn = trunk_layer(xs, p, cache_mem_k[l], cache_mem_v[l], state_rwkv[l], state_shift[l],
                                         cache_swa_k[l], cache_swa_v[l])
        s_rw.append(rw); s_sh.append(sh); s_k.append(kn); s_v.append(vn)
    return (xp, xs,
            jnp.stack(p_rw), jnp.stack(p_sh), jnp.stack(p_k), jnp.stack(p_v), jnp.stack(p_mk), jnp.stack(p_mv),
            jnp.stack(s_rw), jnp.stack(s_sh), jnp.stack(s_k), jnp.stack(s_v))
```

```python
import functools

import jax
import jax.numpy as jnp
from jax import lax
from jax.experimental import pallas as pl
from jax.experimental.pallas import tpu as pltpu

D_MODEL = 2048
DEPTH = 2
CHUNK = 64
HEAD_A = 64
N_HEADS_A = 16
WIDTH_A = N_HEADS_A * HEAD_A
LORA_W = 64
LORA_A = 64
LORA_G = 128
GN_EPS = 64e-5
SPLIT_A = [WIDTH_A, 2 * WIDTH_A, 3 * WIDTH_A, 3 * WIDTH_A + LORA_W, 3 * WIDTH_A + LORA_W + LORA_A]
C_RWKV = 3 * WIDTH_A + LORA_W + LORA_A + LORA_G
HEAD_B = 64
N_HEADS_B = 16
N_KV_B = 4
GROUP_B = N_HEADS_B // N_KV_B
WIDTH_B = N_HEADS_B * HEAD_B
KV_WIDTH_B = N_KV_B * HEAD_B
WINDOW = 128
WIN_CHUNKS = WINDOW // CHUNK
C_SWA = WIDTH_B + 2 * KV_WIDTH_B
C_GATE = 2 * D_MODEL
C_IN = C_RWKV + C_SWA + C_GATE
N_MEM = 256
N_HEADS_M = 4
HEAD_M = 128
N_KEYS = 128
PEER_HEADS = 8
D_KEY = 256
TOPK = 16
PEER_BLOCK = 128
ALPHA = (2.0 * DEPTH) ** 0.25
NEG_INF = -1e30

VMEM_LIMIT = 56 * 1024 * 1024


def _matmul_kernel(a_ref, b_ref, o_ref, a16_ref):
    @pl.when(pl.program_id(1) == 0)
    def _():
        a16_ref[...] = a_ref[...].astype(jnp.bfloat16)

    o_ref[...] = jnp.dot(a16_ref[...], b_ref[...], preferred_element_type=jnp.float32)


def _pick(n, cands):
    for c in cands:
        if n % c == 0:
            return c
    return n


def matmul(a, b16):
    m, k = a.shape
    n = b16.shape[1]
    tm = _pick(m, (512, 256, 128))
    tn = _pick(n, (2048, 1664, 1536, 1024, 512, 256, 128))
    return pl.pallas_call(
        _matmul_kernel,
        out_shape=jax.ShapeDtypeStruct((m, n), jnp.float32),
        grid=(m // tm, n // tn),
        in_specs=[pl.BlockSpec((tm, k), lambda i, j: (i, 0)),
                  pl.BlockSpec((k, tn), lambda i, j: (0, j))],
        out_specs=pl.BlockSpec((tm, tn), lambda i, j: (i, j)),
        scratch_shapes=[pltpu.VMEM((tm, k), jnp.bfloat16)],
        compiler_params=pltpu.CompilerParams(
            dimension_semantics=("arbitrary", "arbitrary"),
            vmem_limit_bytes=VMEM_LIMIT),
        name="matmul",
    )(a, b16)


def mm(x, w16):
    lead = x.shape[:-1]
    return matmul(x.reshape(-1, x.shape[-1]), w16).reshape(*lead, w16.shape[1])


N_PAIR = N_HEADS_A // 2
PAIR_W = 2 * HEAD_A


def _split2(x):
    hi = x.astype(jnp.bfloat16)
    lo = (x - hi.astype(jnp.float32)).astype(jnp.bfloat16)
    return jnp.concatenate([hi, lo], axis=1)


def _rwkv_scan_kernel(r_ref, w_ref, k_ref, v_ref, kk_ref, a_ref, s0_ref, ones_ref, eye_ref,
                      y_ref, sT_ref, s_ref, *, tb):
    tblk = pl.program_id(1)

    @pl.when(tblk == 0)
    def _():
        s_ref[...] = s0_ref[0]

    ones2 = ones_ref[...]
    eye2 = eye_ref[...]

    def segsum(tiles):
        lhs = jnp.concatenate([_split2(m) for m in tiles], axis=0)
        return jnp.dot(lhs, ones2, preferred_element_type=jnp.float32)

    def step(g, carry):
        rows = pl.ds(pl.multiple_of(g * 8, 8), 8)
        pairs = range(N_PAIR)
        sls = [pl.ds(p * PAIR_W, PAIR_W) for p in pairs]
        kk8 = [kk_ref[0, rows, sl] for sl in sls]
        kka8 = [kk8[p] * a_ref[0, rows, sls[p]] for p in pairs]
        w8 = [w_ref[0, rows, sl] for sl in sls]
        k8 = [k_ref[0, rows, sl] for sl in sls]
        v8 = [v_ref[0, rows, sl] for sl in sls]
        r8 = [r_ref[0, rows, sl] for sl in sls]
        s = [s_ref[p] for p in pairs]
        ys = [[] for _ in pairs]
        tile = lambda x, p: x[p * HEAD_A:(p + 1) * HEAD_A]
        for j in range(8):
            row = slice(j, j + 1)
            sa = segsum([s[p] * (-kk8[p][row]) for p in pairs])
            vb = segsum([eye2 * v8[p][row] for p in pairs])
            s = [s[p] * w8[p][row] + tile(sa, p) * kka8[p][row] + tile(vb, p) * k8[p][row] for p in pairs]
            yb = segsum([s[p] * r8[p][row] for p in pairs])
            for p in pairs:
                ys[p].append(jnp.sum(tile(yb, p) * eye2, axis=0, keepdims=True))
        for p in pairs:
            s_ref[p] = s[p]
            y_ref[0, rows, sls[p]] = jnp.concatenate(ys[p], axis=0)
        return carry

    lax.fori_loop(0, tb // 8, step, 0)

    @pl.when(tblk == pl.num_programs(1) - 1)
    def _():
        sT_ref[0] = s_ref[...]


def rwkv7_scan_pallas(r, w, k, v, kk, a, state0, tb=256):
    B, T, _ = r.shape
    tb = min(tb, T)
    s0 = state0.astype(jnp.float32).reshape(B, N_PAIR, 2, HEAD_A, HEAD_A)
    s0 = s0.transpose(0, 1, 3, 2, 4).reshape(B, N_PAIR, HEAD_A, PAIR_W)
    lane_head = jnp.arange(PAIR_W) // HEAD_A
    ones2 = (jnp.tile(lane_head, 2)[:, None] == lane_head[None, :]).astype(jnp.bfloat16)
    eye2 = (jnp.arange(HEAD_A)[:, None] == (jnp.arange(PAIR_W) % HEAD_A)[None, :]).astype(jnp.float32)
    seq = pl.BlockSpec((1, tb, WIDTH_A), lambda b, t: (b, t, 0))
    st = pl.BlockSpec((1, N_PAIR, HEAD_A, PAIR_W), lambda b, t: (b, 0, 0, 0))
    y, sT = pl.pallas_call(
        functools.partial(_rwkv_scan_kernel, tb=tb),
        out_shape=(jax.ShapeDtypeStruct((B, T, WIDTH_A), jnp.float32),
                   jax.ShapeDtypeStruct((B, N_PAIR, HEAD_A, PAIR_W), jnp.float32)),
        grid=(B, T // tb),
        in_specs=[seq] * 6 + [st,
                              pl.BlockSpec((2 * PAIR_W, PAIR_W), lambda b, t: (0, 0)),
                              pl.BlockSpec((HEAD_A, PAIR_W), lambda b, t: (0, 0))],
        out_specs=(seq, st),
        scratch_shapes=[pltpu.VMEM((N_PAIR, HEAD_A, PAIR_W), jnp.float32)],
        compiler_params=pltpu.CompilerParams(
            dimension_semantics=("arbitrary", "arbitrary"),
            vmem_limit_bytes=VMEM_LIMIT),
        name="rwkv7_scan",
    )(r, w, k, v, kk, a, s0, ones2, eye2)
    sT = sT.reshape(B, N_PAIR, HEAD_A, 2, HEAD_A).transpose(0, 1, 3, 2, 4)
    return y, sT.reshape(B, N_HEADS_A, HEAD_A, HEAD_A)


def _hl(x):
    hi = x.astype(jnp.bfloat16)
    return hi, (x - hi.astype(jnp.float32)).astype(jnp.bfloat16)


def _dot3(a, b):
    ah, al = _hl(a)
    bh, bl = _hl(b)
    return jnp.dot(jnp.concatenate([ah, ah, al], axis=1), jnp.concatenate([bh, bl, bh], axis=0),
                   preferred_element_type=jnp.float32)


def _dot3_nt(a, b):
    ah, al = _hl(a)
    bh, bl = _hl(b)
    return lax.dot_general(jnp.concatenate([ah, ah, al], axis=1), jnp.concatenate([bh, bl, bh], axis=1),
                           (((1,), (1,)), ((), ())), preferred_element_type=jnp.float32)


def _rwkv_chunk_kernel(r_ref, w_ref, k_ref, v_ref, kk_ref, a_ref, s0_ref, ltri_ref, ones_ref, msk_ref,
                       y_ref, sT_ref, p_ref, *, n_chunks):
    tblk = pl.program_id(1)

    @pl.when(tblk == 0)
    def _():
        p_ref[...] = s0_ref[0]

    eye2, m0, m1, strict, incl = (msk_ref[i] for i in range(5))
    bd = lambda y: jnp.concatenate([y * m0, y * m1], axis=0)

    def split3(x, axis):
        t1 = x.astype(jnp.bfloat16)
        d = x - t1.astype(jnp.float32)
        t2 = d.astype(jnp.bfloat16)
        t3 = (d - t2.astype(jnp.float32)).astype(jnp.bfloat16)
        return jnp.concatenate([t1, t2, t3], axis=axis)

    def chunk(c, carry):
        rows = pl.ds(pl.multiple_of(c * CHUNK, CHUNK), CHUNK)
        pairs = range(N_PAIR)
        sls = [pl.ds(p * PAIR_W, PAIR_W) for p in pairs]
        lw = [w_ref[0, rows, sl] for sl in sls]
        cum = [jnp.dot(ltri_ref[...], split3(lw[p], 0), preferred_element_type=jnp.float32) for p in pairs]
        g = [jnp.exp(cum[p]) for p in pairs]
        ginv = [jnp.exp(-cum[p]) for p in pairs]
        kk = [kk_ref[0, rows, sl] for sl in sls]
        kh = [kk[p] * jnp.exp(cum[p] - lw[p]) for p in pairs]
        bh = [kk[p] * a_ref[0, rows, sls[p]] * ginv[p] for p in pairs]
        kf = [k_ref[0, rows, sls[p]] * ginv[p] for p in pairs]
        rh = [r_ref[0, rows, sls[p]] * g[p] for p in pairs]
        v = [v_ref[0, rows, sl] for sl in sls]
        g_last = [g[p][CHUNK - 1:CHUNK, :] for p in pairs]
        kr = [jnp.concatenate([kh[p], rh[p]], axis=0) for p in pairs]
        gram = [_dot3_nt(kr[p], jnp.concatenate([bd(bh[p]), bd(kf[p])], axis=0)) for p in pairs]
        a_b = [gram[p][:CHUNK, :PAIR_W] * strict for p in pairs]
        a_k = [gram[p][:CHUNK, PAIR_W:] * strict for p in pairs]
        a_r = [jnp.concatenate([gram[p][CHUNK:, :PAIR_W] * incl, gram[p][CHUNK:, PAIR_W:] * incl], axis=1)
               for p in pairs]
        t_inv = [eye2 - a_b[p] * msk_ref[5] for p in pairs]
        for lvl in range(1, 6):
            half = [_dot3(t_inv[p], bd(a_b[p] * msk_ref[5 + lvl])) for p in pairs]
            t_inv = [t_inv[p] - _dot3(half[p], bd(t_inv[p])) for p in pairs]
        akv = [_dot3(a_k[p], bd(v[p])) for p in pairs]
        g_col = [jnp.dot(split3(eye2 * g_last[p], 1), ones_ref[...], preferred_element_type=jnp.float32)
                 for p in pairs]
        xt = [jnp.concatenate([bh[p] * g_last[p], kf[p] * g_last[p]], axis=0).T for p in pairs]
        p0 = [p_ref[p] for p in pairs]
        zy = [_dot3(kr[p], bd(p0[p])) for p in pairs]
        u = [-_dot3(t_inv[p], bd(zy[p][:CHUNK] + akv[p])) for p in pairs]
        y = [zy[p][CHUNK:] + _dot3(a_r[p], jnp.concatenate([bd(u[p]), bd(v[p])], axis=0)) for p in pairs]
        delta = [_dot3(xt[p], jnp.concatenate([u[p], v[p]], axis=0)) for p in pairs]
        for p in pairs:
            p_ref[p] = g_col[p] * p0[p] + delta[p][:CHUNK] * m0 + delta[p][CHUNK:] * m1
            y_ref[0, rows, sls[p]] = y[p]
        return carry

    lax.fori_loop(0, n_chunks, chunk, 0)

    @pl.when(tblk == pl.num_programs(1) - 1)
    def _():
        sT_ref[0] = p_ref[...]


def rwkv7_chunked(r, logw, k, v, kk, a, state0, tb=256):
    B, T, _ = r.shape
    tb = min(tb, T)
    s0 = state0.astype(jnp.float32).reshape(B, N_PAIR, 2, HEAD_A, HEAD_A)
    s0 = s0.transpose(0, 1, 4, 2, 3).reshape(B, N_PAIR, HEAD_A, PAIR_W)
    t_i = jnp.arange(CHUNK)[:, None]
    s_i = (jnp.arange(PAIR_W) % HEAD_A)[None, :]
    lane_head = (jnp.arange(PAIR_W) // HEAD_A)[None, :]
    ones_row = jnp.ones((CHUNK, 1), jnp.int32)
    masks = [t_i == s_i, (lane_head == 0) * ones_row, (lane_head == 1) * ones_row, s_i < t_i, s_i <= t_i]
    for m in (1, 2, 4, 8, 16, 32):
        masks.append((t_i // (2 * m) == s_i // (2 * m)) & (t_i % (2 * m) >= m) & (s_i % (2 * m) < m))
    masks = jnp.stack([mk.astype(jnp.float32) for mk in masks])
    ltri = jnp.tile((jnp.arange(CHUNK)[None, :] <= jnp.arange(CHUNK)[:, None]), (1, 3)).astype(jnp.bfloat16)
    ones3 = jnp.tile((lane_head.T == lane_head), (3, 1)).astype(jnp.bfloat16)
    seq = pl.BlockSpec((1, tb, WIDTH_A), lambda b, t: (b, t, 0))
    st = pl.BlockSpec((1, N_PAIR, HEAD_A, PAIR_W), lambda b, t: (b, 0, 0, 0))
    y, sT = pl.pallas_call(
        functools.partial(_rwkv_chunk_kernel, n_chunks=tb // CHUNK),
        out_shape=(jax.ShapeDtypeStruct((B, T, WIDTH_A), jnp.float32),
                   jax.ShapeDtypeStruct((B, N_PAIR, HEAD_A, PAIR_W), jnp.float32)),
        grid=(B, T // tb),
        in_specs=[seq] * 6 + [st, _const_spec((CHUNK, 3 * CHUNK)), _const_spec((3 * PAIR_W, PAIR_W)),
                              _const_spec((11, CHUNK, PAIR_W))],
        out_specs=(seq, st),
        scratch_shapes=[pltpu.VMEM((N_PAIR, HEAD_A, PAIR_W), jnp.float32)],
        compiler_params=pltpu.CompilerParams(
            dimension_semantics=("arbitrary", "arbitrary"), vmem_limit_bytes=VMEM_LIMIT),
        name="rwkv7_chunked",
    )(r, logw, k, v, kk, a, s0, ltri, ones3, masks)
    sT = sT.reshape(B, N_PAIR, HEAD_A, 2, HEAD_A).transpose(0, 1, 3, 4, 2)
    return y, sT.reshape(B, N_HEADS_A, HEAD_A, HEAD_A)


def _head_sums(x, ones2):
    tiles = [jnp.dot(_split2(x[:, c:c + PAIR_W]), ones2, preferred_element_type=jnp.float32)
             for c in range(0, WIDTH_A, PAIR_W)]
    return jnp.concatenate(tiles, axis=1)


def _softplus(z):
    return jnp.maximum(z, 0.0) + jnp.log1p(jnp.exp(-jnp.abs(z)))


def _rwkv_pre_kernel(seg_ref, prev_ref, shift_ref, mu_ref, vec_ref, wa_ref, gup_ref, ones_ref,
                     r_ref, w_ref, k_ref, v_ref, kk_ref, a_ref, g_ref, bonus_ref):
    seg = seg_ref[0]
    tb = seg.shape[0]
    before = jnp.where(pl.program_id(1) == 0, shift_ref[0], prev_ref[0, 7:8, :])
    row = lax.broadcasted_iota(jnp.int32, seg.shape, 0)
    shifted = jnp.where(row == 0, before, pltpu.roll(seg, 1, axis=0))
    xm = seg + mu_ref[...] * (shifted - seg)
    r = xm[:, :WIDTH_A]
    k = xm[:, WIDTH_A:2 * WIDTH_A]
    v = xm[:, 2 * WIDTH_A:3 * WIDTH_A]
    wa = xm[:, 3 * WIDTH_A:3 * WIDTH_A + LORA_W + LORA_A]
    gl = xm[:, 3 * WIDTH_A + LORA_W + LORA_A:]
    lane = lax.broadcasted_iota(jnp.int32, wa.shape, 1)
    wa = jnp.where(lane < LORA_W, jnp.tanh(wa), wa).astype(jnp.bfloat16)
    lora = jnp.dot(wa, wa_ref[...], preferred_element_type=jnp.float32)
    w0, a0, k_k, k_a, r_k = (vec_ref[i:i + 1, :] for i in range(5))
    w_log = -_softplus(-(w0 + lora[:, :WIDTH_A])) - 0.5
    log_decay = -jnp.exp(w_log)
    a = jax.nn.sigmoid(a0 + lora[:, WIDTH_A:])
    g = jnp.dot(jax.nn.sigmoid(gl).astype(jnp.bfloat16), gup_ref[...], preferred_element_type=jnp.float32)
    ones2 = ones_ref[...]
    kk = k * k_k
    kk = kk * lax.rsqrt(jnp.maximum(_head_sums(kk * kk, ones2), 1e-24))
    kf = k * (1.0 + (a - 1.0) * k_a)
    r_ref[0], w_ref[0], k_ref[0], v_ref[0], kk_ref[0], a_ref[0], g_ref[0] = r, log_decay, kf, v, kk, a, g
    bonus_ref[0] = _head_sums(r * kf * r_k, ones2) * v


def _block_ones2():
    lane_head = jnp.arange(PAIR_W) // HEAD_A
    return (jnp.tile(lane_head, 2)[:, None] == lane_head[None, :]).astype(jnp.bfloat16)


def rwkv7_pre(seg, prev_row, p, tb=256):
    B, T, _ = seg.shape
    tb = min(tb, T)
    zeros = jnp.zeros((LORA_W, WIDTH_A), jnp.float32)
    wa_up = jnp.concatenate([jnp.concatenate([p['w_lora_up'], zeros], 1),
                             jnp.concatenate([zeros, p['a_lora_up']], 1)], 0).astype(jnp.bfloat16)
    vecs = jnp.stack([p['w0'], p['a0'], p['k_k'], p['k_a'], p['r_k'].reshape(WIDTH_A)])
    out = jax.ShapeDtypeStruct((B, T, WIDTH_A), jnp.float32)
    ospec = pl.BlockSpec((1, tb, WIDTH_A), lambda b, t: (b, t, 0))
    full = lambda shape: pl.BlockSpec(shape, lambda b, t: (0,) * len(shape))
    return pl.pallas_call(
        _rwkv_pre_kernel,
        out_shape=(out,) * 8,
        grid=(B, T // tb),
        in_specs=[pl.BlockSpec((1, tb, C_RWKV), lambda b, t: (b, t, 0)),
                  pl.BlockSpec((1, 8, C_RWKV), lambda b, t: (b, jnp.maximum(t * (tb // 8) - 1, 0), 0)),
                  pl.BlockSpec((1, 1, C_RWKV), lambda b, t: (b, 0, 0)),
                  full((1, C_RWKV)), full((5, WIDTH_A)), full((LORA_W + LORA_A, 2 * WIDTH_A)),
                  full((LORA_G, WIDTH_A)), full((2 * PAIR_W, PAIR_W))],
        out_specs=(ospec,) * 8,
        compiler_params=pltpu.CompilerParams(
            dimension_semantics=("arbitrary", "arbitrary"), vmem_limit_bytes=VMEM_LIMIT),
        name="rwkv7_pre",
    )(seg, seg, prev_row, p['shift_mu'].reshape(1, C_RWKV), vecs, wa_up,
      p['g_lora_up'].astype(jnp.bfloat16), _block_ones2())


SWA_BAND = WINDOW + CHUNK


def _swa_kernel(q_ref, k_ref, v_ref, kp_ref, vp_ref, bias_ref, sink_ref, o_ref, *, mask_start):
    n_chunks = q_ref.shape[1] // CHUNK
    k_all = jnp.concatenate([kp_ref[0], k_ref[0]], axis=0).astype(jnp.bfloat16)
    v_all = jnp.concatenate([vp_ref[0], v_ref[0]], axis=0).astype(jnp.bfloat16)
    first = pl.program_id(1) == 0
    key_chunk = lax.broadcasted_iota(jnp.int32, (GROUP_B * CHUNK, SWA_BAND), 1) // CHUNK
    for c in range(n_chunks):
        q_c = q_ref[0, c * CHUNK:(c + 1) * CHUNK, :].astype(jnp.bfloat16)
        k_c = k_all[c * CHUNK:c * CHUNK + SWA_BAND]
        v_c = v_all[c * CHUNK:c * CHUNK + SWA_BAND]
        dead = jnp.logical_and(first, key_chunk + (c - WIN_CHUNKS) < 0) if (mask_start and c < WIN_CHUNKS) else None
        outs = []
        for kv in range(N_KV_B):
            qg = jnp.concatenate([q_c[:, (kv * GROUP_B + g) * HEAD_B:(kv * GROUP_B + g + 1) * HEAD_B]
                                  for g in range(GROUP_B)], axis=0)
            s = lax.dot_general(qg, k_c[:, kv * HEAD_B:(kv + 1) * HEAD_B],
                                (((1,), (1,)), ((), ())), preferred_element_type=jnp.float32)
            s = s * (HEAD_B ** -0.5) + bias_ref[kv]
            if dead is not None:
                s = jnp.where(dead, NEG_INF, s)
            sink = sink_ref[kv * GROUP_B * CHUNK:(kv + 1) * GROUP_B * CHUNK, 0:1]
            m = jnp.maximum(jnp.max(s, axis=-1, keepdims=True), sink)
            e = jnp.exp(s - m)
            pr = e / (jnp.sum(e, axis=-1, keepdims=True) + jnp.exp(sink - m))
            og = jnp.dot(pr.astype(jnp.bfloat16), v_c[:, kv * HEAD_B:(kv + 1) * HEAD_B],
                         preferred_element_type=jnp.float32)
            outs.extend(og[g * CHUNK:(g + 1) * CHUNK] for g in range(GROUP_B))
        o_ref[0, c * CHUNK:(c + 1) * CHUNK, :] = jnp.concatenate(outs, axis=1)


def swa_attention(seg_b, prev_k, prev_v, sinks, *, prev_is_seq, qb=512):
    B, T, _ = seg_b.shape
    qb = min(qb, T)
    slopes = 2.0 ** (-8.0 * jnp.arange(1, N_HEADS_B + 1, dtype=jnp.float32) / N_HEADS_B)
    dist = jnp.abs(jnp.arange(CHUNK)[:, None] - (jnp.arange(SWA_BAND) - WINDOW)[None, :]).astype(jnp.float32)
    bias = (-slopes[:, None, None] * dist).reshape(N_KV_B, GROUP_B * CHUNK, SWA_BAND)
    sink_tab = jnp.broadcast_to(jnp.repeat(sinks.astype(jnp.float32), CHUNK)[:, None], (N_HEADS_B * CHUNK, 128))
    kcol, vcol = WIDTH_B // KV_WIDTH_B, WIDTH_B // KV_WIDTH_B + 1
    if prev_is_seq:
        per = qb // WINDOW
        kp_spec = pl.BlockSpec((1, WINDOW, KV_WIDTH_B), lambda b, i: (b, jnp.maximum(i * per - 1, 0), kcol))
        vp_spec = pl.BlockSpec((1, WINDOW, KV_WIDTH_B), lambda b, i: (b, jnp.maximum(i * per - 1, 0), vcol))
    else:
        kp_spec = vp_spec = pl.BlockSpec((1, WINDOW, KV_WIDTH_B), lambda b, i: (b, 0, 0))
    return pl.pallas_call(
        functools.partial(_swa_kernel, mask_start=prev_is_seq),
        out_shape=jax.ShapeDtypeStruct((B, T, WIDTH_B), jnp.float32),
        grid=(B, T // qb),
        in_specs=[pl.BlockSpec((1, qb, WIDTH_B), lambda b, i: (b, i, 0)),
                  pl.BlockSpec((1, qb, KV_WIDTH_B), lambda b, i: (b, i, kcol)),
                  pl.BlockSpec((1, qb, KV_WIDTH_B), lambda b, i: (b, i, vcol)),
                  kp_spec, vp_spec,
                  pl.BlockSpec((N_KV_B, GROUP_B * CHUNK, SWA_BAND), lambda b, i: (0, 0, 0)),
                  pl.BlockSpec((N_HEADS_B * CHUNK, 128), lambda b, i: (0, 0))],
        out_specs=pl.BlockSpec((1, qb, WIDTH_B), lambda b, i: (b, i, 0)),
        compiler_params=pltpu.CompilerParams(
            dimension_semantics=("arbitrary", "arbitrary"), vmem_limit_bytes=VMEM_LIMIT),
        name="swa_attention",
    )(seg_b, seg_b, seg_b, prev_k, prev_v, bias, sink_tab)


def _layer_norm_rows(h, g, b, eps=1e-5):
    mu = jnp.mean(h, axis=-1, keepdims=True)
    d = h - mu
    var = jnp.mean(d * d, axis=-1, keepdims=True)
    return d * lax.rsqrt(var + eps) * g + b


def _const_spec(shape):
    return pl.BlockSpec(shape, lambda *_: (0,) * len(shape), pipeline_mode=pl.Buffered(1))


def _merge_kernel(x_ref, y_ref, bonus_ref, g_ref, ob_ref, gates_ref, lnx_ref, ln1_ref, ones_ref,
                  pa_ref, pb_ref, wout_ref, o_ref):
    ones2 = ones_ref[...]
    y = y_ref[...]
    mean = _head_sums(y, ones2) * (1.0 / HEAD_A)
    d = y - mean
    var = _head_sums(d * d, ones2) * (1.0 / HEAD_A)
    yn = d * lax.rsqrt(var + GN_EPS) * lnx_ref[0:1, :] + lnx_ref[1:2, :]
    o_a = ((yn + bonus_ref[...]) * g_ref[...]).astype(jnp.bfloat16)
    br_a = jnp.dot(o_a, pa_ref[...], preferred_element_type=jnp.float32)
    br_b = jnp.dot(ob_ref[...].astype(jnp.bfloat16), pb_ref[...], preferred_element_type=jnp.float32)
    gates = jax.nn.sigmoid(gates_ref[...])
    merged = gates[:, :D_MODEL] * br_a + gates[:, D_MODEL:] * br_b
    h = ALPHA * x_ref[...] + jnp.dot(merged.astype(jnp.bfloat16), wout_ref[...], preferred_element_type=jnp.float32)
    o_ref[...] = _layer_norm_rows(h, ln1_ref[0:1, :], ln1_ref[1:2, :])


def branch_merge(x, y, bonus, g, o_b, gates, p, tm=256):
    n = x.shape[0]
    tm = _pick(n, (tm, 128, 64))
    rows = lambda w: pl.BlockSpec((tm, w), lambda i: (i, 0))
    return pl.pallas_call(
        _merge_kernel,
        out_shape=jax.ShapeDtypeStruct((n, D_MODEL), jnp.float32),
        grid=(n // tm,),
        in_specs=[rows(D_MODEL), rows(WIDTH_A), rows(WIDTH_A), rows(WIDTH_A), rows(WIDTH_B), rows(C_GATE),
                  _const_spec((2, WIDTH_A)), _const_spec((2, D_MODEL)), _const_spec((2 * PAIR_W, PAIR_W)),
                  _const_spec((WIDTH_A, D_MODEL)), _const_spec((WIDTH_B, D_MODEL)), _const_spec((D_MODEL, D_MODEL))],
        out_specs=rows(D_MODEL),
        compiler_params=pltpu.CompilerParams(dimension_semantics=("arbitrary",), vmem_limit_bytes=VMEM_LIMIT),
        name="branch_merge",
    )(x, y, bonus, g, o_b, gates, jnp.stack([p['lnx_g'], p['lnx_b']]), jnp.stack([p['ln1_g'], p['ln1_b']]),
      _block_ones2(), p['w_branch_a16'], p['w_branch_b16'], p['w_out16'])


def _mem_kernel(x_ref, mk_ref, mv_ref, ln2_ref, wq_ref, wo_ref, pwq_ref, o_ref, q_ref):
    x = x_ref[0]
    qm = jnp.dot(x.astype(jnp.bfloat16), wq_ref[...], preferred_element_type=jnp.float32).astype(jnp.bfloat16)
    mk = mk_ref[0].astype(jnp.bfloat16)
    mv = mv_ref[0].astype(jnp.bfloat16)
    outs = []
    for h in range(N_HEADS_M):
        cols = slice(h * HEAD_M, (h + 1) * HEAD_M)
        s = lax.dot_general(qm[:, cols], mk[:, cols], (((1,), (1,)), ((), ())),
                            preferred_element_type=jnp.float32) * (HEAD_M ** -0.5)
        e = jnp.exp(s - jnp.max(s, axis=-1, keepdims=True))
        pr = e / jnp.sum(e, axis=-1, keepdims=True)
        outs.append(jnp.dot(pr.astype(jnp.bfloat16), mv[:, cols], preferred_element_type=jnp.float32))
    o = jnp.concatenate(outs, axis=1).astype(jnp.bfloat16)
    h2 = ALPHA * x + jnp.dot(o, wo_ref[...], preferred_element_type=jnp.float32)
    x2 = _layer_norm_rows(h2, ln2_ref[0:1, :], ln2_ref[1:2, :])
    o_ref[0] = x2
    q_ref[0] = jnp.dot(x2.astype(jnp.bfloat16), pwq_ref[...], preferred_element_type=jnp.float32)


def mem_block(x, mk, mv, p, tm=256):
    B, T, _ = x.shape
    tm = _pick(T, (tm, 128, 64))
    wm = N_HEADS_M * HEAD_M
    rows = lambda w: pl.BlockSpec((1, tm, w), lambda b, i: (b, i, 0))
    mem = pl.BlockSpec((1, N_MEM, wm), lambda b, i: (b, 0, 0))
    out = jax.ShapeDtypeStruct((B, T, D_MODEL), jnp.float32)
    return pl.pallas_call(
        _mem_kernel,
        out_shape=(out, jax.ShapeDtypeStruct((B, T, PEER_HEADS * D_KEY), jnp.float32)),
        grid=(B, T // tm),
        in_specs=[rows(D_MODEL), mem, mem, _const_spec((2, D_MODEL)), _const_spec((D_MODEL, wm)),
                  _const_spec((wm, D_MODEL)), _const_spec((D_MODEL, PEER_HEADS * D_KEY))],
        out_specs=(rows(D_MODEL), rows(PEER_HEADS * D_KEY)),
        compiler_params=pltpu.CompilerParams(
            dimension_semantics=("arbitrary", "arbitrary"), vmem_limit_bytes=VMEM_LIMIT),
        name="mem_block",
    )(x, mk, mv, jnp.stack([p['ln2_g'], p['ln2_b']]), p['wq_mem16'], p['wo_mem16'], p['peer_wq16'])


ROUTE_TQ = 256


def _top_values(s, k):
    n_rows = s.shape[0]
    iota = lax.broadcasted_iota(jnp.int32, s.shape, 0).astype(jnp.float32)
    rank = jnp.full(s.shape, float(k), jnp.float32)
    out = []
    for step in range(k):
        m = jnp.max(s, axis=0, keepdims=True)
        first = jnp.min(jnp.where(s == m, iota, float(n_rows)), axis=0, keepdims=True)
        taken = iota == first
        s = jnp.where(taken, -jnp.inf, s)
        rank = jnp.where(taken, float(step), rank)
        out.append(m)
    return out, rank


def _peer_route_kernel(q_ref, keys_ref, n1_ref, c1_ref, rank2_ref, e2_ref):
    half = D_KEY // 2
    for h in range(PEER_HEADS):
        tops, scores = [], []
        for p in range(2):
            c0 = (2 * h + p) * half
            qs = q_ref[:, c0:c0 + half].astype(jnp.bfloat16)
            s = lax.dot_general(keys_ref[2 * h + p], qs, (((1,), (1,)), ((), ())),
                                preferred_element_type=jnp.float32)
            scores.append(s)
            tops.append(_top_values(s, TOPK))
        (t1, _), (t2, rank2) = tops
        t2all = jnp.concatenate(t2, axis=0)
        t1all = jnp.concatenate(t1, axis=0)
        rank8 = lax.broadcasted_iota(jnp.int32, (8, t1all.shape[1]), 0)
        cand_rows = [t1all + t2[0], t1all[:8] + t2[1]]
        for b in range(2, 8):
            cand_rows.append(jnp.where(rank8 < TOPK // (b + 1), t1all[:8] + t2[b], -jnp.inf))
        cand_rows.append(t1[0] + t2all[8:])
        sc, _ = _top_values(jnp.concatenate(cand_rows, axis=0), TOPK)
        z = jnp.zeros_like(sc[0])
        for kq in range(TOPK):
            z = z + jnp.exp(sc[kq] - sc[0])
        theta = sc[TOPK - 1]
        n1 = jnp.zeros_like(scores[0])
        for a in range(TOPK):
            n_a = jnp.sum((t1[a] + t2all >= theta).astype(jnp.float32), axis=0, keepdims=True)
            n1 = jnp.where(scores[0] == t1[a], n_a, n1)
        n1_ref[h] = n1
        rank2_ref[h] = rank2.astype(jnp.bfloat16)
        c1_ref[h] = jnp.exp(scores[0] - t1[0]) / z
        e2_ref[h] = jnp.exp(scores[1] - t2[0]).astype(jnp.bfloat16)


def peer_route(q, keys16):
    n = q.shape[0]
    tq = _pick(n, (ROUTE_TQ, 128))
    big = jax.ShapeDtypeStruct((PEER_HEADS, N_KEYS, n), jnp.float32)
    big16 = jax.ShapeDtypeStruct((PEER_HEADS, N_KEYS, n), jnp.bfloat16)
    bspec = pl.BlockSpec((PEER_HEADS, N_KEYS, tq), lambda i: (0, 0, i))
    return pl.pallas_call(
        _peer_route_kernel,
        out_shape=(big, big, big16, big16),
        grid=(n // tq,),
        in_specs=[pl.BlockSpec((tq, PEER_HEADS * D_KEY), lambda i: (i, 0)),
                  pl.BlockSpec((2 * PEER_HEADS, N_KEYS, D_KEY // 2), lambda i: (0, 0, 0))],
        out_specs=(bspec, bspec, bspec, bspec),
        compiler_params=pltpu.CompilerParams(
            dimension_semantics=("arbitrary",), vmem_limit_bytes=VMEM_LIMIT),
        name="peer_route",
    )(q, keys16)


PEER_TM = 512
PEER_ROWS = 8
PEER_TE = PEER_ROWS * N_KEYS
PEER_STRIP = 256


def _gelu(x):
    return 0.5 * x * (1.0 + lax.erf(x * (2.0 ** -0.5)))


def _peer_mix_kernel(x_ref, u_ref, vt_ref, n1_ref, c1_ref, rank2_ref, e2_ref, ln3_ref, o_ref,
                     x16_ref, h_ref, acc_ref):
    j = pl.program_id(1)

    @pl.when(j == 0)
    def _():
        x16_ref[...] = x_ref[...].astype(jnp.bfloat16)
        acc_ref[...] = jnp.zeros_like(acc_ref)

    a_t = lax.dot_general(u_ref[...], x16_ref[...], (((1,), (1,)), ((), ())),
                          preferred_element_type=jnp.float32)

    def sublane_bcast16(row):
        tile = jnp.broadcast_to(row, (16, row.shape[1])).astype(jnp.bfloat16)
        return pltpu.repeat(tile, N_KEYS // 16, axis=0)

    for r in range(PEER_ROWS):
        rows = slice(r * N_KEYS, (r + 1) * N_KEYS)
        for c0 in range(0, x_ref.shape[0], 128):
            cols = slice(c0, c0 + 128)
            gate = None
            for h in range(PEER_HEADS):
                keep = rank2_ref[h, :, cols] < sublane_bcast16(n1_ref[h, r:r + 1, cols])
                w = jnp.where(keep, e2_ref[h, :, cols], jnp.zeros((), jnp.bfloat16))
                w = w * sublane_bcast16(c1_ref[h, r:r + 1, cols])
                gate = w if gate is None else gate + w
            h_ref[rows, cols] = gate * _gelu(a_t[rows, cols]).astype(jnp.bfloat16)
    acc_ref[...] += jnp.dot(vt_ref[...], h_ref[...], preferred_element_type=jnp.float32)

    @pl.when(j == pl.num_programs(1) - 1)
    def _():
        o_ref[...] = _layer_norm_rows(ALPHA * x_ref[...] + acc_ref[...].T, ln3_ref[0:1, :], ln3_ref[1:2, :])


def peer_mix(x, u16, vt16, n1, c1, rank2, e2, ln3):
    n, d = x.shape
    tm = _pick(n, (PEER_TM, 256, 128))
    n_exp = u16.shape[0]
    row_spec = pl.BlockSpec((PEER_HEADS, PEER_ROWS, tm), lambda i, j: (0, j, i))
    all_spec = pl.BlockSpec((PEER_HEADS, N_KEYS, tm), lambda i, j: (0, 0, i), pipeline_mode=pl.Buffered(1))
    return pl.pallas_call(
        _peer_mix_kernel,
        out_shape=jax.ShapeDtypeStruct((n, d), jnp.float32),
        grid=(n // tm, n_exp // PEER_TE),
        in_specs=[pl.BlockSpec((tm, d), lambda i, j: (i, 0), pipeline_mode=pl.Buffered(1)),
                  pl.BlockSpec((PEER_TE, d), lambda i, j: (j, 0)),
                  pl.BlockSpec((d, PEER_TE), lambda i, j: (0, j)),
                  row_spec, row_spec, all_spec, all_spec, _const_spec((2, d))],
        out_specs=pl.BlockSpec((tm, d), lambda i, j: (i, 0)),
        scratch_shapes=[pltpu.VMEM((tm, d), jnp.bfloat16), pltpu.VMEM((PEER_TE, tm), jnp.bfloat16),
                        pltpu.VMEM((d, tm), jnp.float32)],
        compiler_params=pltpu.CompilerParams(
            dimension_semantics=("arbitrary", "arbitrary"), vmem_limit_bytes=VMEM_LIMIT),
        name="peer_mix",
    )(x, u16, vt16, n1, c1, rank2, e2, ln3)


def peer_block(x, q, p):
    stats = peer_route(q, p['peer_keys16'])
    return peer_mix(x, p['peer_u16'], p['peer_vt16'], *stats, jnp.stack([p['ln3_g'], p['ln3_b']]))


def layer_norm(x, g, b, eps=1e-5):
    xf = x.astype(jnp.float32)
    mu = jnp.mean(xf, -1, keepdims=True)
    var = jnp.mean(jnp.square(xf - mu), -1, keepdims=True)
    return ((xf - mu) * lax.rsqrt(var + eps) * g + b).astype(x.dtype)


def rwkv7_scan(r, w, k, v, kk, a, state0):
    def step(S, inp):
        r_t, w_t, k_t, v_t, kk_t, a_t = inp
        sa = jnp.einsum('bhij,bhj->bhi', S, -kk_t)
        S = (S * w_t[:, :, None, :] + sa[..., None] * (kk_t * a_t)[:, :, None, :]
             + v_t[..., None] * k_t[:, :, None, :])
        y = jnp.einsum('bhij,bhj->bhi', S, r_t)
        return S, y
    xs = tuple(jnp.moveaxis(t, 1, 0) for t in (r, w, k, v, kk, a))
    S, ys = lax.scan(step, state0.astype(jnp.float32), xs)
    return jnp.moveaxis(ys, 0, 1), S


def rwkv7_branch(seg, prev_row, state0, p):
    B, T, _ = seg.shape
    f32 = jnp.float32
    shifted = jnp.concatenate([prev_row.astype(seg.dtype), seg[:, :-1]], axis=1)
    xm = seg + p['shift_mu'] * (shifted - seg)
    r, k, v, wl, al, gl = jnp.split(xm, SPLIT_A, axis=-1)
    w_log = -jax.nn.softplus(-(p['w0'] + jnp.tanh(wl) @ p['w_lora_up']).astype(f32)) - 0.5
    decay = jnp.exp(-jnp.exp(w_log))
    a = jax.nn.sigmoid((p['a0'] + al @ p['a_lora_up']).astype(f32))
    g = jax.nn.sigmoid(gl) @ p['g_lora_up']
    heads = lambda t: t.reshape(B, T, N_HEADS_A, HEAD_A)
    kk = heads((k * p['k_k']).astype(f32))
    kk = kk * lax.rsqrt(jnp.maximum(jnp.sum(kk * kk, -1, keepdims=True), 1e-24))
    kf = k.astype(f32) * (1.0 + (a - 1.0) * p['k_a'].astype(f32))
    rh, kh, vh, ah, wh = heads(r.astype(f32)), heads(kf), heads(v.astype(f32)), heads(a), heads(decay)
    flat = lambda t: t.reshape(B, T, WIDTH_A)
    y, s_final = rwkv7_scan_pallas(flat(rh), flat(wh), flat(kh), flat(vh), flat(kk), flat(ah), state0)
    y = heads(y)
    mean = jnp.mean(y, -1, keepdims=True)
    var = jnp.mean(jnp.square(y - mean), -1, keepdims=True)
    yn = ((y - mean) * lax.rsqrt(var + GN_EPS)).reshape(B, T, WIDTH_A) * p['lnx_g'] + p['lnx_b']
    bonus = jnp.sum(rh * kh * p['r_k'].astype(f32), -1, keepdims=True) * vh
    out = (yn + bonus.reshape(B, T, WIDTH_A)) * g
    return out.astype(seg.dtype), s_final, seg[:, -1:]


def alibi_bias(q_pos, k_pos):
    slopes = 2.0 ** (-8.0 * jnp.arange(1, N_HEADS_B + 1, dtype=jnp.float32) / N_HEADS_B)
    dist = jnp.abs(q_pos[:, None] - k_pos[None, :]).astype(jnp.float32)
    return -slopes.reshape(N_KV_B, GROUP_B, 1, 1) * dist


def sink_attention(q, k, v, bias, mask, sinks):
    s = jnp.einsum('bnqhgd,bnjhd->bnhgqj', q, k).astype(jnp.float32) * (HEAD_B ** -0.5) + bias
    if mask is not None:
        s = jnp.where(mask, s, NEG_INF)
    sink = sinks.astype(jnp.float32).reshape(N_KV_B, GROUP_B, 1, 1)
    m = jnp.maximum(jnp.max(s, -1, keepdims=True), sink)
    p = jnp.exp(s - m)
    p = p / (jnp.sum(p, -1, keepdims=True) + jnp.exp(sink - m))
    return jnp.einsum('bnhgqj,bnjhd->bnqhgd', p.astype(v.dtype), v)


def swa_prompt(q, k, v, sinks):
    B, S = q.shape[:2]
    NC = S // CHUNK
    band = (WIN_CHUNKS + 1) * CHUNK
    qc = q.reshape(B, NC, CHUNK, N_KV_B, GROUP_B, HEAD_B)
    pad = ((0, 0), (WIN_CHUNKS, 0), (0, 0), (0, 0), (0, 0))
    kp = jnp.pad(k.reshape(B, NC, CHUNK, N_KV_B, HEAD_B), pad)
    vp = jnp.pad(v.reshape(B, NC, CHUNK, N_KV_B, HEAD_B), pad)
    kb = jnp.concatenate([kp[:, o:o + NC] for o in range(WIN_CHUNKS + 1)], axis=2)
    vb = jnp.concatenate([vp[:, o:o + NC] for o in range(WIN_CHUNKS + 1)], axis=2)
    bias = alibi_bias(jnp.arange(CHUNK), jnp.arange(band) - WIN_CHUNKS * CHUNK)
    key_chunk = jnp.arange(NC)[:, None] - WIN_CHUNKS + (jnp.arange(band) // CHUNK)[None, :]
    mask = (key_chunk >= 0)[None, :, None, None, None, :]
    o = sink_attention(qc, kb, vb, bias, mask, sinks)
    return o.reshape(B, S, WIDTH_B)


def swa_sample(q, k, v, k_cache, v_cache, sinks):
    B, T = q.shape[:2]
    Lc = k_cache.shape[1]
    k_all = jnp.concatenate([k_cache.astype(k.dtype), k], axis=1)
    v_all = jnp.concatenate([v_cache.astype(v.dtype), v], axis=1)
    bias = alibi_bias(jnp.arange(T), jnp.arange(Lc + T) - Lc)
    o = sink_attention(q[:, None], k_all[:, None], v_all[:, None], bias, None, sinks)
    return o.reshape(B, T, WIDTH_B), k_all[:, -Lc:], v_all[:, -Lc:]


def mem_attention(x, mk, mv, p):
    B, T, _ = x.shape
    q = mm(x, p['wq_mem16']).reshape(B, T, N_HEADS_M, HEAD_M)
    s = jnp.einsum('bthd,bmhd->bhtm', q, mk.astype(q.dtype)).astype(jnp.float32) * (HEAD_M ** -0.5)
    pr = jax.nn.softmax(s, axis=-1).astype(x.dtype)
    o = jnp.einsum('bhtm,bmhd->bthd', pr, mv.astype(x.dtype)).reshape(B, T, N_HEADS_M * HEAD_M)
    return mm(o, p['wo_mem16'])


def peer_ffn(x, p):
    B, T, D = x.shape
    n = B * T
    nblk = -(-n // PEER_BLOCK)
    xt = x.reshape(nblk, PEER_BLOCK, D)
    qa = mm(x, p['peer_wq16']).reshape(nblk, PEER_BLOCK, PEER_HEADS * D_KEY)
    sub_keys, u_tab, v_tab = p['peer_sub_keys'], p['peer_u'], p['peer_v']

    def peer_block(args):
        xb, qb = args
        q = qb.reshape(PEER_BLOCK, PEER_HEADS, 2, D_KEY // 2)
        s = jnp.einsum('thpd,hpnd->thpn', q, sub_keys).astype(jnp.float32)
        s1, i1 = lax.top_k(s[:, :, 0], TOPK)
        s2, i2 = lax.top_k(s[:, :, 1], TOPK)
        cand = (s1[..., :, None] + s2[..., None, :]).reshape(PEER_BLOCK, PEER_HEADS, TOPK * TOPK)
        cidx = (i1[..., :, None] * N_KEYS + i2[..., None, :]).reshape(PEER_BLOCK, PEER_HEADS, TOPK * TOPK)
        sc, pos = lax.top_k(cand, TOPK)
        e = jnp.take_along_axis(cidx, pos, axis=-1)
        g = jax.nn.softmax(sc, axis=-1)
        act = jax.nn.gelu(jnp.einsum('td,thkd->thk', xb, u_tab[e]).astype(jnp.float32), approximate=False)
        return jnp.einsum('thk,thkd->td', (g * act).astype(xb.dtype), v_tab[e])

    y = lax.map(peer_block, (xt, qa)).reshape(nblk * PEER_BLOCK, D)
    return y.reshape(B, T, D)


def trunk_layer(x, p, mem_k, mem_v, rwkv_state, shift_row, swa_k_cache, swa_v_cache):
    B, T, _ = x.shape
    n = B * T
    x2d = x.reshape(n, D_MODEL)
    seg_a = matmul(x2d, p['w_in_a16']).reshape(B, T, C_RWKV)
    seg_b = matmul(x2d, p['w_in_b16']).reshape(B, T, C_SWA)
    gates = matmul(x2d, p['w_in_g16'])
    r, w, k, v, kk, a, g, bonus = rwkv7_pre(seg_a, shift_row.astype(jnp.float32), p)
    y, rwkv_new = rwkv7_chunked(r, w, k, v, kk, a, rwkv_state)
    shift_new = seg_a[:, -1:]
    k_new = seg_b[:, :, WIDTH_B:WIDTH_B + KV_WIDTH_B].reshape(B, T, N_KV_B, HEAD_B)
    v_new = seg_b[:, :, WIDTH_B + KV_WIDTH_B:].reshape(B, T, N_KV_B, HEAD_B)
    if swa_k_cache is None:
        o_b = swa_attention(seg_b, seg_b, seg_b, p['attn_sinks'], prev_is_seq=True)
        swa_k_new, swa_v_new = k_new[:, -WINDOW:], v_new[:, -WINDOW:]
    else:
        o_b = swa_attention(seg_b, swa_k_cache.reshape(B, WINDOW, KV_WIDTH_B),
                            swa_v_cache.reshape(B, WINDOW, KV_WIDTH_B), p['attn_sinks'], prev_is_seq=False)
        swa_k_new = jnp.concatenate([swa_k_cache, k_new], axis=1)[:, -WINDOW:]
        swa_v_new = jnp.concatenate([swa_v_cache, v_new], axis=1)[:, -WINDOW:]
    flat = lambda t: t.reshape(n, t.shape[-1])
    x1 = branch_merge(x2d, flat(y), flat(bonus), flat(g), flat(o_b), gates, p)
    wm = N_HEADS_M * HEAD_M
    x2, q = mem_block(x1.reshape(B, T, D_MODEL), mem_k.reshape(-1, N_MEM, wm), mem_v.reshape(-1, N_MEM, wm), p)
    x3 = peer_block(flat(x2), flat(q), p)
    return x3.reshape(B, T, D_MODEL), rwkv_new, shift_new, swa_k_new, swa_v_new


_MM_WEIGHTS = ('w_branch_a', 'w_branch_b', 'w_out', 'wq_mem', 'wk_mem', 'wv_mem', 'wo_mem', 'peer_wq')


def kernel(x_prompt, x_sample, state_rwkv, state_shift, cache_swa_k, cache_swa_v, cache_mem_k, cache_mem_v, mem_prompt, w_in, shift_mu, w0, w_lora_up, a0, a_lora_up, g_lora_up, k_k, k_a, r_k, lnx_g, lnx_b, attn_sinks, w_branch_a, w_branch_b, w_out, ln1_g, ln1_b, wq_mem, wk_mem, wv_mem, wo_mem, ln2_g, ln2_b, peer_wq, peer_sub_keys, peer_u, peer_v, ln3_g, ln3_b):
    params = {
        'w_in': w_in, 'shift_mu': shift_mu, 'w0': w0, 'w_lora_up': w_lora_up, 'a0': a0,
        'a_lora_up': a_lora_up, 'g_lora_up': g_lora_up, 'k_k': k_k, 'k_a': k_a, 'r_k': r_k,
        'lnx_g': lnx_g, 'lnx_b': lnx_b, 'attn_sinks': attn_sinks, 'w_branch_a': w_branch_a,
        'w_branch_b': w_branch_b, 'w_out': w_out, 'ln1_g': ln1_g, 'ln1_b': ln1_b, 'wq_mem': wq_mem,
        'wk_mem': wk_mem, 'wv_mem': wv_mem, 'wo_mem': wo_mem, 'ln2_g': ln2_g, 'ln2_b': ln2_b,
        'peer_wq': peer_wq, 'peer_sub_keys': peer_sub_keys, 'peer_u': peer_u, 'peer_v': peer_v,
        'ln3_g': ln3_g, 'ln3_b': ln3_b,
    }
    B = x_prompt.shape[0]
    rwkv0 = jnp.zeros((B, N_HEADS_A, HEAD_A, HEAD_A), jnp.float32)
    shift0 = jnp.zeros((B, 1, C_RWKV), x_prompt.dtype)
    xp, xs = x_prompt, x_sample
    p_rw, p_sh, p_k, p_v, p_mk, p_mv = [], [], [], [], [], []
    s_rw, s_sh, s_k, s_v = [], [], [], []
    for l in range(DEPTH):
        p = {name: arr[l] for name, arr in params.items()}
        for name in _MM_WEIGHTS:
            p[name + '16'] = p[name].astype(jnp.bfloat16)
        w_in16 = p['w_in'].astype(jnp.bfloat16)
        p['w_in_a16'] = w_in16[:, :C_RWKV]
        p['w_in_b16'] = w_in16[:, C_RWKV:C_RWKV + C_SWA]
        p['w_in_g16'] = w_in16[:, C_RWKV + C_SWA:]
        p['peer_keys16'] = p['peer_sub_keys'].reshape(2 * PEER_HEADS, N_KEYS, D_KEY // 2).astype(jnp.bfloat16)
        p['peer_u16'] = p['peer_u'].astype(jnp.bfloat16)
        p['peer_vt16'] = p['peer_v'].T.astype(jnp.bfloat16)
        mk = mm(mem_prompt, p['wk_mem16']).reshape(B, N_MEM, N_HEADS_M, HEAD_M)
        mv = mm(mem_prompt, p['wv_mem16']).reshape(B, N_MEM, N_HEADS_M, HEAD_M)
        xp, rw, sh, kn, vn = trunk_layer(xp, p, mk, mv, rwkv0, shift0, None, None)
        p_rw.append(rw); p_sh.append(sh); p_k.append(kn); p_v.append(vn); p_mk.append(mk); p_mv.append(mv)
        xs, rw, sh, kn, vn = trunk_layer(xs, p, cache_mem_k[l], cache_mem_v[l], state_rwkv[l], state_shift[l],
                                         cache_swa_k[l], cache_swa_v[l])
        s_rw.append(rw); s_sh.append(sh); s_k.append(kn); s_v.append(vn)
    return (xp, xs,
            jnp.stack(p_rw), jnp.stack(p_sh), jnp.stack(p_k), jnp.stack(p_v), jnp.stack(p_mk), jnp.stack(p_mv),
            jnp.stack(s_rw), jnp.stack(s_sh), jnp.stack(s_k), jnp.stack(s_v))
```

```python
import functools

import jax
import jax.numpy as jnp
from jax import lax
from jax.experimental import pallas as pl
from jax.experimental.pallas import tpu as pltpu

D_MODEL = 2048
DEPTH = 2
CHUNK = 64
HEAD_A = 64
N_HEADS_A = 16
WIDTH_A = N_HEADS_A * HEAD_A
LORA_W = 64
LORA_A = 64
LORA_G = 128
GN_EPS = 64e-5
SPLIT_A = [WIDTH_A, 2 * WIDTH_A, 3 * WIDTH_A, 3 * WIDTH_A + LORA_W, 3 * WIDTH_A + LORA_W + LORA_A]
C_RWKV = 3 * WIDTH_A + LORA_W + LORA_A + LORA_G
HEAD_B = 64
N_HEADS_B = 16
N_KV_B = 4
GROUP_B = N_HEADS_B // N_KV_B
WIDTH_B = N_HEADS_B * HEAD_B
KV_WIDTH_B = N_KV_B * HEAD_B
WINDOW = 128
WIN_CHUNKS = WINDOW // CHUNK
C_SWA = WIDTH_B + 2 * KV_WIDTH_B
C_GATE = 2 * D_MODEL
C_IN = C_RWKV + C_SWA + C_GATE
N_MEM = 256
N_HEADS_M = 4
HEAD_M = 128
N_KEYS = 128
PEER_HEADS = 8
D_KEY = 256
TOPK = 16
PEER_BLOCK = 128
ALPHA = (2.0 * DEPTH) ** 0.25
NEG_INF = -1e30

VMEM_LIMIT = 56 * 1024 * 1024


def _matmul_kernel(a_ref, b_ref, o_ref, a16_ref):
    @pl.when(pl.program_id(1) == 0)
    def _():
        a16_ref[...] = a_ref[...].astype(jnp.bfloat16)

    o_ref[...] = jnp.dot(a16_ref[...], b_ref[...], preferred_element_type=jnp.float32)


def _pick(n, cands):
    for c in cands:
        if n % c == 0:
            return c
    return n


def matmul(a, b16):
    m, k = a.shape
    n = b16.shape[1]
    tm = _pick(m, (512, 256, 128))
    tn = _pick(n, (2048, 1664, 1536, 1024, 512, 256, 128))
    return pl.pallas_call(
        _matmul_kernel,
        out_shape=jax.ShapeDtypeStruct((m, n), jnp.float32),
        grid=(m // tm, n // tn),
        in_specs=[pl.BlockSpec((tm, k), lambda i, j: (i, 0)),
                  pl.BlockSpec((k, tn), lambda i, j: (0, j))],
        out_specs=pl.BlockSpec((tm, tn), lambda i, j: (i, j)),
        scratch_shapes=[pltpu.VMEM((tm, k), jnp.bfloat16)],
        compiler_params=pltpu.CompilerParams(
            dimension_semantics=("arbitrary", "arbitrary"),
            vmem_limit_bytes=VMEM_LIMIT),
        name="matmul",
    )(a, b16)


def mm(x, w16):
    lead = x.shape[:-1]
    return matmul(x.reshape(-1, x.shape[-1]), w16).reshape(*lead, w16.shape[1])


N_PAIR = N_HEADS_A // 2
PAIR_W = 2 * HEAD_A


def _split2(x):
    hi = x.astype(jnp.bfloat16)
    lo = (x - hi.astype(jnp.float32)).astype(jnp.bfloat16)
    return jnp.concatenate([hi, lo], axis=1)


def _rwkv_scan_kernel(r_ref, w_ref, k_ref, v_ref, kk_ref, a_ref, s0_ref, ones_ref, eye_ref,
                      y_ref, sT_ref, s_ref, *, tb):
    tblk = pl.program_id(1)

    @pl.when(tblk == 0)
    def _():
        s_ref[...] = s0_ref[0]

    ones2 = ones_ref[...]
    eye2 = eye_ref[...]

    def segsum(tiles):
        lhs = jnp.concatenate([_split2(m) for m in tiles], axis=0)
        return jnp.dot(lhs, ones2, preferred_element_type=jnp.float32)

    def step(g, carry):
        rows = pl.ds(pl.multiple_of(g * 8, 8), 8)
        pairs = range(N_PAIR)
        sls = [pl.ds(p * PAIR_W, PAIR_W) for p in pairs]
        kk8 = [kk_ref[0, rows, sl] for sl in sls]
        kka8 = [kk8[p] * a_ref[0, rows, sls[p]] for p in pairs]
        w8 = [w_ref[0, rows, sl] for sl in sls]
        k8 = [k_ref[0, rows, sl] for sl in sls]
        v8 = [v_ref[0, rows, sl] for sl in sls]
        r8 = [r_ref[0, rows, sl] for sl in sls]
        s = [s_ref[p] for p in pairs]
        ys = [[] for _ in pairs]
        tile = lambda x, p: x[p * HEAD_A:(p + 1) * HEAD_A]
        for j in range(8):
            row = slice(j, j + 1)
            sa = segsum([s[p] * (-kk8[p][row]) for p in pairs])
            vb = segsum([eye2 * v8[p][row] for p in pairs])
            s = [s[p] * w8[p][row] + tile(sa, p) * kka8[p][row] + tile(vb, p) * k8[p][row] for p in pairs]
            yb = segsum([s[p] * r8[p][row] for p in pairs])
            for p in pairs:
                ys[p].append(jnp.sum(tile(yb, p) * eye2, axis=0, keepdims=True))
        for p in pairs:
            s_ref[p] = s[p]
            y_ref[0, rows, sls[p]] = jnp.concatenate(ys[p], axis=0)
        return carry

    lax.fori_loop(0, tb // 8, step, 0)

    @pl.when(tblk == pl.num_programs(1) - 1)
    def _():
        sT_ref[0] = s_ref[...]


def rwkv7_scan_pallas(r, w, k, v, kk, a, state0, tb=256):
    B, T, _ = r.shape
    tb = min(tb, T)
    s0 = state0.astype(jnp.float32).reshape(B, N_PAIR, 2, HEAD_A, HEAD_A)
    s0 = s0.transpose(0, 1, 3, 2, 4).reshape(B, N_PAIR, HEAD_A, PAIR_W)
    lane_head = jnp.arange(PAIR_W) // HEAD_A
    ones2 = (jnp.tile(lane_head, 2)[:, None] == lane_head[None, :]).astype(jnp.bfloat16)
    eye2 = (jnp.arange(HEAD_A)[:, None] == (jnp.arange(PAIR_W) % HEAD_A)[None, :]).astype(jnp.float32)
    seq = pl.BlockSpec((1, tb, WIDTH_A), lambda b, t: (b, t, 0))
    st = pl.BlockSpec((1, N_PAIR, HEAD_A, PAIR_W), lambda b, t: (b, 0, 0, 0))
    y, sT = pl.pallas_call(
        functools.partial(_rwkv_scan_kernel, tb=tb),
        out_shape=(jax.ShapeDtypeStruct((B, T, WIDTH_A), jnp.float32),
                   jax.ShapeDtypeStruct((B, N_PAIR, HEAD_A, PAIR_W), jnp.float32)),
        grid=(B, T // tb),
        in_specs=[seq] * 6 + [st,
                              pl.BlockSpec((2 * PAIR_W, PAIR_W), lambda b, t: (0, 0)),
                              pl.BlockSpec((HEAD_A, PAIR_W), lambda b, t: (0, 0))],
        out_specs=(seq, st),
        scratch_shapes=[pltpu.VMEM((N_PAIR, HEAD_A, PAIR_W), jnp.float32)],
        compiler_params=pltpu.CompilerParams(
            dimension_semantics=("arbitrary", "arbitrary"),
            vmem_limit_bytes=VMEM_LIMIT),
        name="rwkv7_scan",
    )(r, w, k, v, kk, a, s0, ones2, eye2)
    sT = sT.reshape(B, N_PAIR, HEAD_A, 2, HEAD_A).transpose(0, 1, 3, 2, 4)
    return y, sT.reshape(B, N_HEADS_A, HEAD_A, HEAD_A)


def _hl(x):
    hi = x.astype(jnp.bfloat16)
    return hi, (x - hi.astype(jnp.float32)).astype(jnp.bfloat16)


def _dot3(a, b):
    ah, al = _hl(a)
    bh, bl = _hl(b)
    return jnp.dot(jnp.concatenate([ah, ah, al], axis=1), jnp.concatenate([bh, bl, bh], axis=0),
                   preferred_element_type=jnp.float32)


def _dot3_nt(a, b):
    ah, al = _hl(a)
    bh, bl = _hl(b)
    return lax.dot_general(jnp.concatenate([ah, ah, al], axis=1), jnp.concatenate([bh, bl, bh], axis=1),
                           (((1,), (1,)), ((), ())), preferred_element_type=jnp.float32)


def _rwkv_chunk_kernel(r_ref, w_ref, k_ref, v_ref, kk_ref, a_ref, s0_ref, ltri_ref, ones_ref, msk_ref,
                       y_ref, sT_ref, p_ref, *, n_chunks):
    tblk = pl.program_id(1)

    @pl.when(tblk == 0)
    def _():
        p_ref[...] = s0_ref[0]

    eye2, m0, m1, strict, incl = (msk_ref[i] for i in range(5))
    bd = lambda y: jnp.concatenate([y * m0, y * m1], axis=0)

    def split3(x, axis):
        t1 = x.astype(jnp.bfloat16)
        d = x - t1.astype(jnp.float32)
        t2 = d.astype(jnp.bfloat16)
        t3 = (d - t2.astype(jnp.float32)).astype(jnp.bfloat16)
        return jnp.concatenate([t1, t2, t3], axis=axis)

    def chunk(c, carry):
        rows = pl.ds(pl.multiple_of(c * CHUNK, CHUNK), CHUNK)
        pairs = range(N_PAIR)
        sls = [pl.ds(p * PAIR_W, PAIR_W) for p in pairs]
        lw = [w_ref[0, rows, sl] for sl in sls]
        cum = [jnp.dot(ltri_ref[...], split3(lw[p], 0), preferred_element_type=jnp.float32) for p in pairs]
        g = [jnp.exp(cum[p]) for p in pairs]
        ginv = [jnp.exp(-cum[p]) for p in pairs]
        kk = [kk_ref[0, rows, sl] for sl in sls]
        kh = [kk[p] * jnp.exp(cum[p] - lw[p]) for p in pairs]
        bh = [kk[p] * a_ref[0, rows, sls[p]] * ginv[p] for p in pairs]
        kf = [k_ref[0, rows, sls[p]] * ginv[p] for p in pairs]
        rh = [r_ref[0, rows, sls[p]] * g[p] for p in pairs]
        v = [v_ref[0, rows, sl] for sl in sls]
        g_last = [g[p][CHUNK - 1:CHUNK, :] for p in pairs]
        kr = [jnp.concatenate([kh[p], rh[p]], axis=0) for p in pairs]
        gram = [_dot3_nt(kr[p], jnp.concatenate([bd(bh[p]), bd(kf[p])], axis=0)) for p in pairs]
        a_b = [gram[p][:CHUNK, :PAIR_W] * strict for p in pairs]
        a_k = [gram[p][:CHUNK, PAIR_W:] * strict for p in pairs]
        a_r = [jnp.concatenate([gram[p][CHUNK:, :PAIR_W] * incl, gram[p][CHUNK:, PAIR_W:] * incl], axis=1)
               for p in pairs]
        t_inv = [eye2 - a_b[p] * msk_ref[5] for p in pairs]
        for lvl in range(1, 6):
            half = [_dot3(t_inv[p], bd(a_b[p] * msk_ref[5 + lvl])) for p in pairs]
            t_inv = [t_inv[p] - _dot3(half[p], bd(t_inv[p])) for p in pairs]
        akv = [_dot3(a_k[p], bd(v[p])) for p in pairs]
        g_col = [jnp.dot(split3(eye2 * g_last[p], 1), ones_ref[...], preferred_element_type=jnp.float32)
                 for p in pairs]
        xt = [jnp.concatenate([bh[p] * g_last[p], kf[p] * g_last[p]], axis=0).T for p in pairs]
        p0 = [p_ref[p] for p in pairs]
        zy = [_dot3(kr[p], bd(p0[p])) for p in pairs]
        u = [-_dot3(t_inv[p], bd(zy[p][:CHUNK] + akv[p])) for p in pairs]
        y = [zy[p][CHUNK:] + _dot3(a_r[p], jnp.concatenate([bd(u[p]), bd(v[p])], axis=0)) for p in pairs]
        delta = [_dot3(xt[p], jnp.concatenate([u[p], v[p]], axis=0)) for p in pairs]
        for p in pairs:
            p_ref[p] = g_col[p] * p0[p] + delta[p][:CHUNK] * m0 + delta[p][CHUNK:] * m1
            y_ref[0, rows, sls[p]] = y[p]
        return carry

    lax.fori_loop(0, n_chunks, chunk, 0)

    @pl.when(tblk == pl.num_programs(1) - 1)
    def _():
        sT_ref[0] = p_ref[...]


def rwkv7_chunked(r, logw, k, v, kk, a, state0, tb=256):
    B, T, _ = r.shape
    tb = min(tb, T)
    s0 = state0.astype(jnp.float32).reshape(B, N_PAIR, 2, HEAD_A, HEAD_A)
    s0 = s0.transpose(0, 1, 4, 2, 3).reshape(B, N_PAIR, HEAD_A, PAIR_W)
    t_i = jnp.arange(CHUNK)[:, None]
    s_i = (jnp.arange(PAIR_W) % HEAD_A)[None, :]
    lane_head = (jnp.arange(PAIR_W) // HEAD_A)[None, :]
    ones_row = jnp.ones((CHUNK, 1), jnp.int32)
    masks = [t_i == s_i, (lane_head == 0) * ones_row, (lane_head == 1) * ones_row, s_i < t_i, s_i <= t_i]
    for m in (1, 2, 4, 8, 16, 32):
        masks.append((t_i // (2 * m) == s_i // (2 * m)) & (t_i % (2 * m) >= m) & (s_i % (2 * m) < m))
    masks = jnp.stack([mk.astype(jnp.float32) for mk in masks])
    ltri = jnp.tile((jnp.arange(CHUNK)[None, :] <= jnp.arange(CHUNK)[:, None]), (1, 3)).astype(jnp.bfloat16)
    ones3 = jnp.tile((lane_head.T == lane_head), (3, 1)).astype(jnp.bfloat16)
    seq = pl.BlockSpec((1, tb, WIDTH_A), lambda b, t: (b, t, 0))
    st = pl.BlockSpec((1, N_PAIR, HEAD_A, PAIR_W), lambda b, t: (b, 0, 0, 0))
    y, sT = pl.pallas_call(
        functools.partial(_rwkv_chunk_kernel, n_chunks=tb // CHUNK),
        out_shape=(jax.ShapeDtypeStruct((B, T, WIDTH_A), jnp.float32),
                   jax.ShapeDtypeStruct((B, N_PAIR, HEAD_A, PAIR_W), jnp.float32)),
        grid=(B, T // tb),
        in_specs=[seq] * 6 + [st, _const_spec((CHUNK, 3 * CHUNK)), _const_spec((3 * PAIR_W, PAIR_W)),
                              _const_spec((11, CHUNK, PAIR_W))],
        out_specs=(seq, st),
        scratch_shapes=[pltpu.VMEM((N_PAIR, HEAD_A, PAIR_W), jnp.float32)],
        compiler_params=pltpu.CompilerParams(
            dimension_semantics=("arbitrary", "arbitrary"), vmem_limit_bytes=VMEM_LIMIT),
        name="rwkv7_chunked",
    )(r, logw, k, v, kk, a, s0, ltri, ones3, masks)
    sT = sT.reshape(B, N_PAIR, HEAD_A, 2, HEAD_A).transpose(0, 1, 3, 4, 2)
    return y, sT.reshape(B, N_HEADS_A, HEAD_A, HEAD_A)


def _head_sums(x, ones2):
    tiles = [jnp.dot(_split2(x[:, c:c + PAIR_W]), ones2, preferred_element_type=jnp.float32)
             for c in range(0, WIDTH_A, PAIR_W)]
    return jnp.concatenate(tiles, axis=1)


def _softplus(z):
    return jnp.maximum(z, 0.0) + jnp.log1p(jnp.exp(-jnp.abs(z)))


def _rwkv_pre_kernel(seg_ref, prev_ref, shift_ref, mu_ref, vec_ref, wa_ref, gup_ref, ones_ref,
                     r_ref, w_ref, k_ref, v_ref, kk_ref, a_ref, g_ref, bonus_ref):
    seg = seg_ref[0]
    tb = seg.shape[0]
    before = jnp.where(pl.program_id(1) == 0, shift_ref[0], prev_ref[0, 7:8, :])
    row = lax.broadcasted_iota(jnp.int32, seg.shape, 0)
    shifted = jnp.where(row == 0, before, pltpu.roll(seg, 1, axis=0))
    xm = seg + mu_ref[...] * (shifted - seg)
    r = xm[:, :WIDTH_A]
    k = xm[:, WIDTH_A:2 * WIDTH_A]
    v = xm[:, 2 * WIDTH_A:3 * WIDTH_A]
    wa = xm[:, 3 * WIDTH_A:3 * WIDTH_A + LORA_W + LORA_A]
    gl = xm[:, 3 * WIDTH_A + LORA_W + LORA_A:]
    lane = lax.broadcasted_iota(jnp.int32, wa.shape, 1)
    wa = jnp.where(lane < LORA_W, jnp.tanh(wa), wa).astype(jnp.bfloat16)
    lora = jnp.dot(wa, wa_ref[...], preferred_element_type=jnp.float32)
    w0, a0, k_k, k_a, r_k = (vec_ref[i:i + 1, :] for i in range(5))
    w_log = -_softplus(-(w0 + lora[:, :WIDTH_A])) - 0.5
    log_decay = -jnp.exp(w_log)
    a = jax.nn.sigmoid(a0 + lora[:, WIDTH_A:])
    g = jnp.dot(jax.nn.sigmoid(gl).astype(jnp.bfloat16), gup_ref[...], preferred_element_type=jnp.float32)
    ones2 = ones_ref[...]
    kk = k * k_k
    kk = kk * lax.rsqrt(jnp.maximum(_head_sums(kk * kk, ones2), 1e-24))
    kf = k * (1.0 + (a - 1.0) * k_a)
    r_ref[0], w_ref[0], k_ref[0], v_ref[0], kk_ref[0], a_ref[0], g_ref[0] = r, log_decay, kf, v, kk, a, g
    bonus_ref[0] = _head_sums(r * kf * r_k, ones2) * v


def _block_ones2():
    lane_head = jnp.arange(PAIR_W) // HEAD_A
    return (jnp.tile(lane_head, 2)[:, None] == lane_head[None, :]).astype(jnp.bfloat16)


def rwkv7_pre(seg, prev_row, p, tb=256):
    B, T, _ = seg.shape
    tb = min(tb, T)
    zeros = jnp.zeros((LORA_W, WIDTH_A), jnp.float32)
    wa_up = jnp.concatenate([jnp.concatenate([p['w_lora_up'], zeros], 1),
                             jnp.concatenate([zeros, p['a_lora_up']], 1)], 0).astype(jnp.bfloat16)
    vecs = jnp.stack([p['w0'], p['a0'], p['k_k'], p['k_a'], p['r_k'].reshape(WIDTH_A)])
    out = jax.ShapeDtypeStruct((B, T, WIDTH_A), jnp.float32)
    ospec = pl.BlockSpec((1, tb, WIDTH_A), lambda b, t: (b, t, 0))
    full = lambda shape: pl.BlockSpec(shape, lambda b, t: (0,) * len(shape))
    return pl.pallas_call(
        _rwkv_pre_kernel,
        out_shape=(out,) * 8,
        grid=(B, T // tb),
        in_specs=[pl.BlockSpec((1, tb, C_RWKV), lambda b, t: (b, t, 0)),
                  pl.BlockSpec((1, 8, C_RWKV), lambda b, t: (b, jnp.maximum(t * (tb // 8) - 1, 0), 0)),
                  pl.BlockSpec((1, 1, C_RWKV), lambda b, t: (b, 0, 0)),
                  full((1, C_RWKV)), full((5, WIDTH_A)), full((LORA_W + LORA_A, 2 * WIDTH_A)),
                  full((LORA_G, WIDTH_A)), full((2 * PAIR_W, PAIR_W))],
        out_specs=(ospec,) * 8,
        compiler_params=pltpu.CompilerParams(
            dimension_semantics=("arbitrary", "arbitrary"), vmem_limit_bytes=VMEM_LIMIT),
        name="rwkv7_pre",
    )(seg, seg, prev_row, p['shift_mu'].reshape(1, C_RWKV), vecs, wa_up,
      p['g_lora_up'].astype(jnp.bfloat16), _block_ones2())


SWA_BAND = WINDOW + CHUNK


def _swa_kernel(q_ref, k_ref, v_ref, kp_ref, vp_ref, bias_ref, sink_ref, o_ref, *, mask_start):
    n_chunks = q_ref.shape[1] // CHUNK
    k_all = jnp.concatenate([kp_ref[0], k_ref[0]], axis=0).astype(jnp.bfloat16)
    v_all = jnp.concatenate([vp_ref[0], v_ref[0]], axis=0).astype(jnp.bfloat16)
    first = pl.program_id(1) == 0
    key_chunk = lax.broadcasted_iota(jnp.int32, (GROUP_B * CHUNK, SWA_BAND), 1) // CHUNK
    for c in range(n_chunks):
        q_c = q_ref[0, c * CHUNK:(c + 1) * CHUNK, :].astype(jnp.bfloat16)
        k_c = k_all[c * CHUNK:c * CHUNK + SWA_BAND]
        v_c = v_all[c * CHUNK:c * CHUNK + SWA_BAND]
        dead = jnp.logical_and(first, key_chunk + (c - WIN_CHUNKS) < 0) if (mask_start and c < WIN_CHUNKS) else None
        outs = []
        for kv in range(N_KV_B):
            qg = jnp.concatenate([q_c[:, (kv * GROUP_B + g) * HEAD_B:(kv * GROUP_B + g + 1) * HEAD_B]
                                  for g in range(GROUP_B)], axis=0)
            s = lax.dot_general(qg, k_c[:, kv * HEAD_B:(kv + 1) * HEAD_B],
                                (((1,), (1,)), ((), ())), preferred_element_type=jnp.float32)
            s = s * (HEAD_B ** -0.5) + bias_ref[kv]
            if dead is not None:
                s = jnp.where(dead, NEG_INF, s)
            sink = sink_ref[kv * GROUP_B * CHUNK:(kv + 1) * GROUP_B * CHUNK, 0:1]
            m = jnp.maximum(jnp.max(s, axis=-1, keepdims=True), sink)
            e = jnp.exp(s - m)
            pr = e / (jnp.sum(e, axis=-1, keepdims=True) + jnp.exp(sink - m))
            og = jnp.dot(pr.astype(jnp.bfloat16), v_c[:, kv * HEAD_B:(kv + 1) * HEAD_B],
                         preferred_element_type=jnp.float32)
            outs.extend(og[g * CHUNK:(g + 1) * CHUNK] for g in range(GROUP_B))
        o_ref[0, c * CHUNK:(c + 1) * CHUNK, :] = jnp.concatenate(outs, axis=1)


def swa_attention(seg_b, prev_k, prev_v, sinks, *, prev_is_seq, qb=512):
    B, T, _ = seg_b.shape
    qb = min(qb, T)
    slopes = 2.0 ** (-8.0 * jnp.arange(1, N_HEADS_B + 1, dtype=jnp.float32) / N_HEADS_B)
    dist = jnp.abs(jnp.arange(CHUNK)[:, None] - (jnp.arange(SWA_BAND) - WINDOW)[None, :]).astype(jnp.float32)
    bias = (-slopes[:, None, None] * dist).reshape(N_KV_B, GROUP_B * CHUNK, SWA_BAND)
    sink_tab = jnp.broadcast_to(jnp.repeat(sinks.astype(jnp.float32), CHUNK)[:, None], (N_HEADS_B * CHUNK, 128))
    kcol, vcol = WIDTH_B // KV_WIDTH_B, WIDTH_B // KV_WIDTH_B + 1
    if prev_is_seq:
        per = qb // WINDOW
        kp_spec = pl.BlockSpec((1, WINDOW, KV_WIDTH_B), lambda b, i: (b, jnp.maximum(i * per - 1, 0), kcol))
        vp_spec = pl.BlockSpec((1, WINDOW, KV_WIDTH_B), lambda b, i: (b, jnp.maximum(i * per - 1, 0), vcol))
    else:
        kp_spec = vp_spec = pl.BlockSpec((1, WINDOW, KV_WIDTH_B), lambda b, i: (b, 0, 0))
    return pl.pallas_call(
        functools.partial(_swa_kernel, mask_start=prev_is_seq),
        out_shape=jax.ShapeDtypeStruct((B, T, WIDTH_B), jnp.float32),
        grid=(B, T // qb),
        in_specs=[pl.BlockSpec((1, qb, WIDTH_B), lambda b, i: (b, i, 0)),
                  pl.BlockSpec((1, qb, KV_WIDTH_B), lambda b, i: (b, i, kcol)),
                  pl.BlockSpec((1, qb, KV_WIDTH_B), lambda b, i: (b, i, vcol)),
                  kp_spec, vp_spec,
                  pl.BlockSpec((N_KV_B, GROUP_B * CHUNK, SWA_BAND), lambda b, i: (0, 0, 0)),
                  pl.BlockSpec((N_HEADS_B * CHUNK, 128), lambda b, i: (0, 0))],
        out_specs=pl.BlockSpec((1, qb, WIDTH_B), lambda b, i: (b, i, 0)),
        compiler_params=pltpu.CompilerParams(
            dimension_semantics=("arbitrary", "arbitrary"), vmem_limit_bytes=VMEM_LIMIT),
        name="swa_attention",
    )(seg_b, seg_b, seg_b, prev_k, prev_v, bias, sink_tab)


def _layer_norm_rows(h, g, b, eps=1e-5):
    mu = jnp.mean(h, axis=-1, keepdims=True)
    d = h - mu
    var = jnp.mean(d * d, axis=-1, keepdims=True)
    return d * lax.rsqrt(var + eps) * g + b


def _const_spec(shape):
    return pl.BlockSpec(shape, lambda *_: (0,) * len(shape), pipeline_mode=pl.Buffered(1))


def _merge_kernel(x_ref, y_ref, bonus_ref, g_ref, ob_ref, gates_ref, lnx_ref, ln1_ref, ones_ref,
                  pa_ref, pb_ref, wout_ref, o_ref):
    ones2 = ones_ref[...]
    y = y_ref[...]
    mean = _head_sums(y, ones2) * (1.0 / HEAD_A)
    d = y - mean
    var = _head_sums(d * d, ones2) * (1.0 / HEAD_A)
    yn = d * lax.rsqrt(var + GN_EPS) * lnx_ref[0:1, :] + lnx_ref[1:2, :]
    o_a = ((yn + bonus_ref[...]) * g_ref[...]).astype(jnp.bfloat16)
    br_a = jnp.dot(o_a, pa_ref[...], preferred_element_type=jnp.float32)
    br_b = jnp.dot(ob_ref[...].astype(jnp.bfloat16), pb_ref[...], preferred_element_type=jnp.float32)
    gates = jax.nn.sigmoid(gates_ref[...])
    merged = gates[:, :D_MODEL] * br_a + gates[:, D_MODEL:] * br_b
    h = ALPHA * x_ref[...] + jnp.dot(merged.astype(jnp.bfloat16), wout_ref[...], preferred_element_type=jnp.float32)
    o_ref[...] = _layer_norm_rows(h, ln1_ref[0:1, :], ln1_ref[1:2, :])


def branch_merge(x, y, bonus, g, o_b, gates, p, tm=256):
    n = x.shape[0]
    tm = _pick(n, (tm, 128, 64))
    rows = lambda w: pl.BlockSpec((tm, w), lambda i: (i, 0))
    return pl.pallas_call(
        _merge_kernel,
        out_shape=jax.ShapeDtypeStruct((n, D_MODEL), jnp.float32),
        grid=(n // tm,),
        in_specs=[rows(D_MODEL), rows(WIDTH_A), rows(WIDTH_A), rows(WIDTH_A), rows(WIDTH_B), rows(C_GATE),
                  _const_spec((2, WIDTH_A)), _const_spec((2, D_MODEL)), _const_spec((2 * PAIR_W, PAIR_W)),
                  _const_spec((WIDTH_A, D_MODEL)), _const_spec((WIDTH_B, D_MODEL)), _const_spec((D_MODEL, D_MODEL))],
        out_specs=rows(D_MODEL),
        compiler_params=pltpu.CompilerParams(dimension_semantics=("arbitrary",), vmem_limit_bytes=VMEM_LIMIT),
        name="branch_merge",
    )(x, y, bonus, g, o_b, gates, jnp.stack([p['lnx_g'], p['lnx_b']]), jnp.stack([p['ln1_g'], p['ln1_b']]),
      _block_ones2(), p['w_branch_a16'], p['w_branch_b16'], p['w_out16'])


def _mem_kernel(x_ref, mk_ref, mv_ref, ln2_ref, wq_ref, wo_ref, pwq_ref, o_ref, q_ref):
    x = x_ref[0]
    qm = jnp.dot(x.astype(jnp.bfloat16), wq_ref[...], preferred_element_type=jnp.float32).astype(jnp.bfloat16)
    mk = mk_ref[0].astype(jnp.bfloat16)
    mv = mv_ref[0].astype(jnp.bfloat16)
    outs = []
    for h in range(N_HEADS_M):
        cols = slice(h * HEAD_M, (h + 1) * HEAD_M)
        s = lax.dot_general(qm[:, cols], mk[:, cols], (((1,), (1,)), ((), ())),
                            preferred_element_type=jnp.float32) * (HEAD_M ** -0.5)
        e = jnp.exp(s - jnp.max(s, axis=-1, keepdims=True))
        pr = e / jnp.sum(e, axis=-1, keepdims=True)
        outs.append(jnp.dot(pr.astype(jnp.bfloat16), mv[:, cols], preferred_element_type=jnp.float32))
    o = jnp.concatenate(outs, axis=1).astype(jnp.bfloat16)
    h2 = ALPHA * x + jnp.dot(o, wo_ref[...], preferred_element_type=jnp.float32)
    x2 = _layer_norm_rows(h2, ln2_ref[0:1, :], ln2_ref[1:2, :])
    o_ref[0] = x2
    q_ref[0] = jnp.dot(x2.astype(jnp.bfloat16), pwq_ref[...], preferred_element_type=jnp.float32)


def mem_block(x, mk, mv, p, tm=256):
    B, T, _ = x.shape
    tm = _pick(T, (tm, 128, 64))
    wm = N_HEADS_M * HEAD_M
    rows = lambda w: pl.BlockSpec((1, tm, w), lambda b, i: (b, i, 0))
    mem = pl.BlockSpec((1, N_MEM, wm), lambda b, i: (b, 0, 0))
    out = jax.ShapeDtypeStruct((B, T, D_MODEL), jnp.float32)
    return pl.pallas_call(
        _mem_kernel,
        out_shape=(out, jax.ShapeDtypeStruct((B, T, PEER_HEADS * D_KEY), jnp.float32)),
        grid=(B, T // tm),
        in_specs=[rows(D_MODEL), mem, mem, _const_spec((2, D_MODEL)), _const_spec((D_MODEL, wm)),
                  _const_spec((wm, D_MODEL)), _const_spec((D_MODEL, PEER_HEADS * D_KEY))],
        out_specs=(rows(D_MODEL), rows(PEER_HEADS * D_KEY)),
        compiler_params=pltpu.CompilerParams(
            dimension_semantics=("arbitrary", "arbitrary"), vmem_limit_bytes=VMEM_LIMIT),
        name="mem_block",
    )(x, mk, mv, jnp.stack([p['ln2_g'], p['ln2_b']]), p['wq_mem16'], p['wo_mem16'], p['peer_wq16'])


ROUTE_TQ = 256


def _top_values(s, k):
    n_rows = s.shape[0]
    iota = lax.broadcasted_iota(jnp.int32, s.shape, 0).astype(jnp.float32)
    rank = jnp.full(s.shape, float(k), jnp.float32)
    out = []
    for step in range(k):
        m = jnp.max(s, axis=0, keepdims=True)
        first = jnp.min(jnp.where(s == m, iota, float(n_rows)), axis=0, keepdims=True)
        taken = iota == first
        s = jnp.where(taken, -jnp.inf, s)
        rank = jnp.where(taken, float(step), rank)
        out.append(m)
    return out, rank


def _peer_route_kernel(q_ref, keys_ref, n1_ref, c1_ref, rank2_ref, e2_ref):
    half = D_KEY // 2
    for h in range(PEER_HEADS):
        tops, scores = [], []
        for p in range(2):
            c0 = (2 * h + p) * half
            qs = q_ref[:, c0:c0 + half].astype(jnp.bfloat16)
            s = lax.dot_general(keys_ref[2 * h + p], qs, (((1,), (1,)), ((), ())),
                                preferred_element_type=jnp.float32)
            scores.append(s)
            tops.append(_top_values(s, TOPK))
        (t1, _), (t2, rank2) = tops
        t2all = jnp.concatenate(t2, axis=0)
        t1all = jnp.concatenate(t1, axis=0)
        rank8 = lax.broadcasted_iota(jnp.int32, (8, t1all.shape[1]), 0)
        cand_rows = [t1all + t2[0], t1all[:8] + t2[1]]
        for b in range(2, 8):
            cand_rows.append(jnp.where(rank8 < TOPK // (b + 1), t1all[:8] + t2[b], -jnp.inf))
        cand_rows.append(t1[0] + t2all[8:])
        sc, _ = _top_values(jnp.concatenate(cand_rows, axis=0), TOPK)
        z = jnp.zeros_like(sc[0])
        for kq in range(TOPK):
            z = z + jnp.exp(sc[kq] - sc[0])
        theta = sc[TOPK - 1]
        n1 = jnp.zeros_like(scores[0])
        for a in range(TOPK):
            n_a = jnp.sum((t1[a] + t2all >= theta).astype(jnp.float32), axis=0, keepdims=True)
            n1 = jnp.where(scores[0] == t1[a], n_a, n1)
        n1_ref[h] = n1
        rank2_ref[h] = rank2.astype(jnp.bfloat16)
        c1_ref[h] = jnp.exp(scores[0] - t1[0]) / z
        e2_ref[h] = jnp.exp(scores[1] - t2[0]).astype(jnp.bfloat16)


def peer_route(q, keys16):
    n = q.shape[0]
    tq = _pick(n, (ROUTE_TQ, 128))
    big = jax.ShapeDtypeStruct((PEER_HEADS, N_KEYS, n), jnp.float32)
    big16 = jax.ShapeDtypeStruct((PEER_HEADS, N_KEYS, n), jnp.bfloat16)
    bspec = pl.BlockSpec((PEER_HEADS, N_KEYS, tq), lambda i: (0, 0, i))
    return pl.pallas_call(
        _peer_route_kernel,
        out_shape=(big, big, big16, big16),
        grid=(n // tq,),
        in_specs=[pl.BlockSpec((tq, PEER_HEADS * D_KEY), lambda i: (i, 0)),
                  pl.BlockSpec((2 * PEER_HEADS, N_KEYS, D_KEY // 2), lambda i: (0, 0, 0))],
        out_specs=(bspec, bspec, bspec, bspec),
        compiler_params=pltpu.CompilerParams(
            dimension_semantics=("arbitrary",), vmem_limit_bytes=VMEM_LIMIT),
        name="peer_route",
    )(q, keys16)


PEER_TM = 512
PEER_ROWS = 8
PEER_TE = PEER_ROWS * N_KEYS
PEER_STRIP = 256


def _gelu(x):
    return 0.5 * x * (1.0 + lax.erf(x * (2.0 ** -0.5)))


def _peer_mix_kernel(x_ref, u_ref, vt_ref, n1_ref, c1_ref, rank2_ref, e2_ref, ln3_ref, o_ref,
                     x16_ref, h_ref, acc_ref):
    j = pl.program_id(1)

    @pl.when(j == 0)
    def _():
        x16_ref[...] = x_ref[...].astype(jnp.bfloat16)
        acc_ref[...] = jnp.zeros_like(acc_ref)

    a_t = lax.dot_general(u_ref[...], x16_ref[...], (((1,), (1,)), ((), ())),
                          preferred_element_type=jnp.float32)

    def sublane_bcast16(row):
        tile = jnp.broadcast_to(row, (16, row.shape[1])).astype(jnp.bfloat16)
        return pltpu.repeat(tile, N_KEYS // 16, axis=0)

    for r in range(PEER_ROWS):
        rows = slice(r * N_KEYS, (r + 1) * N_KEYS)
        for c0 in range(0, x_ref.shape[0], 128):
            cols = slice(c0, c0 + 128)
            gate = None
            for h in range(PEER_HEADS):
                keep = rank2_ref[h, :, cols] < sublane_bcast16(n1_ref[h, r:r + 1, cols])
                w = jnp.where(keep, e2_ref[h, :, cols], jnp.zeros((), jnp.bfloat16))
                w = w * sublane_bcast16(c1_ref[h, r:r + 1, cols])
                gate = w if gate is None else gate + w
            h_ref[rows, cols] = gate * _gelu(a_t[rows, cols]).astype(jnp.bfloat16)
    acc_ref[...] += jnp.dot(vt_ref[0], h_ref[...], preferred_element_type=jnp.float32)

    @pl.when(j == pl.num_programs(1) - 1)
    def _():
        o_ref[...] = _layer_norm_rows(ALPHA * x_ref[...] + acc_ref[...].T, ln3_ref[0:1, :], ln3_ref[1:2, :])


def peer_mix(x, u16, vt16, n1, c1, rank2, e2, ln3):
    n, d = x.shape
    tm = _pick(n, (PEER_TM, 256, 128))
    n_exp = u16.shape[0]
    row_spec = pl.BlockSpec((PEER_HEADS, PEER_ROWS, tm), lambda i, j: (0, j, i))
    all_spec = pl.BlockSpec((PEER_HEADS, N_KEYS, tm), lambda i, j: (0, 0, i), pipeline_mode=pl.Buffered(1))
    return pl.pallas_call(
        _peer_mix_kernel,
        out_shape=jax.ShapeDtypeStruct((n, d), jnp.float32),
        grid=(n // tm, n_exp // PEER_TE),
        in_specs=[pl.BlockSpec((tm, d), lambda i, j: (i, 0), pipeline_mode=pl.Buffered(1)),
                  pl.BlockSpec((PEER_TE, d), lambda i, j: (j, 0)),
                  pl.BlockSpec((1, d, PEER_TE), lambda i, j: (j, 0, 0)),
                  row_spec, row_spec, all_spec, all_spec, _const_spec((2, d))],
        out_specs=pl.BlockSpec((tm, d), lambda i, j: (i, 0)),
        scratch_shapes=[pltpu.VMEM((tm, d), jnp.bfloat16), pltpu.VMEM((PEER_TE, tm), jnp.bfloat16),
                        pltpu.VMEM((d, tm), jnp.float32)],
        compiler_params=pltpu.CompilerParams(
            dimension_semantics=("arbitrary", "arbitrary"), vmem_limit_bytes=VMEM_LIMIT),
        name="peer_mix",
    )(x, u16, vt16, n1, c1, rank2, e2, ln3)


def peer_block(x, q, p):
    stats = peer_route(q, p['peer_keys16'])
    return peer_mix(x, p['peer_u16'], p['peer_vt16'], *stats, jnp.stack([p['ln3_g'], p['ln3_b']]))


def layer_norm(x, g, b, eps=1e-5):
    xf = x.astype(jnp.float32)
    mu = jnp.mean(xf, -1, keepdims=True)
    var = jnp.mean(jnp.square(xf - mu), -1, keepdims=True)
    return ((xf - mu) * lax.rsqrt(var + eps) * g + b).astype(x.dtype)


def rwkv7_scan(r, w, k, v, kk, a, state0):
    def step(S, inp):
        r_t, w_t, k_t, v_t, kk_t, a_t = inp
        sa = jnp.einsum('bhij,bhj->bhi', S, -kk_t)
        S = (S * w_t[:, :, None, :] + sa[..., None] * (kk_t * a_t)[:, :, None, :]
             + v_t[..., None] * k_t[:, :, None, :])
        y = jnp.einsum('bhij,bhj->bhi', S, r_t)
        return S, y
    xs = tuple(jnp.moveaxis(t, 1, 0) for t in (r, w, k, v, kk, a))
    S, ys = lax.scan(step, state0.astype(jnp.float32), xs)
    return jnp.moveaxis(ys, 0, 1), S


def rwkv7_branch(seg, prev_row, state0, p):
    B, T, _ = seg.shape
    f32 = jnp.float32
    shifted = jnp.concatenate([prev_row.astype(seg.dtype), seg[:, :-1]], axis=1)
    xm = seg + p['shift_mu'] * (shifted - seg)
    r, k, v, wl, al, gl = jnp.split(xm, SPLIT_A, axis=-1)
    w_log = -jax.nn.softplus(-(p['w0'] + jnp.tanh(wl) @ p['w_lora_up']).astype(f32)) - 0.5
    decay = jnp.exp(-jnp.exp(w_log))
    a = jax.nn.sigmoid((p['a0'] + al @ p['a_lora_up']).astype(f32))
    g = jax.nn.sigmoid(gl) @ p['g_lora_up']
    heads = lambda t: t.reshape(B, T, N_HEADS_A, HEAD_A)
    kk = heads((k * p['k_k']).astype(f32))
    kk = kk * lax.rsqrt(jnp.maximum(jnp.sum(kk * kk, -1, keepdims=True), 1e-24))
    kf = k.astype(f32) * (1.0 + (a - 1.0) * p['k_a'].astype(f32))
    rh, kh, vh, ah, wh = heads(r.astype(f32)), heads(kf), heads(v.astype(f32)), heads(a), heads(decay)
    flat = lambda t: t.reshape(B, T, WIDTH_A)
    y, s_final = rwkv7_scan_pallas(flat(rh), flat(wh), flat(kh), flat(vh), flat(kk), flat(ah), state0)
    y = heads(y)
    mean = jnp.mean(y, -1, keepdims=True)
    var = jnp.mean(jnp.square(y - mean), -1, keepdims=True)
    yn = ((y - mean) * lax.rsqrt(var + GN_EPS)).reshape(B, T, WIDTH_A) * p['lnx_g'] + p['lnx_b']
    bonus = jnp.sum(rh * kh * p['r_k'].astype(f32), -1, keepdims=True) * vh
    out = (yn + bonus.reshape(B, T, WIDTH_A)) * g
    return out.astype(seg.dtype), s_final, seg[:, -1:]


def alibi_bias(q_pos, k_pos):
    slopes = 2.0 ** (-8.0 * jnp.arange(1, N_HEADS_B + 1, dtype=jnp.float32) / N_HEADS_B)
    dist = jnp.abs(q_pos[:, None] - k_pos[None, :]).astype(jnp.float32)
    return -slopes.reshape(N_KV_B, GROUP_B, 1, 1) * dist


def sink_attention(q, k, v, bias, mask, sinks):
    s = jnp.einsum('bnqhgd,bnjhd->bnhgqj', q, k).astype(jnp.float32) * (HEAD_B ** -0.5) + bias
    if mask is not None:
        s = jnp.where(mask, s, NEG_INF)
    sink = sinks.astype(jnp.float32).reshape(N_KV_B, GROUP_B, 1, 1)
    m = jnp.maximum(jnp.max(s, -1, keepdims=True), sink)
    p = jnp.exp(s - m)
    p = p / (jnp.sum(p, -1, keepdims=True) + jnp.exp(sink - m))
    return jnp.einsum('bnhgqj,bnjhd->bnqhgd', p.astype(v.dtype), v)


def swa_prompt(q, k, v, sinks):
    B, S = q.shape[:2]
    NC = S // CHUNK
    band = (WIN_CHUNKS + 1) * CHUNK
    qc = q.reshape(B, NC, CHUNK, N_KV_B, GROUP_B, HEAD_B)
    pad = ((0, 0), (WIN_CHUNKS, 0), (0, 0), (0, 0), (0, 0))
    kp = jnp.pad(k.reshape(B, NC, CHUNK, N_KV_B, HEAD_B), pad)
    vp = jnp.pad(v.reshape(B, NC, CHUNK, N_KV_B, HEAD_B), pad)
    kb = jnp.concatenate([kp[:, o:o + NC] for o in range(WIN_CHUNKS + 1)], axis=2)
    vb = jnp.concatenate([vp[:, o:o + NC] for o in range(WIN_CHUNKS + 1)], axis=2)
    bias = alibi_bias(jnp.arange(CHUNK), jnp.arange(band) - WIN_CHUNKS * CHUNK)
    key_chunk = jnp.arange(NC)[:, None] - WIN_CHUNKS + (jnp.arange(band) // CHUNK)[None, :]
    mask = (key_chunk >= 0)[None, :, None, None, None, :]
    o = sink_attention(qc, kb, vb, bias, mask, sinks)
    return o.reshape(B, S, WIDTH_B)


def swa_sample(q, k, v, k_cache, v_cache, sinks):
    B, T = q.shape[:2]
    Lc = k_cache.shape[1]
    k_all = jnp.concatenate([k_cache.astype(k.dtype), k], axis=1)
    v_all = jnp.concatenate([v_cache.astype(v.dtype), v], axis=1)
    bias = alibi_bias(jnp.arange(T), jnp.arange(Lc + T) - Lc)
    o = sink_attention(q[:, None], k_all[:, None], v_all[:, None], bias, None, sinks)
    return o.reshape(B, T, WIDTH_B), k_all[:, -Lc:], v_all[:, -Lc:]


def mem_attention(x, mk, mv, p):
    B, T, _ = x.shape
    q = mm(x, p['wq_mem16']).reshape(B, T, N_HEADS_M, HEAD_M)
    s = jnp.einsum('bthd,bmhd->bhtm', q, mk.astype(q.dtype)).astype(jnp.float32) * (HEAD_M ** -0.5)
    pr = jax.nn.softmax(s, axis=-1).astype(x.dtype)
    o = jnp.einsum('bhtm,bmhd->bthd', pr, mv.astype(x.dtype)).reshape(B, T, N_HEADS_M * HEAD_M)
    return mm(o, p['wo_mem16'])


def peer_ffn(x, p):
    B, T, D = x.shape
    n = B * T
    nblk = -(-n // PEER_BLOCK)
    xt = x.reshape(nblk, PEER_BLOCK, D)
    qa = mm(x, p['peer_wq16']).reshape(nblk, PEER_BLOCK, PEER_HEADS * D_KEY)
    sub_keys, u_tab, v_tab = p['peer_sub_keys'], p['peer_u'], p['peer_v']

    def peer_block(args):
        xb, qb = args
        q = qb.reshape(PEER_BLOCK, PEER_HEADS, 2, D_KEY // 2)
        s = jnp.einsum('thpd,hpnd->thpn', q, sub_keys).astype(jnp.float32)
        s1, i1 = lax.top_k(s[:, :, 0], TOPK)
        s2, i2 = lax.top_k(s[:, :, 1], TOPK)
        cand = (s1[..., :, None] + s2[..., None, :]).reshape(PEER_BLOCK, PEER_HEADS, TOPK * TOPK)
        cidx = (i1[..., :, None] * N_KEYS + i2[..., None, :]).reshape(PEER_BLOCK, PEER_HEADS, TOPK * TOPK)
        sc, pos = lax.top_k(cand, TOPK)
        e = jnp.take_along_axis(cidx, pos, axis=-1)
        g = jax.nn.softmax(sc, axis=-1)
        act = jax.nn.gelu(jnp.einsum('td,thkd->thk', xb, u_tab[e]).astype(jnp.float32), approximate=False)
        return jnp.einsum('thk,thkd->td', (g * act).astype(xb.dtype), v_tab[e])

    y = lax.map(peer_block, (xt, qa)).reshape(nblk * PEER_BLOCK, D)
    return y.reshape(B, T, D)


def trunk_layer(x, p, mem_k, mem_v, rwkv_state, shift_row, swa_k_cache, swa_v_cache):
    B, T, _ = x.shape
    n = B * T
    x2d = x.reshape(n, D_MODEL)
    seg_a = matmul(x2d, p['w_in_a16']).reshape(B, T, C_RWKV)
    seg_b = matmul(x2d, p['w_in_b16']).reshape(B, T, C_SWA)
    gates = matmul(x2d, p['w_in_g16'])
    r, w, k, v, kk, a, g, bonus = rwkv7_pre(seg_a, shift_row.astype(jnp.float32), p)
    y, rwkv_new = rwkv7_chunked(r, w, k, v, kk, a, rwkv_state)
    shift_new = seg_a[:, -1:]
    k_new = seg_b[:, :, WIDTH_B:WIDTH_B + KV_WIDTH_B].reshape(B, T, N_KV_B, HEAD_B)
    v_new = seg_b[:, :, WIDTH_B + KV_WIDTH_B:].reshape(B, T, N_KV_B, HEAD_B)
    if swa_k_cache is None:
        o_b = swa_attention(seg_b, seg_b, seg_b, p['attn_sinks'], prev_is_seq=True)
        swa_k_new, swa_v_new = k_new[:, -WINDOW:], v_new[:, -WINDOW:]
    else:
        o_b = swa_attention(seg_b, swa_k_cache.reshape(B, WINDOW, KV_WIDTH_B),
                            swa_v_cache.reshape(B, WINDOW, KV_WIDTH_B), p['attn_sinks'], prev_is_seq=False)
        swa_k_new = jnp.concatenate([swa_k_cache, k_new], axis=1)[:, -WINDOW:]
        swa_v_new = jnp.concatenate([swa_v_cache, v_new], axis=1)[:, -WINDOW:]
    flat = lambda t: t.reshape(n, t.shape[-1])
    x1 = branch_merge(x2d, flat(y), flat(bonus), flat(g), flat(o_b), gates, p)
    wm = N_HEADS_M * HEAD_M
    x2, q = mem_block(x1.reshape(B, T, D_MODEL), mem_k.reshape(-1, N_MEM, wm), mem_v.reshape(-1, N_MEM, wm), p)
    x3 = peer_block(flat(x2), flat(q), p)
    return x3.reshape(B, T, D_MODEL), rwkv_new, shift_new, swa_k_new, swa_v_new


_MM_WEIGHTS = ('w_branch_a', 'w_branch_b', 'w_out', 'wq_mem', 'wk_mem', 'wv_mem', 'wo_mem', 'peer_wq')


def kernel(x_prompt, x_sample, state_rwkv, state_shift, cache_swa_k, cache_swa_v, cache_mem_k, cache_mem_v, mem_prompt, w_in, shift_mu, w0, w_lora_up, a0, a_lora_up, g_lora_up, k_k, k_a, r_k, lnx_g, lnx_b, attn_sinks, w_branch_a, w_branch_b, w_out, ln1_g, ln1_b, wq_mem, wk_mem, wv_mem, wo_mem, ln2_g, ln2_b, peer_wq, peer_sub_keys, peer_u, peer_v, ln3_g, ln3_b):
    params = {
        'w_in': w_in, 'shift_mu': shift_mu, 'w0': w0, 'w_lora_up': w_lora_up, 'a0': a0,
        'a_lora_up': a_lora_up, 'g_lora_up': g_lora_up, 'k_k': k_k, 'k_a': k_a, 'r_k': r_k,
        'lnx_g': lnx_g, 'lnx_b': lnx_b, 'attn_sinks': attn_sinks, 'w_branch_a': w_branch_a,
        'w_branch_b': w_branch_b, 'w_out': w_out, 'ln1_g': ln1_g, 'ln1_b': ln1_b, 'wq_mem': wq_mem,
        'wk_mem': wk_mem, 'wv_mem': wv_mem, 'wo_mem': wo_mem, 'ln2_g': ln2_g, 'ln2_b': ln2_b,
        'peer_wq': peer_wq, 'peer_sub_keys': peer_sub_keys, 'peer_u': peer_u, 'peer_v': peer_v,
        'ln3_g': ln3_g, 'ln3_b': ln3_b,
    }
    B = x_prompt.shape[0]
    rwkv0 = jnp.zeros((B, N_HEADS_A, HEAD_A, HEAD_A), jnp.float32)
    shift0 = jnp.zeros((B, 1, C_RWKV), x_prompt.dtype)
    xp, xs = x_prompt, x_sample
    p_rw, p_sh, p_k, p_v, p_mk, p_mv = [], [], [], [], [], []
    s_rw, s_sh, s_k, s_v = [], [], [], []
    for l in range(DEPTH):
        p = {name: arr[l] for name, arr in params.items()}
        for name in _MM_WEIGHTS:
            p[name + '16'] = p[name].astype(jnp.bfloat16)
        w_in16 = p['w_in'].astype(jnp.bfloat16)
        p['w_in_a16'] = w_in16[:, :C_RWKV]
        p['w_in_b16'] = w_in16[:, C_RWKV:C_RWKV + C_SWA]
        p['w_in_g16'] = w_in16[:, C_RWKV + C_SWA:]
        p['peer_keys16'] = p['peer_sub_keys'].reshape(2 * PEER_HEADS, N_KEYS, D_KEY // 2).astype(jnp.bfloat16)
        p['peer_u16'] = p['peer_u'].astype(jnp.bfloat16)
        p['peer_vt16'] = (p['peer_v'].astype(jnp.bfloat16)
                          .reshape(p['peer_v'].shape[0] // PEER_TE, PEER_TE, D_MODEL).transpose(0, 2, 1))
        mk = mm(mem_prompt, p['wk_mem16']).reshape(B, N_MEM, N_HEADS_M, HEAD_M)
        mv = mm(mem_prompt, p['wv_mem16']).reshape(B, N_MEM, N_HEADS_M, HEAD_M)
        xp, rw, sh, kn, vn = trunk_layer(xp, p, mk, mv, rwkv0, shift0, None, None)
        p_rw.append(rw); p_sh.append(sh); p_k.append(kn); p_v.append(vn); p_mk.append(mk); p_mv.append(mv)
        xs, rw, sh, kn, vn = trunk_layer(xs, p, cache_mem_k[l], cache_mem_v[l], state_rwkv[l], state_shift[l],
                                         cache_swa_k[l], cache_swa_v[l])
        s_rw.append(rw); s_sh.append(sh); s_k.append(kn); s_v.append(vn)
    return (xp, xs,
            jnp.stack(p_rw), jnp.stack(p_sh), jnp.stack(p_k), jnp.stack(p_v), jnp.stack(p_mk), jnp.stack(p_mv),
            jnp.stack(s_rw), jnp.stack(s_sh), jnp.stack(s_k), jnp.stack(s_v))
```

```python
import functools

import jax
import jax.numpy as jnp
from jax import lax
from jax.experimental import pallas as pl
from jax.experimental.pallas import tpu as pltpu

D_MODEL = 2048
DEPTH = 2
CHUNK = 64
HEAD_A = 64
N_HEADS_A = 16
WIDTH_A = N_HEADS_A * HEAD_A
LORA_W = 64
LORA_A = 64
LORA_G = 128
GN_EPS = 64e-5
SPLIT_A = [WIDTH_A, 2 * WIDTH_A, 3 * WIDTH_A, 3 * WIDTH_A + LORA_W, 3 * WIDTH_A + LORA_W + LORA_A]
C_RWKV = 3 * WIDTH_A + LORA_W + LORA_A + LORA_G
HEAD_B = 64
N_HEADS_B = 16
N_KV_B = 4
GROUP_B = N_HEADS_B // N_KV_B
WIDTH_B = N_HEADS_B * HEAD_B
KV_WIDTH_B = N_KV_B * HEAD_B
WINDOW = 128
WIN_CHUNKS = WINDOW // CHUNK
C_SWA = WIDTH_B + 2 * KV_WIDTH_B
C_GATE = 2 * D_MODEL
C_IN = C_RWKV + C_SWA + C_GATE
N_MEM = 256
N_HEADS_M = 4
HEAD_M = 128
N_KEYS = 128
PEER_HEADS = 8
D_KEY = 256
TOPK = 16
PEER_BLOCK = 128
ALPHA = (2.0 * DEPTH) ** 0.25
NEG_INF = -1e30

VMEM_LIMIT = 56 * 1024 * 1024


def _matmul_kernel(a_ref, b_ref, o_ref, a16_ref):
    @pl.when(pl.program_id(1) == 0)
    def _():
        a16_ref[...] = a_ref[...].astype(jnp.bfloat16)

    o_ref[...] = jnp.dot(a16_ref[...], b_ref[...], preferred_element_type=jnp.float32)


def _pick(n, cands):
    for c in cands:
        if n % c == 0:
            return c
    return n


def matmul(a, b16):
    m, k = a.shape
    n = b16.shape[1]
    tm = _pick(m, (512, 256, 128))
    tn = _pick(n, (2048, 1664, 1536, 1024, 512, 256, 128))
    return pl.pallas_call(
        _matmul_kernel,
        out_shape=jax.ShapeDtypeStruct((m, n), jnp.float32),
        grid=(m // tm, n // tn),
        in_specs=[pl.BlockSpec((tm, k), lambda i, j: (i, 0)),
                  pl.BlockSpec((k, tn), lambda i, j: (0, j))],
        out_specs=pl.BlockSpec((tm, tn), lambda i, j: (i, j)),
        scratch_shapes=[pltpu.VMEM((tm, k), jnp.bfloat16)],
        compiler_params=pltpu.CompilerParams(
            dimension_semantics=("arbitrary", "arbitrary"),
            vmem_limit_bytes=VMEM_LIMIT),
        name="matmul",
    )(a, b16)


def mm(x, w16):
    lead = x.shape[:-1]
    return matmul(x.reshape(-1, x.shape[-1]), w16).reshape(*lead, w16.shape[1])


N_PAIR = N_HEADS_A // 2
PAIR_W = 2 * HEAD_A


def _split2(x):
    hi = x.astype(jnp.bfloat16)
    lo = (x - hi.astype(jnp.float32)).astype(jnp.bfloat16)
    return jnp.concatenate([hi, lo], axis=1)


def _rwkv_scan_kernel(r_ref, w_ref, k_ref, v_ref, kk_ref, a_ref, s0_ref, ones_ref, eye_ref,
                      y_ref, sT_ref, s_ref, *, tb):
    tblk = pl.program_id(1)

    @pl.when(tblk == 0)
    def _():
        s_ref[...] = s0_ref[0]

    ones2 = ones_ref[...]
    eye2 = eye_ref[...]

    def segsum(tiles):
        lhs = jnp.concatenate([_split2(m) for m in tiles], axis=0)
        return jnp.dot(lhs, ones2, preferred_element_type=jnp.float32)

    def step(g, carry):
        rows = pl.ds(pl.multiple_of(g * 8, 8), 8)
        pairs = range(N_PAIR)
        sls = [pl.ds(p * PAIR_W, PAIR_W) for p in pairs]
        kk8 = [kk_ref[0, rows, sl] for sl in sls]
        kka8 = [kk8[p] * a_ref[0, rows, sls[p]] for p in pairs]
        w8 = [w_ref[0, rows, sl] for sl in sls]
        k8 = [k_ref[0, rows, sl] for sl in sls]
        v8 = [v_ref[0, rows, sl] for sl in sls]
        r8 = [r_ref[0, rows, sl] for sl in sls]
        s = [s_ref[p] for p in pairs]
        ys = [[] for _ in pairs]
        tile = lambda x, p: x[p * HEAD_A:(p + 1) * HEAD_A]
        for j in range(8):
            row = slice(j, j + 1)
            sa = segsum([s[p] * (-kk8[p][row]) for p in pairs])
            vb = segsum([eye2 * v8[p][row] for p in pairs])
            s = [s[p] * w8[p][row] + tile(sa, p) * kka8[p][row] + tile(vb, p) * k8[p][row] for p in pairs]
            yb = segsum([s[p] * r8[p][row] for p in pairs])
            for p in pairs:
                ys[p].append(jnp.sum(tile(yb, p) * eye2, axis=0, keepdims=True))
        for p in pairs:
            s_ref[p] = s[p]
            y_ref[0, rows, sls[p]] = jnp.concatenate(ys[p], axis=0)
        return carry

    lax.fori_loop(0, tb // 8, step, 0)

    @pl.when(tblk == pl.num_programs(1) - 1)
    def _():
        sT_ref[0] = s_ref[...]


def rwkv7_scan_pallas(r, w, k, v, kk, a, state0, tb=256):
    B, T, _ = r.shape
    tb = min(tb, T)
    s0 = state0.astype(jnp.float32).reshape(B, N_PAIR, 2, HEAD_A, HEAD_A)
    s0 = s0.transpose(0, 1, 3, 2, 4).reshape(B, N_PAIR, HEAD_A, PAIR_W)
    lane_head = jnp.arange(PAIR_W) // HEAD_A
    ones2 = (jnp.tile(lane_head, 2)[:, None] == lane_head[None, :]).astype(jnp.bfloat16)
    eye2 = (jnp.arange(HEAD_A)[:, None] == (jnp.arange(PAIR_W) % HEAD_A)[None, :]).astype(jnp.float32)
    seq = pl.BlockSpec((1, tb, WIDTH_A), lambda b, t: (b, t, 0))
    st = pl.BlockSpec((1, N_PAIR, HEAD_A, PAIR_W), lambda b, t: (b, 0, 0, 0))
    y, sT = pl.pallas_call(
        functools.partial(_rwkv_scan_kernel, tb=tb),
        out_shape=(jax.ShapeDtypeStruct((B, T, WIDTH_A), jnp.float32),
                   jax.ShapeDtypeStruct((B, N_PAIR, HEAD_A, PAIR_W), jnp.float32)),
        grid=(B, T // tb),
        in_specs=[seq] * 6 + [st,
                              pl.BlockSpec((2 * PAIR_W, PAIR_W), lambda b, t: (0, 0)),
                              pl.BlockSpec((HEAD_A, PAIR_W), lambda b, t: (0, 0))],
        out_specs=(seq, st),
        scratch_shapes=[pltpu.VMEM((N_PAIR, HEAD_A, PAIR_W), jnp.float32)],
        compiler_params=pltpu.CompilerParams(
            dimension_semantics=("arbitrary", "arbitrary"),
            vmem_limit_bytes=VMEM_LIMIT),
        name="rwkv7_scan",
    )(r, w, k, v, kk, a, s0, ones2, eye2)
    sT = sT.reshape(B, N_PAIR, HEAD_A, 2, HEAD_A).transpose(0, 1, 3, 2, 4)
    return y, sT.reshape(B, N_HEADS_A, HEAD_A, HEAD_A)


def _hl(x):
    hi = x.astype(jnp.bfloat16)
    return hi, (x - hi.astype(jnp.float32)).astype(jnp.bfloat16)


def _dot3(a, b):
    ah, al = _hl(a)
    bh, bl = _hl(b)
    return jnp.dot(jnp.concatenate([ah, ah, al], axis=1), jnp.concatenate([bh, bl, bh], axis=0),
                   preferred_element_type=jnp.float32)


def _dot3_nt(a, b):
    ah, al = _hl(a)
    bh, bl = _hl(b)
    return lax.dot_general(jnp.concatenate([ah, ah, al], axis=1), jnp.concatenate([bh, bl, bh], axis=1),
                           (((1,), (1,)), ((), ())), preferred_element_type=jnp.float32)


def _rwkv_chunk_kernel(r_ref, w_ref, k_ref, v_ref, kk_ref, a_ref, s0_ref, ltri_ref, ones_ref, msk_ref,
                       y_ref, sT_ref, p_ref, *, n_chunks):
    tblk = pl.program_id(1)

    @pl.when(tblk == 0)
    def _():
        p_ref[...] = s0_ref[0]

    eye2, m0, m1, strict, incl = (msk_ref[i] for i in range(5))
    bd = lambda y: jnp.concatenate([y * m0, y * m1], axis=0)

    def split3(x, axis):
        t1 = x.astype(jnp.bfloat16)
        d = x - t1.astype(jnp.float32)
        t2 = d.astype(jnp.bfloat16)
        t3 = (d - t2.astype(jnp.float32)).astype(jnp.bfloat16)
        return jnp.concatenate([t1, t2, t3], axis=axis)

    def chunk(c, carry):
        rows = pl.ds(pl.multiple_of(c * CHUNK, CHUNK), CHUNK)
        pairs = range(N_PAIR)
        sls = [pl.ds(p * PAIR_W, PAIR_W) for p in pairs]
        lw = [w_ref[0, rows, sl] for sl in sls]
        cum = [jnp.dot(ltri_ref[...], split3(lw[p], 0), preferred_element_type=jnp.float32) for p in pairs]
        g = [jnp.exp(cum[p]) for p in pairs]
        ginv = [jnp.exp(-cum[p]) for p in pairs]
        kk = [kk_ref[0, rows, sl] for sl in sls]
        kh = [kk[p] * jnp.exp(cum[p] - lw[p]) for p in pairs]
        bh = [kk[p] * a_ref[0, rows, sls[p]] * ginv[p] for p in pairs]
        kf = [k_ref[0, rows, sls[p]] * ginv[p] for p in pairs]
        rh = [r_ref[0, rows, sls[p]] * g[p] for p in pairs]
        v = [v_ref[0, rows, sl] for sl in sls]
        g_last = [g[p][CHUNK - 1:CHUNK, :] for p in pairs]
        kr = [jnp.concatenate([kh[p], rh[p]], axis=0) for p in pairs]
        gram = [_dot3_nt(kr[p], jnp.concatenate([bd(bh[p]), bd(kf[p])], axis=0)) for p in pairs]
        a_b = [gram[p][:CHUNK, :PAIR_W] * strict for p in pairs]
        a_k = [gram[p][:CHUNK, PAIR_W:] * strict for p in pairs]
        a_r = [jnp.concatenate([gram[p][CHUNK:, :PAIR_W] * incl, gram[p][CHUNK:, PAIR_W:] * incl], axis=1)
               for p in pairs]
        t_inv = [eye2 - a_b[p] * msk_ref[5] for p in pairs]
        for lvl in range(1, 6):
            half = [_dot3(t_inv[p], bd(a_b[p] * msk_ref[5 + lvl])) for p in pairs]
            t_inv = [t_inv[p] - _dot3(half[p], bd(t_inv[p])) for p in pairs]
        akv = [_dot3(a_k[p], bd(v[p])) for p in pairs]
        g_col = [jnp.dot(split3(eye2 * g_last[p], 1), ones_ref[...], preferred_element_type=jnp.float32)
                 for p in pairs]
        xt = [jnp.concatenate([bh[p] * g_last[p], kf[p] * g_last[p]], axis=0).T for p in pairs]
        p0 = [p_ref[p] for p in pairs]
        zy = [_dot3(kr[p], bd(p0[p])) for p in pairs]
        u = [-_dot3(t_inv[p], bd(zy[p][:CHUNK] + akv[p])) for p in pairs]
        y = [zy[p][CHUNK:] + _dot3(a_r[p], jnp.concatenate([bd(u[p]), bd(v[p])], axis=0)) for p in pairs]
        delta = [_dot3(xt[p], jnp.concatenate([u[p], v[p]], axis=0)) for p in pairs]
        for p in pairs:
            p_ref[p] = g_col[p] * p0[p] + delta[p][:CHUNK] * m0 + delta[p][CHUNK:] * m1
            y_ref[0, rows, sls[p]] = y[p]
        return carry

    lax.fori_loop(0, n_chunks, chunk, 0)

    @pl.when(tblk == pl.num_programs(1) - 1)
    def _():
        sT_ref[0] = p_ref[...]


def rwkv7_chunked(r, logw, k, v, kk, a, state0, tb=256):
    B, T, _ = r.shape
    tb = min(tb, T)
    s0 = state0.astype(jnp.float32).reshape(B, N_PAIR, 2, HEAD_A, HEAD_A)
    s0 = s0.transpose(0, 1, 4, 2, 3).reshape(B, N_PAIR, HEAD_A, PAIR_W)
    t_i = jnp.arange(CHUNK)[:, None]
    s_i = (jnp.arange(PAIR_W) % HEAD_A)[None, :]
    lane_head = (jnp.arange(PAIR_W) // HEAD_A)[None, :]
    ones_row = jnp.ones((CHUNK, 1), jnp.int32)
    masks = [t_i == s_i, (lane_head == 0) * ones_row, (lane_head == 1) * ones_row, s_i < t_i, s_i <= t_i]
    for m in (1, 2, 4, 8, 16, 32):
        masks.append((t_i // (2 * m) == s_i // (2 * m)) & (t_i % (2 * m) >= m) & (s_i % (2 * m) < m))
    masks = jnp.stack([mk.astype(jnp.float32) for mk in masks])
    ltri = jnp.tile((jnp.arange(CHUNK)[None, :] <= jnp.arange(CHUNK)[:, None]), (1, 3)).astype(jnp.bfloat16)
    ones3 = jnp.tile((lane_head.T == lane_head), (3, 1)).astype(jnp.bfloat16)
    seq = pl.BlockSpec((1, tb, WIDTH_A), lambda b, t: (b, t, 0))
    st = pl.BlockSpec((1, N_PAIR, HEAD_A, PAIR_W), lambda b, t: (b, 0, 0, 0))
    y, sT = pl.pallas_call(
        functools.partial(_rwkv_chunk_kernel, n_chunks=tb // CHUNK),
        out_shape=(jax.ShapeDtypeStruct((B, T, WIDTH_A), jnp.float32),
                   jax.ShapeDtypeStruct((B, N_PAIR, HEAD_A, PAIR_W), jnp.float32)),
        grid=(B, T // tb),
        in_specs=[seq] * 6 + [st, _const_spec((CHUNK, 3 * CHUNK)), _const_spec((3 * PAIR_W, PAIR_W)),
                              _const_spec((11, CHUNK, PAIR_W))],
        out_specs=(seq, st),
        scratch_shapes=[pltpu.VMEM((N_PAIR, HEAD_A, PAIR_W), jnp.float32)],
        compiler_params=pltpu.CompilerParams(
            dimension_semantics=("arbitrary", "arbitrary"), vmem_limit_bytes=VMEM_LIMIT),
        name="rwkv7_chunked",
    )(r, logw, k, v, kk, a, s0, ltri, ones3, masks)
    sT = sT.reshape(B, N_PAIR, HEAD_A, 2, HEAD_A).transpose(0, 1, 3, 4, 2)
    return y, sT.reshape(B, N_HEADS_A, HEAD_A, HEAD_A)


def _head_sums(x, ones2):
    tiles = [jnp.dot(_split2(x[:, c:c + PAIR_W]), ones2, preferred_element_type=jnp.float32)
             for c in range(0, WIDTH_A, PAIR_W)]
    return jnp.concatenate(tiles, axis=1)


def _softplus(z):
    return jnp.maximum(z, 0.0) + jnp.log1p(jnp.exp(-jnp.abs(z)))


def _rwkv_pre_kernel(seg_ref, prev_ref, shift_ref, mu_ref, vec_ref, wa_ref, gup_ref, ones_ref,
                     r_ref, w_ref, k_ref, v_ref, kk_ref, a_ref, g_ref, bonus_ref):
    seg = seg_ref[0]
    tb = seg.shape[0]
    before = jnp.where(pl.program_id(1) == 0, shift_ref[0], prev_ref[0, 7:8, :])
    row = lax.broadcasted_iota(jnp.int32, seg.shape, 0)
    shifted = jnp.where(row == 0, before, pltpu.roll(seg, 1, axis=0))
    xm = seg + mu_ref[...] * (shifted - seg)
    r = xm[:, :WIDTH_A]
    k = xm[:, WIDTH_A:2 * WIDTH_A]
    v = xm[:, 2 * WIDTH_A:3 * WIDTH_A]
    wa = xm[:, 3 * WIDTH_A:3 * WIDTH_A + LORA_W + LORA_A]
    gl = xm[:, 3 * WIDTH_A + LORA_W + LORA_A:]
    lane = lax.broadcasted_iota(jnp.int32, wa.shape, 1)
    wa = jnp.where(lane < LORA_W, jnp.tanh(wa), wa).astype(jnp.bfloat16)
    lora = jnp.dot(wa, wa_ref[...], preferred_element_type=jnp.float32)
    w0, a0, k_k, k_a, r_k = (vec_ref[i:i + 1, :] for i in range(5))
    w_log = -_softplus(-(w0 + lora[:, :WIDTH_A])) - 0.5
    log_decay = -jnp.exp(w_log)
    a = jax.nn.sigmoid(a0 + lora[:, WIDTH_A:])
    g = jnp.dot(jax.nn.sigmoid(gl).astype(jnp.bfloat16), gup_ref[...], preferred_element_type=jnp.float32)
    ones2 = ones_ref[...]
    kk = k * k_k
    kk = kk * lax.rsqrt(jnp.maximum(_head_sums(kk * kk, ones2), 1e-24))
    kf = k * (1.0 + (a - 1.0) * k_a)
    r_ref[0], w_ref[0], k_ref[0], v_ref[0], kk_ref[0], a_ref[0], g_ref[0] = r, log_decay, kf, v, kk, a, g
    bonus_ref[0] = _head_sums(r * kf * r_k, ones2) * v


def _block_ones2():
    lane_head = jnp.arange(PAIR_W) // HEAD_A
    return (jnp.tile(lane_head, 2)[:, None] == lane_head[None, :]).astype(jnp.bfloat16)


def rwkv7_pre(seg, prev_row, p, tb=256):
    B, T, _ = seg.shape
    tb = min(tb, T)
    zeros = jnp.zeros((LORA_W, WIDTH_A), jnp.float32)
    wa_up = jnp.concatenate([jnp.concatenate([p['w_lora_up'], zeros], 1),
                             jnp.concatenate([zeros, p['a_lora_up']], 1)], 0).astype(jnp.bfloat16)
    vecs = jnp.stack([p['w0'], p['a0'], p['k_k'], p['k_a'], p['r_k'].reshape(WIDTH_A)])
    out = jax.ShapeDtypeStruct((B, T, WIDTH_A), jnp.float32)
    ospec = pl.BlockSpec((1, tb, WIDTH_A), lambda b, t: (b, t, 0))
    full = lambda shape: pl.BlockSpec(shape, lambda b, t: (0,) * len(shape))
    return pl.pallas_call(
        _rwkv_pre_kernel,
        out_shape=(out,) * 8,
        grid=(B, T // tb),
        in_specs=[pl.BlockSpec((1, tb, C_RWKV), lambda b, t: (b, t, 0)),
                  pl.BlockSpec((1, 8, C_RWKV), lambda b, t: (b, jnp.maximum(t * (tb // 8) - 1, 0), 0)),
                  pl.BlockSpec((1, 1, C_RWKV), lambda b, t: (b, 0, 0)),
                  full((1, C_RWKV)), full((5, WIDTH_A)), full((LORA_W + LORA_A, 2 * WIDTH_A)),
                  full((LORA_G, WIDTH_A)), full((2 * PAIR_W, PAIR_W))],
        out_specs=(ospec,) * 8,
        compiler_params=pltpu.CompilerParams(
            dimension_semantics=("arbitrary", "arbitrary"), vmem_limit_bytes=VMEM_LIMIT),
        name="rwkv7_pre",
    )(seg, seg, prev_row, p['shift_mu'].reshape(1, C_RWKV), vecs, wa_up,
      p['g_lora_up'].astype(jnp.bfloat16), _block_ones2())


SWA_BAND = WINDOW + CHUNK


def _swa_kernel(q_ref, k_ref, v_ref, kp_ref, vp_ref, bias_ref, sink_ref, o_ref, *, mask_start):
    n_chunks = q_ref.shape[1] // CHUNK
    k_all = jnp.concatenate([kp_ref[0], k_ref[0]], axis=0).astype(jnp.bfloat16)
    v_all = jnp.concatenate([vp_ref[0], v_ref[0]], axis=0).astype(jnp.bfloat16)
    first = pl.program_id(1) == 0
    key_chunk = lax.broadcasted_iota(jnp.int32, (GROUP_B * CHUNK, SWA_BAND), 1) // CHUNK
    for c in range(n_chunks):
        q_c = q_ref[0, c * CHUNK:(c + 1) * CHUNK, :].astype(jnp.bfloat16)
        k_c = k_all[c * CHUNK:c * CHUNK + SWA_BAND]
        v_c = v_all[c * CHUNK:c * CHUNK + SWA_BAND]
        dead = jnp.logical_and(first, key_chunk + (c - WIN_CHUNKS) < 0) if (mask_start and c < WIN_CHUNKS) else None
        kvs = range(N_KV_B)
        qg = [jnp.concatenate([q_c[:, (kv * GROUP_B + g) * HEAD_B:(kv * GROUP_B + g + 1) * HEAD_B]
                               for g in range(GROUP_B)], axis=0) for kv in kvs]
        s = [lax.dot_general(qg[kv], k_c[:, kv * HEAD_B:(kv + 1) * HEAD_B], (((1,), (1,)), ((), ())),
                             preferred_element_type=jnp.float32) * (HEAD_B ** -0.5) + bias_ref[kv] for kv in kvs]
        if dead is not None:
            s = [jnp.where(dead, NEG_INF, s[kv]) for kv in kvs]
        sink = [sink_ref[kv * GROUP_B * CHUNK:(kv + 1) * GROUP_B * CHUNK, 0:1] for kv in kvs]
        m = [jnp.maximum(jnp.max(s[kv], axis=-1, keepdims=True), sink[kv]) for kv in kvs]
        e = [jnp.exp(s[kv] - m[kv]) for kv in kvs]
        pr = [e[kv] / (jnp.sum(e[kv], axis=-1, keepdims=True) + jnp.exp(sink[kv] - m[kv])) for kv in kvs]
        og = [jnp.dot(pr[kv].astype(jnp.bfloat16), v_c[:, kv * HEAD_B:(kv + 1) * HEAD_B],
                      preferred_element_type=jnp.float32) for kv in kvs]
        o_ref[0, c * CHUNK:(c + 1) * CHUNK, :] = jnp.concatenate(
            [og[kv][g * CHUNK:(g + 1) * CHUNK] for kv in kvs for g in range(GROUP_B)], axis=1)


def swa_attention(seg_b, prev_k, prev_v, sinks, *, prev_is_seq, qb=512):
    B, T, _ = seg_b.shape
    qb = min(qb, T)
    slopes = 2.0 ** (-8.0 * jnp.arange(1, N_HEADS_B + 1, dtype=jnp.float32) / N_HEADS_B)
    dist = jnp.abs(jnp.arange(CHUNK)[:, None] - (jnp.arange(SWA_BAND) - WINDOW)[None, :]).astype(jnp.float32)
    bias = (-slopes[:, None, None] * dist).reshape(N_KV_B, GROUP_B * CHUNK, SWA_BAND)
    sink_tab = jnp.broadcast_to(jnp.repeat(sinks.astype(jnp.float32), CHUNK)[:, None], (N_HEADS_B * CHUNK, 128))
    kcol, vcol = WIDTH_B // KV_WIDTH_B, WIDTH_B // KV_WIDTH_B + 1
    if prev_is_seq:
        per = qb // WINDOW
        kp_spec = pl.BlockSpec((1, WINDOW, KV_WIDTH_B), lambda b, i: (b, jnp.maximum(i * per - 1, 0), kcol))
        vp_spec = pl.BlockSpec((1, WINDOW, KV_WIDTH_B), lambda b, i: (b, jnp.maximum(i * per - 1, 0), vcol))
    else:
        kp_spec = vp_spec = pl.BlockSpec((1, WINDOW, KV_WIDTH_B), lambda b, i: (b, 0, 0))
    return pl.pallas_call(
        functools.partial(_swa_kernel, mask_start=prev_is_seq),
        out_shape=jax.ShapeDtypeStruct((B, T, WIDTH_B), jnp.float32),
        grid=(B, T // qb),
        in_specs=[pl.BlockSpec((1, qb, WIDTH_B), lambda b, i: (b, i, 0)),
                  pl.BlockSpec((1, qb, KV_WIDTH_B), lambda b, i: (b, i, kcol)),
                  pl.BlockSpec((1, qb, KV_WIDTH_B), lambda b, i: (b, i, vcol)),
                  kp_spec, vp_spec,
                  pl.BlockSpec((N_KV_B, GROUP_B * CHUNK, SWA_BAND), lambda b, i: (0, 0, 0)),
                  pl.BlockSpec((N_HEADS_B * CHUNK, 128), lambda b, i: (0, 0))],
        out_specs=pl.BlockSpec((1, qb, WIDTH_B), lambda b, i: (b, i, 0)),
        compiler_params=pltpu.CompilerParams(
            dimension_semantics=("arbitrary", "arbitrary"), vmem_limit_bytes=VMEM_LIMIT),
        name="swa_attention",
    )(seg_b, seg_b, seg_b, prev_k, prev_v, bias, sink_tab)


def _layer_norm_rows(h, g, b, eps=1e-5):
    mu = jnp.mean(h, axis=-1, keepdims=True)
    d = h - mu
    var = jnp.mean(d * d, axis=-1, keepdims=True)
    return d * lax.rsqrt(var + eps) * g + b


def _const_spec(shape):
    return pl.BlockSpec(shape, lambda *_: (0,) * len(shape), pipeline_mode=pl.Buffered(1))


def _merge_kernel(x_ref, y_ref, bonus_ref, g_ref, ob_ref, gates_ref, lnx_ref, ln1_ref, ones_ref,
                  pa_ref, pb_ref, wout_ref, o_ref):
    ones2 = ones_ref[...]
    y = y_ref[...]
    mean = _head_sums(y, ones2) * (1.0 / HEAD_A)
    d = y - mean
    var = _head_sums(d * d, ones2) * (1.0 / HEAD_A)
    yn = d * lax.rsqrt(var + GN_EPS) * lnx_ref[0:1, :] + lnx_ref[1:2, :]
    o_a = ((yn + bonus_ref[...]) * g_ref[...]).astype(jnp.bfloat16)
    br_a = jnp.dot(o_a, pa_ref[...], preferred_element_type=jnp.float32)
    br_b = jnp.dot(ob_ref[...].astype(jnp.bfloat16), pb_ref[...], preferred_element_type=jnp.float32)
    gates = jax.nn.sigmoid(gates_ref[...])
    merged = gates[:, :D_MODEL] * br_a + gates[:, D_MODEL:] * br_b
    h = ALPHA * x_ref[...] + jnp.dot(merged.astype(jnp.bfloat16), wout_ref[...], preferred_element_type=jnp.float32)
    o_ref[...] = _layer_norm_rows(h, ln1_ref[0:1, :], ln1_ref[1:2, :])


def branch_merge(x, y, bonus, g, o_b, gates, p, tm=256):
    n = x.shape[0]
    tm = _pick(n, (tm, 128, 64))
    rows = lambda w: pl.BlockSpec((tm, w), lambda i: (i, 0))
    return pl.pallas_call(
        _merge_kernel,
        out_shape=jax.ShapeDtypeStruct((n, D_MODEL), jnp.float32),
        grid=(n // tm,),
        in_specs=[rows(D_MODEL), rows(WIDTH_A), rows(WIDTH_A), rows(WIDTH_A), rows(WIDTH_B), rows(C_GATE),
                  _const_spec((2, WIDTH_A)), _const_spec((2, D_MODEL)), _const_spec((2 * PAIR_W, PAIR_W)),
                  _const_spec((WIDTH_A, D_MODEL)), _const_spec((WIDTH_B, D_MODEL)), _const_spec((D_MODEL, D_MODEL))],
        out_specs=rows(D_MODEL),
        compiler_params=pltpu.CompilerParams(dimension_semantics=("arbitrary",), vmem_limit_bytes=VMEM_LIMIT),
        name="branch_merge",
    )(x, y, bonus, g, o_b, gates, jnp.stack([p['lnx_g'], p['lnx_b']]), jnp.stack([p['ln1_g'], p['ln1_b']]),
      _block_ones2(), p['w_branch_a16'], p['w_branch_b16'], p['w_out16'])


def _mem_kernel(x_ref, mk_ref, mv_ref, ln2_ref, wq_ref, wo_ref, pwq_ref, o_ref, q_ref):
    x = x_ref[0]
    qm = jnp.dot(x.astype(jnp.bfloat16), wq_ref[...], preferred_element_type=jnp.float32).astype(jnp.bfloat16)
    mk = mk_ref[0].astype(jnp.bfloat16)
    mv = mv_ref[0].astype(jnp.bfloat16)
    outs = []
    for h in range(N_HEADS_M):
        cols = slice(h * HEAD_M, (h + 1) * HEAD_M)
        s = lax.dot_general(qm[:, cols], mk[:, cols], (((1,), (1,)), ((), ())),
                            preferred_element_type=jnp.float32) * (HEAD_M ** -0.5)
        e = jnp.exp(s - jnp.max(s, axis=-1, keepdims=True))
        pr = e / jnp.sum(e, axis=-1, keepdims=True)
        outs.append(jnp.dot(pr.astype(jnp.bfloat16), mv[:, cols], preferred_element_type=jnp.float32))
    o = jnp.concatenate(outs, axis=1).astype(jnp.bfloat16)
    h2 = ALPHA * x + jnp.dot(o, wo_ref[...], preferred_element_type=jnp.float32)
    x2 = _layer_norm_rows(h2, ln2_ref[0:1, :], ln2_ref[1:2, :])
    o_ref[0] = x2
    q_ref[0] = jnp.dot(x2.astype(jnp.bfloat16), pwq_ref[...], preferred_element_type=jnp.float32)


def mem_block(x, mk, mv, p, tm=256):
    B, T, _ = x.shape
    tm = _pick(T, (tm, 128, 64))
    wm = N_HEADS_M * HEAD_M
    rows = lambda w: pl.BlockSpec((1, tm, w), lambda b, i: (b, i, 0))
    mem = pl.BlockSpec((1, N_MEM, wm), lambda b, i: (b, 0, 0))
    out = jax.ShapeDtypeStruct((B, T, D_MODEL), jnp.float32)
    return pl.pallas_call(
        _mem_kernel,
        out_shape=(out, jax.ShapeDtypeStruct((B, T, PEER_HEADS * D_KEY), jnp.float32)),
        grid=(B, T // tm),
        in_specs=[rows(D_MODEL), mem, mem, _const_spec((2, D_MODEL)), _const_spec((D_MODEL, wm)),
                  _const_spec((wm, D_MODEL)), _const_spec((D_MODEL, PEER_HEADS * D_KEY))],
        out_specs=(rows(D_MODEL), rows(PEER_HEADS * D_KEY)),
        compiler_params=pltpu.CompilerParams(
            dimension_semantics=("arbitrary", "arbitrary"), vmem_limit_bytes=VMEM_LIMIT),
        name="mem_block",
    )(x, mk, mv, jnp.stack([p['ln2_g'], p['ln2_b']]), p['wq_mem16'], p['wo_mem16'], p['peer_wq16'])


ROUTE_TQ = 256


def _top_values(s, k):
    n_rows = s.shape[0]
    iota = lax.broadcasted_iota(jnp.int32, s.shape, 0).astype(jnp.float32)
    rank = jnp.full(s.shape, float(k), jnp.float32)
    out = []
    for step in range(k):
        m = jnp.max(s, axis=0, keepdims=True)
        first = jnp.min(jnp.where(s == m, iota, float(n_rows)), axis=0, keepdims=True)
        taken = iota == first
        s = jnp.where(taken, -jnp.inf, s)
        rank = jnp.where(taken, float(step), rank)
        out.append(m)
    return out, rank


def _peer_route_kernel(q_ref, keys_ref, n1_ref, c1_ref, rank2_ref, e2_ref):
    half = D_KEY // 2
    for h in range(PEER_HEADS):
        tops, scores = [], []
        for p in range(2):
            c0 = (2 * h + p) * half
            qs = q_ref[:, c0:c0 + half].astype(jnp.bfloat16)
            s = lax.dot_general(keys_ref[2 * h + p], qs, (((1,), (1,)), ((), ())),
                                preferred_element_type=jnp.float32)
            scores.append(s)
            tops.append(_top_values(s, TOPK))
        (t1, _), (t2, rank2) = tops
        t2all = jnp.concatenate(t2, axis=0)
        t1all = jnp.concatenate(t1, axis=0)
        rank8 = lax.broadcasted_iota(jnp.int32, (8, t1all.shape[1]), 0)
        cand_rows = [t1all + t2[0], t1all[:8] + t2[1]]
        for b in range(2, 8):
            cand_rows.append(jnp.where(rank8 < TOPK // (b + 1), t1all[:8] + t2[b], -jnp.inf))
        cand_rows.append(t1[0] + t2all[8:])
        sc, _ = _top_values(jnp.concatenate(cand_rows, axis=0), TOPK)
        z = jnp.zeros_like(sc[0])
        for kq in range(TOPK):
            z = z + jnp.exp(sc[kq] - sc[0])
        theta = sc[TOPK - 1]
        n1 = jnp.zeros_like(scores[0])
        for a in range(TOPK):
            n_a = jnp.sum((t1[a] + t2all >= theta).astype(jnp.float32), axis=0, keepdims=True)
            n1 = jnp.where(scores[0] == t1[a], n_a, n1)
        n1_ref[h] = n1
        rank2_ref[h] = rank2.astype(jnp.bfloat16)
        c1_ref[h] = jnp.exp(scores[0] - t1[0]) / z
        e2_ref[h] = jnp.exp(scores[1] - t2[0]).astype(jnp.bfloat16)


def peer_route(q, keys16):
    n = q.shape[0]
    tq = _pick(n, (ROUTE_TQ, 128))
    big = jax.ShapeDtypeStruct((PEER_HEADS, N_KEYS, n), jnp.float32)
    big16 = jax.ShapeDtypeStruct((PEER_HEADS, N_KEYS, n), jnp.bfloat16)
    bspec = pl.BlockSpec((PEER_HEADS, N_KEYS, tq), lambda i: (0, 0, i))
    return pl.pallas_call(
        _peer_route_kernel,
        out_shape=(big, big, big16, big16),
        grid=(n // tq,),
        in_specs=[pl.BlockSpec((tq, PEER_HEADS * D_KEY), lambda i: (i, 0)),
                  pl.BlockSpec((2 * PEER_HEADS, N_KEYS, D_KEY // 2), lambda i: (0, 0, 0))],
        out_specs=(bspec, bspec, bspec, bspec),
        compiler_params=pltpu.CompilerParams(
            dimension_semantics=("arbitrary",), vmem_limit_bytes=VMEM_LIMIT),
        name="peer_route",
    )(q, keys16)


PEER_TM = 512
PEER_ROWS = 8
PEER_TE = PEER_ROWS * N_KEYS
PEER_STRIP = 256


def _gelu(x):
    return 0.5 * x * (1.0 + lax.erf(x * (2.0 ** -0.5)))


def _peer_mix_kernel(x_ref, u_ref, vt_ref, n1_ref, c1_ref, rank2_ref, e2_ref, ln3_ref, o_ref,
                     x16_ref, h_ref, acc_ref):
    j = pl.program_id(1)

    @pl.when(j == 0)
    def _():
        x16_ref[...] = x_ref[...].astype(jnp.bfloat16)
        acc_ref[...] = jnp.zeros_like(acc_ref)

    half_te = PEER_TE // 2
    a_halves = [lax.dot_general(u_ref[k * half_te:(k + 1) * half_te, :], x16_ref[...], (((1,), (1,)), ((), ())),
                                preferred_element_type=jnp.float32) for k in range(2)]

    def sublane_bcast16(row):
        return jnp.broadcast_to(row, (N_KEYS, row.shape[1])).astype(jnp.bfloat16)

    for r in range(PEER_ROWS):
        rows = slice(r * N_KEYS, (r + 1) * N_KEYS)
        for c0 in range(0, x_ref.shape[0], 128):
            cols = slice(c0, c0 + 128)
            gate = None
            for h in range(PEER_HEADS):
                keep = rank2_ref[h, :, cols] < sublane_bcast16(n1_ref[h, r:r + 1, cols])
                w = jnp.where(keep, e2_ref[h, :, cols], jnp.zeros((), jnp.bfloat16))
                w = w * sublane_bcast16(c1_ref[h, r:r + 1, cols])
                gate = w if gate is None else gate + w
            a_rows = a_halves[r // (PEER_ROWS // 2)][(r % (PEER_ROWS // 2)) * N_KEYS:(r % (PEER_ROWS // 2) + 1) * N_KEYS]
            h_ref[rows, cols] = gate * _gelu(a_rows[:, cols]).astype(jnp.bfloat16)
    acc_ref[...] += jnp.dot(vt_ref[0], h_ref[...], preferred_element_type=jnp.float32)

    @pl.when(j == pl.num_programs(1) - 1)
    def _():
        o_ref[...] = _layer_norm_rows(ALPHA * x_ref[...] + acc_ref[...].T, ln3_ref[0:1, :], ln3_ref[1:2, :])


def peer_mix(x, u16, vt16, n1, c1, rank2, e2, ln3):
    n, d = x.shape
    tm = _pick(n, (PEER_TM, 256, 128))
    n_exp = u16.shape[0]
    row_spec = pl.BlockSpec((PEER_HEADS, PEER_ROWS, tm), lambda i, j: (0, j, i))
    all_spec = pl.BlockSpec((PEER_HEADS, N_KEYS, tm), lambda i, j: (0, 0, i), pipeline_mode=pl.Buffered(1))
    return pl.pallas_call(
        _peer_mix_kernel,
        out_shape=jax.ShapeDtypeStruct((n, d), jnp.float32),
        grid=(n // tm, n_exp // PEER_TE),
        in_specs=[pl.BlockSpec((tm, d), lambda i, j: (i, 0), pipeline_mode=pl.Buffered(1)),
                  pl.BlockSpec((PEER_TE, d), lambda i, j: (j, 0)),
                  pl.BlockSpec((1, d, PEER_TE), lambda i, j: (j, 0, 0)),
                  row_spec, row_spec, all_spec, all_spec, _const_spec((2, d))],
        out_specs=pl.BlockSpec((tm, d), lambda i, j: (i, 0)),
        scratch_shapes=[pltpu.VMEM((tm, d), jnp.bfloat16), pltpu.VMEM((PEER_TE, tm), jnp.bfloat16),
                        pltpu.VMEM((d, tm), jnp.float32)],
        compiler_params=pltpu.CompilerParams(
            dimension_semantics=("arbitrary", "arbitrary"), vmem_limit_bytes=VMEM_LIMIT),
        name="peer_mix",
    )(x, u16, vt16, n1, c1, rank2, e2, ln3)


def peer_block(x, q, p):
    stats = peer_route(q, p['peer_keys16'])
    return peer_mix(x, p['peer_u16'], p['peer_vt16'], *stats, jnp.stack([p['ln3_g'], p['ln3_b']]))


def layer_norm(x, g, b, eps=1e-5):
    xf = x.astype(jnp.float32)
    mu = jnp.mean(xf, -1, keepdims=True)
    var = jnp.mean(jnp.square(xf - mu), -1, keepdims=True)
    return ((xf - mu) * lax.rsqrt(var + eps) * g + b).astype(x.dtype)


def rwkv7_scan(r, w, k, v, kk, a, state0):
    def step(S, inp):
        r_t, w_t, k_t, v_t, kk_t, a_t = inp
        sa = jnp.einsum('bhij,bhj->bhi', S, -kk_t)
        S = (S * w_t[:, :, None, :] + sa[..., None] * (kk_t * a_t)[:, :, None, :]
             + v_t[..., None] * k_t[:, :, None, :])
        y = jnp.einsum('bhij,bhj->bhi', S, r_t)
        return S, y
    xs = tuple(jnp.moveaxis(t, 1, 0) for t in (r, w, k, v, kk, a))
    S, ys = lax.scan(step, state0.astype(jnp.float32), xs)
    return jnp.moveaxis(ys, 0, 1), S


def rwkv7_branch(seg, prev_row, state0, p):
    B, T, _ = seg.shape
    f32 = jnp.float32
    shifted = jnp.concatenate([prev_row.astype(seg.dtype), seg[:, :-1]], axis=1)
    xm = seg + p['shift_mu'] * (shifted - seg)
    r, k, v, wl, al, gl = jnp.split(xm, SPLIT_A, axis=-1)
    w_log = -jax.nn.softplus(-(p['w0'] + jnp.tanh(wl) @ p['w_lora_up']).astype(f32)) - 0.5
    decay = jnp.exp(-jnp.exp(w_log))
    a = jax.nn.sigmoid((p['a0'] + al @ p['a_lora_up']).astype(f32))
    g = jax.nn.sigmoid(gl) @ p['g_lora_up']
    heads = lambda t: t.reshape(B, T, N_HEADS_A, HEAD_A)
    kk = heads((k * p['k_k']).astype(f32))
    kk = kk * lax.rsqrt(jnp.maximum(jnp.sum(kk * kk, -1, keepdims=True), 1e-24))
    kf = k.astype(f32) * (1.0 + (a - 1.0) * p['k_a'].astype(f32))
    rh, kh, vh, ah, wh = heads(r.astype(f32)), heads(kf), heads(v.astype(f32)), heads(a), heads(decay)
    flat = lambda t: t.reshape(B, T, WIDTH_A)
    y, s_final = rwkv7_scan_pallas(flat(rh), flat(wh), flat(kh), flat(vh), flat(kk), flat(ah), state0)
    y = heads(y)
    mean = jnp.mean(y, -1, keepdims=True)
    var = jnp.mean(jnp.square(y - mean), -1, keepdims=True)
    yn = ((y - mean) * lax.rsqrt(var + GN_EPS)).reshape(B, T, WIDTH_A) * p['lnx_g'] + p['lnx_b']
    bonus = jnp.sum(rh * kh * p['r_k'].astype(f32), -1, keepdims=True) * vh
    out = (yn + bonus.reshape(B, T, WIDTH_A)) * g
    return out.astype(seg.dtype), s_final, seg[:, -1:]


def alibi_bias(q_pos, k_pos):
    slopes = 2.0 ** (-8.0 * jnp.arange(1, N_HEADS_B + 1, dtype=jnp.float32) / N_HEADS_B)
    dist = jnp.abs(q_pos[:, None] - k_pos[None, :]).astype(jnp.float32)
    return -slopes.reshape(N_KV_B, GROUP_B, 1, 1) * dist


def sink_attention(q, k, v, bias, mask, sinks):
    s = jnp.einsum('bnqhgd,bnjhd->bnhgqj', q, k).astype(jnp.float32) * (HEAD_B ** -0.5) + bias
    if mask is not None:
        s = jnp.where(mask, s, NEG_INF)
    sink = sinks.astype(jnp.float32).reshape(N_KV_B, GROUP_B, 1, 1)
    m = jnp.maximum(jnp.max(s, -1, keepdims=True), sink)
    p = jnp.exp(s - m)
    p = p / (jnp.sum(p, -1, keepdims=True) + jnp.exp(sink - m))
    return jnp.einsum('bnhgqj,bnjhd->bnqhgd', p.astype(v.dtype), v)


def swa_prompt(q, k, v, sinks):
    B, S = q.shape[:2]
    NC = S // CHUNK
    band = (WIN_CHUNKS + 1) * CHUNK
    qc = q.reshape(B, NC, CHUNK, N_KV_B, GROUP_B, HEAD_B)
    pad = ((0, 0), (WIN_CHUNKS, 0), (0, 0), (0, 0), (0, 0))
    kp = jnp.pad(k.reshape(B, NC, CHUNK, N_KV_B, HEAD_B), pad)
    vp = jnp.pad(v.reshape(B, NC, CHUNK, N_KV_B, HEAD_B), pad)
    kb = jnp.concatenate([kp[:, o:o + NC] for o in range(WIN_CHUNKS + 1)], axis=2)
    vb = jnp.concatenate([vp[:, o:o + NC] for o in range(WIN_CHUNKS + 1)], axis=2)
    bias = alibi_bias(jnp.arange(CHUNK), jnp.arange(band) - WIN_CHUNKS * CHUNK)
    key_chunk = jnp.arange(NC)[:, None] - WIN_CHUNKS + (jnp.arange(band) // CHUNK)[None, :]
    mask = (key_chunk >= 0)[None, :, None, None, None, :]
    o = sink_attention(qc, kb, vb, bias, mask, sinks)
    return o.reshape(B, S, WIDTH_B)


def swa_sample(q, k, v, k_cache, v_cache, sinks):
    B, T = q.shape[:2]
    Lc = k_cache.shape[1]
    k_all = jnp.concatenate([k_cache.astype(k.dtype), k], axis=1)
    v_all = jnp.concatenate([v_cache.astype(v.dtype), v], axis=1)
    bias = alibi_bias(jnp.arange(T), jnp.arange(Lc + T) - Lc)
    o = sink_attention(q[:, None], k_all[:, None], v_all[:, None], bias, None, sinks)
    return o.reshape(B, T, WIDTH_B), k_all[:, -Lc:], v_all[:, -Lc:]


def mem_attention(x, mk, mv, p):
    B, T, _ = x.shape
    q = mm(x, p['wq_mem16']).reshape(B, T, N_HEADS_M, HEAD_M)
    s = jnp.einsum('bthd,bmhd->bhtm', q, mk.astype(q.dtype)).astype(jnp.float32) * (HEAD_M ** -0.5)
    pr = jax.nn.softmax(s, axis=-1).astype(x.dtype)
    o = jnp.einsum('bhtm,bmhd->bthd', pr, mv.astype(x.dtype)).reshape(B, T, N_HEADS_M * HEAD_M)
    return mm(o, p['wo_mem16'])


def peer_ffn(x, p):
    B, T, D = x.shape
    n = B * T
    nblk = -(-n // PEER_BLOCK)
    xt = x.reshape(nblk, PEER_BLOCK, D)
    qa = mm(x, p['peer_wq16']).reshape(nblk, PEER_BLOCK, PEER_HEADS * D_KEY)
    sub_keys, u_tab, v_tab = p['peer_sub_keys'], p['peer_u'], p['peer_v']

    def peer_block(args):
        xb, qb = args
        q = qb.reshape(PEER_BLOCK, PEER_HEADS, 2, D_KEY // 2)
        s = jnp.einsum('thpd,hpnd->thpn', q, sub_keys).astype(jnp.float32)
        s1, i1 = lax.top_k(s[:, :, 0], TOPK)
        s2, i2 = lax.top_k(s[:, :, 1], TOPK)
        cand = (s1[..., :, None] + s2[..., None, :]).reshape(PEER_BLOCK, PEER_HEADS, TOPK * TOPK)
        cidx = (i1[..., :, None] * N_KEYS + i2[..., None, :]).reshape(PEER_BLOCK, PEER_HEADS, TOPK * TOPK)
        sc, pos = lax.top_k(cand, TOPK)
        e = jnp.take_along_axis(cidx, pos, axis=-1)
        g = jax.nn.softmax(sc, axis=-1)
        act = jax.nn.gelu(jnp.einsum('td,thkd->thk', xb, u_tab[e]).astype(jnp.float32), approximate=False)
        return jnp.einsum('thk,thkd->td', (g * act).astype(xb.dtype), v_tab[e])

    y = lax.map(peer_block, (xt, qa)).reshape(nblk * PEER_BLOCK, D)
    return y.reshape(B, T, D)


def trunk_layer(x, p, mem_k, mem_v, rwkv_state, shift_row, swa_k_cache, swa_v_cache):
    B, T, _ = x.shape
    n = B * T
    x2d = x.reshape(n, D_MODEL)
    seg_a = matmul(x2d, p['w_in_a16']).reshape(B, T, C_RWKV)
    seg_b = matmul(x2d, p['w_in_b16']).reshape(B, T, C_SWA)
    gates = matmul(x2d, p['w_in_g16'])
    r, w, k, v, kk, a, g, bonus = rwkv7_pre(seg_a, shift_row.astype(jnp.float32), p)
    y, rwkv_new = rwkv7_chunked(r, w, k, v, kk, a, rwkv_state)
    shift_new = seg_a[:, -1:]
    k_new = seg_b[:, :, WIDTH_B:WIDTH_B + KV_WIDTH_B].reshape(B, T, N_KV_B, HEAD_B)
    v_new = seg_b[:, :, WIDTH_B + KV_WIDTH_B:].reshape(B, T, N_KV_B, HEAD_B)
    if swa_k_cache is None:
        o_b = swa_attention(seg_b, seg_b, seg_b, p['attn_sinks'], prev_is_seq=True)
        swa_k_new, swa_v_new = k_new[:, -WINDOW:], v_new[:, -WINDOW:]
    else:
        o_b = swa_attention(seg_b, swa_k_cache.reshape(B, WINDOW, KV_WIDTH_B),
                            swa_v_cache.reshape(B, WINDOW, KV_WIDTH_B), p['attn_sinks'], prev_is_seq=False)
        swa_k_new = jnp.concatenate([swa_k_cache, k_new], axis=1)[:, -WINDOW:]
        swa_v_new = jnp.concatenate([swa_v_cache, v_new], axis=1)[:, -WINDOW:]
    flat = lambda t: t.reshape(n, t.shape[-1])
    x1 = branch_merge(x2d, flat(y), flat(bonus), flat(g), flat(o_b), gates, p)
    wm = N_HEADS_M * HEAD_M
    x2, q = mem_block(x1.reshape(B, T, D_MODEL), mem_k.reshape(-1, N_MEM, wm), mem_v.reshape(-1, N_MEM, wm), p)
    x3 = peer_block(flat(x2), flat(q), p)
    return x3.reshape(B, T, D_MODEL), rwkv_new, shift_new, swa_k_new, swa_v_new


_MM_WEIGHTS = ('w_branch_a', 'w_branch_b', 'w_out', 'wq_mem', 'wk_mem', 'wv_mem', 'wo_mem', 'peer_wq')


def kernel(x_prompt, x_sample, state_rwkv, state_shift, cache_swa_k, cache_swa_v, cache_mem_k, cache_mem_v, mem_prompt, w_in, shift_mu, w0, w_lora_up, a0, a_lora_up, g_lora_up, k_k, k_a, r_k, lnx_g, lnx_b, attn_sinks, w_branch_a, w_branch_b, w_out, ln1_g, ln1_b, wq_mem, wk_mem, wv_mem, wo_mem, ln2_g, ln2_b, peer_wq, peer_sub_keys, peer_u, peer_v, ln3_g, ln3_b):
    params = {
        'w_in': w_in, 'shift_mu': shift_mu, 'w0': w0, 'w_lora_up': w_lora_up, 'a0': a0,
        'a_lora_up': a_lora_up, 'g_lora_up': g_lora_up, 'k_k': k_k, 'k_a': k_a, 'r_k': r_k,
        'lnx_g': lnx_g, 'lnx_b': lnx_b, 'attn_sinks': attn_sinks, 'w_branch_a': w_branch_a,
        'w_branch_b': w_branch_b, 'w_out': w_out, 'ln1_g': ln1_g, 'ln1_b': ln1_b, 'wq_mem': wq_mem,
        'wk_mem': wk_mem, 'wv_mem': wv_mem, 'wo_mem': wo_mem, 'ln2_g': ln2_g, 'ln2_b': ln2_b,
        'peer_wq': peer_wq, 'peer_sub_keys': peer_sub_keys, 'peer_u': peer_u, 'peer_v': peer_v,
        'ln3_g': ln3_g, 'ln3_b': ln3_b,
    }
    B = x_prompt.shape[0]
    rwkv0 = jnp.zeros((B, N_HEADS_A, HEAD_A, HEAD_A), jnp.float32)
    shift0 = jnp.zeros((B, 1, C_RWKV), x_prompt.dtype)
    xp, xs = x_prompt, x_sample
    p_rw, p_sh, p_k, p_v, p_mk, p_mv = [], [], [], [], [], []
    s_rw, s_sh, s_k, s_v = [], [], [], []
    for l in range(DEPTH):
        p = {name: arr[l] for name, arr in params.items()}
        for name in _MM_WEIGHTS:
            p[name + '16'] = p[name].astype(jnp.bfloat16)
        w_in16 = p['w_in'].astype(jnp.bfloat16)
        p['w_in_a16'] = w_in16[:, :C_RWKV]
        p['w_in_b16'] = w_in16[:, C_RWKV:C_RWKV + C_SWA]
        p['w_in_g16'] = w_in16[:, C_RWKV + C_SWA:]
        p['peer_keys16'] = p['peer_sub_keys'].reshape(2 * PEER_HEADS, N_KEYS, D_KEY // 2).astype(jnp.bfloat16)
        p['peer_u16'] = p['peer_u'].astype(jnp.bfloat16)
        p['peer_vt16'] = (p['peer_v'].astype(jnp.bfloat16)
                          .reshape(p['peer_v'].shape[0] // PEER_TE, PEER_TE, D_MODEL).transpose(0, 2, 1))
        mk = mm(mem_prompt, p['wk_mem16']).reshape(B, N_MEM, N_HEADS_M, HEAD_M)
        mv = mm(mem_prompt, p['wv_mem16']).reshape(B, N_MEM, N_HEADS_M, HEAD_M)
        xp, rw, sh, kn, vn = trunk_layer(xp, p, mk, mv, rwkv0, shift0, None, None)
        p_rw.append(rw); p_sh.append(sh); p_k.append(kn); p_v.append(vn); p_mk.append(mk); p_mv.append(mv)
        xs, rw, sh, kn, vn = trunk_layer(xs, p, cache_mem_k[l], cache_mem_v[l], state_rwkv[l], state_shift[l],
                                         cache_swa_k[l], cache_swa_v[l])
        s_rw.append(rw); s_sh.append(sh); s_k.append(kn); s_v.append(vn)
    return (xp, xs,
            jnp.stack(p_rw), jnp.stack(p_sh), jnp.stack(p_k), jnp.stack(p_v), jnp.stack(p_mk), jnp.stack(p_mv),
            jnp.stack(s_rw), jnp.stack(s_sh), jnp.stack(s_k), jnp.stack(s_v))
```

```python
import functools

import jax
import jax.numpy as jnp
from jax import lax
from jax.experimental import pallas as pl
from jax.experimental.pallas import tpu as pltpu

D_MODEL = 2048
DEPTH = 2
CHUNK = 64
HEAD_A = 64
N_HEADS_A = 16
WIDTH_A = N_HEADS_A * HEAD_A
LORA_W = 64
LORA_A = 64
LORA_G = 128
GN_EPS = 64e-5
SPLIT_A = [WIDTH_A, 2 * WIDTH_A, 3 * WIDTH_A, 3 * WIDTH_A + LORA_W, 3 * WIDTH_A + LORA_W + LORA_A]
C_RWKV = 3 * WIDTH_A + LORA_W + LORA_A + LORA_G
HEAD_B = 64
N_HEADS_B = 16
N_KV_B = 4
GROUP_B = N_HEADS_B // N_KV_B
WIDTH_B = N_HEADS_B * HEAD_B
KV_WIDTH_B = N_KV_B * HEAD_B
WINDOW = 128
WIN_CHUNKS = WINDOW // CHUNK
C_SWA = WIDTH_B + 2 * KV_WIDTH_B
C_GATE = 2 * D_MODEL
C_IN = C_RWKV + C_SWA + C_GATE
N_MEM = 256
N_HEADS_M = 4
HEAD_M = 128
N_KEYS = 128
PEER_HEADS = 8
D_KEY = 256
TOPK = 16
PEER_BLOCK = 128
ALPHA = (2.0 * DEPTH) ** 0.25
NEG_INF = -1e30

VMEM_LIMIT = 56 * 1024 * 1024


def _matmul_kernel(a_ref, b_ref, o_ref, a16_ref):
    @pl.when(pl.program_id(1) == 0)
    def _():
        a16_ref[...] = a_ref[...].astype(jnp.bfloat16)

    o_ref[...] = jnp.dot(a16_ref[...], b_ref[...], preferred_element_type=jnp.float32)


def _pick(n, cands):
    for c in cands:
        if n % c == 0:
            return c
    return n


def matmul(a, b16):
    m, k = a.shape
    n = b16.shape[1]
    tm = _pick(m, (512, 256, 128))
    tn = _pick(n, (2048, 1664, 1536, 1024, 512, 256, 128))
    return pl.pallas_call(
        _matmul_kernel,
        out_shape=jax.ShapeDtypeStruct((m, n), jnp.float32),
        grid=(m // tm, n // tn),
        in_specs=[pl.BlockSpec((tm, k), lambda i, j: (i, 0)),
                  pl.BlockSpec((k, tn), lambda i, j: (0, j))],
        out_specs=pl.BlockSpec((tm, tn), lambda i, j: (i, j)),
        scratch_shapes=[pltpu.VMEM((tm, k), jnp.bfloat16)],
        compiler_params=pltpu.CompilerParams(
            dimension_semantics=("arbitrary", "arbitrary"),
            vmem_limit_bytes=VMEM_LIMIT),
        name="matmul",
    )(a, b16)


def mm(x, w16):
    lead = x.shape[:-1]
    return matmul(x.reshape(-1, x.shape[-1]), w16).reshape(*lead, w16.shape[1])


N_PAIR = N_HEADS_A // 2
PAIR_W = 2 * HEAD_A


def _split2(x):
    hi = x.astype(jnp.bfloat16)
    lo = (x - hi.astype(jnp.float32)).astype(jnp.bfloat16)
    return jnp.concatenate([hi, lo], axis=1)


def _rwkv_scan_kernel(r_ref, w_ref, k_ref, v_ref, kk_ref, a_ref, s0_ref, ones_ref, eye_ref,
                      y_ref, sT_ref, s_ref, *, tb):
    tblk = pl.program_id(1)

    @pl.when(tblk == 0)
    def _():
        s_ref[...] = s0_ref[0]

    ones2 = ones_ref[...]
    eye2 = eye_ref[...]

    def segsum(tiles):
        lhs = jnp.concatenate([_split2(m) for m in tiles], axis=0)
        return jnp.dot(lhs, ones2, preferred_element_type=jnp.float32)

    def step(g, carry):
        rows = pl.ds(pl.multiple_of(g * 8, 8), 8)
        pairs = range(N_PAIR)
        sls = [pl.ds(p * PAIR_W, PAIR_W) for p in pairs]
        kk8 = [kk_ref[0, rows, sl] for sl in sls]
        kka8 = [kk8[p] * a_ref[0, rows, sls[p]] for p in pairs]
        w8 = [w_ref[0, rows, sl] for sl in sls]
        k8 = [k_ref[0, rows, sl] for sl in sls]
        v8 = [v_ref[0, rows, sl] for sl in sls]
        r8 = [r_ref[0, rows, sl] for sl in sls]
        s = [s_ref[p] for p in pairs]
        ys = [[] for _ in pairs]
        tile = lambda x, p: x[p * HEAD_A:(p + 1) * HEAD_A]
        for j in range(8):
            row = slice(j, j + 1)
            sa = segsum([s[p] * (-kk8[p][row]) for p in pairs])
            vb = segsum([eye2 * v8[p][row] for p in pairs])
            s = [s[p] * w8[p][row] + tile(sa, p) * kka8[p][row] + tile(vb, p) * k8[p][row] for p in pairs]
            yb = segsum([s[p] * r8[p][row] for p in pairs])
            for p in pairs:
                ys[p].append(jnp.sum(tile(yb, p) * eye2, axis=0, keepdims=True))
        for p in pairs:
            s_ref[p] = s[p]
            y_ref[0, rows, sls[p]] = jnp.concatenate(ys[p], axis=0)
        return carry

    lax.fori_loop(0, tb // 8, step, 0)

    @pl.when(tblk == pl.num_programs(1) - 1)
    def _():
        sT_ref[0] = s_ref[...]


def rwkv7_scan_pallas(r, w, k, v, kk, a, state0, tb=256):
    B, T, _ = r.shape
    tb = min(tb, T)
    s0 = state0.astype(jnp.float32).reshape(B, N_PAIR, 2, HEAD_A, HEAD_A)
    s0 = s0.transpose(0, 1, 3, 2, 4).reshape(B, N_PAIR, HEAD_A, PAIR_W)
    lane_head = jnp.arange(PAIR_W) // HEAD_A
    ones2 = (jnp.tile(lane_head, 2)[:, None] == lane_head[None, :]).astype(jnp.bfloat16)
    eye2 = (jnp.arange(HEAD_A)[:, None] == (jnp.arange(PAIR_W) % HEAD_A)[None, :]).astype(jnp.float32)
    seq = pl.BlockSpec((1, tb, WIDTH_A), lambda b, t: (b, t, 0))
    st = pl.BlockSpec((1, N_PAIR, HEAD_A, PAIR_W), lambda b, t: (b, 0, 0, 0))
    y, sT = pl.pallas_call(
        functools.partial(_rwkv_scan_kernel, tb=tb),
        out_shape=(jax.ShapeDtypeStruct((B, T, WIDTH_A), jnp.float32),
                   jax.ShapeDtypeStruct((B, N_PAIR, HEAD_A, PAIR_W), jnp.float32)),
        grid=(B, T // tb),
        in_specs=[seq] * 6 + [st,
                              pl.BlockSpec((2 * PAIR_W, PAIR_W), lambda b, t: (0, 0)),
                              pl.BlockSpec((HEAD_A, PAIR_W), lambda b, t: (0, 0))],
        out_specs=(seq, st),
        scratch_shapes=[pltpu.VMEM((N_PAIR, HEAD_A, PAIR_W), jnp.float32)],
        compiler_params=pltpu.CompilerParams(
            dimension_semantics=("arbitrary", "arbitrary"),
            vmem_limit_bytes=VMEM_LIMIT),
        name="rwkv7_scan",
    )(r, w, k, v, kk, a, s0, ones2, eye2)
    sT = sT.reshape(B, N_PAIR, HEAD_A, 2, HEAD_A).transpose(0, 1, 3, 2, 4)
    return y, sT.reshape(B, N_HEADS_A, HEAD_A, HEAD_A)


def _hl(x):
    hi = x.astype(jnp.bfloat16)
    return hi, (x - hi.astype(jnp.float32)).astype(jnp.bfloat16)


def _dot3(a, b):
    ah, al = _hl(a)
    bh, bl = _hl(b)
    return jnp.dot(jnp.concatenate([ah, ah, al], axis=1), jnp.concatenate([bh, bl, bh], axis=0),
                   preferred_element_type=jnp.float32)


def _dot3_nt(a, b):
    ah, al = _hl(a)
    bh, bl = _hl(b)
    return lax.dot_general(jnp.concatenate([ah, ah, al], axis=1), jnp.concatenate([bh, bl, bh], axis=1),
                           (((1,), (1,)), ((), ())), preferred_element_type=jnp.float32)


def _rwkv_chunk_kernel(r_ref, w_ref, k_ref, v_ref, kk_ref, a_ref, s0_ref, ltri_ref, ones_ref, msk_ref,
                       y_ref, sT_ref, p_ref, *, n_chunks):
    tblk = pl.program_id(1)

    @pl.when(tblk == 0)
    def _():
        p_ref[...] = s0_ref[0]

    eye2, m0, m1, strict, incl = (msk_ref[i] for i in range(5))
    bd = lambda y: jnp.concatenate([y * m0, y * m1], axis=0)

    def split3(x, axis):
        t1 = x.astype(jnp.bfloat16)
        d = x - t1.astype(jnp.float32)
        t2 = d.astype(jnp.bfloat16)
        t3 = (d - t2.astype(jnp.float32)).astype(jnp.bfloat16)
        return jnp.concatenate([t1, t2, t3], axis=axis)

    def chunk(c, carry):
        rows = pl.ds(pl.multiple_of(c * CHUNK, CHUNK), CHUNK)
        pairs = range(N_PAIR)
        sls = [pl.ds(p * PAIR_W, PAIR_W) for p in pairs]
        lw = [w_ref[0, rows, sl] for sl in sls]
        cum = [jnp.dot(ltri_ref[...], split3(lw[p], 0), preferred_element_type=jnp.float32) for p in pairs]
        g = [jnp.exp(cum[p]) for p in pairs]
        ginv = [jnp.exp(-cum[p]) for p in pairs]
        kk = [kk_ref[0, rows, sl] for sl in sls]
        kh = [kk[p] * jnp.exp(cum[p] - lw[p]) for p in pairs]
        bh = [kk[p] * a_ref[0, rows, sls[p]] * ginv[p] for p in pairs]
        kf = [k_ref[0, rows, sls[p]] * ginv[p] for p in pairs]
        rh = [r_ref[0, rows, sls[p]] * g[p] for p in pairs]
        v = [v_ref[0, rows, sl] for sl in sls]
        g_last = [g[p][CHUNK - 1:CHUNK, :] for p in pairs]
        kr = [jnp.concatenate([kh[p], rh[p]], axis=0) for p in pairs]
        gram = [_dot3_nt(kr[p], jnp.concatenate([bd(bh[p]), bd(kf[p])], axis=0)) for p in pairs]
        a_b = [gram[p][:CHUNK, :PAIR_W] * strict for p in pairs]
        a_k = [gram[p][:CHUNK, PAIR_W:] * strict for p in pairs]
        a_r = [jnp.concatenate([gram[p][CHUNK:, :PAIR_W] * incl, gram[p][CHUNK:, PAIR_W:] * incl], axis=1)
               for p in pairs]
        t_inv = [eye2 - a_b[p] * msk_ref[5] for p in pairs]
        for lvl in range(1, 6):
            half = [_dot3(t_inv[p], bd(a_b[p] * msk_ref[5 + lvl])) for p in pairs]
            t_inv = [t_inv[p] - _dot3(half[p], bd(t_inv[p])) for p in pairs]
        akv = [_dot3(a_k[p], bd(v[p])) for p in pairs]
        g_col = [jnp.dot(split3(eye2 * g_last[p], 1), ones_ref[...], preferred_element_type=jnp.float32)
                 for p in pairs]
        xt = [jnp.concatenate([bh[p] * g_last[p], kf[p] * g_last[p]], axis=0).T for p in pairs]
        p0 = [p_ref[p] for p in pairs]
        zy = [_dot3(kr[p], bd(p0[p])) for p in pairs]
        u = [-_dot3(t_inv[p], bd(zy[p][:CHUNK] + akv[p])) for p in pairs]
        y = [zy[p][CHUNK:] + _dot3(a_r[p], jnp.concatenate([bd(u[p]), bd(v[p])], axis=0)) for p in pairs]
        delta = [_dot3(xt[p], jnp.concatenate([u[p], v[p]], axis=0)) for p in pairs]
        for p in pairs:
            p_ref[p] = g_col[p] * p0[p] + delta[p][:CHUNK] * m0 + delta[p][CHUNK:] * m1
            y_ref[0, rows, sls[p]] = y[p]
        return carry

    lax.fori_loop(0, n_chunks, chunk, 0)

    @pl.when(tblk == pl.num_programs(1) - 1)
    def _():
        sT_ref[0] = p_ref[...]


def rwkv7_chunked(r, logw, k, v, kk, a, state0, tb=256):
    B, T, _ = r.shape
    tb = min(tb, T)
    s0 = state0.astype(jnp.float32).reshape(B, N_PAIR, 2, HEAD_A, HEAD_A)
    s0 = s0.transpose(0, 1, 4, 2, 3).reshape(B, N_PAIR, HEAD_A, PAIR_W)
    t_i = jnp.arange(CHUNK)[:, None]
    s_i = (jnp.arange(PAIR_W) % HEAD_A)[None, :]
    lane_head = (jnp.arange(PAIR_W) // HEAD_A)[None, :]
    ones_row = jnp.ones((CHUNK, 1), jnp.int32)
    masks = [t_i == s_i, (lane_head == 0) * ones_row, (lane_head == 1) * ones_row, s_i < t_i, s_i <= t_i]
    for m in (1, 2, 4, 8, 16, 32):
        masks.append((t_i // (2 * m) == s_i // (2 * m)) & (t_i % (2 * m) >= m) & (s_i % (2 * m) < m))
    masks = jnp.stack([mk.astype(jnp.float32) for mk in masks])
    ltri = jnp.tile((jnp.arange(CHUNK)[None, :] <= jnp.arange(CHUNK)[:, None]), (1, 3)).astype(jnp.bfloat16)
    ones3 = jnp.tile((lane_head.T == lane_head), (3, 1)).astype(jnp.bfloat16)
    seq = pl.BlockSpec((1, tb, WIDTH_A), lambda b, t: (b, t, 0))
    st = pl.BlockSpec((1, N_PAIR, HEAD_A, PAIR_W), lambda b, t: (b, 0, 0, 0))
    y, sT = pl.pallas_call(
        functools.partial(_rwkv_chunk_kernel, n_chunks=tb // CHUNK),
        out_shape=(jax.ShapeDtypeStruct((B, T, WIDTH_A), jnp.float32),
                   jax.ShapeDtypeStruct((B, N_PAIR, HEAD_A, PAIR_W), jnp.float32)),
        grid=(B, T // tb),
        in_specs=[seq] * 6 + [st, _const_spec((CHUNK, 3 * CHUNK)), _const_spec((3 * PAIR_W, PAIR_W)),
                              _const_spec((11, CHUNK, PAIR_W))],
        out_specs=(seq, st),
        scratch_shapes=[pltpu.VMEM((N_PAIR, HEAD_A, PAIR_W), jnp.float32)],
        compiler_params=pltpu.CompilerParams(
            dimension_semantics=("arbitrary", "arbitrary"), vmem_limit_bytes=VMEM_LIMIT),
        name="rwkv7_chunked",
    )(r, logw, k, v, kk, a, s0, ltri, ones3, masks)
    sT = sT.reshape(B, N_PAIR, HEAD_A, 2, HEAD_A).transpose(0, 1, 3, 4, 2)
    return y, sT.reshape(B, N_HEADS_A, HEAD_A, HEAD_A)


def _head_sums(x, ones2):
    tiles = [jnp.dot(_split2(x[:, c:c + PAIR_W]), ones2, preferred_element_type=jnp.float32)
             for c in range(0, WIDTH_A, PAIR_W)]
    return jnp.concatenate(tiles, axis=1)


def _softplus(z):
    return jnp.maximum(z, 0.0) + jnp.log1p(jnp.exp(-jnp.abs(z)))


def _rwkv_pre_kernel(seg_ref, prev_ref, shift_ref, mu_ref, vec_ref, wa_ref, gup_ref, ones_ref,
                     r_ref, w_ref, k_ref, v_ref, kk_ref, a_ref, g_ref, bonus_ref):
    seg = seg_ref[0]
    tb = seg.shape[0]
    before = jnp.where(pl.program_id(1) == 0, shift_ref[0], prev_ref[0, 7:8, :])
    row = lax.broadcasted_iota(jnp.int32, seg.shape, 0)
    shifted = jnp.where(row == 0, before, pltpu.roll(seg, 1, axis=0))
    xm = seg + mu_ref[...] * (shifted - seg)
    r = xm[:, :WIDTH_A]
    k = xm[:, WIDTH_A:2 * WIDTH_A]
    v = xm[:, 2 * WIDTH_A:3 * WIDTH_A]
    wa = xm[:, 3 * WIDTH_A:3 * WIDTH_A + LORA_W + LORA_A]
    gl = xm[:, 3 * WIDTH_A + LORA_W + LORA_A:]
    lane = lax.broadcasted_iota(jnp.int32, wa.shape, 1)
    wa = jnp.where(lane < LORA_W, jnp.tanh(wa), wa).astype(jnp.bfloat16)
    lora = jnp.dot(wa, wa_ref[...], preferred_element_type=jnp.float32)
    w0, a0, k_k, k_a, r_k = (vec_ref[i:i + 1, :] for i in range(5))
    w_log = -_softplus(-(w0 + lora[:, :WIDTH_A])) - 0.5
    log_decay = -jnp.exp(w_log)
    a = jax.nn.sigmoid(a0 + lora[:, WIDTH_A:])
    g = jnp.dot(jax.nn.sigmoid(gl).astype(jnp.bfloat16), gup_ref[...], preferred_element_type=jnp.float32)
    ones2 = ones_ref[...]
    kk = k * k_k
    kk = kk * lax.rsqrt(jnp.maximum(_head_sums(kk * kk, ones2), 1e-24))
    kf = k * (1.0 + (a - 1.0) * k_a)
    r_ref[0], w_ref[0], k_ref[0], v_ref[0], kk_ref[0], a_ref[0], g_ref[0] = r, log_decay, kf, v, kk, a, g
    bonus_ref[0] = _head_sums(r * kf * r_k, ones2) * v


def _block_ones2():
    lane_head = jnp.arange(PAIR_W) // HEAD_A
    return (jnp.tile(lane_head, 2)[:, None] == lane_head[None, :]).astype(jnp.bfloat16)


def rwkv7_pre(seg, prev_row, p, tb=256):
    B, T, _ = seg.shape
    tb = min(tb, T)
    zeros = jnp.zeros((LORA_W, WIDTH_A), jnp.float32)
    wa_up = jnp.concatenate([jnp.concatenate([p['w_lora_up'], zeros], 1),
                             jnp.concatenate([zeros, p['a_lora_up']], 1)], 0).astype(jnp.bfloat16)
    vecs = jnp.stack([p['w0'], p['a0'], p['k_k'], p['k_a'], p['r_k'].reshape(WIDTH_A)])
    out = jax.ShapeDtypeStruct((B, T, WIDTH_A), jnp.float32)
    ospec = pl.BlockSpec((1, tb, WIDTH_A), lambda b, t: (b, t, 0))
    full = lambda shape: pl.BlockSpec(shape, lambda b, t: (0,) * len(shape))
    return pl.pallas_call(
        _rwkv_pre_kernel,
        out_shape=(out,) * 8,
        grid=(B, T // tb),
        in_specs=[pl.BlockSpec((1, tb, C_RWKV), lambda b, t: (b, t, 0)),
                  pl.BlockSpec((1, 8, C_RWKV), lambda b, t: (b, jnp.maximum(t * (tb // 8) - 1, 0), 0)),
                  pl.BlockSpec((1, 1, C_RWKV), lambda b, t: (b, 0, 0)),
                  full((1, C_RWKV)), full((5, WIDTH_A)), full((LORA_W + LORA_A, 2 * WIDTH_A)),
                  full((LORA_G, WIDTH_A)), full((2 * PAIR_W, PAIR_W))],
        out_specs=(ospec,) * 8,
        compiler_params=pltpu.CompilerParams(
            dimension_semantics=("arbitrary", "arbitrary"), vmem_limit_bytes=VMEM_LIMIT),
        name="rwkv7_pre",
    )(seg, seg, prev_row, p['shift_mu'].reshape(1, C_RWKV), vecs, wa_up,
      p['g_lora_up'].astype(jnp.bfloat16), _block_ones2())


SWA_BAND = WINDOW + CHUNK


def _swa_kernel(q_ref, k_ref, v_ref, kp_ref, vp_ref, bias_ref, sink_ref, o_ref, *, mask_start):
    n_chunks = q_ref.shape[1] // CHUNK
    k_all = jnp.concatenate([kp_ref[0], k_ref[0]], axis=0).astype(jnp.bfloat16)
    v_all = jnp.concatenate([vp_ref[0], v_ref[0]], axis=0).astype(jnp.bfloat16)
    first = pl.program_id(1) == 0
    key_chunk = lax.broadcasted_iota(jnp.int32, (GROUP_B * CHUNK, SWA_BAND), 1) // CHUNK
    for c in range(n_chunks):
        q_c = q_ref[0, c * CHUNK:(c + 1) * CHUNK, :].astype(jnp.bfloat16)
        k_c = k_all[c * CHUNK:c * CHUNK + SWA_BAND]
        v_c = v_all[c * CHUNK:c * CHUNK + SWA_BAND]
        dead = jnp.logical_and(first, key_chunk + (c - WIN_CHUNKS) < 0) if (mask_start and c < WIN_CHUNKS) else None
        kvs = range(N_KV_B)
        qg = [jnp.concatenate([q_c[:, (kv * GROUP_B + g) * HEAD_B:(kv * GROUP_B + g + 1) * HEAD_B]
                               for g in range(GROUP_B)], axis=0) for kv in kvs]
        s = [lax.dot_general(qg[kv], k_c[:, kv * HEAD_B:(kv + 1) * HEAD_B], (((1,), (1,)), ((), ())),
                             preferred_element_type=jnp.float32) * (HEAD_B ** -0.5) + bias_ref[kv] for kv in kvs]
        if dead is not None:
            s = [jnp.where(dead, NEG_INF, s[kv]) for kv in kvs]
        sink = [sink_ref[kv * GROUP_B * CHUNK:(kv + 1) * GROUP_B * CHUNK, 0:1] for kv in kvs]
        m = [jnp.maximum(jnp.max(s[kv], axis=-1, keepdims=True), sink[kv]) for kv in kvs]
        e = [jnp.exp(s[kv] - m[kv]) for kv in kvs]
        pr = [e[kv] / (jnp.sum(e[kv], axis=-1, keepdims=True) + jnp.exp(sink[kv] - m[kv])) for kv in kvs]
        og = [jnp.dot(pr[kv].astype(jnp.bfloat16), v_c[:, kv * HEAD_B:(kv + 1) * HEAD_B],
                      preferred_element_type=jnp.float32) for kv in kvs]
        o_ref[0, c * CHUNK:(c + 1) * CHUNK, :] = jnp.concatenate(
            [og[kv][g * CHUNK:(g + 1) * CHUNK] for kv in kvs for g in range(GROUP_B)], axis=1)


def swa_attention(seg_b, prev_k, prev_v, sinks, *, prev_is_seq, qb=512):
    B, T, _ = seg_b.shape
    qb = min(qb, T)
    slopes = 2.0 ** (-8.0 * jnp.arange(1, N_HEADS_B + 1, dtype=jnp.float32) / N_HEADS_B)
    dist = jnp.abs(jnp.arange(CHUNK)[:, None] - (jnp.arange(SWA_BAND) - WINDOW)[None, :]).astype(jnp.float32)
    bias = (-slopes[:, None, None] * dist).reshape(N_KV_B, GROUP_B * CHUNK, SWA_BAND)
    sink_tab = jnp.broadcast_to(jnp.repeat(sinks.astype(jnp.float32), CHUNK)[:, None], (N_HEADS_B * CHUNK, 128))
    kcol, vcol = WIDTH_B // KV_WIDTH_B, WIDTH_B // KV_WIDTH_B + 1
    if prev_is_seq:
        per = qb // WINDOW
        kp_spec = pl.BlockSpec((1, WINDOW, KV_WIDTH_B), lambda b, i: (b, jnp.maximum(i * per - 1, 0), kcol))
        vp_spec = pl.BlockSpec((1, WINDOW, KV_WIDTH_B), lambda b, i: (b, jnp.maximum(i * per - 1, 0), vcol))
    else:
        kp_spec = vp_spec = pl.BlockSpec((1, WINDOW, KV_WIDTH_B), lambda b, i: (b, 0, 0))
    return pl.pallas_call(
        functools.partial(_swa_kernel, mask_start=prev_is_seq),
        out_shape=jax.ShapeDtypeStruct((B, T, WIDTH_B), jnp.float32),
        grid=(B, T // qb),
        in_specs=[pl.BlockSpec((1, qb, WIDTH_B), lambda b, i: (b, i, 0)),
                  pl.BlockSpec((1, qb, KV_WIDTH_B), lambda b, i: (b, i, kcol)),
                  pl.BlockSpec((1, qb, KV_WIDTH_B), lambda b, i: (b, i, vcol)),
                  kp_spec, vp_spec,
                  pl.BlockSpec((N_KV_B, GROUP_B * CHUNK, SWA_BAND), lambda b, i: (0, 0, 0)),
                  pl.BlockSpec((N_HEADS_B * CHUNK, 128), lambda b, i: (0, 0))],
        out_specs=pl.BlockSpec((1, qb, WIDTH_B), lambda b, i: (b, i, 0)),
        compiler_params=pltpu.CompilerParams(
            dimension_semantics=("arbitrary", "arbitrary"), vmem_limit_bytes=VMEM_LIMIT),
        name="swa_attention",
    )(seg_b, seg_b, seg_b, prev_k, prev_v, bias, sink_tab)


def _layer_norm_rows(h, g, b, eps=1e-5):
    mu = jnp.mean(h, axis=-1, keepdims=True)
    d = h - mu
    var = jnp.mean(d * d, axis=-1, keepdims=True)
    return d * lax.rsqrt(var + eps) * g + b


def _const_spec(shape):
    return pl.BlockSpec(shape, lambda *_: (0,) * len(shape), pipeline_mode=pl.Buffered(1))


def _merge_kernel(x_ref, y_ref, bonus_ref, g_ref, ob_ref, gates_ref, lnx_ref, ln1_ref, ones_ref,
                  pa_ref, pb_ref, wout_ref, o_ref):
    ones2 = ones_ref[...]
    y = y_ref[...]
    mean = _head_sums(y, ones2) * (1.0 / HEAD_A)
    d = y - mean
    var = _head_sums(d * d, ones2) * (1.0 / HEAD_A)
    yn = d * lax.rsqrt(var + GN_EPS) * lnx_ref[0:1, :] + lnx_ref[1:2, :]
    o_a = ((yn + bonus_ref[...]) * g_ref[...]).astype(jnp.bfloat16)
    br_a = jnp.dot(o_a, pa_ref[...], preferred_element_type=jnp.float32)
    br_b = jnp.dot(ob_ref[...].astype(jnp.bfloat16), pb_ref[...], preferred_element_type=jnp.float32)
    gates = jax.nn.sigmoid(gates_ref[...])
    merged = gates[:, :D_MODEL] * br_a + gates[:, D_MODEL:] * br_b
    h = ALPHA * x_ref[...] + jnp.dot(merged.astype(jnp.bfloat16), wout_ref[...], preferred_element_type=jnp.float32)
    o_ref[...] = _layer_norm_rows(h, ln1_ref[0:1, :], ln1_ref[1:2, :])


def branch_merge(x, y, bonus, g, o_b, gates, p, tm=256):
    n = x.shape[0]
    tm = _pick(n, (tm, 128, 64))
    rows = lambda w: pl.BlockSpec((tm, w), lambda i: (i, 0))
    return pl.pallas_call(
        _merge_kernel,
        out_shape=jax.ShapeDtypeStruct((n, D_MODEL), jnp.float32),
        grid=(n // tm,),
        in_specs=[rows(D_MODEL), rows(WIDTH_A), rows(WIDTH_A), rows(WIDTH_A), rows(WIDTH_B), rows(C_GATE),
                  _const_spec((2, WIDTH_A)), _const_spec((2, D_MODEL)), _const_spec((2 * PAIR_W, PAIR_W)),
                  _const_spec((WIDTH_A, D_MODEL)), _const_spec((WIDTH_B, D_MODEL)), _const_spec((D_MODEL, D_MODEL))],
        out_specs=rows(D_MODEL),
        compiler_params=pltpu.CompilerParams(dimension_semantics=("arbitrary",), vmem_limit_bytes=VMEM_LIMIT),
        name="branch_merge",
    )(x, y, bonus, g, o_b, gates, jnp.stack([p['lnx_g'], p['lnx_b']]), jnp.stack([p['ln1_g'], p['ln1_b']]),
      _block_ones2(), p['w_branch_a16'], p['w_branch_b16'], p['w_out16'])


def _mem_kernel(x_ref, mk_ref, mv_ref, ln2_ref, wq_ref, wo_ref, pwq_ref, o_ref, q_ref):
    x = x_ref[0]
    qm = jnp.dot(x.astype(jnp.bfloat16), wq_ref[...], preferred_element_type=jnp.float32).astype(jnp.bfloat16)
    mk = mk_ref[0].astype(jnp.bfloat16)
    mv = mv_ref[0].astype(jnp.bfloat16)
    outs = []
    for h in range(N_HEADS_M):
        cols = slice(h * HEAD_M, (h + 1) * HEAD_M)
        s = lax.dot_general(qm[:, cols], mk[:, cols], (((1,), (1,)), ((), ())),
                            preferred_element_type=jnp.float32) * (HEAD_M ** -0.5)
        e = jnp.exp(s - jnp.max(s, axis=-1, keepdims=True))
        pr = e / jnp.sum(e, axis=-1, keepdims=True)
        outs.append(jnp.dot(pr.astype(jnp.bfloat16), mv[:, cols], preferred_element_type=jnp.float32))
    o = jnp.concatenate(outs, axis=1).astype(jnp.bfloat16)
    h2 = ALPHA * x + jnp.dot(o, wo_ref[...], preferred_element_type=jnp.float32)
    x2 = _layer_norm_rows(h2, ln2_ref[0:1, :], ln2_ref[1:2, :])
    o_ref[0] = x2
    q_ref[0] = jnp.dot(x2.astype(jnp.bfloat16), pwq_ref[...], preferred_element_type=jnp.float32)


def mem_block(x, mk, mv, p, tm=256):
    B, T, _ = x.shape
    tm = _pick(T, (tm, 128, 64))
    wm = N_HEADS_M * HEAD_M
    rows = lambda w: pl.BlockSpec((1, tm, w), lambda b, i: (b, i, 0))
    mem = pl.BlockSpec((1, N_MEM, wm), lambda b, i: (b, 0, 0))
    out = jax.ShapeDtypeStruct((B, T, D_MODEL), jnp.float32)
    return pl.pallas_call(
        _mem_kernel,
        out_shape=(out, jax.ShapeDtypeStruct((B, T, PEER_HEADS * D_KEY), jnp.float32)),
        grid=(B, T // tm),
        in_specs=[rows(D_MODEL), mem, mem, _const_spec((2, D_MODEL)), _const_spec((D_MODEL, wm)),
                  _const_spec((wm, D_MODEL)), _const_spec((D_MODEL, PEER_HEADS * D_KEY))],
        out_specs=(rows(D_MODEL), rows(PEER_HEADS * D_KEY)),
        compiler_params=pltpu.CompilerParams(
            dimension_semantics=("arbitrary", "arbitrary"), vmem_limit_bytes=VMEM_LIMIT),
        name="mem_block",
    )(x, mk, mv, jnp.stack([p['ln2_g'], p['ln2_b']]), p['wq_mem16'], p['wo_mem16'], p['peer_wq16'])


ROUTE_TQ = 256


def _top_values(s, k):
    n_rows = s.shape[0]
    iota = lax.broadcasted_iota(jnp.int32, s.shape, 0).astype(jnp.float32)
    rank = jnp.full(s.shape, float(k), jnp.float32)
    out = []
    for step in range(k):
        m = jnp.max(s, axis=0, keepdims=True)
        first = jnp.min(jnp.where(s == m, iota, float(n_rows)), axis=0, keepdims=True)
        taken = iota == first
        s = jnp.where(taken, -jnp.inf, s)
        rank = jnp.where(taken, float(step), rank)
        out.append(m)
    return out, rank


def _peer_route_kernel(q_ref, keys_ref, n1_ref, c1_ref, rank2_ref, e2_ref):
    half = D_KEY // 2
    for h in range(PEER_HEADS):
        tops, scores = [], []
        for p in range(2):
            c0 = (2 * h + p) * half
            qs = q_ref[:, c0:c0 + half].astype(jnp.bfloat16)
            s = lax.dot_general(keys_ref[2 * h + p], qs, (((1,), (1,)), ((), ())),
                                preferred_element_type=jnp.float32)
            scores.append(s)
            tops.append(_top_values(s, TOPK))
        (t1, _), (t2, rank2) = tops
        t2all = jnp.concatenate(t2, axis=0)
        t1all = jnp.concatenate(t1, axis=0)
        rank8 = lax.broadcasted_iota(jnp.int32, (8, t1all.shape[1]), 0)
        cand_rows = [t1all + t2[0], t1all[:8] + t2[1]]
        for b in range(2, 8):
            cand_rows.append(jnp.where(rank8 < TOPK // (b + 1), t1all[:8] + t2[b], -jnp.inf))
        cand_rows.append(t1[0] + t2all[8:])
        sc, _ = _top_values(jnp.concatenate(cand_rows, axis=0), TOPK)
        z = jnp.zeros_like(sc[0])
        for kq in range(TOPK):
            z = z + jnp.exp(sc[kq] - sc[0])
        theta = sc[TOPK - 1]
        n1 = jnp.zeros_like(scores[0])
        for a in range(TOPK):
            n_a = jnp.sum((t1[a] + t2all >= theta).astype(jnp.float32), axis=0, keepdims=True)
            n1 = jnp.where(scores[0] == t1[a], n_a, n1)
        n1_ref[h] = n1
        rank2_ref[h] = rank2.astype(jnp.bfloat16)
        c1_ref[h] = jnp.exp(scores[0] - t1[0]) / z
        e2_ref[h] = jnp.exp(scores[1] - t2[0]).astype(jnp.bfloat16)


def peer_route(q, keys16):
    n = q.shape[0]
    tq = _pick(n, (ROUTE_TQ, 128))
    big = jax.ShapeDtypeStruct((PEER_HEADS, N_KEYS, n), jnp.float32)
    big16 = jax.ShapeDtypeStruct((PEER_HEADS, N_KEYS, n), jnp.bfloat16)
    bspec = pl.BlockSpec((PEER_HEADS, N_KEYS, tq), lambda i: (0, 0, i))
    return pl.pallas_call(
        _peer_route_kernel,
        out_shape=(big, big, big16, big16),
        grid=(n // tq,),
        in_specs=[pl.BlockSpec((tq, PEER_HEADS * D_KEY), lambda i: (i, 0)),
                  pl.BlockSpec((2 * PEER_HEADS, N_KEYS, D_KEY // 2), lambda i: (0, 0, 0))],
        out_specs=(bspec, bspec, bspec, bspec),
        compiler_params=pltpu.CompilerParams(
            dimension_semantics=("arbitrary",), vmem_limit_bytes=VMEM_LIMIT),
        name="peer_route",
    )(q, keys16)


PEER_TM = 512
PEER_ROWS = 8
PEER_TE = PEER_ROWS * N_KEYS
PEER_STRIP = 256


def _gelu(x):
    return 0.5 * x * (1.0 + lax.erf(x * (2.0 ** -0.5)))


def _peer_mix_kernel(x_ref, u_ref, vt_ref, n1_ref, c1_ref, rank2_ref, e2_ref, ln3_ref, o_ref,
                     x16_ref, h_ref, acc_ref):
    j = pl.program_id(1)

    @pl.when(j == 0)
    def _():
        x16_ref[...] = x_ref[...].T.astype(jnp.bfloat16)
        acc_ref[...] = jnp.zeros_like(acc_ref)

    half_te = PEER_TE // 2
    a_halves = [jnp.dot(u_ref[k * half_te:(k + 1) * half_te, :], x16_ref[...],
                        preferred_element_type=jnp.float32) for k in range(2)]

    def sublane_bcast16(row):
        return jnp.broadcast_to(row, (N_KEYS, row.shape[1])).astype(jnp.bfloat16)

    for r in range(PEER_ROWS):
        rows = slice(r * N_KEYS, (r + 1) * N_KEYS)
        for c0 in range(0, x_ref.shape[0], 128):
            cols = slice(c0, c0 + 128)
            gate = None
            for h in range(PEER_HEADS):
                keep = rank2_ref[h, :, cols] < sublane_bcast16(n1_ref[h, r:r + 1, cols])
                w = jnp.where(keep, e2_ref[h, :, cols], jnp.zeros((), jnp.bfloat16))
                w = w * sublane_bcast16(c1_ref[h, r:r + 1, cols])
                gate = w if gate is None else gate + w
            a_rows = a_halves[r // (PEER_ROWS // 2)][(r % (PEER_ROWS // 2)) * N_KEYS:(r % (PEER_ROWS // 2) + 1) * N_KEYS]
            h_ref[rows, cols] = gate * _gelu(a_rows[:, cols]).astype(jnp.bfloat16)
    acc_ref[...] += jnp.dot(vt_ref[0], h_ref[...], preferred_element_type=jnp.float32)

    @pl.when(j == pl.num_programs(1) - 1)
    def _():
        o_ref[...] = _layer_norm_rows(ALPHA * x_ref[...] + acc_ref[...].T, ln3_ref[0:1, :], ln3_ref[1:2, :])


def peer_mix(x, u16, vt16, n1, c1, rank2, e2, ln3):
    n, d = x.shape
    tm = _pick(n, (PEER_TM, 256, 128))
    n_exp = u16.shape[0]
    row_spec = pl.BlockSpec((PEER_HEADS, PEER_ROWS, tm), lambda i, j: (0, j, i))
    all_spec = pl.BlockSpec((PEER_HEADS, N_KEYS, tm), lambda i, j: (0, 0, i), pipeline_mode=pl.Buffered(1))
    return pl.pallas_call(
        _peer_mix_kernel,
        out_shape=jax.ShapeDtypeStruct((n, d), jnp.float32),
        grid=(n // tm, n_exp // PEER_TE),
        in_specs=[pl.BlockSpec((tm, d), lambda i, j: (i, 0), pipeline_mode=pl.Buffered(1)),
                  pl.BlockSpec((PEER_TE, d), lambda i, j: (j, 0)),
                  pl.BlockSpec((1, d, PEER_TE), lambda i, j: (j, 0, 0)),
                  row_spec, row_spec, all_spec, all_spec, _const_spec((2, d))],
        out_specs=pl.BlockSpec((tm, d), lambda i, j: (i, 0)),
        scratch_shapes=[pltpu.VMEM((d, tm), jnp.bfloat16), pltpu.VMEM((PEER_TE, tm), jnp.bfloat16),
                        pltpu.VMEM((d, tm), jnp.float32)],
        compiler_params=pltpu.CompilerParams(
            dimension_semantics=("arbitrary", "arbitrary"), vmem_limit_bytes=VMEM_LIMIT),
        name="peer_mix",
    )(x, u16, vt16, n1, c1, rank2, e2, ln3)


def peer_block(x, q, p):
    stats = peer_route(q, p['peer_keys16'])
    return peer_mix(x, p['peer_u16'], p['peer_vt16'], *stats, jnp.stack([p['ln3_g'], p['ln3_b']]))


def layer_norm(x, g, b, eps=1e-5):
    xf = x.astype(jnp.float32)
    mu = jnp.mean(xf, -1, keepdims=True)
    var = jnp.mean(jnp.square(xf - mu), -1, keepdims=True)
    return ((xf - mu) * lax.rsqrt(var + eps) * g + b).astype(x.dtype)


def rwkv7_scan(r, w, k, v, kk, a, state0):
    def step(S, inp):
        r_t, w_t, k_t, v_t, kk_t, a_t = inp
        sa = jnp.einsum('bhij,bhj->bhi', S, -kk_t)
        S = (S * w_t[:, :, None, :] + sa[..., None] * (kk_t * a_t)[:, :, None, :]
             + v_t[..., None] * k_t[:, :, None, :])
        y = jnp.einsum('bhij,bhj->bhi', S, r_t)
        return S, y
    xs = tuple(jnp.moveaxis(t, 1, 0) for t in (r, w, k, v, kk, a))
    S, ys = lax.scan(step, state0.astype(jnp.float32), xs)
    return jnp.moveaxis(ys, 0, 1), S


def rwkv7_branch(seg, prev_row, state0, p):
    B, T, _ = seg.shape
    f32 = jnp.float32
    shifted = jnp.concatenate([prev_row.astype(seg.dtype), seg[:, :-1]], axis=1)
    xm = seg + p['shift_mu'] * (shifted - seg)
    r, k, v, wl, al, gl = jnp.split(xm, SPLIT_A, axis=-1)
    w_log = -jax.nn.softplus(-(p['w0'] + jnp.tanh(wl) @ p['w_lora_up']).astype(f32)) - 0.5
    decay = jnp.exp(-jnp.exp(w_log))
    a = jax.nn.sigmoid((p['a0'] + al @ p['a_lora_up']).astype(f32))
    g = jax.nn.sigmoid(gl) @ p['g_lora_up']
    heads = lambda t: t.reshape(B, T, N_HEADS_A, HEAD_A)
    kk = heads((k * p['k_k']).astype(f32))
    kk = kk * lax.rsqrt(jnp.maximum(jnp.sum(kk * kk, -1, keepdims=True), 1e-24))
    kf = k.astype(f32) * (1.0 + (a - 1.0) * p['k_a'].astype(f32))
    rh, kh, vh, ah, wh = heads(r.astype(f32)), heads(kf), heads(v.astype(f32)), heads(a), heads(decay)
    flat = lambda t: t.reshape(B, T, WIDTH_A)
    y, s_final = rwkv7_scan_pallas(flat(rh), flat(wh), flat(kh), flat(vh), flat(kk), flat(ah), state0)
    y = heads(y)
    mean = jnp.mean(y, -1, keepdims=True)
    var = jnp.mean(jnp.square(y - mean), -1, keepdims=True)
    yn = ((y - mean) * lax.rsqrt(var + GN_EPS)).reshape(B, T, WIDTH_A) * p['lnx_g'] + p['lnx_b']
    bonus = jnp.sum(rh * kh * p['r_k'].astype(f32), -1, keepdims=True) * vh
    out = (yn + bonus.reshape(B, T, WIDTH_A)) * g
    return out.astype(seg.dtype), s_final, seg[:, -1:]


def alibi_bias(q_pos, k_pos):
    slopes = 2.0 ** (-8.0 * jnp.arange(1, N_HEADS_B + 1, dtype=jnp.float32) / N_HEADS_B)
    dist = jnp.abs(q_pos[:, None] - k_pos[None, :]).astype(jnp.float32)
    return -slopes.reshape(N_KV_B, GROUP_B, 1, 1) * dist


def sink_attention(q, k, v, bias, mask, sinks):
    s = jnp.einsum('bnqhgd,bnjhd->bnhgqj', q, k).astype(jnp.float32) * (HEAD_B ** -0.5) + bias
    if mask is not None:
        s = jnp.where(mask, s, NEG_INF)
    sink = sinks.astype(jnp.float32).reshape(N_KV_B, GROUP_B, 1, 1)
    m = jnp.maximum(jnp.max(s, -1, keepdims=True), sink)
    p = jnp.exp(s - m)
    p = p / (jnp.sum(p, -1, keepdims=True) + jnp.exp(sink - m))
    return jnp.einsum('bnhgqj,bnjhd->bnqhgd', p.astype(v.dtype), v)


def swa_prompt(q, k, v, sinks):
    B, S = q.shape[:2]
    NC = S // CHUNK
    band = (WIN_CHUNKS + 1) * CHUNK
    qc = q.reshape(B, NC, CHUNK, N_KV_B, GROUP_B, HEAD_B)
    pad = ((0, 0), (WIN_CHUNKS, 0), (0, 0), (0, 0), (0, 0))
    kp = jnp.pad(k.reshape(B, NC, CHUNK, N_KV_B, HEAD_B), pad)
    vp = jnp.pad(v.reshape(B, NC, CHUNK, N_KV_B, HEAD_B), pad)
    kb = jnp.concatenate([kp[:, o:o + NC] for o in range(WIN_CHUNKS + 1)], axis=2)
    vb = jnp.concatenate([vp[:, o:o + NC] for o in range(WIN_CHUNKS + 1)], axis=2)
    bias = alibi_bias(jnp.arange(CHUNK), jnp.arange(band) - WIN_CHUNKS * CHUNK)
    key_chunk = jnp.arange(NC)[:, None] - WIN_CHUNKS + (jnp.arange(band) // CHUNK)[None, :]
    mask = (key_chunk >= 0)[None, :, None, None, None, :]
    o = sink_attention(qc, kb, vb, bias, mask, sinks)
    return o.reshape(B, S, WIDTH_B)


def swa_sample(q, k, v, k_cache, v_cache, sinks):
    B, T = q.shape[:2]
    Lc = k_cache.shape[1]
    k_all = jnp.concatenate([k_cache.astype(k.dtype), k], axis=1)
    v_all = jnp.concatenate([v_cache.astype(v.dtype), v], axis=1)
    bias = alibi_bias(jnp.arange(T), jnp.arange(Lc + T) - Lc)
    o = sink_attention(q[:, None], k_all[:, None], v_all[:, None], bias, None, sinks)
    return o.reshape(B, T, WIDTH_B), k_all[:, -Lc:], v_all[:, -Lc:]


def mem_attention(x, mk, mv, p):
    B, T, _ = x.shape
    q = mm(x, p['wq_mem16']).reshape(B, T, N_HEADS_M, HEAD_M)
    s = jnp.einsum('bthd,bmhd->bhtm', q, mk.astype(q.dtype)).astype(jnp.float32) * (HEAD_M ** -0.5)
    pr = jax.nn.softmax(s, axis=-1).astype(x.dtype)
    o = jnp.einsum('bhtm,bmhd->bthd', pr, mv.astype(x.dtype)).reshape(B, T, N_HEADS_M * HEAD_M)
    return mm(o, p['wo_mem16'])


def peer_ffn(x, p):
    B, T, D = x.shape
    n = B * T
    nblk = -(-n // PEER_BLOCK)
    xt = x.reshape(nblk, PEER_BLOCK, D)
    qa = mm(x, p['peer_wq16']).reshape(nblk, PEER_BLOCK, PEER_HEADS * D_KEY)
    sub_keys, u_tab, v_tab = p['peer_sub_keys'], p['peer_u'], p['peer_v']

    def peer_block(args):
        xb, qb = args
        q = qb.reshape(PEER_BLOCK, PEER_HEADS, 2, D_KEY // 2)
        s = jnp.einsum('thpd,hpnd->thpn', q, sub_keys).astype(jnp.float32)
        s1, i1 = lax.top_k(s[:, :, 0], TOPK)
        s2, i2 = lax.top_k(s[:, :, 1], TOPK)
        cand = (s1[..., :, None] + s2[..., None, :]).reshape(PEER_BLOCK, PEER_HEADS, TOPK * TOPK)
        cidx = (i1[..., :, None] * N_KEYS + i2[..., None, :]).reshape(PEER_BLOCK, PEER_HEADS, TOPK * TOPK)
        sc, pos = lax.top_k(cand, TOPK)
        e = jnp.take_along_axis(cidx, pos, axis=-1)
        g = jax.nn.softmax(sc, axis=-1)
        act = jax.nn.gelu(jnp.einsum('td,thkd->thk', xb, u_tab[e]).astype(jnp.float32), approximate=False)
        return jnp.einsum('thk,thkd->td', (g * act).astype(xb.dtype), v_tab[e])

    y = lax.map(peer_block, (xt, qa)).reshape(nblk * PEER_BLOCK, D)
    return y.reshape(B, T, D)


def trunk_layer(x, p, mem_k, mem_v, rwkv_state, shift_row, swa_k_cache, swa_v_cache):
    B, T, _ = x.shape
    n = B * T
    x2d = x.reshape(n, D_MODEL)
    seg_a = matmul(x2d, p['w_in_a16']).reshape(B, T, C_RWKV)
    seg_b = matmul(x2d, p['w_in_b16']).reshape(B, T, C_SWA)
    gates = matmul(x2d, p['w_in_g16'])
    r, w, k, v, kk, a, g, bonus = rwkv7_pre(seg_a, shift_row.astype(jnp.float32), p)
    y, rwkv_new = rwkv7_chunked(r, w, k, v, kk, a, rwkv_state)
    shift_new = seg_a[:, -1:]
    k_new = seg_b[:, :, WIDTH_B:WIDTH_B + KV_WIDTH_B].reshape(B, T, N_KV_B, HEAD_B)
    v_new = seg_b[:, :, WIDTH_B + KV_WIDTH_B:].reshape(B, T, N_KV_B, HEAD_B)
    if swa_k_cache is None:
        o_b = swa_attention(seg_b, seg_b, seg_b, p['attn_sinks'], prev_is_seq=True)
        swa_k_new, swa_v_new = k_new[:, -WINDOW:], v_new[:, -WINDOW:]
    else:
        o_b = swa_attention(seg_b, swa_k_cache.reshape(B, WINDOW, KV_WIDTH_B),
                            swa_v_cache.reshape(B, WINDOW, KV_WIDTH_B), p['attn_sinks'], prev_is_seq=False)
        swa_k_new = jnp.concatenate([swa_k_cache, k_new], axis=1)[:, -WINDOW:]
        swa_v_new = jnp.concatenate([swa_v_cache, v_new], axis=1)[:, -WINDOW:]
    flat = lambda t: t.reshape(n, t.shape[-1])
    x1 = branch_merge(x2d, flat(y), flat(bonus), flat(g), flat(o_b), gates, p)
    wm = N_HEADS_M * HEAD_M
    x2, q = mem_block(x1.reshape(B, T, D_MODEL), mem_k.reshape(-1, N_MEM, wm), mem_v.reshape(-1, N_MEM, wm), p)
    x3 = peer_block(flat(x2), flat(q), p)
    return x3.reshape(B, T, D_MODEL), rwkv_new, shift_new, swa_k_new, swa_v_new


_MM_WEIGHTS = ('w_branch_a', 'w_branch_b', 'w_out', 'wq_mem', 'wk_mem', 'wv_mem', 'wo_mem', 'peer_wq')


def kernel(x_prompt, x_sample, state_rwkv, state_shift, cache_swa_k, cache_swa_v, cache_mem_k, cache_mem_v, mem_prompt, w_in, shift_mu, w0, w_lora_up, a0, a_lora_up, g_lora_up, k_k, k_a, r_k, lnx_g, lnx_b, attn_sinks, w_branch_a, w_branch_b, w_out, ln1_g, ln1_b, wq_mem, wk_mem, wv_mem, wo_mem, ln2_g, ln2_b, peer_wq, peer_sub_keys, peer_u, peer_v, ln3_g, ln3_b):
    params = {
        'w_in': w_in, 'shift_mu': shift_mu, 'w0': w0, 'w_lora_up': w_lora_up, 'a0': a0,
        'a_lora_up': a_lora_up, 'g_lora_up': g_lora_up, 'k_k': k_k, 'k_a': k_a, 'r_k': r_k,
        'lnx_g': lnx_g, 'lnx_b': lnx_b, 'attn_sinks': attn_sinks, 'w_branch_a': w_branch_a,
        'w_branch_b': w_branch_b, 'w_out': w_out, 'ln1_g': ln1_g, 'ln1_b': ln1_b, 'wq_mem': wq_mem,
        'wk_mem': wk_mem, 'wv_mem': wv_mem, 'wo_mem': wo_mem, 'ln2_g': ln2_g, 'ln2_b': ln2_b,
        'peer_wq': peer_wq, 'peer_sub_keys': peer_sub_keys, 'peer_u': peer_u, 'peer_v': peer_v,
        'ln3_g': ln3_g, 'ln3_b': ln3_b,
    }
    B = x_prompt.shape[0]
    rwkv0 = jnp.zeros((B, N_HEADS_A, HEAD_A, HEAD_A), jnp.float32)
    shift0 = jnp.zeros((B, 1, C_RWKV), x_prompt.dtype)
    xp, xs = x_prompt, x_sample
    p_rw, p_sh, p_k, p_v, p_mk, p_mv = [], [], [], [], [], []
    s_rw, s_sh, s_k, s_v = [], [], [], []
    for l in range(DEPTH):
        p = {name: arr[l] for name, arr in params.items()}
        for name in _MM_WEIGHTS:
            p[name + '16'] = p[name].astype(jnp.bfloat16)
        w_in16 = p['w_in'].astype(jnp.bfloat16)
        p['w_in_a16'] = w_in16[:, :C_RWKV]
        p['w_in_b16'] = w_in16[:, C_RWKV:C_RWKV + C_SWA]
        p['w_in_g16'] = w_in16[:, C_RWKV + C_SWA:]
        p['peer_keys16'] = p['peer_sub_keys'].reshape(2 * PEER_HEADS, N_KEYS, D_KEY // 2).astype(jnp.bfloat16)
        p['peer_u16'] = p['peer_u'].astype(jnp.bfloat16)
        p['peer_vt16'] = (p['peer_v'].astype(jnp.bfloat16)
                          .reshape(p['peer_v'].shape[0] // PEER_TE, PEER_TE, D_MODEL).transpose(0, 2, 1))
        mk = mm(mem_prompt, p['wk_mem16']).reshape(B, N_MEM, N_HEADS_M, HEAD_M)
        mv = mm(mem_prompt, p['wv_mem16']).reshape(B, N_MEM, N_HEADS_M, HEAD_M)
        xp, rw, sh, kn, vn = trunk_layer(xp, p, mk, mv, rwkv0, shift0, None, None)
        p_rw.append(rw); p_sh.append(sh); p_k.append(kn); p_v.append(vn); p_mk.append(mk); p_mv.append(mv)
        xs, rw, sh, kn, vn = trunk_layer(xs, p, cache_mem_k[l], cache_mem_v[l], state_rwkv[l], state_shift[l],
                                         cache_swa_k[l], cache_swa_v[l])
        s_rw.append(rw); s_sh.append(sh); s_k.append(kn); s_v.append(vn)
    return (xp, xs,
            jnp.stack(p_rw), jnp.stack(p_sh), jnp.stack(p_k), jnp.stack(p_v), jnp.stack(p_mk), jnp.stack(p_mv),
            jnp.stack(s_rw), jnp.stack(s_sh), jnp.stack(s_k), jnp.stack(s_v))
```

```python
import functools

import jax
import jax.numpy as jnp
from jax import lax
from jax.experimental import pallas as pl
from jax.experimental.pallas import tpu as pltpu

D_MODEL = 2048
DEPTH = 2
CHUNK = 64
HEAD_A = 64
N_HEADS_A = 16
WIDTH_A = N_HEADS_A * HEAD_A
LORA_W = 64
LORA_A = 64
LORA_G = 128
GN_EPS = 64e-5
C_RWKV = 3 * WIDTH_A + LORA_W + LORA_A + LORA_G
HEAD_B = 64
N_HEADS_B = 16
N_KV_B = 4
GROUP_B = N_HEADS_B // N_KV_B
WIDTH_B = N_HEADS_B * HEAD_B
KV_WIDTH_B = N_KV_B * HEAD_B
WINDOW = 128
WIN_CHUNKS = WINDOW // CHUNK
C_SWA = WIDTH_B + 2 * KV_WIDTH_B
C_GATE = 2 * D_MODEL
N_MEM = 256
N_HEADS_M = 4
HEAD_M = 128
N_KEYS = 128
PEER_HEADS = 8
D_KEY = 256
TOPK = 16
ALPHA = (2.0 * DEPTH) ** 0.25
NEG_INF = -1e30

VMEM_LIMIT = 56 * 1024 * 1024


def _matmul_kernel(a_ref, b_ref, o_ref, a16_ref):
    @pl.when(pl.program_id(1) == 0)
    def _():
        a16_ref[...] = a_ref[...].astype(jnp.bfloat16)

    o_ref[...] = jnp.dot(a16_ref[...], b_ref[...], preferred_element_type=jnp.float32)


def _pick(n, cands):
    for c in cands:
        if n % c == 0:
            return c
    return n


def matmul(a, b16):
    m, k = a.shape
    n = b16.shape[1]
    tm = _pick(m, (512, 256, 128))
    tn = _pick(n, (2048, 1664, 1536, 1024, 512, 256, 128))
    return pl.pallas_call(
        _matmul_kernel,
        out_shape=jax.ShapeDtypeStruct((m, n), jnp.float32),
        grid=(m // tm, n // tn),
        in_specs=[pl.BlockSpec((tm, k), lambda i, j: (i, 0)),
                  pl.BlockSpec((k, tn), lambda i, j: (0, j))],
        out_specs=pl.BlockSpec((tm, tn), lambda i, j: (i, j)),
        scratch_shapes=[pltpu.VMEM((tm, k), jnp.bfloat16)],
        compiler_params=pltpu.CompilerParams(
            dimension_semantics=("arbitrary", "arbitrary"),
            vmem_limit_bytes=VMEM_LIMIT),
        name="matmul",
    )(a, b16)


def mm(x, w16):
    lead = x.shape[:-1]
    return matmul(x.reshape(-1, x.shape[-1]), w16).reshape(*lead, w16.shape[1])


N_PAIR = N_HEADS_A // 2
PAIR_W = 2 * HEAD_A


def _split2(x):
    hi = x.astype(jnp.bfloat16)
    lo = (x - hi.astype(jnp.float32)).astype(jnp.bfloat16)
    return jnp.concatenate([hi, lo], axis=1)


def _hl(x):
    hi = x.astype(jnp.bfloat16)
    return hi, (x - hi.astype(jnp.float32)).astype(jnp.bfloat16)


def _dot3(a, b):
    ah, al = _hl(a)
    bh, bl = _hl(b)
    return jnp.dot(jnp.concatenate([ah, ah, al], axis=1), jnp.concatenate([bh, bl, bh], axis=0),
                   preferred_element_type=jnp.float32)


def _dot3_nt(a, b):
    ah, al = _hl(a)
    bh, bl = _hl(b)
    return lax.dot_general(jnp.concatenate([ah, ah, al], axis=1), jnp.concatenate([bh, bl, bh], axis=1),
                           (((1,), (1,)), ((), ())), preferred_element_type=jnp.float32)


def _rwkv_chunk_kernel(r_ref, w_ref, k_ref, v_ref, kk_ref, a_ref, s0_ref, ltri_ref, ones_ref, msk_ref,
                       y_ref, sT_ref, p_ref, *, n_chunks):
    tblk = pl.program_id(1)

    @pl.when(tblk == 0)
    def _():
        p_ref[...] = s0_ref[0]

    eye2, m0, m1, strict, incl = (msk_ref[i] for i in range(5))
    bd = lambda y: jnp.concatenate([y * m0, y * m1], axis=0)

    def split3(x, axis):
        t1 = x.astype(jnp.bfloat16)
        d = x - t1.astype(jnp.float32)
        t2 = d.astype(jnp.bfloat16)
        t3 = (d - t2.astype(jnp.float32)).astype(jnp.bfloat16)
        return jnp.concatenate([t1, t2, t3], axis=axis)

    def chunk(c, carry):
        rows = pl.ds(pl.multiple_of(c * CHUNK, CHUNK), CHUNK)
        pairs = range(N_PAIR)
        sls = [pl.ds(p * PAIR_W, PAIR_W) for p in pairs]
        lw = [w_ref[0, rows, sl] for sl in sls]
        cum = [jnp.dot(ltri_ref[...], split3(lw[p], 0), preferred_element_type=jnp.float32) for p in pairs]
        g = [jnp.exp(cum[p]) for p in pairs]
        ginv = [jnp.exp(-cum[p]) for p in pairs]
        kk = [kk_ref[0, rows, sl] for sl in sls]
        kh = [kk[p] * jnp.exp(cum[p] - lw[p]) for p in pairs]
        bh = [kk[p] * a_ref[0, rows, sls[p]] * ginv[p] for p in pairs]
        kf = [k_ref[0, rows, sls[p]] * ginv[p] for p in pairs]
        rh = [r_ref[0, rows, sls[p]] * g[p] for p in pairs]
        v = [v_ref[0, rows, sl] for sl in sls]
        g_last = [g[p][CHUNK - 1:CHUNK, :] for p in pairs]
        kr = [jnp.concatenate([kh[p], rh[p]], axis=0) for p in pairs]
        gram = [_dot3_nt(kr[p], jnp.concatenate([bd(bh[p]), bd(kf[p])], axis=0)) for p in pairs]
        a_b = [gram[p][:CHUNK, :PAIR_W] * strict for p in pairs]
        a_k = [gram[p][:CHUNK, PAIR_W:] * strict for p in pairs]
        a_r = [jnp.concatenate([gram[p][CHUNK:, :PAIR_W] * incl, gram[p][CHUNK:, PAIR_W:] * incl], axis=1)
               for p in pairs]
        t_inv = [eye2 - a_b[p] * msk_ref[5] for p in pairs]
        for lvl in range(1, 6):
            half = [_dot3(t_inv[p], bd(a_b[p] * msk_ref[5 + lvl])) for p in pairs]
            t_inv = [t_inv[p] - _dot3(half[p], bd(t_inv[p])) for p in pairs]
        akv = [_dot3(a_k[p], bd(v[p])) for p in pairs]
        g_col = [jnp.dot(split3(eye2 * g_last[p], 1), ones_ref[...], preferred_element_type=jnp.float32)
                 for p in pairs]
        xt = [jnp.concatenate([bh[p] * g_last[p], kf[p] * g_last[p]], axis=0).T for p in pairs]
        p0 = [p_ref[p] for p in pairs]
        zy = [_dot3(kr[p], bd(p0[p])) for p in pairs]
        u = [-_dot3(t_inv[p], bd(zy[p][:CHUNK] + akv[p])) for p in pairs]
        y = [zy[p][CHUNK:] + _dot3(a_r[p], jnp.concatenate([bd(u[p]), bd(v[p])], axis=0)) for p in pairs]
        delta = [_dot3(xt[p], jnp.concatenate([u[p], v[p]], axis=0)) for p in pairs]
        for p in pairs:
            p_ref[p] = g_col[p] * p0[p] + delta[p][:CHUNK] * m0 + delta[p][CHUNK:] * m1
            y_ref[0, rows, sls[p]] = y[p]
        return carry

    lax.fori_loop(0, n_chunks, chunk, 0)

    @pl.when(tblk == pl.num_programs(1) - 1)
    def _():
        sT_ref[0] = p_ref[...]


def rwkv7_chunked(r, logw, k, v, kk, a, state0, tb=256):
    B, T, _ = r.shape
    tb = min(tb, T)
    s0 = state0.astype(jnp.float32).reshape(B, N_PAIR, 2, HEAD_A, HEAD_A)
    s0 = s0.transpose(0, 1, 4, 2, 3).reshape(B, N_PAIR, HEAD_A, PAIR_W)
    t_i = jnp.arange(CHUNK)[:, None]
    s_i = (jnp.arange(PAIR_W) % HEAD_A)[None, :]
    lane_head = (jnp.arange(PAIR_W) // HEAD_A)[None, :]
    ones_row = jnp.ones((CHUNK, 1), jnp.int32)
    masks = [t_i == s_i, (lane_head == 0) * ones_row, (lane_head == 1) * ones_row, s_i < t_i, s_i <= t_i]
    for m in (1, 2, 4, 8, 16, 32):
        masks.append((t_i // (2 * m) == s_i // (2 * m)) & (t_i % (2 * m) >= m) & (s_i % (2 * m) < m))
    masks = jnp.stack([mk.astype(jnp.float32) for mk in masks])
    ltri = jnp.tile((jnp.arange(CHUNK)[None, :] <= jnp.arange(CHUNK)[:, None]), (1, 3)).astype(jnp.bfloat16)
    ones3 = jnp.tile((lane_head.T == lane_head), (3, 1)).astype(jnp.bfloat16)
    seq = pl.BlockSpec((1, tb, WIDTH_A), lambda b, t: (b, t, 0))
    st = pl.BlockSpec((1, N_PAIR, HEAD_A, PAIR_W), lambda b, t: (b, 0, 0, 0))
    y, sT = pl.pallas_call(
        functools.partial(_rwkv_chunk_kernel, n_chunks=tb // CHUNK),
        out_shape=(jax.ShapeDtypeStruct((B, T, WIDTH_A), jnp.float32),
                   jax.ShapeDtypeStruct((B, N_PAIR, HEAD_A, PAIR_W), jnp.float32)),
        grid=(B, T // tb),
        in_specs=[seq] * 6 + [st, _const_spec((CHUNK, 3 * CHUNK)), _const_spec((3 * PAIR_W, PAIR_W)),
                              _const_spec((11, CHUNK, PAIR_W))],
        out_specs=(seq, st),
        scratch_shapes=[pltpu.VMEM((N_PAIR, HEAD_A, PAIR_W), jnp.float32)],
        compiler_params=pltpu.CompilerParams(
            dimension_semantics=("arbitrary", "arbitrary"), vmem_limit_bytes=VMEM_LIMIT),
        name="rwkv7_chunked",
    )(r, logw, k, v, kk, a, s0, ltri, ones3, masks)
    sT = sT.reshape(B, N_PAIR, HEAD_A, 2, HEAD_A).transpose(0, 1, 3, 4, 2)
    return y, sT.reshape(B, N_HEADS_A, HEAD_A, HEAD_A)


def _head_sums(x, ones2):
    tiles = [jnp.dot(_split2(x[:, c:c + PAIR_W]), ones2, preferred_element_type=jnp.float32)
             for c in range(0, WIDTH_A, PAIR_W)]
    return jnp.concatenate(tiles, axis=1)


def _softplus(z):
    return jnp.maximum(z, 0.0) + jnp.log1p(jnp.exp(-jnp.abs(z)))


def _rwkv_pre_kernel(seg_ref, prev_ref, shift_ref, mu_ref, vec_ref, wa_ref, gup_ref, ones_ref,
                     r_ref, w_ref, k_ref, v_ref, kk_ref, a_ref, g_ref, bonus_ref):
    seg = seg_ref[0]
    tb = seg.shape[0]
    before = jnp.where(pl.program_id(1) == 0, shift_ref[0], prev_ref[0, 7:8, :])
    row = lax.broadcasted_iota(jnp.int32, seg.shape, 0)
    shifted = jnp.where(row == 0, before, pltpu.roll(seg, 1, axis=0))
    xm = seg + mu_ref[...] * (shifted - seg)
    r = xm[:, :WIDTH_A]
    k = xm[:, WIDTH_A:2 * WIDTH_A]
    v = xm[:, 2 * WIDTH_A:3 * WIDTH_A]
    wa = xm[:, 3 * WIDTH_A:3 * WIDTH_A + LORA_W + LORA_A]
    gl = xm[:, 3 * WIDTH_A + LORA_W + LORA_A:]
    lane = lax.broadcasted_iota(jnp.int32, wa.shape, 1)
    wa = jnp.where(lane < LORA_W, jnp.tanh(wa), wa).astype(jnp.bfloat16)
    lora = jnp.dot(wa, wa_ref[...], preferred_element_type=jnp.float32)
    w0, a0, k_k, k_a, r_k = (vec_ref[i:i + 1, :] for i in range(5))
    w_log = -_softplus(-(w0 + lora[:, :WIDTH_A])) - 0.5
    log_decay = -jnp.exp(w_log)
    a = jax.nn.sigmoid(a0 + lora[:, WIDTH_A:])
    g = jnp.dot(jax.nn.sigmoid(gl).astype(jnp.bfloat16), gup_ref[...], preferred_element_type=jnp.float32)
    ones2 = ones_ref[...]
    kk = k * k_k
    kk = kk * lax.rsqrt(jnp.maximum(_head_sums(kk * kk, ones2), 1e-24))
    kf = k * (1.0 + (a - 1.0) * k_a)
    r_ref[0], w_ref[0], k_ref[0], v_ref[0], kk_ref[0], a_ref[0], g_ref[0] = r, log_decay, kf, v, kk, a, g
    bonus_ref[0] = _head_sums(r * kf * r_k, ones2) * v


def _block_ones2():
    lane_head = jnp.arange(PAIR_W) // HEAD_A
    return (jnp.tile(lane_head, 2)[:, None] == lane_head[None, :]).astype(jnp.bfloat16)


def rwkv7_pre(seg, prev_row, p, tb=256):
    B, T, _ = seg.shape
    tb = min(tb, T)
    zeros = jnp.zeros((LORA_W, WIDTH_A), jnp.float32)
    wa_up = jnp.concatenate([jnp.concatenate([p['w_lora_up'], zeros], 1),
                             jnp.concatenate([zeros, p['a_lora_up']], 1)], 0).astype(jnp.bfloat16)
    vecs = jnp.stack([p['w0'], p['a0'], p['k_k'], p['k_a'], p['r_k'].reshape(WIDTH_A)])
    out = jax.ShapeDtypeStruct((B, T, WIDTH_A), jnp.float32)
    ospec = pl.BlockSpec((1, tb, WIDTH_A), lambda b, t: (b, t, 0))
    full = lambda shape: pl.BlockSpec(shape, lambda b, t: (0,) * len(shape))
    return pl.pallas_call(
        _rwkv_pre_kernel,
        out_shape=(out,) * 8,
        grid=(B, T // tb),
        in_specs=[pl.BlockSpec((1, tb, C_RWKV), lambda b, t: (b, t, 0)),
                  pl.BlockSpec((1, 8, C_RWKV), lambda b, t: (b, jnp.maximum(t * (tb // 8) - 1, 0), 0)),
                  pl.BlockSpec((1, 1, C_RWKV), lambda b, t: (b, 0, 0)),
                  full((1, C_RWKV)), full((5, WIDTH_A)), full((LORA_W + LORA_A, 2 * WIDTH_A)),
                  full((LORA_G, WIDTH_A)), full((2 * PAIR_W, PAIR_W))],
        out_specs=(ospec,) * 8,
        compiler_params=pltpu.CompilerParams(
            dimension_semantics=("arbitrary", "arbitrary"), vmem_limit_bytes=VMEM_LIMIT),
        name="rwkv7_pre",
    )(seg, seg, prev_row, p['shift_mu'].reshape(1, C_RWKV), vecs, wa_up,
      p['g_lora_up'].astype(jnp.bfloat16), _block_ones2())


SWA_BAND = WINDOW + CHUNK


def _swa_kernel(q_ref, k_ref, v_ref, kp_ref, vp_ref, bias_ref, sink_ref, o_ref, *, mask_start):
    n_chunks = q_ref.shape[1] // CHUNK
    k_all = jnp.concatenate([kp_ref[0], k_ref[0]], axis=0).astype(jnp.bfloat16)
    v_all = jnp.concatenate([vp_ref[0], v_ref[0]], axis=0).astype(jnp.bfloat16)
    first = pl.program_id(1) == 0
    key_chunk = lax.broadcasted_iota(jnp.int32, (GROUP_B * CHUNK, SWA_BAND), 1) // CHUNK
    for c in range(n_chunks):
        q_c = q_ref[0, c * CHUNK:(c + 1) * CHUNK, :].astype(jnp.bfloat16)
        k_c = k_all[c * CHUNK:c * CHUNK + SWA_BAND]
        v_c = v_all[c * CHUNK:c * CHUNK + SWA_BAND]
        dead = jnp.logical_and(first, key_chunk + (c - WIN_CHUNKS) < 0) if (mask_start and c < WIN_CHUNKS) else None
        kvs = range(N_KV_B)
        qg = [jnp.concatenate([q_c[:, (kv * GROUP_B + g) * HEAD_B:(kv * GROUP_B + g + 1) * HEAD_B]
                               for g in range(GROUP_B)], axis=0) for kv in kvs]
        s = [lax.dot_general(qg[kv], k_c[:, kv * HEAD_B:(kv + 1) * HEAD_B], (((1,), (1,)), ((), ())),
                             preferred_element_type=jnp.float32) * (HEAD_B ** -0.5) + bias_ref[kv] for kv in kvs]
        if dead is not None:
            s = [jnp.where(dead, NEG_INF, s[kv]) for kv in kvs]
        sink = [sink_ref[kv * GROUP_B * CHUNK:(kv + 1) * GROUP_B * CHUNK, 0:1] for kv in kvs]
        m = [jnp.maximum(jnp.max(s[kv], axis=-1, keepdims=True), sink[kv]) for kv in kvs]
        e = [jnp.exp(s[kv] - m[kv]) for kv in kvs]
        pr = [e[kv] / (jnp.sum(e[kv], axis=-1, keepdims=True) + jnp.exp(sink[kv] - m[kv])) for kv in kvs]
        og = [jnp.dot(pr[kv].astype(jnp.bfloat16), v_c[:, kv * HEAD_B:(kv + 1) * HEAD_B],
                      preferred_element_type=jnp.float32) for kv in kvs]
        o_ref[0, c * CHUNK:(c + 1) * CHUNK, :] = jnp.concatenate(
            [og[kv][g * CHUNK:(g + 1) * CHUNK] for kv in kvs for g in range(GROUP_B)], axis=1)


def swa_attention(seg_b, prev_k, prev_v, sinks, *, prev_is_seq, qb=512):
    B, T, _ = seg_b.shape
    qb = min(qb, T)
    slopes = 2.0 ** (-8.0 * jnp.arange(1, N_HEADS_B + 1, dtype=jnp.float32) / N_HEADS_B)
    dist = jnp.abs(jnp.arange(CHUNK)[:, None] - (jnp.arange(SWA_BAND) - WINDOW)[None, :]).astype(jnp.float32)
    bias = (-slopes[:, None, None] * dist).reshape(N_KV_B, GROUP_B * CHUNK, SWA_BAND)
    sink_tab = jnp.broadcast_to(jnp.repeat(sinks.astype(jnp.float32), CHUNK)[:, None], (N_HEADS_B * CHUNK, 128))
    kcol, vcol = WIDTH_B // KV_WIDTH_B, WIDTH_B // KV_WIDTH_B + 1
    if prev_is_seq:
        per = qb // WINDOW
        kp_spec = pl.BlockSpec((1, WINDOW, KV_WIDTH_B), lambda b, i: (b, jnp.maximum(i * per - 1, 0), kcol))
        vp_spec = pl.BlockSpec((1, WINDOW, KV_WIDTH_B), lambda b, i: (b, jnp.maximum(i * per - 1, 0), vcol))
    else:
        kp_spec = vp_spec = pl.BlockSpec((1, WINDOW, KV_WIDTH_B), lambda b, i: (b, 0, 0))
    return pl.pallas_call(
        functools.partial(_swa_kernel, mask_start=prev_is_seq),
        out_shape=jax.ShapeDtypeStruct((B, T, WIDTH_B), jnp.float32),
        grid=(B, T // qb),
        in_specs=[pl.BlockSpec((1, qb, WIDTH_B), lambda b, i: (b, i, 0)),
                  pl.BlockSpec((1, qb, KV_WIDTH_B), lambda b, i: (b, i, kcol)),
                  pl.BlockSpec((1, qb, KV_WIDTH_B), lambda b, i: (b, i, vcol)),
                  kp_spec, vp_spec,
                  pl.BlockSpec((N_KV_B, GROUP_B * CHUNK, SWA_BAND), lambda b, i: (0, 0, 0)),
                  pl.BlockSpec((N_HEADS_B * CHUNK, 128), lambda b, i: (0, 0))],
        out_specs=pl.BlockSpec((1, qb, WIDTH_B), lambda b, i: (b, i, 0)),
        compiler_params=pltpu.CompilerParams(
            dimension_semantics=("arbitrary", "arbitrary"), vmem_limit_bytes=VMEM_LIMIT),
        name="swa_attention",
    )(seg_b, seg_b, seg_b, prev_k, prev_v, bias, sink_tab)


def _layer_norm_rows(h, g, b, eps=1e-5):
    mu = jnp.mean(h, axis=-1, keepdims=True)
    d = h - mu
    var = jnp.mean(d * d, axis=-1, keepdims=True)
    return d * lax.rsqrt(var + eps) * g + b


def _const_spec(shape):
    return pl.BlockSpec(shape, lambda *_: (0,) * len(shape), pipeline_mode=pl.Buffered(1))


def _merge_kernel(x_ref, y_ref, bonus_ref, g_ref, ob_ref, gates_ref, lnx_ref, ln1_ref, ones_ref,
                  pa_ref, pb_ref, wout_ref, o_ref):
    ones2 = ones_ref[...]
    y = y_ref[...]
    mean = _head_sums(y, ones2) * (1.0 / HEAD_A)
    d = y - mean
    var = _head_sums(d * d, ones2) * (1.0 / HEAD_A)
    yn = d * lax.rsqrt(var + GN_EPS) * lnx_ref[0:1, :] + lnx_ref[1:2, :]
    o_a = ((yn + bonus_ref[...]) * g_ref[...]).astype(jnp.bfloat16)
    br_a = jnp.dot(o_a, pa_ref[...], preferred_element_type=jnp.float32)
    br_b = jnp.dot(ob_ref[...].astype(jnp.bfloat16), pb_ref[...], preferred_element_type=jnp.float32)
    gates = jax.nn.sigmoid(gates_ref[...])
    merged = gates[:, :D_MODEL] * br_a + gates[:, D_MODEL:] * br_b
    h = ALPHA * x_ref[...] + jnp.dot(merged.astype(jnp.bfloat16), wout_ref[...], preferred_element_type=jnp.float32)
    o_ref[...] = _layer_norm_rows(h, ln1_ref[0:1, :], ln1_ref[1:2, :])


def branch_merge(x, y, bonus, g, o_b, gates, p, tm=256):
    n = x.shape[0]
    tm = _pick(n, (tm, 128, 64))
    rows = lambda w: pl.BlockSpec((tm, w), lambda i: (i, 0))
    return pl.pallas_call(
        _merge_kernel,
        out_shape=jax.ShapeDtypeStruct((n, D_MODEL), jnp.float32),
        grid=(n // tm,),
        in_specs=[rows(D_MODEL), rows(WIDTH_A), rows(WIDTH_A), rows(WIDTH_A), rows(WIDTH_B), rows(C_GATE),
                  _const_spec((2, WIDTH_A)), _const_spec((2, D_MODEL)), _const_spec((2 * PAIR_W, PAIR_W)),
                  _const_spec((WIDTH_A, D_MODEL)), _const_spec((WIDTH_B, D_MODEL)), _const_spec((D_MODEL, D_MODEL))],
        out_specs=rows(D_MODEL),
        compiler_params=pltpu.CompilerParams(dimension_semantics=("arbitrary",), vmem_limit_bytes=VMEM_LIMIT),
        name="branch_merge",
    )(x, y, bonus, g, o_b, gates, jnp.stack([p['lnx_g'], p['lnx_b']]), jnp.stack([p['ln1_g'], p['ln1_b']]),
      _block_ones2(), p['w_branch_a16'], p['w_branch_b16'], p['w_out16'])


def _mem_kernel(x_ref, mk_ref, mv_ref, ln2_ref, wq_ref, wo_ref, pwq_ref, o_ref, q_ref):
    x = x_ref[0]
    qm = jnp.dot(x.astype(jnp.bfloat16), wq_ref[...], preferred_element_type=jnp.float32).astype(jnp.bfloat16)
    mk = mk_ref[0].astype(jnp.bfloat16)
    mv = mv_ref[0].astype(jnp.bfloat16)
    outs = []
    for h in range(N_HEADS_M):
        cols = slice(h * HEAD_M, (h + 1) * HEAD_M)
        s = lax.dot_general(qm[:, cols], mk[:, cols], (((1,), (1,)), ((), ())),
                            preferred_element_type=jnp.float32) * (HEAD_M ** -0.5)
        e = jnp.exp(s - jnp.max(s, axis=-1, keepdims=True))
        pr = e / jnp.sum(e, axis=-1, keepdims=True)
        outs.append(jnp.dot(pr.astype(jnp.bfloat16), mv[:, cols], preferred_element_type=jnp.float32))
    o = jnp.concatenate(outs, axis=1).astype(jnp.bfloat16)
    h2 = ALPHA * x + jnp.dot(o, wo_ref[...], preferred_element_type=jnp.float32)
    x2 = _layer_norm_rows(h2, ln2_ref[0:1, :], ln2_ref[1:2, :])
    o_ref[0] = x2
    q_ref[0] = jnp.dot(x2.astype(jnp.bfloat16), pwq_ref[...], preferred_element_type=jnp.float32)


def mem_block(x, mk, mv, p, tm=256):
    B, T, _ = x.shape
    tm = _pick(T, (tm, 128, 64))
    wm = N_HEADS_M * HEAD_M
    rows = lambda w: pl.BlockSpec((1, tm, w), lambda b, i: (b, i, 0))
    mem = pl.BlockSpec((1, N_MEM, wm), lambda b, i: (b, 0, 0))
    out = jax.ShapeDtypeStruct((B, T, D_MODEL), jnp.float32)
    return pl.pallas_call(
        _mem_kernel,
        out_shape=(out, jax.ShapeDtypeStruct((B, T, PEER_HEADS * D_KEY), jnp.float32)),
        grid=(B, T // tm),
        in_specs=[rows(D_MODEL), mem, mem, _const_spec((2, D_MODEL)), _const_spec((D_MODEL, wm)),
                  _const_spec((wm, D_MODEL)), _const_spec((D_MODEL, PEER_HEADS * D_KEY))],
        out_specs=(rows(D_MODEL), rows(PEER_HEADS * D_KEY)),
        compiler_params=pltpu.CompilerParams(
            dimension_semantics=("arbitrary", "arbitrary"), vmem_limit_bytes=VMEM_LIMIT),
        name="mem_block",
    )(x, mk, mv, jnp.stack([p['ln2_g'], p['ln2_b']]), p['wq_mem16'], p['wo_mem16'], p['peer_wq16'])


ROUTE_TQ = 256


def _top_values(s, k):
    n_rows = s.shape[0]
    iota = lax.broadcasted_iota(jnp.int32, s.shape, 0).astype(jnp.float32)
    rank = jnp.full(s.shape, float(k), jnp.float32)
    out = []
    for step in range(k):
        m = jnp.max(s, axis=0, keepdims=True)
        first = jnp.min(jnp.where(s == m, iota, float(n_rows)), axis=0, keepdims=True)
        taken = iota == first
        s = jnp.where(taken, -jnp.inf, s)
        rank = jnp.where(taken, float(step), rank)
        out.append(m)
    return out, rank


def _peer_route_kernel(q_ref, keys_ref, n1_ref, c1_ref, rank2_ref, e2_ref):
    half = D_KEY // 2
    for h in range(PEER_HEADS):
        tops, scores = [], []
        for p in range(2):
            c0 = (2 * h + p) * half
            qs = q_ref[:, c0:c0 + half].astype(jnp.bfloat16)
            s = lax.dot_general(keys_ref[2 * h + p], qs, (((1,), (1,)), ((), ())),
                                preferred_element_type=jnp.float32)
            scores.append(s)
            tops.append(_top_values(s, TOPK))
        (t1, _), (t2, rank2) = tops
        t2all = jnp.concatenate(t2, axis=0)
        t1all = jnp.concatenate(t1, axis=0)
        rank8 = lax.broadcasted_iota(jnp.int32, (8, t1all.shape[1]), 0)
        cand_rows = [t1all + t2[0], t1all[:8] + t2[1]]
        for b in range(2, 8):
            cand_rows.append(jnp.where(rank8 < TOPK // (b + 1), t1all[:8] + t2[b], -jnp.inf))
        cand_rows.append(t1[0] + t2all[8:])
        sc, _ = _top_values(jnp.concatenate(cand_rows, axis=0), TOPK)
        z = jnp.zeros_like(sc[0])
        for kq in range(TOPK):
            z = z + jnp.exp(sc[kq] - sc[0])
        theta = sc[TOPK - 1]
        n1 = jnp.zeros_like(scores[0])
        for a in range(TOPK):
            n_a = jnp.sum((t1[a] + t2all >= theta).astype(jnp.float32), axis=0, keepdims=True)
            n1 = jnp.where(scores[0] == t1[a], n_a, n1)
        n1_ref[h] = n1
        rank2_ref[h] = rank2.astype(jnp.bfloat16)
        c1_ref[h] = jnp.exp(scores[0] - t1[0]) / z
        e2_ref[h] = jnp.exp(scores[1] - t2[0]).astype(jnp.bfloat16)


def peer_route(q, keys16):
    n = q.shape[0]
    tq = _pick(n, (ROUTE_TQ, 128))
    big = jax.ShapeDtypeStruct((PEER_HEADS, N_KEYS, n), jnp.float32)
    big16 = jax.ShapeDtypeStruct((PEER_HEADS, N_KEYS, n), jnp.bfloat16)
    bspec = pl.BlockSpec((PEER_HEADS, N_KEYS, tq), lambda i: (0, 0, i))
    return pl.pallas_call(
        _peer_route_kernel,
        out_shape=(big, big, big16, big16),
        grid=(n // tq,),
        in_specs=[pl.BlockSpec((tq, PEER_HEADS * D_KEY), lambda i: (i, 0)),
                  pl.BlockSpec((2 * PEER_HEADS, N_KEYS, D_KEY // 2), lambda i: (0, 0, 0))],
        out_specs=(bspec, bspec, bspec, bspec),
        compiler_params=pltpu.CompilerParams(
            dimension_semantics=("arbitrary",), vmem_limit_bytes=VMEM_LIMIT),
        name="peer_route",
    )(q, keys16)


PEER_TM = 512
PEER_ROWS = 8
PEER_TE = PEER_ROWS * N_KEYS


def _gelu(x):
    return 0.5 * x * (1.0 + lax.erf(x * (2.0 ** -0.5)))


def _peer_mix_kernel(x_ref, u_ref, vt_ref, n1_ref, c1_ref, rank2_ref, e2_ref, ln3_ref, o_ref,
                     x16_ref, h_ref, acc_ref):
    j = pl.program_id(1)

    @pl.when(j == 0)
    def _():
        x16_ref[...] = x_ref[...].T.astype(jnp.bfloat16)
        acc_ref[...] = jnp.zeros_like(acc_ref)

    half_te = PEER_TE // 2
    a_halves = [jnp.dot(u_ref[k * half_te:(k + 1) * half_te, :], x16_ref[...],
                        preferred_element_type=jnp.float32) for k in range(2)]

    def sublane_bcast16(row):
        return jnp.broadcast_to(row, (N_KEYS, row.shape[1])).astype(jnp.bfloat16)

    for r in range(PEER_ROWS):
        rows = slice(r * N_KEYS, (r + 1) * N_KEYS)
        for c0 in range(0, x_ref.shape[0], 128):
            cols = slice(c0, c0 + 128)
            gate = None
            for h in range(PEER_HEADS):
                keep = rank2_ref[h, :, cols] < sublane_bcast16(n1_ref[h, r:r + 1, cols])
                w = jnp.where(keep, e2_ref[h, :, cols], jnp.zeros((), jnp.bfloat16))
                w = w * sublane_bcast16(c1_ref[h, r:r + 1, cols])
                gate = w if gate is None else gate + w
            a_rows = a_halves[r // (PEER_ROWS // 2)][(r % (PEER_ROWS // 2)) * N_KEYS:(r % (PEER_ROWS // 2) + 1) * N_KEYS]
            h_ref[rows, cols] = gate * _gelu(a_rows[:, cols]).astype(jnp.bfloat16)
    acc_ref[...] += jnp.dot(vt_ref[0], h_ref[...], preferred_element_type=jnp.float32)

    @pl.when(j == pl.num_programs(1) - 1)
    def _():
        o_ref[...] = _layer_norm_rows(ALPHA * x_ref[...] + acc_ref[...].T, ln3_ref[0:1, :], ln3_ref[1:2, :])


def peer_mix(x, u16, vt16, n1, c1, rank2, e2, ln3):
    n, d = x.shape
    tm = _pick(n, (PEER_TM, 256, 128))
    n_exp = u16.shape[0]
    row_spec = pl.BlockSpec((PEER_HEADS, PEER_ROWS, tm), lambda i, j: (0, j, i))
    all_spec = pl.BlockSpec((PEER_HEADS, N_KEYS, tm), lambda i, j: (0, 0, i), pipeline_mode=pl.Buffered(1))
    return pl.pallas_call(
        _peer_mix_kernel,
        out_shape=jax.ShapeDtypeStruct((n, d), jnp.float32),
        grid=(n // tm, n_exp // PEER_TE),
        in_specs=[pl.BlockSpec((tm, d), lambda i, j: (i, 0), pipeline_mode=pl.Buffered(1)),
                  pl.BlockSpec((PEER_TE, d), lambda i, j: (j, 0)),
                  pl.BlockSpec((1, d, PEER_TE), lambda i, j: (j, 0, 0)),
                  row_spec, row_spec, all_spec, all_spec, _const_spec((2, d))],
        out_specs=pl.BlockSpec((tm, d), lambda i, j: (i, 0)),
        scratch_shapes=[pltpu.VMEM((d, tm), jnp.bfloat16), pltpu.VMEM((PEER_TE, tm), jnp.bfloat16),
                        pltpu.VMEM((d, tm), jnp.float32)],
        compiler_params=pltpu.CompilerParams(
            dimension_semantics=("arbitrary", "arbitrary"), vmem_limit_bytes=VMEM_LIMIT),
        name="peer_mix",
    )(x, u16, vt16, n1, c1, rank2, e2, ln3)


def peer_block(x, q, p):
    stats = peer_route(q, p['peer_keys16'])
    return peer_mix(x, p['peer_u16'], p['peer_vt16'], *stats, jnp.stack([p['ln3_g'], p['ln3_b']]))


def trunk_layer(x, p, mem_k, mem_v, rwkv_state, shift_row, swa_k_cache, swa_v_cache):
    B, T, _ = x.shape
    n = B * T
    x2d = x.reshape(n, D_MODEL)
    seg_a = matmul(x2d, p['w_in_a16']).reshape(B, T, C_RWKV)
    seg_b = matmul(x2d, p['w_in_b16']).reshape(B, T, C_SWA)
    gates = matmul(x2d, p['w_in_g16'])
    r, w, k, v, kk, a, g, bonus = rwkv7_pre(seg_a, shift_row.astype(jnp.float32), p)
    y, rwkv_new = rwkv7_chunked(r, w, k, v, kk, a, rwkv_state)
    shift_new = seg_a[:, -1:]
    k_new = seg_b[:, :, WIDTH_B:WIDTH_B + KV_WIDTH_B].reshape(B, T, N_KV_B, HEAD_B)
    v_new = seg_b[:, :, WIDTH_B + KV_WIDTH_B:].reshape(B, T, N_KV_B, HEAD_B)
    if swa_k_cache is None:
        o_b = swa_attention(seg_b, seg_b, seg_b, p['attn_sinks'], prev_is_seq=True)
        swa_k_new, swa_v_new = k_new[:, -WINDOW:], v_new[:, -WINDOW:]
    else:
        o_b = swa_attention(seg_b, swa_k_cache.reshape(B, WINDOW, KV_WIDTH_B),
                            swa_v_cache.reshape(B, WINDOW, KV_WIDTH_B), p['attn_sinks'], prev_is_seq=False)
        swa_k_new = jnp.concatenate([swa_k_cache, k_new], axis=1)[:, -WINDOW:]
        swa_v_new = jnp.concatenate([swa_v_cache, v_new], axis=1)[:, -WINDOW:]
    flat = lambda t: t.reshape(n, t.shape[-1])
    x1 = branch_merge(x2d, flat(y), flat(bonus), flat(g), flat(o_b), gates, p)
    wm = N_HEADS_M * HEAD_M
    x2, q = mem_block(x1.reshape(B, T, D_MODEL), mem_k.reshape(-1, N_MEM, wm), mem_v.reshape(-1, N_MEM, wm), p)
    x3 = peer_block(flat(x2), flat(q), p)
    return x3.reshape(B, T, D_MODEL), rwkv_new, shift_new, swa_k_new, swa_v_new


_MM_WEIGHTS = ('w_branch_a', 'w_branch_b', 'w_out', 'wq_mem', 'wk_mem', 'wv_mem', 'wo_mem', 'peer_wq')


def kernel(x_prompt, x_sample, state_rwkv, state_shift, cache_swa_k, cache_swa_v, cache_mem_k, cache_mem_v, mem_prompt, w_in, shift_mu, w0, w_lora_up, a0, a_lora_up, g_lora_up, k_k, k_a, r_k, lnx_g, lnx_b, attn_sinks, w_branch_a, w_branch_b, w_out, ln1_g, ln1_b, wq_mem, wk_mem, wv_mem, wo_mem, ln2_g, ln2_b, peer_wq, peer_sub_keys, peer_u, peer_v, ln3_g, ln3_b):
    params = {
        'w_in': w_in, 'shift_mu': shift_mu, 'w0': w0, 'w_lora_up': w_lora_up, 'a0': a0,
        'a_lora_up': a_lora_up, 'g_lora_up': g_lora_up, 'k_k': k_k, 'k_a': k_a, 'r_k': r_k,
        'lnx_g': lnx_g, 'lnx_b': lnx_b, 'attn_sinks': attn_sinks, 'w_branch_a': w_branch_a,
        'w_branch_b': w_branch_b, 'w_out': w_out, 'ln1_g': ln1_g, 'ln1_b': ln1_b, 'wq_mem': wq_mem,
        'wk_mem': wk_mem, 'wv_mem': wv_mem, 'wo_mem': wo_mem, 'ln2_g': ln2_g, 'ln2_b': ln2_b,
        'peer_wq': peer_wq, 'peer_sub_keys': peer_sub_keys, 'peer_u': peer_u, 'peer_v': peer_v,
        'ln3_g': ln3_g, 'ln3_b': ln3_b,
    }
    B = x_prompt.shape[0]
    rwkv0 = jnp.zeros((B, N_HEADS_A, HEAD_A, HEAD_A), jnp.float32)
    shift0 = jnp.zeros((B, 1, C_RWKV), x_prompt.dtype)
    xp, xs = x_prompt, x_sample
    p_rw, p_sh, p_k, p_v, p_mk, p_mv = [], [], [], [], [], []
    s_rw, s_sh, s_k, s_v = [], [], [], []
    for l in range(DEPTH):
        p = {name: arr[l] for name, arr in params.items()}
        for name in _MM_WEIGHTS:
            p[name + '16'] = p[name].astype(jnp.bfloat16)
        w_in16 = p['w_in'].astype(jnp.bfloat16)
        p['w_in_a16'] = w_in16[:, :C_RWKV]
        p['w_in_b16'] = w_in16[:, C_RWKV:C_RWKV + C_SWA]
        p['w_in_g16'] = w_in16[:, C_RWKV + C_SWA:]
        p['peer_keys16'] = p['peer_sub_keys'].reshape(2 * PEER_HEADS, N_KEYS, D_KEY // 2).astype(jnp.bfloat16)
        p['peer_u16'] = p['peer_u'].astype(jnp.bfloat16)
        p['peer_vt16'] = (p['peer_v'].astype(jnp.bfloat16)
                          .reshape(p['peer_v'].shape[0] // PEER_TE, PEER_TE, D_MODEL).transpose(0, 2, 1))
        mk = mm(mem_prompt, p['wk_mem16']).reshape(B, N_MEM, N_HEADS_M, HEAD_M)
        mv = mm(mem_prompt, p['wv_mem16']).reshape(B, N_MEM, N_HEADS_M, HEAD_M)
        xp, rw, sh, kn, vn = trunk_layer(xp, p, mk, mv, rwkv0, shift0, None, None)
        p_rw.append(rw); p_sh.append(sh); p_k.append(kn); p_v.append(vn); p_mk.append(mk); p_mv.append(mv)
        xs, rw, sh, kn, vn = trunk_layer(xs, p, cache_mem_k[l], cache_mem_v[l], state_rwkv[l], state_shift[l],
                                         cache_swa_k[l], cache_swa_v[l])
        s_rw.append(rw); s_sh.append(sh); s_k.append(kn); s_v.append(vn)
    return (xp, xs,
            jnp.stack(p_rw), jnp.stack(p_sh), jnp.stack(p_k), jnp.stack(p_v), jnp.stack(p_mk), jnp.stack(p_mv),
            jnp.stack(s_rw), jnp.stack(s_sh), jnp.stack(s_k), jnp.stack(s_v))
```

```python
import functools

import jax
import jax.numpy as jnp
from jax import lax
from jax.experimental import pallas as pl
from jax.experimental.pallas import tpu as pltpu

D_MODEL = 2048
DEPTH = 2
CHUNK = 64
HEAD_A = 64
N_HEADS_A = 16
WIDTH_A = N_HEADS_A * HEAD_A
LORA_W = 64
LORA_A = 64
LORA_G = 128
GN_EPS = 64e-5
C_RWKV = 3 * WIDTH_A + LORA_W + LORA_A + LORA_G
HEAD_B = 64
N_HEADS_B = 16
N_KV_B = 4
GROUP_B = N_HEADS_B // N_KV_B
WIDTH_B = N_HEADS_B * HEAD_B
KV_WIDTH_B = N_KV_B * HEAD_B
WINDOW = 128
WIN_CHUNKS = WINDOW // CHUNK
C_SWA = WIDTH_B + 2 * KV_WIDTH_B
C_GATE = 2 * D_MODEL
N_MEM = 256
N_HEADS_M = 4
HEAD_M = 128
N_KEYS = 128
PEER_HEADS = 8
D_KEY = 256
TOPK = 16
ALPHA = (2.0 * DEPTH) ** 0.25
NEG_INF = -1e30

VMEM_LIMIT = 56 * 1024 * 1024


def _matmul_kernel(a_ref, b_ref, o_ref, a16_ref):
    @pl.when(pl.program_id(1) == 0)
    def _():
        a16_ref[...] = a_ref[...].astype(jnp.bfloat16)

    o_ref[...] = jnp.dot(a16_ref[...], b_ref[...], preferred_element_type=jnp.float32)


def _pick(n, cands):
    for c in cands:
        if n % c == 0:
            return c
    return n


def matmul(a, b16):
    m, k = a.shape
    n = b16.shape[1]
    tm = _pick(m, (512, 256, 128))
    tn = _pick(n, (2048, 1664, 1536, 1024, 512, 256, 128))
    return pl.pallas_call(
        _matmul_kernel,
        out_shape=jax.ShapeDtypeStruct((m, n), jnp.float32),
        grid=(m // tm, n // tn),
        in_specs=[pl.BlockSpec((tm, k), lambda i, j: (i, 0)),
                  pl.BlockSpec((k, tn), lambda i, j: (0, j))],
        out_specs=pl.BlockSpec((tm, tn), lambda i, j: (i, j)),
        scratch_shapes=[pltpu.VMEM((tm, k), jnp.bfloat16)],
        compiler_params=pltpu.CompilerParams(
            dimension_semantics=("arbitrary", "arbitrary"),
            vmem_limit_bytes=VMEM_LIMIT),
        name="matmul",
    )(a, b16)


def mm(x, w16):
    lead = x.shape[:-1]
    return matmul(x.reshape(-1, x.shape[-1]), w16).reshape(*lead, w16.shape[1])


N_PAIR = N_HEADS_A // 2
PAIR_W = 2 * HEAD_A


def _split2(x):
    hi = x.astype(jnp.bfloat16)
    lo = (x - hi.astype(jnp.float32)).astype(jnp.bfloat16)
    return jnp.concatenate([hi, lo], axis=1)


def _hl(x):
    hi = x.astype(jnp.bfloat16)
    return hi, (x - hi.astype(jnp.float32)).astype(jnp.bfloat16)


def _dot3(a, b):
    ah, al = _hl(a)
    bh, bl = _hl(b)
    return jnp.dot(jnp.concatenate([ah, ah, al], axis=1), jnp.concatenate([bh, bl, bh], axis=0),
                   preferred_element_type=jnp.float32)


def _dot3_nt(a, b):
    ah, al = _hl(a)
    bh, bl = _hl(b)
    return lax.dot_general(jnp.concatenate([ah, ah, al], axis=1), jnp.concatenate([bh, bl, bh], axis=1),
                           (((1,), (1,)), ((), ())), preferred_element_type=jnp.float32)


def _rwkv_chunk_kernel(r_ref, w_ref, k_ref, v_ref, kk_ref, a_ref, s0_ref, ltri_ref, ones_ref, msk_ref,
                       y_ref, sT_ref, p_ref, *, n_chunks):
    tblk = pl.program_id(1)

    @pl.when(tblk == 0)
    def _():
        p_ref[...] = s0_ref[0]

    eye2, m0, m1, strict, incl = (msk_ref[i] for i in range(5))
    bd = lambda y: jnp.concatenate([y * m0, y * m1], axis=0)

    def split3(x, axis):
        t1 = x.astype(jnp.bfloat16)
        d = x - t1.astype(jnp.float32)
        t2 = d.astype(jnp.bfloat16)
        t3 = (d - t2.astype(jnp.float32)).astype(jnp.bfloat16)
        return jnp.concatenate([t1, t2, t3], axis=axis)

    def chunk(c, carry):
        rows = pl.ds(pl.multiple_of(c * CHUNK, CHUNK), CHUNK)
        pairs = range(N_PAIR)
        sls = [pl.ds(p * PAIR_W, PAIR_W) for p in pairs]
        lw = [w_ref[0, rows, sl] for sl in sls]
        cum = [jnp.dot(ltri_ref[...], split3(lw[p], 0), preferred_element_type=jnp.float32) for p in pairs]
        g = [jnp.exp(cum[p]) for p in pairs]
        ginv = [jnp.exp(-cum[p]) for p in pairs]
        kk = [kk_ref[0, rows, sl] for sl in sls]
        kh = [kk[p] * jnp.exp(cum[p] - lw[p]) for p in pairs]
        bh = [kk[p] * a_ref[0, rows, sls[p]] * ginv[p] for p in pairs]
        kf = [k_ref[0, rows, sls[p]] * ginv[p] for p in pairs]
        rh = [r_ref[0, rows, sls[p]] * g[p] for p in pairs]
        v = [v_ref[0, rows, sl] for sl in sls]
        g_last = [g[p][CHUNK - 1:CHUNK, :] for p in pairs]
        kr = [jnp.concatenate([kh[p], rh[p]], axis=0) for p in pairs]
        gram = [_dot3_nt(kr[p], jnp.concatenate([bd(bh[p]), bd(kf[p])], axis=0)) for p in pairs]
        a_b = [gram[p][:CHUNK, :PAIR_W] * strict for p in pairs]
        a_k = [gram[p][:CHUNK, PAIR_W:] * strict for p in pairs]
        a_r = [jnp.concatenate([gram[p][CHUNK:, :PAIR_W] * incl, gram[p][CHUNK:, PAIR_W:] * incl], axis=1)
               for p in pairs]
        t_inv = [eye2 - a_b[p] * msk_ref[5] for p in pairs]
        for lvl in range(1, 6):
            half = [_dot3(t_inv[p], bd(a_b[p] * msk_ref[5 + lvl])) for p in pairs]
            t_inv = [t_inv[p] - _dot3(half[p], bd(t_inv[p])) for p in pairs]
        akv = [_dot3(a_k[p], bd(v[p])) for p in pairs]
        g_col = [jnp.dot(split3(eye2 * g_last[p], 1), ones_ref[...], preferred_element_type=jnp.float32)
                 for p in pairs]
        xt = [jnp.concatenate([bh[p] * g_last[p], kf[p] * g_last[p]], axis=0).T for p in pairs]
        p0 = [p_ref[p] for p in pairs]
        zy = [_dot3(kr[p], bd(p0[p])) for p in pairs]
        u = [-_dot3(t_inv[p], bd(zy[p][:CHUNK] + akv[p])) for p in pairs]
        y = [zy[p][CHUNK:] + _dot3(a_r[p], jnp.concatenate([bd(u[p]), bd(v[p])], axis=0)) for p in pairs]
        delta = [_dot3(xt[p], jnp.concatenate([u[p], v[p]], axis=0)) for p in pairs]
        for p in pairs:
            p_ref[p] = g_col[p] * p0[p] + delta[p][:CHUNK] * m0 + delta[p][CHUNK:] * m1
            y_ref[0, rows, sls[p]] = y[p]
        return carry

    lax.fori_loop(0, n_chunks, chunk, 0)

    @pl.when(tblk == pl.num_programs(1) - 1)
    def _():
        sT_ref[0] = p_ref[...]


def rwkv7_chunked(r, logw, k, v, kk, a, state0, tb=256):
    B, T, _ = r.shape
    tb = min(tb, T)
    s0 = state0.astype(jnp.float32).reshape(B, N_PAIR, 2, HEAD_A, HEAD_A)
    s0 = s0.transpose(0, 1, 4, 2, 3).reshape(B, N_PAIR, HEAD_A, PAIR_W)
    t_i = jnp.arange(CHUNK)[:, None]
    s_i = (jnp.arange(PAIR_W) % HEAD_A)[None, :]
    lane_head = (jnp.arange(PAIR_W) // HEAD_A)[None, :]
    ones_row = jnp.ones((CHUNK, 1), jnp.int32)
    masks = [t_i == s_i, (lane_head == 0) * ones_row, (lane_head == 1) * ones_row, s_i < t_i, s_i <= t_i]
    for m in (1, 2, 4, 8, 16, 32):
        masks.append((t_i // (2 * m) == s_i // (2 * m)) & (t_i % (2 * m) >= m) & (s_i % (2 * m) < m))
    masks = jnp.stack([mk.astype(jnp.float32) for mk in masks])
    ltri = jnp.tile((jnp.arange(CHUNK)[None, :] <= jnp.arange(CHUNK)[:, None]), (1, 3)).astype(jnp.bfloat16)
    ones3 = jnp.tile((lane_head.T == lane_head), (3, 1)).astype(jnp.bfloat16)
    seq = pl.BlockSpec((1, tb, WIDTH_A), lambda b, t: (b, t, 0))
    st = pl.BlockSpec((1, N_PAIR, HEAD_A, PAIR_W), lambda b, t: (b, 0, 0, 0))
    y, sT = pl.pallas_call(
        functools.partial(_rwkv_chunk_kernel, n_chunks=tb // CHUNK),
        out_shape=(jax.ShapeDtypeStruct((B, T, WIDTH_A), jnp.float32),
                   jax.ShapeDtypeStruct((B, N_PAIR, HEAD_A, PAIR_W), jnp.float32)),
        grid=(B, T // tb),
        in_specs=[seq] * 6 + [st, _const_spec((CHUNK, 3 * CHUNK)), _const_spec((3 * PAIR_W, PAIR_W)),
                              _const_spec((11, CHUNK, PAIR_W))],
        out_specs=(seq, st),
        scratch_shapes=[pltpu.VMEM((N_PAIR, HEAD_A, PAIR_W), jnp.float32)],
        compiler_params=pltpu.CompilerParams(
            dimension_semantics=("arbitrary", "arbitrary"), vmem_limit_bytes=VMEM_LIMIT),
        name="rwkv7_chunked",
    )(r, logw, k, v, kk, a, s0, ltri, ones3, masks)
    sT = sT.reshape(B, N_PAIR, HEAD_A, 2, HEAD_A).transpose(0, 1, 3, 4, 2)
    return y, sT.reshape(B, N_HEADS_A, HEAD_A, HEAD_A)


def _head_sums(x, ones2):
    tiles = [jnp.dot(_split2(x[:, c:c + PAIR_W]), ones2, preferred_element_type=jnp.float32)
             for c in range(0, WIDTH_A, PAIR_W)]
    return jnp.concatenate(tiles, axis=1)


def _softplus(z):
    return jnp.maximum(z, 0.0) + jnp.log1p(jnp.exp(-jnp.abs(z)))


def _rwkv_pre_kernel(seg_ref, prev_ref, shift_ref, mu_ref, vec_ref, wa_ref, gup_ref, ones_ref,
                     r_ref, w_ref, k_ref, v_ref, kk_ref, a_ref, g_ref, bonus_ref):
    seg = seg_ref[0]
    tb = seg.shape[0]
    before = jnp.where(pl.program_id(1) == 0, shift_ref[0], prev_ref[0, 7:8, :])
    row = lax.broadcasted_iota(jnp.int32, seg.shape, 0)
    shifted = jnp.where(row == 0, before, pltpu.roll(seg, 1, axis=0))
    xm = seg + mu_ref[...] * (shifted - seg)
    r = xm[:, :WIDTH_A]
    k = xm[:, WIDTH_A:2 * WIDTH_A]
    v = xm[:, 2 * WIDTH_A:3 * WIDTH_A]
    wa = xm[:, 3 * WIDTH_A:3 * WIDTH_A + LORA_W + LORA_A]
    gl = xm[:, 3 * WIDTH_A + LORA_W + LORA_A:]
    lane = lax.broadcasted_iota(jnp.int32, wa.shape, 1)
    wa = jnp.where(lane < LORA_W, jnp.tanh(wa), wa).astype(jnp.bfloat16)
    lora = jnp.dot(wa, wa_ref[...], preferred_element_type=jnp.float32)
    w0, a0, k_k, k_a, r_k = (vec_ref[i:i + 1, :] for i in range(5))
    w_log = -_softplus(-(w0 + lora[:, :WIDTH_A])) - 0.5
    log_decay = -jnp.exp(w_log)
    a = jax.nn.sigmoid(a0 + lora[:, WIDTH_A:])
    g = jnp.dot(jax.nn.sigmoid(gl).astype(jnp.bfloat16), gup_ref[...], preferred_element_type=jnp.float32)
    ones2 = ones_ref[...]
    kk = k * k_k
    kk = kk * lax.rsqrt(jnp.maximum(_head_sums(kk * kk, ones2), 1e-24))
    kf = k * (1.0 + (a - 1.0) * k_a)
    r_ref[0], w_ref[0], k_ref[0], v_ref[0], kk_ref[0], a_ref[0], g_ref[0] = r, log_decay, kf, v, kk, a, g
    bonus_ref[0] = _head_sums(r * kf * r_k, ones2) * v


def _block_ones2():
    lane_head = jnp.arange(PAIR_W) // HEAD_A
    return (jnp.tile(lane_head, 2)[:, None] == lane_head[None, :]).astype(jnp.bfloat16)


def rwkv7_pre(seg, prev_row, p, tb=256):
    B, T, _ = seg.shape
    tb = min(tb, T)
    zeros = jnp.zeros((LORA_W, WIDTH_A), jnp.float32)
    wa_up = jnp.concatenate([jnp.concatenate([p['w_lora_up'], zeros], 1),
                             jnp.concatenate([zeros, p['a_lora_up']], 1)], 0).astype(jnp.bfloat16)
    vecs = jnp.stack([p['w0'], p['a0'], p['k_k'], p['k_a'], p['r_k'].reshape(WIDTH_A)])
    out = jax.ShapeDtypeStruct((B, T, WIDTH_A), jnp.float32)
    ospec = pl.BlockSpec((1, tb, WIDTH_A), lambda b, t: (b, t, 0))
    full = lambda shape: pl.BlockSpec(shape, lambda b, t: (0,) * len(shape))
    return pl.pallas_call(
        _rwkv_pre_kernel,
        out_shape=(out,) * 8,
        grid=(B, T // tb),
        in_specs=[pl.BlockSpec((1, tb, C_RWKV), lambda b, t: (b, t, 0)),
                  pl.BlockSpec((1, 8, C_RWKV), lambda b, t: (b, jnp.maximum(t * (tb // 8) - 1, 0), 0)),
                  pl.BlockSpec((1, 1, C_RWKV), lambda b, t: (b, 0, 0)),
                  full((1, C_RWKV)), full((5, WIDTH_A)), full((LORA_W + LORA_A, 2 * WIDTH_A)),
                  full((LORA_G, WIDTH_A)), full((2 * PAIR_W, PAIR_W))],
        out_specs=(ospec,) * 8,
        compiler_params=pltpu.CompilerParams(
            dimension_semantics=("arbitrary", "arbitrary"), vmem_limit_bytes=VMEM_LIMIT),
        name="rwkv7_pre",
    )(seg, seg, prev_row, p['shift_mu'].reshape(1, C_RWKV), vecs, wa_up,
      p['g_lora_up'].astype(jnp.bfloat16), _block_ones2())


SWA_BAND = WINDOW + CHUNK


def _swa_kernel(q_ref, k_ref, v_ref, kp_ref, vp_ref, bias_ref, sink_ref, o_ref, *, mask_start):
    n_chunks = q_ref.shape[1] // CHUNK
    k_all = jnp.concatenate([kp_ref[0], k_ref[0]], axis=0).astype(jnp.bfloat16)
    v_all = jnp.concatenate([vp_ref[0], v_ref[0]], axis=0).astype(jnp.bfloat16)
    first = pl.program_id(1) == 0
    key_chunk = lax.broadcasted_iota(jnp.int32, (GROUP_B * CHUNK, SWA_BAND), 1) // CHUNK
    for c in range(n_chunks):
        q_c = q_ref[0, c * CHUNK:(c + 1) * CHUNK, :].astype(jnp.bfloat16)
        k_c = k_all[c * CHUNK:c * CHUNK + SWA_BAND]
        v_c = v_all[c * CHUNK:c * CHUNK + SWA_BAND]
        dead = jnp.logical_and(first, key_chunk + (c - WIN_CHUNKS) < 0) if (mask_start and c < WIN_CHUNKS) else None
        kvs = range(N_KV_B)
        qg = [jnp.concatenate([q_c[:, (kv * GROUP_B + g) * HEAD_B:(kv * GROUP_B + g + 1) * HEAD_B]
                               for g in range(GROUP_B)], axis=0) for kv in kvs]
        s = [lax.dot_general(qg[kv], k_c[:, kv * HEAD_B:(kv + 1) * HEAD_B], (((1,), (1,)), ((), ())),
                             preferred_element_type=jnp.float32) * (HEAD_B ** -0.5) + bias_ref[kv] for kv in kvs]
        if dead is not None:
            s = [jnp.where(dead, NEG_INF, s[kv]) for kv in kvs]
        sink = [sink_ref[kv * GROUP_B * CHUNK:(kv + 1) * GROUP_B * CHUNK, 0:1] for kv in kvs]
        m = [jnp.maximum(jnp.max(s[kv], axis=-1, keepdims=True), sink[kv]) for kv in kvs]
        e = [jnp.exp(s[kv] - m[kv]) for kv in kvs]
        pr = [e[kv] / (jnp.sum(e[kv], axis=-1, keepdims=True) + jnp.exp(sink[kv] - m[kv])) for kv in kvs]
        og = [jnp.dot(pr[kv].astype(jnp.bfloat16), v_c[:, kv * HEAD_B:(kv + 1) * HEAD_B],
                      preferred_element_type=jnp.float32) for kv in kvs]
        o_ref[0, c * CHUNK:(c + 1) * CHUNK, :] = jnp.concatenate(
            [og[kv][g * CHUNK:(g + 1) * CHUNK] for kv in kvs for g in range(GROUP_B)], axis=1)


def swa_attention(seg_b, prev_k, prev_v, sinks, *, prev_is_seq, qb=512):
    B, T, _ = seg_b.shape
    qb = min(qb, T)
    slopes = 2.0 ** (-8.0 * jnp.arange(1, N_HEADS_B + 1, dtype=jnp.float32) / N_HEADS_B)
    dist = jnp.abs(jnp.arange(CHUNK)[:, None] - (jnp.arange(SWA_BAND) - WINDOW)[None, :]).astype(jnp.float32)
    bias = (-slopes[:, None, None] * dist).reshape(N_KV_B, GROUP_B * CHUNK, SWA_BAND)
    sink_tab = jnp.broadcast_to(jnp.repeat(sinks.astype(jnp.float32), CHUNK)[:, None], (N_HEADS_B * CHUNK, 128))
    kcol, vcol = WIDTH_B // KV_WIDTH_B, WIDTH_B // KV_WIDTH_B + 1
    if prev_is_seq:
        per = qb // WINDOW
        kp_spec = pl.BlockSpec((1, WINDOW, KV_WIDTH_B), lambda b, i: (b, jnp.maximum(i * per - 1, 0), kcol))
        vp_spec = pl.BlockSpec((1, WINDOW, KV_WIDTH_B), lambda b, i: (b, jnp.maximum(i * per - 1, 0), vcol))
    else:
        kp_spec = vp_spec = pl.BlockSpec((1, WINDOW, KV_WIDTH_B), lambda b, i: (b, 0, 0))
    return pl.pallas_call(
        functools.partial(_swa_kernel, mask_start=prev_is_seq),
        out_shape=jax.ShapeDtypeStruct((B, T, WIDTH_B), jnp.float32),
        grid=(B, T // qb),
        in_specs=[pl.BlockSpec((1, qb, WIDTH_B), lambda b, i: (b, i, 0)),
                  pl.BlockSpec((1, qb, KV_WIDTH_B), lambda b, i: (b, i, kcol)),
                  pl.BlockSpec((1, qb, KV_WIDTH_B), lambda b, i: (b, i, vcol)),
                  kp_spec, vp_spec,
                  pl.BlockSpec((N_KV_B, GROUP_B * CHUNK, SWA_BAND), lambda b, i: (0, 0, 0)),
                  pl.BlockSpec((N_HEADS_B * CHUNK, 128), lambda b, i: (0, 0))],
        out_specs=pl.BlockSpec((1, qb, WIDTH_B), lambda b, i: (b, i, 0)),
        compiler_params=pltpu.CompilerParams(
            dimension_semantics=("arbitrary", "arbitrary"), vmem_limit_bytes=VMEM_LIMIT),
        name="swa_attention",
    )(seg_b, seg_b, seg_b, prev_k, prev_v, bias, sink_tab)


def _layer_norm_rows(h, g, b, eps=1e-5):
    mu = jnp.mean(h, axis=-1, keepdims=True)
    d = h - mu
    var = jnp.mean(d * d, axis=-1, keepdims=True)
    return d * lax.rsqrt(var + eps) * g + b


def _const_spec(shape):
    return pl.BlockSpec(shape, lambda *_: (0,) * len(shape), pipeline_mode=pl.Buffered(1))


def _merge_kernel(x_ref, y_ref, bonus_ref, g_ref, ob_ref, gates_ref, lnx_ref, ln1_ref, ones_ref,
                  pa_ref, pb_ref, wout_ref, o_ref):
    ones2 = ones_ref[...]
    y = y_ref[...]
    mean = _head_sums(y, ones2) * (1.0 / HEAD_A)
    d = y - mean
    var = _head_sums(d * d, ones2) * (1.0 / HEAD_A)
    yn = d * lax.rsqrt(var + GN_EPS) * lnx_ref[0:1, :] + lnx_ref[1:2, :]
    o_a = ((yn + bonus_ref[...]) * g_ref[...]).astype(jnp.bfloat16)
    br_a = jnp.dot(o_a, pa_ref[...], preferred_element_type=jnp.float32)
    br_b = jnp.dot(ob_ref[...].astype(jnp.bfloat16), pb_ref[...], preferred_element_type=jnp.float32)
    gates = jax.nn.sigmoid(gates_ref[...])
    merged = gates[:, :D_MODEL] * br_a + gates[:, D_MODEL:] * br_b
    h = ALPHA * x_ref[...] + jnp.dot(merged.astype(jnp.bfloat16), wout_ref[...], preferred_element_type=jnp.float32)
    o_ref[...] = _layer_norm_rows(h, ln1_ref[0:1, :], ln1_ref[1:2, :])


def branch_merge(x, y, bonus, g, o_b, gates, p, tm=256):
    n = x.shape[0]
    tm = _pick(n, (tm, 128, 64))
    rows = lambda w: pl.BlockSpec((tm, w), lambda i: (i, 0))
    return pl.pallas_call(
        _merge_kernel,
        out_shape=jax.ShapeDtypeStruct((n, D_MODEL), jnp.float32),
        grid=(n // tm,),
        in_specs=[rows(D_MODEL), rows(WIDTH_A), rows(WIDTH_A), rows(WIDTH_A), rows(WIDTH_B), rows(C_GATE),
                  _const_spec((2, WIDTH_A)), _const_spec((2, D_MODEL)), _const_spec((2 * PAIR_W, PAIR_W)),
                  _const_spec((WIDTH_A, D_MODEL)), _const_spec((WIDTH_B, D_MODEL)), _const_spec((D_MODEL, D_MODEL))],
        out_specs=rows(D_MODEL),
        compiler_params=pltpu.CompilerParams(dimension_semantics=("arbitrary",), vmem_limit_bytes=VMEM_LIMIT),
        name="branch_merge",
    )(x, y, bonus, g, o_b, gates, jnp.stack([p['lnx_g'], p['lnx_b']]), jnp.stack([p['ln1_g'], p['ln1_b']]),
      _block_ones2(), p['w_branch_a16'], p['w_branch_b16'], p['w_out16'])


def _mem_kernel(x_ref, mk_ref, mv_ref, ln2_ref, wq_ref, wo_ref, pwq_ref, o_ref, q_ref):
    x = x_ref[0]
    qm = jnp.dot(x.astype(jnp.bfloat16), wq_ref[...], preferred_element_type=jnp.float32).astype(jnp.bfloat16)
    mk = mk_ref[0].astype(jnp.bfloat16)
    mv = mv_ref[0].astype(jnp.bfloat16)
    outs = []
    for h in range(N_HEADS_M):
        cols = slice(h * HEAD_M, (h + 1) * HEAD_M)
        s = lax.dot_general(qm[:, cols], mk[:, cols], (((1,), (1,)), ((), ())),
                            preferred_element_type=jnp.float32) * (HEAD_M ** -0.5)
        e = jnp.exp(s - jnp.max(s, axis=-1, keepdims=True))
        pr = e / jnp.sum(e, axis=-1, keepdims=True)
        outs.append(jnp.dot(pr.astype(jnp.bfloat16), mv[:, cols], preferred_element_type=jnp.float32))
    o = jnp.concatenate(outs, axis=1).astype(jnp.bfloat16)
    h2 = ALPHA * x + jnp.dot(o, wo_ref[...], preferred_element_type=jnp.float32)
    x2 = _layer_norm_rows(h2, ln2_ref[0:1, :], ln2_ref[1:2, :])
    o_ref[0] = x2
    q_ref[0] = jnp.dot(x2.astype(jnp.bfloat16), pwq_ref[...], preferred_element_type=jnp.float32)


def mem_block(x, mk, mv, p, tm=256):
    B, T, _ = x.shape
    tm = _pick(T, (tm, 128, 64))
    wm = N_HEADS_M * HEAD_M
    rows = lambda w: pl.BlockSpec((1, tm, w), lambda b, i: (b, i, 0))
    mem = pl.BlockSpec((1, N_MEM, wm), lambda b, i: (b, 0, 0))
    out = jax.ShapeDtypeStruct((B, T, D_MODEL), jnp.float32)
    return pl.pallas_call(
        _mem_kernel,
        out_shape=(out, jax.ShapeDtypeStruct((B, T, PEER_HEADS * D_KEY), jnp.float32)),
        grid=(B, T // tm),
        in_specs=[rows(D_MODEL), mem, mem, _const_spec((2, D_MODEL)), _const_spec((D_MODEL, wm)),
                  _const_spec((wm, D_MODEL)), _const_spec((D_MODEL, PEER_HEADS * D_KEY))],
        out_specs=(rows(D_MODEL), rows(PEER_HEADS * D_KEY)),
        compiler_params=pltpu.CompilerParams(
            dimension_semantics=("arbitrary", "arbitrary"), vmem_limit_bytes=VMEM_LIMIT),
        name="mem_block",
    )(x, mk, mv, jnp.stack([p['ln2_g'], p['ln2_b']]), p['wq_mem16'], p['wo_mem16'], p['peer_wq16'])


ROUTE_TQ = 256


def _top_values(s, k, exact):
    n_rows = s.shape[0]
    iota = lax.broadcasted_iota(jnp.int32, s.shape, 0).astype(jnp.float32)
    rank = jnp.full(s.shape, float(k), jnp.float32)
    masked0 = jnp.sum((s == -jnp.inf).astype(jnp.float32), axis=0, keepdims=True)
    out = []
    for step in range(k):
        m = jnp.max(s, axis=0, keepdims=True)
        if exact:
            first = jnp.min(jnp.where(s == m, iota, float(n_rows)), axis=0, keepdims=True)
            taken = iota == first
        else:
            taken = s == m
        s = jnp.where(taken, -jnp.inf, s)
        rank = jnp.where(taken, float(step), rank)
        out.append(m)
    surplus = jnp.sum((s == -jnp.inf).astype(jnp.float32), axis=0, keepdims=True) - masked0 - float(k)
    return out, rank, surplus


def _peer_route_kernel(q_ref, keys_ref, n1_ref, c1_ref, rank2_ref, e2_ref):
    surplus = _peer_route_pass(q_ref, keys_ref, n1_ref, c1_ref, rank2_ref, e2_ref, exact=False)

    @pl.when(jnp.max(surplus) > 0.0)
    def _():
        _peer_route_pass(q_ref, keys_ref, n1_ref, c1_ref, rank2_ref, e2_ref, exact=True)


def _peer_route_pass(q_ref, keys_ref, n1_ref, c1_ref, rank2_ref, e2_ref, *, exact):
    half = D_KEY // 2
    surplus = None
    for h in range(PEER_HEADS):
        tops, scores = [], []
        for p in range(2):
            c0 = (2 * h + p) * half
            qs = q_ref[:, c0:c0 + half].astype(jnp.bfloat16)
            s = lax.dot_general(keys_ref[2 * h + p], qs, (((1,), (1,)), ((), ())),
                                preferred_element_type=jnp.float32)
            scores.append(s)
            tops.append(_top_values(s, TOPK, exact))
        (t1, _, sur1), (t2, rank2, sur2) = tops
        t2all = jnp.concatenate(t2, axis=0)
        t1all = jnp.concatenate(t1, axis=0)
        rank8 = lax.broadcasted_iota(jnp.int32, (8, t1all.shape[1]), 0)
        cand_rows = [t1all + t2[0], t1all[:8] + t2[1]]
        for b in range(2, 8):
            cand_rows.append(jnp.where(rank8 < TOPK // (b + 1), t1all[:8] + t2[b], -jnp.inf))
        cand_rows.append(t1[0] + t2all[8:])
        sc, _, sur3 = _top_values(jnp.concatenate(cand_rows, axis=0), TOPK, exact)
        sur = sur1 + sur2 + sur3
        surplus = sur if surplus is None else surplus + sur
        z = jnp.zeros_like(sc[0])
        for kq in range(TOPK):
            z = z + jnp.exp(sc[kq] - sc[0])
        theta = sc[TOPK - 1]
        n1 = jnp.zeros_like(scores[0])
        for a in range(TOPK):
            n_a = jnp.sum((t1[a] + t2all >= theta).astype(jnp.float32), axis=0, keepdims=True)
            n1 = jnp.where(scores[0] == t1[a], n_a, n1)
        n1_ref[h] = n1
        rank2_ref[h] = rank2.astype(jnp.bfloat16)
        c1_ref[h] = jnp.exp(scores[0] - t1[0]) / z
        e2_ref[h] = jnp.exp(scores[1] - t2[0]).astype(jnp.bfloat16)
    return surplus


def peer_route(q, keys16):
    n = q.shape[0]
    tq = _pick(n, (ROUTE_TQ, 128))
    big = jax.ShapeDtypeStruct((PEER_HEADS, N_KEYS, n), jnp.float32)
    big16 = jax.ShapeDtypeStruct((PEER_HEADS, N_KEYS, n), jnp.bfloat16)
    bspec = pl.BlockSpec((PEER_HEADS, N_KEYS, tq), lambda i: (0, 0, i))
    return pl.pallas_call(
        _peer_route_kernel,
        out_shape=(big, big, big16, big16),
        grid=(n // tq,),
        in_specs=[pl.BlockSpec((tq, PEER_HEADS * D_KEY), lambda i: (i, 0)),
                  pl.BlockSpec((2 * PEER_HEADS, N_KEYS, D_KEY // 2), lambda i: (0, 0, 0))],
        out_specs=(bspec, bspec, bspec, bspec),
        compiler_params=pltpu.CompilerParams(
            dimension_semantics=("arbitrary",), vmem_limit_bytes=VMEM_LIMIT),
        name="peer_route",
    )(q, keys16)


PEER_TM = 512
PEER_ROWS = 8
PEER_TE = PEER_ROWS * N_KEYS


def _gelu(x):
    return 0.5 * x * (1.0 + lax.erf(x * (2.0 ** -0.5)))


def _peer_mix_kernel(x_ref, u_ref, vt_ref, n1_ref, c1_ref, rank2_ref, e2_ref, ln3_ref, o_ref,
                     x16_ref, h_ref, acc_ref):
    j = pl.program_id(1)

    @pl.when(j == 0)
    def _():
        x16_ref[...] = x_ref[...].T.astype(jnp.bfloat16)
        acc_ref[...] = jnp.zeros_like(acc_ref)

    half_te = PEER_TE // 2
    a_halves = [jnp.dot(u_ref[k * half_te:(k + 1) * half_te, :], x16_ref[...],
                        preferred_element_type=jnp.float32) for k in range(2)]

    def sublane_bcast16(row):
        return jnp.broadcast_to(row, (N_KEYS, row.shape[1])).astype(jnp.bfloat16)

    for r in range(PEER_ROWS):
        rows = slice(r * N_KEYS, (r + 1) * N_KEYS)
        for c0 in range(0, x_ref.shape[0], 128):
            cols = slice(c0, c0 + 128)
            gate = None
            for h in range(PEER_HEADS):
                keep = rank2_ref[h, :, cols] < sublane_bcast16(n1_ref[h, r:r + 1, cols])
                w = jnp.where(keep, e2_ref[h, :, cols], jnp.zeros((), jnp.bfloat16))
                w = w * sublane_bcast16(c1_ref[h, r:r + 1, cols])
                gate = w if gate is None else gate + w
            a_rows = a_halves[r // (PEER_ROWS // 2)][(r % (PEER_ROWS // 2)) * N_KEYS:(r % (PEER_ROWS // 2) + 1) * N_KEYS]
            h_ref[rows, cols] = gate * _gelu(a_rows[:, cols]).astype(jnp.bfloat16)
    acc_ref[...] += jnp.dot(vt_ref[0], h_ref[...], preferred_element_type=jnp.float32)

    @pl.when(j == pl.num_programs(1) - 1)
    def _():
        o_ref[...] = _layer_norm_rows(ALPHA * x_ref[...] + acc_ref[...].T, ln3_ref[0:1, :], ln3_ref[1:2, :])


def peer_mix(x, u16, vt16, n1, c1, rank2, e2, ln3):
    n, d = x.shape
    tm = _pick(n, (PEER_TM, 256, 128))
    n_exp = u16.shape[0]
    row_spec = pl.BlockSpec((PEER_HEADS, PEER_ROWS, tm), lambda i, j: (0, j, i))
    all_spec = pl.BlockSpec((PEER_HEADS, N_KEYS, tm), lambda i, j: (0, 0, i), pipeline_mode=pl.Buffered(1))
    return pl.pallas_call(
        _peer_mix_kernel,
        out_shape=jax.ShapeDtypeStruct((n, d), jnp.float32),
        grid=(n // tm, n_exp // PEER_TE),
        in_specs=[pl.BlockSpec((tm, d), lambda i, j: (i, 0), pipeline_mode=pl.Buffered(1)),
                  pl.BlockSpec((PEER_TE, d), lambda i, j: (j, 0)),
                  pl.BlockSpec((1, d, PEER_TE), lambda i, j: (j, 0, 0)),
                  row_spec, row_spec, all_spec, all_spec, _const_spec((2, d))],
        out_specs=pl.BlockSpec((tm, d), lambda i, j: (i, 0)),
        scratch_shapes=[pltpu.VMEM((d, tm), jnp.bfloat16), pltpu.VMEM((PEER_TE, tm), jnp.bfloat16),
                        pltpu.VMEM((d, tm), jnp.float32)],
        compiler_params=pltpu.CompilerParams(
            dimension_semantics=("arbitrary", "arbitrary"), vmem_limit_bytes=VMEM_LIMIT),
        name="peer_mix",
    )(x, u16, vt16, n1, c1, rank2, e2, ln3)


def peer_block(x, q, p):
    stats = peer_route(q, p['peer_keys16'])
    return peer_mix(x, p['peer_u16'], p['peer_vt16'], *stats, jnp.stack([p['ln3_g'], p['ln3_b']]))


def trunk_layer(x, p, mem_k, mem_v, rwkv_state, shift_row, swa_k_cache, swa_v_cache):
    B, T, _ = x.shape
    n = B * T
    x2d = x.reshape(n, D_MODEL)
    seg_a = matmul(x2d, p['w_in_a16']).reshape(B, T, C_RWKV)
    seg_b = matmul(x2d, p['w_in_b16']).reshape(B, T, C_SWA)
    gates = matmul(x2d, p['w_in_g16'])
    r, w, k, v, kk, a, g, bonus = rwkv7_pre(seg_a, shift_row.astype(jnp.float32), p)
    y, rwkv_new = rwkv7_chunked(r, w, k, v, kk, a, rwkv_state)
    shift_new = seg_a[:, -1:]
    k_new = seg_b[:, :, WIDTH_B:WIDTH_B + KV_WIDTH_B].reshape(B, T, N_KV_B, HEAD_B)
    v_new = seg_b[:, :, WIDTH_B + KV_WIDTH_B:].reshape(B, T, N_KV_B, HEAD_B)
    if swa_k_cache is None:
        o_b = swa_attention(seg_b, seg_b, seg_b, p['attn_sinks'], prev_is_seq=True)
        swa_k_new, swa_v_new = k_new[:, -WINDOW:], v_new[:, -WINDOW:]
    else:
        o_b = swa_attention(seg_b, swa_k_cache.reshape(B, WINDOW, KV_WIDTH_B),
                            swa_v_cache.reshape(B, WINDOW, KV_WIDTH_B), p['attn_sinks'], prev_is_seq=False)
        swa_k_new = jnp.concatenate([swa_k_cache, k_new], axis=1)[:, -WINDOW:]
        swa_v_new = jnp.concatenate([swa_v_cache, v_new], axis=1)[:, -WINDOW:]
    flat = lambda t: t.reshape(n, t.shape[-1])
    x1 = branch_merge(x2d, flat(y), flat(bonus), flat(g), flat(o_b), gates, p)
    wm = N_HEADS_M * HEAD_M
    x2, q = mem_block(x1.reshape(B, T, D_MODEL), mem_k.reshape(-1, N_MEM, wm), mem_v.reshape(-1, N_MEM, wm), p)
    x3 = peer_block(flat(x2), flat(q), p)
    return x3.reshape(B, T, D_MODEL), rwkv_new, shift_new, swa_k_new, swa_v_new


_MM_WEIGHTS = ('w_branch_a', 'w_branch_b', 'w_out', 'wq_mem', 'wk_mem', 'wv_mem', 'wo_mem', 'peer_wq')


def kernel(x_prompt, x_sample, state_rwkv, state_shift, cache_swa_k, cache_swa_v, cache_mem_k, cache_mem_v, mem_prompt, w_in, shift_mu, w0, w_lora_up, a0, a_lora_up, g_lora_up, k_k, k_a, r_k, lnx_g, lnx_b, attn_sinks, w_branch_a, w_branch_b, w_out, ln1_g, ln1_b, wq_mem, wk_mem, wv_mem, wo_mem, ln2_g, ln2_b, peer_wq, peer_sub_keys, peer_u, peer_v, ln3_g, ln3_b):
    params = {
        'w_in': w_in, 'shift_mu': shift_mu, 'w0': w0, 'w_lora_up': w_lora_up, 'a0': a0,
        'a_lora_up': a_lora_up, 'g_lora_up': g_lora_up, 'k_k': k_k, 'k_a': k_a, 'r_k': r_k,
        'lnx_g': lnx_g, 'lnx_b': lnx_b, 'attn_sinks': attn_sinks, 'w_branch_a': w_branch_a,
        'w_branch_b': w_branch_b, 'w_out': w_out, 'ln1_g': ln1_g, 'ln1_b': ln1_b, 'wq_mem': wq_mem,
        'wk_mem': wk_mem, 'wv_mem': wv_mem, 'wo_mem': wo_mem, 'ln2_g': ln2_g, 'ln2_b': ln2_b,
        'peer_wq': peer_wq, 'peer_sub_keys': peer_sub_keys, 'peer_u': peer_u, 'peer_v': peer_v,
        'ln3_g': ln3_g, 'ln3_b': ln3_b,
    }
    B = x_prompt.shape[0]
    rwkv0 = jnp.zeros((B, N_HEADS_A, HEAD_A, HEAD_A), jnp.float32)
    shift0 = jnp.zeros((B, 1, C_RWKV), x_prompt.dtype)
    xp, xs = x_prompt, x_sample
    p_rw, p_sh, p_k, p_v, p_mk, p_mv = [], [], [], [], [], []
    s_rw, s_sh, s_k, s_v = [], [], [], []
    for l in range(DEPTH):
        p = {name: arr[l] for name, arr in params.items()}
        for name in _MM_WEIGHTS:
            p[name + '16'] = p[name].astype(jnp.bfloat16)
        w_in16 = p['w_in'].astype(jnp.bfloat16)
        p['w_in_a16'] = w_in16[:, :C_RWKV]
        p['w_in_b16'] = w_in16[:, C_RWKV:C_RWKV + C_SWA]
        p['w_in_g16'] = w_in16[:, C_RWKV + C_SWA:]
        p['peer_keys16'] = p['peer_sub_keys'].reshape(2 * PEER_HEADS, N_KEYS, D_KEY // 2).astype(jnp.bfloat16)
        p['peer_u16'] = p['peer_u'].astype(jnp.bfloat16)
        p['peer_vt16'] = (p['peer_v'].astype(jnp.bfloat16)
                          .reshape(p['peer_v'].shape[0] // PEER_TE, PEER_TE, D_MODEL).transpose(0, 2, 1))
        mk = mm(mem_prompt, p['wk_mem16']).reshape(B, N_MEM, N_HEADS_M, HEAD_M)
        mv = mm(mem_prompt, p['wv_mem16']).reshape(B, N_MEM, N_HEADS_M, HEAD_M)
        xp, rw, sh, kn, vn = trunk_layer(xp, p, mk, mv, rwkv0, shift0, None, None)
        p_rw.append(rw); p_sh.append(sh); p_k.append(kn); p_v.append(vn); p_mk.append(mk); p_mv.append(mv)
        xs, rw, sh, kn, vn = trunk_layer(xs, p, cache_mem_k[l], cache_mem_v[l], state_rwkv[l], state_shift[l],
                                         cache_swa_k[l], cache_swa_v[l])
        s_rw.append(rw); s_sh.append(sh); s_k.append(kn); s_v.append(vn)
    return (xp, xs,
            jnp.stack(p_rw), jnp.stack(p_sh), jnp.stack(p_k), jnp.stack(p_v), jnp.stack(p_mk), jnp.stack(p_mv),
            jnp.stack(s_rw), jnp.stack(s_sh), jnp.stack(s_k), jnp.stack(s_v))
```

```python
import functools

import jax
import jax.numpy as jnp
from jax import lax
from jax.experimental import pallas as pl
from jax.experimental.pallas import tpu as pltpu

D_MODEL = 2048
DEPTH = 2
CHUNK = 64
HEAD_A = 64
N_HEADS_A = 16
WIDTH_A = N_HEADS_A * HEAD_A
LORA_W = 64
LORA_A = 64
LORA_G = 128
GN_EPS = 64e-5
C_RWKV = 3 * WIDTH_A + LORA_W + LORA_A + LORA_G
HEAD_B = 64
N_HEADS_B = 16
N_KV_B = 4
GROUP_B = N_HEADS_B // N_KV_B
WIDTH_B = N_HEADS_B * HEAD_B
KV_WIDTH_B = N_KV_B * HEAD_B
WINDOW = 128
WIN_CHUNKS = WINDOW // CHUNK
C_SWA = WIDTH_B + 2 * KV_WIDTH_B
C_GATE = 2 * D_MODEL
N_MEM = 256
N_HEADS_M = 4
HEAD_M = 128
N_KEYS = 128
PEER_HEADS = 8
D_KEY = 256
TOPK = 16
ALPHA = (2.0 * DEPTH) ** 0.25
NEG_INF = -1e30

VMEM_LIMIT = 56 * 1024 * 1024


def _matmul_kernel(a_ref, b_ref, o_ref, a16_ref):
    @pl.when(pl.program_id(1) == 0)
    def _():
        a16_ref[...] = a_ref[...].astype(jnp.bfloat16)

    o_ref[...] = jnp.dot(a16_ref[...], b_ref[...], preferred_element_type=jnp.float32)


def _pick(n, cands):
    for c in cands:
        if n % c == 0:
            return c
    return n


def matmul(a, b16):
    m, k = a.shape
    n = b16.shape[1]
    tm = _pick(m, (512, 256, 128))
    tn = _pick(n, (2048, 1664, 1536, 1024, 512, 256, 128))
    return pl.pallas_call(
        _matmul_kernel,
        out_shape=jax.ShapeDtypeStruct((m, n), jnp.float32),
        grid=(m // tm, n // tn),
        in_specs=[pl.BlockSpec((tm, k), lambda i, j: (i, 0)),
                  pl.BlockSpec((k, tn), lambda i, j: (0, j))],
        out_specs=pl.BlockSpec((tm, tn), lambda i, j: (i, j)),
        scratch_shapes=[pltpu.VMEM((tm, k), jnp.bfloat16)],
        compiler_params=pltpu.CompilerParams(
            dimension_semantics=("arbitrary", "arbitrary"),
            vmem_limit_bytes=VMEM_LIMIT),
        name="matmul",
    )(a, b16)


def mm(x, w16):
    lead = x.shape[:-1]
    return matmul(x.reshape(-1, x.shape[-1]), w16).reshape(*lead, w16.shape[1])


N_PAIR = N_HEADS_A // 2
PAIR_W = 2 * HEAD_A


def _split2(x):
    hi = x.astype(jnp.bfloat16)
    lo = (x - hi.astype(jnp.float32)).astype(jnp.bfloat16)
    return jnp.concatenate([hi, lo], axis=1)


def _hl(x):
    hi = x.astype(jnp.bfloat16)
    return hi, (x - hi.astype(jnp.float32)).astype(jnp.bfloat16)


def _dot3(a, b):
    ah, al = _hl(a)
    bh, bl = _hl(b)
    return jnp.dot(jnp.concatenate([ah, ah, al], axis=1), jnp.concatenate([bh, bl, bh], axis=0),
                   preferred_element_type=jnp.float32)


def _dot3_nt(a, b):
    ah, al = _hl(a)
    bh, bl = _hl(b)
    return lax.dot_general(jnp.concatenate([ah, ah, al], axis=1), jnp.concatenate([bh, bl, bh], axis=1),
                           (((1,), (1,)), ((), ())), preferred_element_type=jnp.float32)


def _rwkv_chunk_kernel(r_ref, w_ref, k_ref, v_ref, kk_ref, a_ref, s0_ref, ltri_ref, ones_ref, msk_ref,
                       y_ref, sT_ref, p_ref, *, n_chunks):
    tblk = pl.program_id(1)

    @pl.when(tblk == 0)
    def _():
        p_ref[...] = s0_ref[0]

    eye2, m0, m1, strict, incl = (msk_ref[i] for i in range(5))
    bd = lambda y: jnp.concatenate([y * m0, y * m1], axis=0)

    def split3(x, axis):
        t1 = x.astype(jnp.bfloat16)
        d = x - t1.astype(jnp.float32)
        t2 = d.astype(jnp.bfloat16)
        t3 = (d - t2.astype(jnp.float32)).astype(jnp.bfloat16)
        return jnp.concatenate([t1, t2, t3], axis=axis)

    def chunk(c, carry):
        rows = pl.ds(pl.multiple_of(c * CHUNK, CHUNK), CHUNK)
        pairs = range(N_PAIR)
        sls = [pl.ds(p * PAIR_W, PAIR_W) for p in pairs]
        lw = [w_ref[0, rows, sl] for sl in sls]
        cum = [jnp.dot(ltri_ref[...], split3(lw[p], 0), preferred_element_type=jnp.float32) for p in pairs]
        g = [jnp.exp(cum[p]) for p in pairs]
        ginv = [jnp.exp(-cum[p]) for p in pairs]
        kk = [kk_ref[0, rows, sl] for sl in sls]
        kh = [kk[p] * jnp.exp(cum[p] - lw[p]) for p in pairs]
        bh = [kk[p] * a_ref[0, rows, sls[p]] * ginv[p] for p in pairs]
        kf = [k_ref[0, rows, sls[p]] * ginv[p] for p in pairs]
        rh = [r_ref[0, rows, sls[p]] * g[p] for p in pairs]
        v = [v_ref[0, rows, sl] for sl in sls]
        g_last = [g[p][CHUNK - 1:CHUNK, :] for p in pairs]
        kr = [jnp.concatenate([kh[p], rh[p]], axis=0) for p in pairs]
        gram = [_dot3_nt(kr[p], jnp.concatenate([bd(bh[p]), bd(kf[p])], axis=0)) for p in pairs]
        a_b = [gram[p][:CHUNK, :PAIR_W] * strict for p in pairs]
        a_k = [gram[p][:CHUNK, PAIR_W:] * strict for p in pairs]
        a_r = [jnp.concatenate([gram[p][CHUNK:, :PAIR_W] * incl, gram[p][CHUNK:, PAIR_W:] * incl], axis=1)
               for p in pairs]
        t_inv = [eye2 - a_b[p] * msk_ref[5] for p in pairs]
        for lvl in range(1, 6):
            half = [_dot3(t_inv[p], bd(a_b[p] * msk_ref[5 + lvl])) for p in pairs]
            t_inv = [t_inv[p] - _dot3(half[p], bd(t_inv[p])) for p in pairs]
        akv = [_dot3(a_k[p], bd(v[p])) for p in pairs]
        g_col = [jnp.dot(split3(eye2 * g_last[p], 1), ones_ref[...], preferred_element_type=jnp.float32)
                 for p in pairs]
        xt = [jnp.concatenate([bh[p] * g_last[p], kf[p] * g_last[p]], axis=0).T for p in pairs]
        p0 = [p_ref[p] for p in pairs]
        zy = [_dot3(kr[p], bd(p0[p])) for p in pairs]
        u = [-_dot3(t_inv[p], bd(zy[p][:CHUNK] + akv[p])) for p in pairs]
        y = [zy[p][CHUNK:] + _dot3(a_r[p], jnp.concatenate([bd(u[p]), bd(v[p])], axis=0)) for p in pairs]
        delta = [_dot3(xt[p], jnp.concatenate([u[p], v[p]], axis=0)) for p in pairs]
        for p in pairs:
            p_ref[p] = g_col[p] * p0[p] + delta[p][:CHUNK] * m0 + delta[p][CHUNK:] * m1
            y_ref[0, rows, sls[p]] = y[p]
        return carry

    lax.fori_loop(0, n_chunks, chunk, 0)

    @pl.when(tblk == pl.num_programs(1) - 1)
    def _():
        sT_ref[0] = p_ref[...]


def rwkv7_chunked(r, logw, k, v, kk, a, state0, tb=256):
    B, T, _ = r.shape
    tb = min(tb, T)
    s0 = state0.astype(jnp.float32).reshape(B, N_PAIR, 2, HEAD_A, HEAD_A)
    s0 = s0.transpose(0, 1, 4, 2, 3).reshape(B, N_PAIR, HEAD_A, PAIR_W)
    t_i = jnp.arange(CHUNK)[:, None]
    s_i = (jnp.arange(PAIR_W) % HEAD_A)[None, :]
    lane_head = (jnp.arange(PAIR_W) // HEAD_A)[None, :]
    ones_row = jnp.ones((CHUNK, 1), jnp.int32)
    masks = [t_i == s_i, (lane_head == 0) * ones_row, (lane_head == 1) * ones_row, s_i < t_i, s_i <= t_i]
    for m in (1, 2, 4, 8, 16, 32):
        masks.append((t_i // (2 * m) == s_i // (2 * m)) & (t_i % (2 * m) >= m) & (s_i % (2 * m) < m))
    masks = jnp.stack([mk.astype(jnp.float32) for mk in masks])
    ltri = jnp.tile((jnp.arange(CHUNK)[None, :] <= jnp.arange(CHUNK)[:, None]), (1, 3)).astype(jnp.bfloat16)
    ones3 = jnp.tile((lane_head.T == lane_head), (3, 1)).astype(jnp.bfloat16)
    seq = pl.BlockSpec((1, tb, WIDTH_A), lambda b, t: (b, t, 0))
    st = pl.BlockSpec((1, N_PAIR, HEAD_A, PAIR_W), lambda b, t: (b, 0, 0, 0))
    y, sT = pl.pallas_call(
        functools.partial(_rwkv_chunk_kernel, n_chunks=tb // CHUNK),
        out_shape=(jax.ShapeDtypeStruct((B, T, WIDTH_A), jnp.float32),
                   jax.ShapeDtypeStruct((B, N_PAIR, HEAD_A, PAIR_W), jnp.float32)),
        grid=(B, T // tb),
        in_specs=[seq] * 6 + [st, _const_spec((CHUNK, 3 * CHUNK)), _const_spec((3 * PAIR_W, PAIR_W)),
                              _const_spec((11, CHUNK, PAIR_W))],
        out_specs=(seq, st),
        scratch_shapes=[pltpu.VMEM((N_PAIR, HEAD_A, PAIR_W), jnp.float32)],
        compiler_params=pltpu.CompilerParams(
            dimension_semantics=("arbitrary", "arbitrary"), vmem_limit_bytes=VMEM_LIMIT),
        name="rwkv7_chunked",
    )(r, logw, k, v, kk, a, s0, ltri, ones3, masks)
    sT = sT.reshape(B, N_PAIR, HEAD_A, 2, HEAD_A).transpose(0, 1, 3, 4, 2)
    return y, sT.reshape(B, N_HEADS_A, HEAD_A, HEAD_A)


def _head_sums(x, ones2):
    tiles = [jnp.dot(_split2(x[:, c:c + PAIR_W]), ones2, preferred_element_type=jnp.float32)
             for c in range(0, WIDTH_A, PAIR_W)]
    return jnp.concatenate(tiles, axis=1)


def _softplus(z):
    return jnp.maximum(z, 0.0) + jnp.log1p(jnp.exp(-jnp.abs(z)))


def _rwkv_pre_kernel(seg_ref, prev_ref, shift_ref, mu_ref, vec_ref, wa_ref, gup_ref, ones_ref,
                     r_ref, w_ref, k_ref, v_ref, kk_ref, a_ref, g_ref, bonus_ref):
    seg = seg_ref[0]
    tb = seg.shape[0]
    before = jnp.where(pl.program_id(1) == 0, shift_ref[0], prev_ref[0, 7:8, :])
    row = lax.broadcasted_iota(jnp.int32, seg.shape, 0)
    shifted = jnp.where(row == 0, before, pltpu.roll(seg, 1, axis=0))
    xm = seg + mu_ref[...] * (shifted - seg)
    r = xm[:, :WIDTH_A]
    k = xm[:, WIDTH_A:2 * WIDTH_A]
    v = xm[:, 2 * WIDTH_A:3 * WIDTH_A]
    wa = xm[:, 3 * WIDTH_A:3 * WIDTH_A + LORA_W + LORA_A]
    gl = xm[:, 3 * WIDTH_A + LORA_W + LORA_A:]
    lane = lax.broadcasted_iota(jnp.int32, wa.shape, 1)
    wa = jnp.where(lane < LORA_W, jnp.tanh(wa), wa).astype(jnp.bfloat16)
    lora = jnp.dot(wa, wa_ref[...], preferred_element_type=jnp.float32)
    w0, a0, k_k, k_a, r_k = (vec_ref[i:i + 1, :] for i in range(5))
    w_log = -_softplus(-(w0 + lora[:, :WIDTH_A])) - 0.5
    log_decay = -jnp.exp(w_log)
    a = jax.nn.sigmoid(a0 + lora[:, WIDTH_A:])
    g = jnp.dot(jax.nn.sigmoid(gl).astype(jnp.bfloat16), gup_ref[...], preferred_element_type=jnp.float32)
    ones2 = ones_ref[...]
    kk = k * k_k
    kk = kk * lax.rsqrt(jnp.maximum(_head_sums(kk * kk, ones2), 1e-24))
    kf = k * (1.0 + (a - 1.0) * k_a)
    r_ref[0], w_ref[0], k_ref[0], v_ref[0], kk_ref[0], a_ref[0], g_ref[0] = r, log_decay, kf, v, kk, a, g
    bonus_ref[0] = _head_sums(r * kf * r_k, ones2) * v


def _block_ones2():
    lane_head = jnp.arange(PAIR_W) // HEAD_A
    return (jnp.tile(lane_head, 2)[:, None] == lane_head[None, :]).astype(jnp.bfloat16)


def rwkv7_pre(seg, prev_row, p, tb=256):
    B, T, _ = seg.shape
    tb = min(tb, T)
    zeros = jnp.zeros((LORA_W, WIDTH_A), jnp.float32)
    wa_up = jnp.concatenate([jnp.concatenate([p['w_lora_up'], zeros], 1),
                             jnp.concatenate([zeros, p['a_lora_up']], 1)], 0).astype(jnp.bfloat16)
    vecs = jnp.stack([p['w0'], p['a0'], p['k_k'], p['k_a'], p['r_k'].reshape(WIDTH_A)])
    out = jax.ShapeDtypeStruct((B, T, WIDTH_A), jnp.float32)
    ospec = pl.BlockSpec((1, tb, WIDTH_A), lambda b, t: (b, t, 0))
    full = lambda shape: pl.BlockSpec(shape, lambda b, t: (0,) * len(shape))
    return pl.pallas_call(
        _rwkv_pre_kernel,
        out_shape=(out,) * 8,
        grid=(B, T // tb),
        in_specs=[pl.BlockSpec((1, tb, C_RWKV), lambda b, t: (b, t, 0)),
                  pl.BlockSpec((1, 8, C_RWKV), lambda b, t: (b, jnp.maximum(t * (tb // 8) - 1, 0), 0)),
                  pl.BlockSpec((1, 1, C_RWKV), lambda b, t: (b, 0, 0)),
                  full((1, C_RWKV)), full((5, WIDTH_A)), full((LORA_W + LORA_A, 2 * WIDTH_A)),
                  full((LORA_G, WIDTH_A)), full((2 * PAIR_W, PAIR_W))],
        out_specs=(ospec,) * 8,
        compiler_params=pltpu.CompilerParams(
            dimension_semantics=("arbitrary", "arbitrary"), vmem_limit_bytes=VMEM_LIMIT),
        name="rwkv7_pre",
    )(seg, seg, prev_row, p['shift_mu'].reshape(1, C_RWKV), vecs, wa_up,
      p['g_lora_up'].astype(jnp.bfloat16), _block_ones2())


SWA_BAND = WINDOW + CHUNK


def _swa_kernel(q_ref, k_ref, v_ref, kp_ref, vp_ref, bias_ref, sink_ref, o_ref, *, mask_start):
    n_chunks = q_ref.shape[1] // CHUNK
    k_all = jnp.concatenate([kp_ref[0], k_ref[0]], axis=0).astype(jnp.bfloat16)
    v_all = jnp.concatenate([vp_ref[0], v_ref[0]], axis=0).astype(jnp.bfloat16)
    first = pl.program_id(1) == 0
    key_chunk = lax.broadcasted_iota(jnp.int32, (GROUP_B * CHUNK, SWA_BAND), 1) // CHUNK
    for c in range(n_chunks):
        q_c = q_ref[0, c * CHUNK:(c + 1) * CHUNK, :].astype(jnp.bfloat16)
        k_c = k_all[c * CHUNK:c * CHUNK + SWA_BAND]
        v_c = v_all[c * CHUNK:c * CHUNK + SWA_BAND]
        dead = jnp.logical_and(first, key_chunk + (c - WIN_CHUNKS) < 0) if (mask_start and c < WIN_CHUNKS) else None
        kvs = range(N_KV_B)
        qg = [jnp.concatenate([q_c[:, (kv * GROUP_B + g) * HEAD_B:(kv * GROUP_B + g + 1) * HEAD_B]
                               for g in range(GROUP_B)], axis=0) for kv in kvs]
        s = [lax.dot_general(qg[kv], k_c[:, kv * HEAD_B:(kv + 1) * HEAD_B], (((1,), (1,)), ((), ())),
                             preferred_element_type=jnp.float32) * (HEAD_B ** -0.5) + bias_ref[kv] for kv in kvs]
        if dead is not None:
            s = [jnp.where(dead, NEG_INF, s[kv]) for kv in kvs]
        sink = [sink_ref[kv * GROUP_B * CHUNK:(kv + 1) * GROUP_B * CHUNK, 0:1] for kv in kvs]
        m = [jnp.maximum(jnp.max(s[kv], axis=-1, keepdims=True), sink[kv]) for kv in kvs]
        e = [jnp.exp(s[kv] - m[kv]) for kv in kvs]
        pr = [e[kv] / (jnp.sum(e[kv], axis=-1, keepdims=True) + jnp.exp(sink[kv] - m[kv])) for kv in kvs]
        og = [jnp.dot(pr[kv].astype(jnp.bfloat16), v_c[:, kv * HEAD_B:(kv + 1) * HEAD_B],
                      preferred_element_type=jnp.float32) for kv in kvs]
        o_ref[0, c * CHUNK:(c + 1) * CHUNK, :] = jnp.concatenate(
            [og[kv][g * CHUNK:(g + 1) * CHUNK] for kv in kvs for g in range(GROUP_B)], axis=1)


def swa_attention(seg_b, prev_k, prev_v, sinks, *, prev_is_seq, qb=512):
    B, T, _ = seg_b.shape
    qb = min(qb, T)
    slopes = 2.0 ** (-8.0 * jnp.arange(1, N_HEADS_B + 1, dtype=jnp.float32) / N_HEADS_B)
    dist = jnp.abs(jnp.arange(CHUNK)[:, None] - (jnp.arange(SWA_BAND) - WINDOW)[None, :]).astype(jnp.float32)
    bias = (-slopes[:, None, None] * dist).reshape(N_KV_B, GROUP_B * CHUNK, SWA_BAND)
    sink_tab = jnp.broadcast_to(jnp.repeat(sinks.astype(jnp.float32), CHUNK)[:, None], (N_HEADS_B * CHUNK, 128))
    kcol, vcol = WIDTH_B // KV_WIDTH_B, WIDTH_B // KV_WIDTH_B + 1
    if prev_is_seq:
        per = qb // WINDOW
        kp_spec = pl.BlockSpec((1, WINDOW, KV_WIDTH_B), lambda b, i: (b, jnp.maximum(i * per - 1, 0), kcol))
        vp_spec = pl.BlockSpec((1, WINDOW, KV_WIDTH_B), lambda b, i: (b, jnp.maximum(i * per - 1, 0), vcol))
    else:
        kp_spec = vp_spec = pl.BlockSpec((1, WINDOW, KV_WIDTH_B), lambda b, i: (b, 0, 0))
    return pl.pallas_call(
        functools.partial(_swa_kernel, mask_start=prev_is_seq),
        out_shape=jax.ShapeDtypeStruct((B, T, WIDTH_B), jnp.float32),
        grid=(B, T // qb),
        in_specs=[pl.BlockSpec((1, qb, WIDTH_B), lambda b, i: (b, i, 0)),
                  pl.BlockSpec((1, qb, KV_WIDTH_B), lambda b, i: (b, i, kcol)),
                  pl.BlockSpec((1, qb, KV_WIDTH_B), lambda b, i: (b, i, vcol)),
                  kp_spec, vp_spec,
                  pl.BlockSpec((N_KV_B, GROUP_B * CHUNK, SWA_BAND), lambda b, i: (0, 0, 0)),
                  pl.BlockSpec((N_HEADS_B * CHUNK, 128), lambda b, i: (0, 0))],
        out_specs=pl.BlockSpec((1, qb, WIDTH_B), lambda b, i: (b, i, 0)),
        compiler_params=pltpu.CompilerParams(
            dimension_semantics=("arbitrary", "arbitrary"), vmem_limit_bytes=VMEM_LIMIT),
        name="swa_attention",
    )(seg_b, seg_b, seg_b, prev_k, prev_v, bias, sink_tab)


def _layer_norm_rows(h, g, b, eps=1e-5):
    mu = jnp.mean(h, axis=-1, keepdims=True)
    d = h - mu
    var = jnp.mean(d * d, axis=-1, keepdims=True)
    return d * lax.rsqrt(var + eps) * g + b


def _const_spec(shape):
    return pl.BlockSpec(shape, lambda *_: (0,) * len(shape), pipeline_mode=pl.Buffered(1))


def _merge_kernel(x_ref, y_ref, bonus_ref, g_ref, ob_ref, gates_ref, lnx_ref, ln1_ref, ones_ref,
                  pa_ref, pb_ref, wout_ref, o_ref):
    ones2 = ones_ref[...]
    y = y_ref[...]
    mean = _head_sums(y, ones2) * (1.0 / HEAD_A)
    d = y - mean
    var = _head_sums(d * d, ones2) * (1.0 / HEAD_A)
    yn = d * lax.rsqrt(var + GN_EPS) * lnx_ref[0:1, :] + lnx_ref[1:2, :]
    o_a = ((yn + bonus_ref[...]) * g_ref[...]).astype(jnp.bfloat16)
    br_a = jnp.dot(o_a, pa_ref[...], preferred_element_type=jnp.float32)
    br_b = jnp.dot(ob_ref[...].astype(jnp.bfloat16), pb_ref[...], preferred_element_type=jnp.float32)
    gates = jax.nn.sigmoid(gates_ref[...])
    merged = gates[:, :D_MODEL] * br_a + gates[:, D_MODEL:] * br_b
    h = ALPHA * x_ref[...] + jnp.dot(merged.astype(jnp.bfloat16), wout_ref[...], preferred_element_type=jnp.float32)
    o_ref[...] = _layer_norm_rows(h, ln1_ref[0:1, :], ln1_ref[1:2, :])


def branch_merge(x, y, bonus, g, o_b, gates, p, tm=256):
    n = x.shape[0]
    tm = _pick(n, (tm, 128, 64))
    rows = lambda w: pl.BlockSpec((tm, w), lambda i: (i, 0))
    return pl.pallas_call(
        _merge_kernel,
        out_shape=jax.ShapeDtypeStruct((n, D_MODEL), jnp.float32),
        grid=(n // tm,),
        in_specs=[rows(D_MODEL), rows(WIDTH_A), rows(WIDTH_A), rows(WIDTH_A), rows(WIDTH_B), rows(C_GATE),
                  _const_spec((2, WIDTH_A)), _const_spec((2, D_MODEL)), _const_spec((2 * PAIR_W, PAIR_W)),
                  _const_spec((WIDTH_A, D_MODEL)), _const_spec((WIDTH_B, D_MODEL)), _const_spec((D_MODEL, D_MODEL))],
        out_specs=rows(D_MODEL),
        compiler_params=pltpu.CompilerParams(dimension_semantics=("arbitrary",), vmem_limit_bytes=VMEM_LIMIT),
        name="branch_merge",
    )(x, y, bonus, g, o_b, gates, jnp.stack([p['lnx_g'], p['lnx_b']]), jnp.stack([p['ln1_g'], p['ln1_b']]),
      _block_ones2(), p['w_branch_a16'], p['w_branch_b16'], p['w_out16'])


def _mem_kernel(x_ref, mk_ref, mv_ref, ln2_ref, wq_ref, wo_ref, pwq_ref, o_ref, q_ref):
    x = x_ref[0]
    qm = jnp.dot(x.astype(jnp.bfloat16), wq_ref[...], preferred_element_type=jnp.float32).astype(jnp.bfloat16)
    mk = mk_ref[0].astype(jnp.bfloat16)
    mv = mv_ref[0].astype(jnp.bfloat16)
    outs = []
    for h in range(N_HEADS_M):
        cols = slice(h * HEAD_M, (h + 1) * HEAD_M)
        s = lax.dot_general(qm[:, cols], mk[:, cols], (((1,), (1,)), ((), ())),
                            preferred_element_type=jnp.float32) * (HEAD_M ** -0.5)
        e = jnp.exp(s - jnp.max(s, axis=-1, keepdims=True))
        pr = e / jnp.sum(e, axis=-1, keepdims=True)
        outs.append(jnp.dot(pr.astype(jnp.bfloat16), mv[:, cols], preferred_element_type=jnp.float32))
    o = jnp.concatenate(outs, axis=1).astype(jnp.bfloat16)
    h2 = ALPHA * x + jnp.dot(o, wo_ref[...], preferred_element_type=jnp.float32)
    x2 = _layer_norm_rows(h2, ln2_ref[0:1, :], ln2_ref[1:2, :])
    o_ref[0] = x2
    q_ref[0] = jnp.dot(x2.astype(jnp.bfloat16), pwq_ref[...], preferred_element_type=jnp.float32)


def mem_block(x, mk, mv, p, tm=256):
    B, T, _ = x.shape
    tm = _pick(T, (tm, 128, 64))
    wm = N_HEADS_M * HEAD_M
    rows = lambda w: pl.BlockSpec((1, tm, w), lambda b, i: (b, i, 0))
    mem = pl.BlockSpec((1, N_MEM, wm), lambda b, i: (b, 0, 0))
    out = jax.ShapeDtypeStruct((B, T, D_MODEL), jnp.float32)
    return pl.pallas_call(
        _mem_kernel,
        out_shape=(out, jax.ShapeDtypeStruct((B, T, PEER_HEADS * D_KEY), jnp.float32)),
        grid=(B, T // tm),
        in_specs=[rows(D_MODEL), mem, mem, _const_spec((2, D_MODEL)), _const_spec((D_MODEL, wm)),
                  _const_spec((wm, D_MODEL)), _const_spec((D_MODEL, PEER_HEADS * D_KEY))],
        out_specs=(rows(D_MODEL), rows(PEER_HEADS * D_KEY)),
        compiler_params=pltpu.CompilerParams(
            dimension_semantics=("arbitrary", "arbitrary"), vmem_limit_bytes=VMEM_LIMIT),
        name="mem_block",
    )(x, mk, mv, jnp.stack([p['ln2_g'], p['ln2_b']]), p['wq_mem16'], p['wo_mem16'], p['peer_wq16'])


ROUTE_TQ = 256


def _top_values(s, k, exact):
    n_rows = s.shape[0]
    iota = lax.broadcasted_iota(jnp.int32, s.shape, 0).astype(jnp.float32)
    rank = jnp.full(s.shape, float(k), jnp.float32)
    masked0 = jnp.sum((s == -jnp.inf).astype(jnp.float32), axis=0, keepdims=True)
    out = []
    for step in range(k):
        m = jnp.max(s, axis=0, keepdims=True)
        if exact:
            first = jnp.min(jnp.where(s == m, iota, float(n_rows)), axis=0, keepdims=True)
            taken = iota == first
        else:
            taken = s == m
        s = jnp.where(taken, -jnp.inf, s)
        rank = jnp.where(taken, float(step), rank)
        out.append(m)
    surplus = jnp.sum((s == -jnp.inf).astype(jnp.float32), axis=0, keepdims=True) - masked0 - float(k)
    return out, rank, surplus


def _peer_route_kernel(q_ref, keys_ref, n1_ref, c1_ref, rank2_ref, e2_ref):
    surplus = _peer_route_pass(q_ref, keys_ref, n1_ref, c1_ref, rank2_ref, e2_ref, exact=False)

    @pl.when(jnp.max(surplus) > 0.0)
    def _():
        _peer_route_pass(q_ref, keys_ref, n1_ref, c1_ref, rank2_ref, e2_ref, exact=True)


def _peer_route_pass(q_ref, keys_ref, n1_ref, c1_ref, rank2_ref, e2_ref, *, exact):
    half = D_KEY // 2
    surplus = None
    for h in range(PEER_HEADS):
        tops, scores = [], []
        for p in range(2):
            c0 = (2 * h + p) * half
            qs = q_ref[:, c0:c0 + half].astype(jnp.bfloat16)
            s = lax.dot_general(keys_ref[2 * h + p], qs, (((1,), (1,)), ((), ())),
                                preferred_element_type=jnp.float32)
            scores.append(s)
            tops.append(_top_values(s, TOPK, exact))
        (t1, _, sur1), (t2, rank2, sur2) = tops
        t2all = jnp.concatenate(t2, axis=0)
        t1all = jnp.concatenate(t1, axis=0)
        rank8 = lax.broadcasted_iota(jnp.int32, (8, t1all.shape[1]), 0)
        cand_rows = [t1all + t2[0], t1all[:8] + t2[1]]
        for b in range(2, 8):
            cand_rows.append(jnp.where(rank8 < TOPK // (b + 1), t1all[:8] + t2[b], -jnp.inf))
        cand_rows.append(t1[0] + t2all[8:])
        sc, _, sur3 = _top_values(jnp.concatenate(cand_rows, axis=0), TOPK, exact)
        sur = sur1 + sur2 + sur3
        surplus = sur if surplus is None else surplus + sur
        z = jnp.zeros_like(sc[0])
        for kq in range(TOPK):
            z = z + jnp.exp(sc[kq] - sc[0])
        theta = sc[TOPK - 1]
        n1 = jnp.zeros_like(scores[0])
        for a in range(TOPK):
            n_a = jnp.sum((t1[a] + t2all >= theta).astype(jnp.float32), axis=0, keepdims=True)
            n1 = jnp.where(scores[0] == t1[a], n_a, n1)
        n1_ref[0, h] = n1
        rank2_ref[0, h] = rank2.astype(jnp.bfloat16)
        c1_ref[0, h] = jnp.exp(scores[0] - t1[0]) / z
        e2_ref[0, h] = jnp.exp(scores[1] - t2[0]).astype(jnp.bfloat16)
    return surplus


def peer_route(q, keys16):
    n = q.shape[0]
    tq = ROUTE_TQ
    big = jax.ShapeDtypeStruct((n // tq, PEER_HEADS, N_KEYS, tq), jnp.float32)
    big16 = jax.ShapeDtypeStruct((n // tq, PEER_HEADS, N_KEYS, tq), jnp.bfloat16)
    bspec = pl.BlockSpec((1, PEER_HEADS, N_KEYS, tq), lambda i: (i, 0, 0, 0))
    return pl.pallas_call(
        _peer_route_kernel,
        out_shape=(big, big, big16, big16),
        grid=(n // tq,),
        in_specs=[pl.BlockSpec((tq, PEER_HEADS * D_KEY), lambda i: (i, 0)),
                  pl.BlockSpec((2 * PEER_HEADS, N_KEYS, D_KEY // 2), lambda i: (0, 0, 0))],
        out_specs=(bspec, bspec, bspec, bspec),
        compiler_params=pltpu.CompilerParams(
            dimension_semantics=("arbitrary",), vmem_limit_bytes=VMEM_LIMIT),
        name="peer_route",
    )(q, keys16)


PEER_TM = 512
PEER_ROWS = 8
PEER_TE = PEER_ROWS * N_KEYS


def _gelu(x):
    return 0.5 * x * (1.0 + lax.erf(x * (2.0 ** -0.5)))


def _peer_mix_kernel(x_ref, u_ref, vt_ref, n1_ref, c1_ref, rank2_ref, e2_ref, ln3_ref, o_ref,
                     x16_ref, h_ref, acc_ref):
    j = pl.program_id(1)

    @pl.when(j == 0)
    def _():
        x16_ref[...] = x_ref[...].T.astype(jnp.bfloat16)
        acc_ref[...] = jnp.zeros_like(acc_ref)

    half_te = PEER_TE // 2
    a_halves = [jnp.dot(u_ref[k * half_te:(k + 1) * half_te, :], x16_ref[...],
                        preferred_element_type=jnp.float32) for k in range(2)]

    def sublane_bcast16(row):
        return jnp.broadcast_to(row, (N_KEYS, row.shape[1])).astype(jnp.bfloat16)

    for r in range(PEER_ROWS):
        rows = slice(r * N_KEYS, (r + 1) * N_KEYS)
        for c0 in range(0, x_ref.shape[0], 128):
            cols = slice(c0, c0 + 128)
            blk, bcols = c0 // ROUTE_TQ, slice(c0 % ROUTE_TQ, c0 % ROUTE_TQ + 128)
            gate = None
            for h in range(PEER_HEADS):
                keep = rank2_ref[blk, h, :, bcols] < sublane_bcast16(n1_ref[blk, h, r:r + 1, bcols])
                w = jnp.where(keep, e2_ref[blk, h, :, bcols], jnp.zeros((), jnp.bfloat16))
                w = w * sublane_bcast16(c1_ref[blk, h, r:r + 1, bcols])
                gate = w if gate is None else gate + w
            a_rows = a_halves[r // (PEER_ROWS // 2)][(r % (PEER_ROWS // 2)) * N_KEYS:(r % (PEER_ROWS // 2) + 1) * N_KEYS]
            h_ref[rows, cols] = gate * _gelu(a_rows[:, cols]).astype(jnp.bfloat16)
    acc_ref[...] += jnp.dot(vt_ref[0], h_ref[...], preferred_element_type=jnp.float32)

    @pl.when(j == pl.num_programs(1) - 1)
    def _():
        o_ref[...] = _layer_norm_rows(ALPHA * x_ref[...] + acc_ref[...].T, ln3_ref[0:1, :], ln3_ref[1:2, :])


def peer_mix(x, u16, vt16, n1, c1, rank2, e2, ln3):
    n, d = x.shape
    tm = PEER_TM
    n_exp = u16.shape[0]
    per = tm // ROUTE_TQ
    row_spec = pl.BlockSpec((per, PEER_HEADS, PEER_ROWS, ROUTE_TQ), lambda i, j: (i, 0, j, 0))
    all_spec = pl.BlockSpec((per, PEER_HEADS, N_KEYS, ROUTE_TQ), lambda i, j: (i, 0, 0, 0),
                            pipeline_mode=pl.Buffered(1))
    return pl.pallas_call(
        _peer_mix_kernel,
        out_shape=jax.ShapeDtypeStruct((n, d), jnp.float32),
        grid=(n // tm, n_exp // PEER_TE),
        in_specs=[pl.BlockSpec((tm, d), lambda i, j: (i, 0), pipeline_mode=pl.Buffered(1)),
                  pl.BlockSpec((PEER_TE, d), lambda i, j: (j, 0)),
                  pl.BlockSpec((1, d, PEER_TE), lambda i, j: (j, 0, 0)),
                  row_spec, row_spec, all_spec, all_spec, _const_spec((2, d))],
        out_specs=pl.BlockSpec((tm, d), lambda i, j: (i, 0)),
        scratch_shapes=[pltpu.VMEM((d, tm), jnp.bfloat16), pltpu.VMEM((PEER_TE, tm), jnp.bfloat16),
                        pltpu.VMEM((d, tm), jnp.float32)],
        compiler_params=pltpu.CompilerParams(
            dimension_semantics=("arbitrary", "arbitrary"), vmem_limit_bytes=VMEM_LIMIT),
        name="peer_mix",
    )(x, u16, vt16, n1, c1, rank2, e2, ln3)


def peer_block(x, q, p):
    stats = peer_route(q, p['peer_keys16'])
    return peer_mix(x, p['peer_u16'], p['peer_vt16'], *stats, jnp.stack([p['ln3_g'], p['ln3_b']]))


def trunk_layer(x, p, mem_k, mem_v, rwkv_state, shift_row, swa_k_cache, swa_v_cache):
    B, T, _ = x.shape
    n = B * T
    x2d = x.reshape(n, D_MODEL)
    seg_a = matmul(x2d, p['w_in_a16']).reshape(B, T, C_RWKV)
    seg_b = matmul(x2d, p['w_in_b16']).reshape(B, T, C_SWA)
    gates = matmul(x2d, p['w_in_g16'])
    r, w, k, v, kk, a, g, bonus = rwkv7_pre(seg_a, shift_row.astype(jnp.float32), p)
    y, rwkv_new = rwkv7_chunked(r, w, k, v, kk, a, rwkv_state)
    shift_new = seg_a[:, -1:]
    k_new = seg_b[:, :, WIDTH_B:WIDTH_B + KV_WIDTH_B].reshape(B, T, N_KV_B, HEAD_B)
    v_new = seg_b[:, :, WIDTH_B + KV_WIDTH_B:].reshape(B, T, N_KV_B, HEAD_B)
    if swa_k_cache is None:
        o_b = swa_attention(seg_b, seg_b, seg_b, p['attn_sinks'], prev_is_seq=True)
        swa_k_new, swa_v_new = k_new[:, -WINDOW:], v_new[:, -WINDOW:]
    else:
        o_b = swa_attention(seg_b, swa_k_cache.reshape(B, WINDOW, KV_WIDTH_B),
                            swa_v_cache.reshape(B, WINDOW, KV_WIDTH_B), p['attn_sinks'], prev_is_seq=False)
        swa_k_new = jnp.concatenate([swa_k_cache, k_new], axis=1)[:, -WINDOW:]
        swa_v_new = jnp.concatenate([swa_v_cache, v_new], axis=1)[:, -WINDOW:]
    flat = lambda t: t.reshape(n, t.shape[-1])
    x1 = branch_merge(x2d, flat(y), flat(bonus), flat(g), flat(o_b), gates, p)
    wm = N_HEADS_M * HEAD_M
    x2, q = mem_block(x1.reshape(B, T, D_MODEL), mem_k.reshape(-1, N_MEM, wm), mem_v.reshape(-1, N_MEM, wm), p)
    x3 = peer_block(flat(x2), flat(q), p)
    return x3.reshape(B, T, D_MODEL), rwkv_new, shift_new, swa_k_new, swa_v_new


_MM_WEIGHTS = ('w_branch_a', 'w_branch_b', 'w_out', 'wq_mem', 'wk_mem', 'wv_mem', 'wo_mem', 'peer_wq')


def kernel(x_prompt, x_sample, state_rwkv, state_shift, cache_swa_k, cache_swa_v, cache_mem_k, cache_mem_v, mem_prompt, w_in, shift_mu, w0, w_lora_up, a0, a_lora_up, g_lora_up, k_k, k_a, r_k, lnx_g, lnx_b, attn_sinks, w_branch_a, w_branch_b, w_out, ln1_g, ln1_b, wq_mem, wk_mem, wv_mem, wo_mem, ln2_g, ln2_b, peer_wq, peer_sub_keys, peer_u, peer_v, ln3_g, ln3_b):
    params = {
        'w_in': w_in, 'shift_mu': shift_mu, 'w0': w0, 'w_lora_up': w_lora_up, 'a0': a0,
        'a_lora_up': a_lora_up, 'g_lora_up': g_lora_up, 'k_k': k_k, 'k_a': k_a, 'r_k': r_k,
        'lnx_g': lnx_g, 'lnx_b': lnx_b, 'attn_sinks': attn_sinks, 'w_branch_a': w_branch_a,
        'w_branch_b': w_branch_b, 'w_out': w_out, 'ln1_g': ln1_g, 'ln1_b': ln1_b, 'wq_mem': wq_mem,
        'wk_mem': wk_mem, 'wv_mem': wv_mem, 'wo_mem': wo_mem, 'ln2_g': ln2_g, 'ln2_b': ln2_b,
        'peer_wq': peer_wq, 'peer_sub_keys': peer_sub_keys, 'peer_u': peer_u, 'peer_v': peer_v,
        'ln3_g': ln3_g, 'ln3_b': ln3_b,
    }
    B = x_prompt.shape[0]
    rwkv0 = jnp.zeros((B, N_HEADS_A, HEAD_A, HEAD_A), jnp.float32)
    shift0 = jnp.zeros((B, 1, C_RWKV), x_prompt.dtype)
    xp, xs = x_prompt, x_sample
    p_rw, p_sh, p_k, p_v, p_mk, p_mv = [], [], [], [], [], []
    s_rw, s_sh, s_k, s_v = [], [], [], []
    for l in range(DEPTH):
        p = {name: arr[l] for name, arr in params.items()}
        for name in _MM_WEIGHTS:
            p[name + '16'] = p[name].astype(jnp.bfloat16)
        w_in16 = p['w_in'].astype(jnp.bfloat16)
        p['w_in_a16'] = w_in16[:, :C_RWKV]
        p['w_in_b16'] = w_in16[:, C_RWKV:C_RWKV + C_SWA]
        p['w_in_g16'] = w_in16[:, C_RWKV + C_SWA:]
        p['peer_keys16'] = p['peer_sub_keys'].reshape(2 * PEER_HEADS, N_KEYS, D_KEY // 2).astype(jnp.bfloat16)
        p['peer_u16'] = p['peer_u'].astype(jnp.bfloat16)
        p['peer_vt16'] = (p['peer_v'].astype(jnp.bfloat16)
                          .reshape(p['peer_v'].shape[0] // PEER_TE, PEER_TE, D_MODEL).transpose(0, 2, 1))
        mk = mm(mem_prompt, p['wk_mem16']).reshape(B, N_MEM, N_HEADS_M, HEAD_M)
        mv = mm(mem_prompt, p['wv_mem16']).reshape(B, N_MEM, N_HEADS_M, HEAD_M)
        xp, rw, sh, kn, vn = trunk_layer(xp, p, mk, mv, rwkv0, shift0, None, None)
        p_rw.append(rw); p_sh.append(sh); p_k.append(kn); p_v.append(vn); p_mk.append(mk); p_mv.append(mv)
        xs, rw, sh, kn, vn = trunk_layer(xs, p, cache_mem_k[l], cache_mem_v[l], state_rwkv[l], state_shift[l],
                                         cache_swa_k[l], cache_swa_v[l])
        s_rw.append(rw); s_sh.append(sh); s_k.append(kn); s_v.append(vn)
    return (xp, xs,
            jnp.stack(p_rw), jnp.stack(p_sh), jnp.stack(p_k), jnp.stack(p_v), jnp.stack(p_mk), jnp.stack(p_mv),
            jnp.stack(s_rw), jnp.stack(s_sh), jnp.stack(s_k), jnp.stack(s_v))
```

```python
import functools

import jax
import jax.numpy as jnp
from jax import lax
from jax.experimental import pallas as pl
from jax.experimental.pallas import tpu as pltpu

D_MODEL = 2048
DEPTH = 2
CHUNK = 64
HEAD_A = 64
N_HEADS_A = 16
WIDTH_A = N_HEADS_A * HEAD_A
LORA_W = 64
LORA_A = 64
LORA_G = 128
GN_EPS = 64e-5
C_RWKV = 3 * WIDTH_A + LORA_W + LORA_A + LORA_G
HEAD_B = 64
N_HEADS_B = 16
N_KV_B = 4
GROUP_B = N_HEADS_B // N_KV_B
WIDTH_B = N_HEADS_B * HEAD_B
KV_WIDTH_B = N_KV_B * HEAD_B
WINDOW = 128
WIN_CHUNKS = WINDOW // CHUNK
C_SWA = WIDTH_B + 2 * KV_WIDTH_B
C_GATE = 2 * D_MODEL
N_MEM = 256
N_HEADS_M = 4
HEAD_M = 128
N_KEYS = 128
PEER_HEADS = 8
D_KEY = 256
TOPK = 16
ALPHA = (2.0 * DEPTH) ** 0.25
NEG_INF = -1e30

VMEM_LIMIT = 56 * 1024 * 1024


def _matmul_kernel(a_ref, b_ref, o_ref, a16_ref):
    @pl.when(pl.program_id(1) == 0)
    def _():
        a16_ref[...] = a_ref[...].astype(jnp.bfloat16)

    o_ref[...] = jnp.dot(a16_ref[...], b_ref[...], preferred_element_type=jnp.float32)


def _pick(n, cands):
    for c in cands:
        if n % c == 0:
            return c
    return n


def matmul(a, b16):
    m, k = a.shape
    n = b16.shape[1]
    tm = _pick(m, (512, 256, 128))
    tn = _pick(n, (2048, 1664, 1536, 1024, 512, 256, 128))
    return pl.pallas_call(
        _matmul_kernel,
        out_shape=jax.ShapeDtypeStruct((m, n), jnp.float32),
        grid=(m // tm, n // tn),
        in_specs=[pl.BlockSpec((tm, k), lambda i, j: (i, 0)),
                  pl.BlockSpec((k, tn), lambda i, j: (0, j))],
        out_specs=pl.BlockSpec((tm, tn), lambda i, j: (i, j)),
        scratch_shapes=[pltpu.VMEM((tm, k), jnp.bfloat16)],
        compiler_params=pltpu.CompilerParams(
            dimension_semantics=("arbitrary", "arbitrary"),
            vmem_limit_bytes=VMEM_LIMIT),
        name="matmul",
    )(a, b16)


def mm(x, w16):
    lead = x.shape[:-1]
    return matmul(x.reshape(-1, x.shape[-1]), w16).reshape(*lead, w16.shape[1])


N_PAIR = N_HEADS_A // 2
PAIR_W = 2 * HEAD_A


def _split2(x):
    hi = x.astype(jnp.bfloat16)
    lo = (x - hi.astype(jnp.float32)).astype(jnp.bfloat16)
    return jnp.concatenate([hi, lo], axis=1)


def _hl(x):
    hi = x.astype(jnp.bfloat16)
    return hi, (x - hi.astype(jnp.float32)).astype(jnp.bfloat16)


def _dot3(a, b):
    ah, al = _hl(a)
    bh, bl = _hl(b)
    return jnp.dot(jnp.concatenate([ah, ah, al], axis=1), jnp.concatenate([bh, bl, bh], axis=0),
                   preferred_element_type=jnp.float32)


def _dot3_nt(a, b):
    ah, al = _hl(a)
    bh, bl = _hl(b)
    return lax.dot_general(jnp.concatenate([ah, ah, al], axis=1), jnp.concatenate([bh, bl, bh], axis=1),
                           (((1,), (1,)), ((), ())), preferred_element_type=jnp.float32)


def _rwkv_chunk_kernel(r_ref, w_ref, k_ref, v_ref, kk_ref, a_ref, s0_ref, ltri_ref, ones_ref, msk_ref,
                       y_ref, sT_ref, p_ref, *, n_chunks):
    tblk = pl.program_id(1)

    @pl.when(tblk == 0)
    def _():
        p_ref[...] = s0_ref[0]

    eye2, m0, m1, strict, incl = (msk_ref[i] for i in range(5))
    bd = lambda y: jnp.concatenate([y * m0, y * m1], axis=0)

    def split3(x, axis):
        t1 = x.astype(jnp.bfloat16)
        d = x - t1.astype(jnp.float32)
        t2 = d.astype(jnp.bfloat16)
        t3 = (d - t2.astype(jnp.float32)).astype(jnp.bfloat16)
        return jnp.concatenate([t1, t2, t3], axis=axis)

    def chunk(c, carry):
        rows = pl.ds(pl.multiple_of(c * CHUNK, CHUNK), CHUNK)
        pairs = range(N_PAIR)
        sls = [pl.ds(p * PAIR_W, PAIR_W) for p in pairs]
        lw = [w_ref[0, rows, sl] for sl in sls]
        cum = [jnp.dot(ltri_ref[...], split3(lw[p], 0), preferred_element_type=jnp.float32) for p in pairs]
        g = [jnp.exp(cum[p]) for p in pairs]
        ginv = [jnp.exp(-cum[p]) for p in pairs]
        kk = [kk_ref[0, rows, sl] for sl in sls]
        kh = [kk[p] * jnp.exp(cum[p] - lw[p]) for p in pairs]
        bh = [kk[p] * a_ref[0, rows, sls[p]] * ginv[p] for p in pairs]
        kf = [k_ref[0, rows, sls[p]] * ginv[p] for p in pairs]
        rh = [r_ref[0, rows, sls[p]] * g[p] for p in pairs]
        v = [v_ref[0, rows, sl] for sl in sls]
        g_last = [g[p][CHUNK - 1:CHUNK, :] for p in pairs]
        kr = [jnp.concatenate([kh[p], rh[p]], axis=0) for p in pairs]
        gram = [_dot3_nt(kr[p], jnp.concatenate([bd(bh[p]), bd(kf[p])], axis=0)) for p in pairs]
        a_b = [gram[p][:CHUNK, :PAIR_W] * strict for p in pairs]
        a_k = [gram[p][:CHUNK, PAIR_W:] * strict for p in pairs]
        a_r = [jnp.concatenate([gram[p][CHUNK:, :PAIR_W] * incl, gram[p][CHUNK:, PAIR_W:] * incl], axis=1)
               for p in pairs]
        t_inv = [eye2 - a_b[p] * msk_ref[5] for p in pairs]
        for lvl in range(1, 6):
            half = [_dot3(t_inv[p], bd(a_b[p] * msk_ref[5 + lvl])) for p in pairs]
            t_inv = [t_inv[p] - _dot3(half[p], bd(t_inv[p])) for p in pairs]
        akv = [_dot3(a_k[p], bd(v[p])) for p in pairs]
        g_col = [jnp.dot(split3(eye2 * g_last[p], 1), ones_ref[...], preferred_element_type=jnp.float32)
                 for p in pairs]
        xt = [jnp.concatenate([bh[p] * g_last[p], kf[p] * g_last[p]], axis=0).T for p in pairs]
        p0 = [p_ref[p] for p in pairs]
        zy = [_dot3(kr[p], bd(p0[p])) for p in pairs]
        u = [-_dot3(t_inv[p], bd(zy[p][:CHUNK] + akv[p])) for p in pairs]
        y = [zy[p][CHUNK:] + _dot3(a_r[p], jnp.concatenate([bd(u[p]), bd(v[p])], axis=0)) for p in pairs]
        delta = [_dot3(xt[p], jnp.concatenate([u[p], v[p]], axis=0)) for p in pairs]
        for p in pairs:
            p_ref[p] = g_col[p] * p0[p] + delta[p][:CHUNK] * m0 + delta[p][CHUNK:] * m1
            y_ref[0, rows, sls[p]] = y[p]
        return carry

    lax.fori_loop(0, n_chunks, chunk, 0)

    @pl.when(tblk == pl.num_programs(1) - 1)
    def _():
        sT_ref[0] = p_ref[...]


def rwkv7_chunked(r, logw, k, v, kk, a, state0, tb=256):
    B, T, _ = r.shape
    tb = min(tb, T)
    s0 = state0.astype(jnp.float32).reshape(B, N_PAIR, 2, HEAD_A, HEAD_A)
    s0 = s0.transpose(0, 1, 4, 2, 3).reshape(B, N_PAIR, HEAD_A, PAIR_W)
    t_i = jnp.arange(CHUNK)[:, None]
    s_i = (jnp.arange(PAIR_W) % HEAD_A)[None, :]
    lane_head = (jnp.arange(PAIR_W) // HEAD_A)[None, :]
    ones_row = jnp.ones((CHUNK, 1), jnp.int32)
    masks = [t_i == s_i, (lane_head == 0) * ones_row, (lane_head == 1) * ones_row, s_i < t_i, s_i <= t_i]
    for m in (1, 2, 4, 8, 16, 32):
        masks.append((t_i // (2 * m) == s_i // (2 * m)) & (t_i % (2 * m) >= m) & (s_i % (2 * m) < m))
    masks = jnp.stack([mk.astype(jnp.float32) for mk in masks])
    ltri = jnp.tile((jnp.arange(CHUNK)[None, :] <= jnp.arange(CHUNK)[:, None]), (1, 3)).astype(jnp.bfloat16)
    ones3 = jnp.tile((lane_head.T == lane_head), (3, 1)).astype(jnp.bfloat16)
    seq = pl.BlockSpec((1, tb, WIDTH_A), lambda b, t: (b, t, 0))
    st = pl.BlockSpec((1, N_PAIR, HEAD_A, PAIR_W), lambda b, t: (b, 0, 0, 0))
    y, sT = pl.pallas_call(
        functools.partial(_rwkv_chunk_kernel, n_chunks=tb // CHUNK),
        out_shape=(jax.ShapeDtypeStruct((B, T, WIDTH_A), jnp.float32),
                   jax.ShapeDtypeStruct((B, N_PAIR, HEAD_A, PAIR_W), jnp.float32)),
        grid=(B, T // tb),
        in_specs=[seq] * 6 + [st, _const_spec((CHUNK, 3 * CHUNK)), _const_spec((3 * PAIR_W, PAIR_W)),
                              _const_spec((11, CHUNK, PAIR_W))],
        out_specs=(seq, st),
        scratch_shapes=[pltpu.VMEM((N_PAIR, HEAD_A, PAIR_W), jnp.float32)],
        compiler_params=pltpu.CompilerParams(
            dimension_semantics=("arbitrary", "arbitrary"), vmem_limit_bytes=VMEM_LIMIT),
        name="rwkv7_chunked",
    )(r, logw, k, v, kk, a, s0, ltri, ones3, masks)
    sT = sT.reshape(B, N_PAIR, HEAD_A, 2, HEAD_A).transpose(0, 1, 3, 4, 2)
    return y, sT.reshape(B, N_HEADS_A, HEAD_A, HEAD_A)


def _head_sums(x, ones2):
    tiles = [jnp.dot(_split2(x[:, c:c + PAIR_W]), ones2, preferred_element_type=jnp.float32)
             for c in range(0, WIDTH_A, PAIR_W)]
    return jnp.concatenate(tiles, axis=1)


def _softplus(z):
    return jnp.maximum(z, 0.0) + jnp.log1p(jnp.exp(-jnp.abs(z)))


def _rwkv_pre_kernel(seg_ref, prev_ref, shift_ref, mu_ref, vec_ref, wa_ref, gup_ref, ones_ref,
                     r_ref, w_ref, k_ref, v_ref, kk_ref, a_ref, g_ref, bonus_ref):
    seg = seg_ref[0]
    tb = seg.shape[0]
    before = jnp.where(pl.program_id(1) == 0, shift_ref[0], prev_ref[0, 7:8, :])
    row = lax.broadcasted_iota(jnp.int32, seg.shape, 0)
    shifted = jnp.where(row == 0, before, pltpu.roll(seg, 1, axis=0))
    xm = seg + mu_ref[...] * (shifted - seg)
    r = xm[:, :WIDTH_A]
    k = xm[:, WIDTH_A:2 * WIDTH_A]
    v = xm[:, 2 * WIDTH_A:3 * WIDTH_A]
    wa = xm[:, 3 * WIDTH_A:3 * WIDTH_A + LORA_W + LORA_A]
    gl = xm[:, 3 * WIDTH_A + LORA_W + LORA_A:]
    lane = lax.broadcasted_iota(jnp.int32, wa.shape, 1)
    wa = jnp.where(lane < LORA_W, jnp.tanh(wa), wa).astype(jnp.bfloat16)
    lora = jnp.dot(wa, wa_ref[...], preferred_element_type=jnp.float32)
    w0, a0, k_k, k_a, r_k = (vec_ref[i:i + 1, :] for i in range(5))
    w_log = -_softplus(-(w0 + lora[:, :WIDTH_A])) - 0.5
    log_decay = -jnp.exp(w_log)
    a = jax.nn.sigmoid(a0 + lora[:, WIDTH_A:])
    g = jnp.dot(jax.nn.sigmoid(gl).astype(jnp.bfloat16), gup_ref[...], preferred_element_type=jnp.float32)
    ones2 = ones_ref[...]
    kk = k * k_k
    kk = kk * lax.rsqrt(jnp.maximum(_head_sums(kk * kk, ones2), 1e-24))
    kf = k * (1.0 + (a - 1.0) * k_a)
    r_ref[0], w_ref[0], k_ref[0], v_ref[0], kk_ref[0], a_ref[0], g_ref[0] = r, log_decay, kf, v, kk, a, g
    bonus_ref[0] = _head_sums(r * kf * r_k, ones2) * v


def _block_ones2():
    lane_head = jnp.arange(PAIR_W) // HEAD_A
    return (jnp.tile(lane_head, 2)[:, None] == lane_head[None, :]).astype(jnp.bfloat16)


def rwkv7_pre(seg, prev_row, p, tb=256):
    B, T, _ = seg.shape
    tb = min(tb, T)
    zeros = jnp.zeros((LORA_W, WIDTH_A), jnp.float32)
    wa_up = jnp.concatenate([jnp.concatenate([p['w_lora_up'], zeros], 1),
                             jnp.concatenate([zeros, p['a_lora_up']], 1)], 0).astype(jnp.bfloat16)
    vecs = jnp.stack([p['w0'], p['a0'], p['k_k'], p['k_a'], p['r_k'].reshape(WIDTH_A)])
    out = jax.ShapeDtypeStruct((B, T, WIDTH_A), jnp.float32)
    ospec = pl.BlockSpec((1, tb, WIDTH_A), lambda b, t: (b, t, 0))
    full = lambda shape: pl.BlockSpec(shape, lambda b, t: (0,) * len(shape))
    return pl.pallas_call(
        _rwkv_pre_kernel,
        out_shape=(out,) * 8,
        grid=(B, T // tb),
        in_specs=[pl.BlockSpec((1, tb, C_RWKV), lambda b, t: (b, t, 0)),
                  pl.BlockSpec((1, 8, C_RWKV), lambda b, t: (b, jnp.maximum(t * (tb // 8) - 1, 0), 0)),
                  pl.BlockSpec((1, 1, C_RWKV), lambda b, t: (b, 0, 0)),
                  full((1, C_RWKV)), full((5, WIDTH_A)), full((LORA_W + LORA_A, 2 * WIDTH_A)),
                  full((LORA_G, WIDTH_A)), full((2 * PAIR_W, PAIR_W))],
        out_specs=(ospec,) * 8,
        compiler_params=pltpu.CompilerParams(
            dimension_semantics=("arbitrary", "arbitrary"), vmem_limit_bytes=VMEM_LIMIT),
        name="rwkv7_pre",
    )(seg, seg, prev_row, p['shift_mu'].reshape(1, C_RWKV), vecs, wa_up,
      p['g_lora_up'].astype(jnp.bfloat16), _block_ones2())


SWA_BAND = WINDOW + CHUNK


def _swa_kernel(q_ref, k_ref, v_ref, kp_ref, vp_ref, bias_ref, sink_ref, o_ref, *, mask_start):
    n_chunks = q_ref.shape[1] // CHUNK
    k_all = jnp.concatenate([kp_ref[0], k_ref[0]], axis=0).astype(jnp.bfloat16)
    v_all = jnp.concatenate([vp_ref[0], v_ref[0]], axis=0).astype(jnp.bfloat16)
    first = pl.program_id(1) == 0
    key_chunk = lax.broadcasted_iota(jnp.int32, (GROUP_B * CHUNK, SWA_BAND), 1) // CHUNK
    for c in range(n_chunks):
        q_c = q_ref[0, c * CHUNK:(c + 1) * CHUNK, :].astype(jnp.bfloat16)
        k_c = k_all[c * CHUNK:c * CHUNK + SWA_BAND]
        v_c = v_all[c * CHUNK:c * CHUNK + SWA_BAND]
        dead = jnp.logical_and(first, key_chunk + (c - WIN_CHUNKS) < 0) if (mask_start and c < WIN_CHUNKS) else None
        kvs = range(N_KV_B)
        qg = [jnp.concatenate([q_c[:, (kv * GROUP_B + g) * HEAD_B:(kv * GROUP_B + g + 1) * HEAD_B]
                               for g in range(GROUP_B)], axis=0) for kv in kvs]
        s = [lax.dot_general(qg[kv], k_c[:, kv * HEAD_B:(kv + 1) * HEAD_B], (((1,), (1,)), ((), ())),
                             preferred_element_type=jnp.float32) * (HEAD_B ** -0.5) + bias_ref[kv] for kv in kvs]
        if dead is not None:
            s = [jnp.where(dead, NEG_INF, s[kv]) for kv in kvs]
        sink = [sink_ref[kv * GROUP_B * CHUNK:(kv + 1) * GROUP_B * CHUNK, 0:1] for kv in kvs]
        m = [jnp.maximum(jnp.max(s[kv], axis=-1, keepdims=True), sink[kv]) for kv in kvs]
        e = [jnp.exp(s[kv] - m[kv]) for kv in kvs]
        pr = [e[kv] / (jnp.sum(e[kv], axis=-1, keepdims=True) + jnp.exp(sink[kv] - m[kv])) for kv in kvs]
        og = [jnp.dot(pr[kv].astype(jnp.bfloat16), v_c[:, kv * HEAD_B:(kv + 1) * HEAD_B],
                      preferred_element_type=jnp.float32) for kv in kvs]
        o_ref[0, c * CHUNK:(c + 1) * CHUNK, :] = jnp.concatenate(
            [og[kv][g * CHUNK:(g + 1) * CHUNK] for kv in kvs for g in range(GROUP_B)], axis=1)


def swa_attention(seg_b, prev_k, prev_v, sinks, *, prev_is_seq, qb=512):
    B, T, _ = seg_b.shape
    qb = min(qb, T)
    slopes = 2.0 ** (-8.0 * jnp.arange(1, N_HEADS_B + 1, dtype=jnp.float32) / N_HEADS_B)
    dist = jnp.abs(jnp.arange(CHUNK)[:, None] - (jnp.arange(SWA_BAND) - WINDOW)[None, :]).astype(jnp.float32)
    bias = (-slopes[:, None, None] * dist).reshape(N_KV_B, GROUP_B * CHUNK, SWA_BAND)
    sink_tab = jnp.broadcast_to(jnp.repeat(sinks.astype(jnp.float32), CHUNK)[:, None], (N_HEADS_B * CHUNK, 128))
    kcol, vcol = WIDTH_B // KV_WIDTH_B, WIDTH_B // KV_WIDTH_B + 1
    if prev_is_seq:
        per = qb // WINDOW
        kp_spec = pl.BlockSpec((1, WINDOW, KV_WIDTH_B), lambda b, i: (b, jnp.maximum(i * per - 1, 0), kcol))
        vp_spec = pl.BlockSpec((1, WINDOW, KV_WIDTH_B), lambda b, i: (b, jnp.maximum(i * per - 1, 0), vcol))
    else:
        kp_spec = vp_spec = pl.BlockSpec((1, WINDOW, KV_WIDTH_B), lambda b, i: (b, 0, 0))
    return pl.pallas_call(
        functools.partial(_swa_kernel, mask_start=prev_is_seq),
        out_shape=jax.ShapeDtypeStruct((B, T, WIDTH_B), jnp.float32),
        grid=(B, T // qb),
        in_specs=[pl.BlockSpec((1, qb, WIDTH_B), lambda b, i: (b, i, 0)),
                  pl.BlockSpec((1, qb, KV_WIDTH_B), lambda b, i: (b, i, kcol)),
                  pl.BlockSpec((1, qb, KV_WIDTH_B), lambda b, i: (b, i, vcol)),
                  kp_spec, vp_spec,
                  pl.BlockSpec((N_KV_B, GROUP_B * CHUNK, SWA_BAND), lambda b, i: (0, 0, 0)),
                  pl.BlockSpec((N_HEADS_B * CHUNK, 128), lambda b, i: (0, 0))],
        out_specs=pl.BlockSpec((1, qb, WIDTH_B), lambda b, i: (b, i, 0)),
        compiler_params=pltpu.CompilerParams(
            dimension_semantics=("arbitrary", "arbitrary"), vmem_limit_bytes=VMEM_LIMIT),
        name="swa_attention",
    )(seg_b, seg_b, seg_b, prev_k, prev_v, bias, sink_tab)


def _layer_norm_rows(h, g, b, eps=1e-5):
    mu = jnp.mean(h, axis=-1, keepdims=True)
    d = h - mu
    var = jnp.mean(d * d, axis=-1, keepdims=True)
    return d * lax.rsqrt(var + eps) * g + b


def _const_spec(shape):
    return pl.BlockSpec(shape, lambda *_: (0,) * len(shape), pipeline_mode=pl.Buffered(1))


def _merge_kernel(x_ref, y_ref, bonus_ref, g_ref, ob_ref, gates_ref, lnx_ref, ln1_ref, ones_ref,
                  pa_ref, pb_ref, wout_ref, o_ref):
    ones2 = ones_ref[...]
    y = y_ref[...]
    mean = _head_sums(y, ones2) * (1.0 / HEAD_A)
    d = y - mean
    var = _head_sums(d * d, ones2) * (1.0 / HEAD_A)
    yn = d * lax.rsqrt(var + GN_EPS) * lnx_ref[0:1, :] + lnx_ref[1:2, :]
    o_a = ((yn + bonus_ref[...]) * g_ref[...]).astype(jnp.bfloat16)
    br_a = jnp.dot(o_a, pa_ref[...], preferred_element_type=jnp.float32)
    br_b = jnp.dot(ob_ref[...].astype(jnp.bfloat16), pb_ref[...], preferred_element_type=jnp.float32)
    gates = jax.nn.sigmoid(gates_ref[...])
    merged = gates[:, :D_MODEL] * br_a + gates[:, D_MODEL:] * br_b
    h = ALPHA * x_ref[...] + jnp.dot(merged.astype(jnp.bfloat16), wout_ref[...], preferred_element_type=jnp.float32)
    o_ref[...] = _layer_norm_rows(h, ln1_ref[0:1, :], ln1_ref[1:2, :])


def branch_merge(x, y, bonus, g, o_b, gates, p, tm=256):
    n = x.shape[0]
    tm = _pick(n, (tm, 128, 64))
    rows = lambda w: pl.BlockSpec((tm, w), lambda i: (i, 0))
    return pl.pallas_call(
        _merge_kernel,
        out_shape=jax.ShapeDtypeStruct((n, D_MODEL), jnp.float32),
        grid=(n // tm,),
        in_specs=[rows(D_MODEL), rows(WIDTH_A), rows(WIDTH_A), rows(WIDTH_A), rows(WIDTH_B), rows(C_GATE),
                  _const_spec((2, WIDTH_A)), _const_spec((2, D_MODEL)), _const_spec((2 * PAIR_W, PAIR_W)),
                  _const_spec((WIDTH_A, D_MODEL)), _const_spec((WIDTH_B, D_MODEL)), _const_spec((D_MODEL, D_MODEL))],
        out_specs=rows(D_MODEL),
        compiler_params=pltpu.CompilerParams(dimension_semantics=("arbitrary",), vmem_limit_bytes=VMEM_LIMIT),
        name="branch_merge",
    )(x, y, bonus, g, o_b, gates, jnp.stack([p['lnx_g'], p['lnx_b']]), jnp.stack([p['ln1_g'], p['ln1_b']]),
      _block_ones2(), p['w_branch_a16'], p['w_branch_b16'], p['w_out16'])


def _mem_kernel(x_ref, mk_ref, mv_ref, ln2_ref, wq_ref, wo_ref, pwq_ref, o_ref, q_ref):
    x = x_ref[0]
    qm = jnp.dot(x.astype(jnp.bfloat16), wq_ref[...], preferred_element_type=jnp.float32).astype(jnp.bfloat16)
    mk = mk_ref[0].astype(jnp.bfloat16)
    mv = mv_ref[0].astype(jnp.bfloat16)
    outs = []
    for h in range(N_HEADS_M):
        cols = slice(h * HEAD_M, (h + 1) * HEAD_M)
        s = lax.dot_general(qm[:, cols], mk[:, cols], (((1,), (1,)), ((), ())),
                            preferred_element_type=jnp.float32) * (HEAD_M ** -0.5)
        e = jnp.exp(s - jnp.max(s, axis=-1, keepdims=True))
        pr = e / jnp.sum(e, axis=-1, keepdims=True)
        outs.append(jnp.dot(pr.astype(jnp.bfloat16), mv[:, cols], preferred_element_type=jnp.float32))
    o = jnp.concatenate(outs, axis=1).astype(jnp.bfloat16)
    h2 = ALPHA * x + jnp.dot(o, wo_ref[...], preferred_element_type=jnp.float32)
    x2 = _layer_norm_rows(h2, ln2_ref[0:1, :], ln2_ref[1:2, :])
    o_ref[0] = x2
    q_ref[0] = jnp.dot(x2.astype(jnp.bfloat16), pwq_ref[...], preferred_element_type=jnp.float32)


def mem_block(x, mk, mv, p, tm=256):
    B, T, _ = x.shape
    tm = _pick(T, (tm, 128, 64))
    wm = N_HEADS_M * HEAD_M
    rows = lambda w: pl.BlockSpec((1, tm, w), lambda b, i: (b, i, 0))
    mem = pl.BlockSpec((1, N_MEM, wm), lambda b, i: (b, 0, 0))
    out = jax.ShapeDtypeStruct((B, T, D_MODEL), jnp.float32)
    return pl.pallas_call(
        _mem_kernel,
        out_shape=(out, jax.ShapeDtypeStruct((B, T, PEER_HEADS * D_KEY), jnp.float32)),
        grid=(B, T // tm),
        in_specs=[rows(D_MODEL), mem, mem, _const_spec((2, D_MODEL)), _const_spec((D_MODEL, wm)),
                  _const_spec((wm, D_MODEL)), _const_spec((D_MODEL, PEER_HEADS * D_KEY))],
        out_specs=(rows(D_MODEL), rows(PEER_HEADS * D_KEY)),
        compiler_params=pltpu.CompilerParams(
            dimension_semantics=("arbitrary", "arbitrary"), vmem_limit_bytes=VMEM_LIMIT),
        name="mem_block",
    )(x, mk, mv, jnp.stack([p['ln2_g'], p['ln2_b']]), p['wq_mem16'], p['wo_mem16'], p['peer_wq16'])


ROUTE_TQ = 256


def _top_values(s, k, exact):
    n_rows = s.shape[0]
    iota = lax.broadcasted_iota(jnp.int32, s.shape, 0).astype(jnp.float32)
    rank = jnp.full(s.shape, float(k), jnp.float32)
    masked0 = jnp.sum((s == -jnp.inf).astype(jnp.float32), axis=0, keepdims=True)
    out = []
    for step in range(k):
        m = jnp.max(s, axis=0, keepdims=True)
        if exact:
            first = jnp.min(jnp.where(s == m, iota, float(n_rows)), axis=0, keepdims=True)
            taken = iota == first
        else:
            taken = s == m
        s = jnp.where(taken, -jnp.inf, s)
        rank = jnp.where(taken, float(step), rank)
        out.append(m)
    surplus = jnp.sum((s == -jnp.inf).astype(jnp.float32), axis=0, keepdims=True) - masked0 - float(k)
    return out, rank, surplus


def _peer_route_kernel(q_ref, keys_ref, n1_ref, c1_ref, rank2_ref, e2_ref):
    refs = (q_ref, keys_ref, n1_ref, c1_ref, rank2_ref, e2_ref)
    for h in range(PEER_HEADS):
        surplus = _peer_route_pass(*refs, exact=False, heads=(h,))

        @pl.when(jnp.max(surplus) > 0.0)
        def _(h=h):
            _peer_route_pass(*refs, exact=True, heads=(h,))


def _peer_route_pass(q_ref, keys_ref, n1_ref, c1_ref, rank2_ref, e2_ref, *, exact, heads):
    half = D_KEY // 2
    surplus = None
    for h in heads:
        tops, scores = [], []
        for p in range(2):
            c0 = (2 * h + p) * half
            qs = q_ref[:, c0:c0 + half].astype(jnp.bfloat16)
            s = lax.dot_general(keys_ref[2 * h + p], qs, (((1,), (1,)), ((), ())),
                                preferred_element_type=jnp.float32)
            scores.append(s)
            tops.append(_top_values(s, TOPK, exact))
        (t1, _, sur1), (t2, rank2, sur2) = tops
        t2all = jnp.concatenate(t2, axis=0)
        t1all = jnp.concatenate(t1, axis=0)
        rank8 = lax.broadcasted_iota(jnp.int32, (8, t1all.shape[1]), 0)
        cand_rows = [t1all + t2[0], t1all[:8] + t2[1]]
        for b in range(2, 8):
            cand_rows.append(jnp.where(rank8 < TOPK // (b + 1), t1all[:8] + t2[b], -jnp.inf))
        cand_rows.append(t1[0] + t2all[8:])
        sc, _, sur3 = _top_values(jnp.concatenate(cand_rows, axis=0), TOPK, exact)
        sur = sur1 + sur2 + sur3
        surplus = sur if surplus is None else surplus + sur
        z = jnp.zeros_like(sc[0])
        for kq in range(TOPK):
            z = z + jnp.exp(sc[kq] - sc[0])
        theta = sc[TOPK - 1]
        n1 = jnp.zeros_like(scores[0])
        for a in range(TOPK):
            n_a = jnp.sum((t1[a] + t2all >= theta).astype(jnp.float32), axis=0, keepdims=True)
            n1 = jnp.where(scores[0] == t1[a], n_a, n1)
        n1_ref[h] = n1
        rank2_ref[h] = rank2.astype(jnp.bfloat16)
        c1_ref[h] = jnp.exp(scores[0] - t1[0]) / z
        e2_ref[h] = jnp.exp(scores[1] - t2[0]).astype(jnp.bfloat16)
    return surplus


def peer_route(q, keys16):
    n = q.shape[0]
    tq = _pick(n, (ROUTE_TQ, 128))
    big = jax.ShapeDtypeStruct((PEER_HEADS, N_KEYS, n), jnp.float32)
    big16 = jax.ShapeDtypeStruct((PEER_HEADS, N_KEYS, n), jnp.bfloat16)
    bspec = pl.BlockSpec((PEER_HEADS, N_KEYS, tq), lambda i: (0, 0, i))
    return pl.pallas_call(
        _peer_route_kernel,
        out_shape=(big, big, big16, big16),
        grid=(n // tq,),
        in_specs=[pl.BlockSpec((tq, PEER_HEADS * D_KEY), lambda i: (i, 0)),
                  pl.BlockSpec((2 * PEER_HEADS, N_KEYS, D_KEY // 2), lambda i: (0, 0, 0))],
        out_specs=(bspec, bspec, bspec, bspec),
        compiler_params=pltpu.CompilerParams(
            dimension_semantics=("arbitrary",), vmem_limit_bytes=VMEM_LIMIT),
        name="peer_route",
    )(q, keys16)


PEER_TM = 512
PEER_ROWS = 8
PEER_TE = PEER_ROWS * N_KEYS


def _gelu(x):
    return 0.5 * x * (1.0 + lax.erf(x * (2.0 ** -0.5)))


def _peer_mix_kernel(x_ref, u_ref, vt_ref, n1_ref, c1_ref, rank2_ref, e2_ref, ln3_ref, o_ref,
                     x16_ref, h_ref, acc_ref):
    j = pl.program_id(1)

    @pl.when(j == 0)
    def _():
        x16_ref[...] = x_ref[...].T.astype(jnp.bfloat16)
        acc_ref[...] = jnp.zeros_like(acc_ref)

    half_te = PEER_TE // 2
    a_halves = [jnp.dot(u_ref[k * half_te:(k + 1) * half_te, :], x16_ref[...],
                        preferred_element_type=jnp.float32) for k in range(2)]

    def sublane_bcast16(row):
        return jnp.broadcast_to(row, (N_KEYS, row.shape[1])).astype(jnp.bfloat16)

    for r in range(PEER_ROWS):
        rows = slice(r * N_KEYS, (r + 1) * N_KEYS)
        for c0 in range(0, x_ref.shape[0], 128):
            cols = slice(c0, c0 + 128)
            gate = None
            for h in range(PEER_HEADS):
                keep = rank2_ref[h, :, cols] < sublane_bcast16(n1_ref[h, r:r + 1, cols])
                w = jnp.where(keep, e2_ref[h, :, cols], jnp.zeros((), jnp.bfloat16))
                w = w * sublane_bcast16(c1_ref[h, r:r + 1, cols])
                gate = w if gate is None else gate + w
            a_rows = a_halves[r // (PEER_ROWS // 2)][(r % (PEER_ROWS // 2)) * N_KEYS:(r % (PEER_ROWS // 2) + 1) * N_KEYS]
            h_ref[rows, cols] = gate * _gelu(a_rows[:, cols]).astype(jnp.bfloat16)
    acc_ref[...] += jnp.dot(vt_ref[0], h_ref[...], preferred_element_type=jnp.float32)

    @pl.when(j == pl.num_programs(1) - 1)
    def _():
        o_ref[...] = _layer_norm_rows(ALPHA * x_ref[...] + acc_ref[...].T, ln3_ref[0:1, :], ln3_ref[1:2, :])


def peer_mix(x, u16, vt16, n1, c1, rank2, e2, ln3):
    n, d = x.shape
    tm = _pick(n, (PEER_TM, 256, 128))
    n_exp = u16.shape[0]
    row_spec = pl.BlockSpec((PEER_HEADS, PEER_ROWS, tm), lambda i, j: (0, j, i))
    all_spec = pl.BlockSpec((PEER_HEADS, N_KEYS, tm), lambda i, j: (0, 0, i), pipeline_mode=pl.Buffered(1))
    return pl.pallas_call(
        _peer_mix_kernel,
        out_shape=jax.ShapeDtypeStruct((n, d), jnp.float32),
        grid=(n // tm, n_exp // PEER_TE),
        in_specs=[pl.BlockSpec((tm, d), lambda i, j: (i, 0), pipeline_mode=pl.Buffered(1)),
                  pl.BlockSpec((PEER_TE, d), lambda i, j: (j, 0)),
                  pl.BlockSpec((1, d, PEER_TE), lambda i, j: (j, 0, 0)),
                  row_spec, row_spec, all_spec, all_spec, _const_spec((2, d))],
        out_specs=pl.BlockSpec((tm, d), lambda i, j: (i, 0)),
        scratch_shapes=[pltpu.VMEM((d, tm), jnp.bfloat16), pltpu.VMEM((PEER_TE, tm), jnp.bfloat16),
                        pltpu.VMEM((d, tm), jnp.float32)],
        compiler_params=pltpu.CompilerParams(
            dimension_semantics=("arbitrary", "arbitrary"), vmem_limit_bytes=VMEM_LIMIT),
        name="peer_mix",
    )(x, u16, vt16, n1, c1, rank2, e2, ln3)


def peer_block(x, q, p):
    stats = peer_route(q, p['peer_keys16'])
    return peer_mix(x, p['peer_u16'], p['peer_vt16'], *stats, jnp.stack([p['ln3_g'], p['ln3_b']]))


def trunk_layer(x, p, mem_k, mem_v, rwkv_state, shift_row, swa_k_cache, swa_v_cache):
    B, T, _ = x.shape
    n = B * T
    x2d = x.reshape(n, D_MODEL)
    seg_a = matmul(x2d, p['w_in_a16']).reshape(B, T, C_RWKV)
    seg_b = matmul(x2d, p['w_in_b16']).reshape(B, T, C_SWA)
    gates = matmul(x2d, p['w_in_g16'])
    r, w, k, v, kk, a, g, bonus = rwkv7_pre(seg_a, shift_row.astype(jnp.float32), p)
    y, rwkv_new = rwkv7_chunked(r, w, k, v, kk, a, rwkv_state)
    shift_new = seg_a[:, -1:]
    k_new = seg_b[:, :, WIDTH_B:WIDTH_B + KV_WIDTH_B].reshape(B, T, N_KV_B, HEAD_B)
    v_new = seg_b[:, :, WIDTH_B + KV_WIDTH_B:].reshape(B, T, N_KV_B, HEAD_B)
    if swa_k_cache is None:
        o_b = swa_attention(seg_b, seg_b, seg_b, p['attn_sinks'], prev_is_seq=True)
        swa_k_new, swa_v_new = k_new[:, -WINDOW:], v_new[:, -WINDOW:]
    else:
        o_b = swa_attention(seg_b, swa_k_cache.reshape(B, WINDOW, KV_WIDTH_B),
                            swa_v_cache.reshape(B, WINDOW, KV_WIDTH_B), p['attn_sinks'], prev_is_seq=False)
        swa_k_new = jnp.concatenate([swa_k_cache, k_new], axis=1)[:, -WINDOW:]
        swa_v_new = jnp.concatenate([swa_v_cache, v_new], axis=1)[:, -WINDOW:]
    flat = lambda t: t.reshape(n, t.shape[-1])
    x1 = branch_merge(x2d, flat(y), flat(bonus), flat(g), flat(o_b), gates, p)
    wm = N_HEADS_M * HEAD_M
    x2, q = mem_block(x1.reshape(B, T, D_MODEL), mem_k.reshape(-1, N_MEM, wm), mem_v.reshape(-1, N_MEM, wm), p)
    x3 = peer_block(flat(x2), flat(q), p)
    return x3.reshape(B, T, D_MODEL), rwkv_new, shift_new, swa_k_new, swa_v_new


_MM_WEIGHTS = ('w_branch_a', 'w_branch_b', 'w_out', 'wq_mem', 'wk_mem', 'wv_mem', 'wo_mem', 'peer_wq')


def kernel(x_prompt, x_sample, state_rwkv, state_shift, cache_swa_k, cache_swa_v, cache_mem_k, cache_mem_v, mem_prompt, w_in, shift_mu, w0, w_lora_up, a0, a_lora_up, g_lora_up, k_k, k_a, r_k, lnx_g, lnx_b, attn_sinks, w_branch_a, w_branch_b, w_out, ln1_g, ln1_b, wq_mem, wk_mem, wv_mem, wo_mem, ln2_g, ln2_b, peer_wq, peer_sub_keys, peer_u, peer_v, ln3_g, ln3_b):
    params = {
        'w_in': w_in, 'shift_mu': shift_mu, 'w0': w0, 'w_lora_up': w_lora_up, 'a0': a0,
        'a_lora_up': a_lora_up, 'g_lora_up': g_lora_up, 'k_k': k_k, 'k_a': k_a, 'r_k': r_k,
        'lnx_g': lnx_g, 'lnx_b': lnx_b, 'attn_sinks': attn_sinks, 'w_branch_a': w_branch_a,
        'w_branch_b': w_branch_b, 'w_out': w_out, 'ln1_g': ln1_g, 'ln1_b': ln1_b, 'wq_mem': wq_mem,
        'wk_mem': wk_mem, 'wv_mem': wv_mem, 'wo_mem': wo_mem, 'ln2_g': ln2_g, 'ln2_b': ln2_b,
        'peer_wq': peer_wq, 'peer_sub_keys': peer_sub_keys, 'peer_u': peer_u, 'peer_v': peer_v,
        'ln3_g': ln3_g, 'ln3_b': ln3_b,
    }
    B = x_prompt.shape[0]
    rwkv0 = jnp.zeros((B, N_HEADS_A, HEAD_A, HEAD_A), jnp.float32)
    shift0 = jnp.zeros((B, 1, C_RWKV), x_prompt.dtype)
    xp, xs = x_prompt, x_sample
    p_rw, p_sh, p_k, p_v, p_mk, p_mv = [], [], [], [], [], []
    s_rw, s_sh, s_k, s_v = [], [], [], []
    for l in range(DEPTH):
        p = {name: arr[l] for name, arr in params.items()}
        for name in _MM_WEIGHTS:
            p[name + '16'] = p[name].astype(jnp.bfloat16)
        w_in16 = p['w_in'].astype(jnp.bfloat16)
        p['w_in_a16'] = w_in16[:, :C_RWKV]
        p['w_in_b16'] = w_in16[:, C_RWKV:C_RWKV + C_SWA]
        p['w_in_g16'] = w_in16[:, C_RWKV + C_SWA:]
        p['peer_keys16'] = p['peer_sub_keys'].reshape(2 * PEER_HEADS, N_KEYS, D_KEY // 2).astype(jnp.bfloat16)
        p['peer_u16'] = p['peer_u'].astype(jnp.bfloat16)
        p['peer_vt16'] = (p['peer_v'].astype(jnp.bfloat16)
                          .reshape(p['peer_v'].shape[0] // PEER_TE, PEER_TE, D_MODEL).transpose(0, 2, 1))
        mk = mm(mem_prompt, p['wk_mem16']).reshape(B, N_MEM, N_HEADS_M, HEAD_M)
        mv = mm(mem_prompt, p['wv_mem16']).reshape(B, N_MEM, N_HEADS_M, HEAD_M)
        xp, rw, sh, kn, vn = trunk_layer(xp, p, mk, mv, rwkv0, shift0, None, None)
        p_rw.append(rw); p_sh.append(sh); p_k.append(kn); p_v.append(vn); p_mk.append(mk); p_mv.append(mv)
        xs, rw, sh, kn, vn = trunk_layer(xs, p, cache_mem_k[l], cache_mem_v[l], state_rwkv[l], state_shift[l],
                                         cache_swa_k[l], cache_swa_v[l])
        s_rw.append(rw); s_sh.append(sh); s_k.append(kn); s_v.append(vn)
    return (xp, xs,
            jnp.stack(p_rw), jnp.stack(p_sh), jnp.stack(p_k), jnp.stack(p_v), jnp.stack(p_mk), jnp.stack(p_mv),
            jnp.stack(s_rw), jnp.stack(s_sh), jnp.stack(s_k), jnp.stack(s_v))
```

```python
import functools

import jax
import jax.numpy as jnp
from jax import lax
from jax.experimental import pallas as pl
from jax.experimental.pallas import tpu as pltpu

D_MODEL = 2048
DEPTH = 2
CHUNK = 64
HEAD_A = 64
N_HEADS_A = 16
WIDTH_A = N_HEADS_A * HEAD_A
LORA_W = 64
LORA_A = 64
LORA_G = 128
GN_EPS = 64e-5
C_RWKV = 3 * WIDTH_A + LORA_W + LORA_A + LORA_G
HEAD_B = 64
N_HEADS_B = 16
N_KV_B = 4
GROUP_B = N_HEADS_B // N_KV_B
WIDTH_B = N_HEADS_B * HEAD_B
KV_WIDTH_B = N_KV_B * HEAD_B
WINDOW = 128
WIN_CHUNKS = WINDOW // CHUNK
C_SWA = WIDTH_B + 2 * KV_WIDTH_B
C_GATE = 2 * D_MODEL
N_MEM = 256
N_HEADS_M = 4
HEAD_M = 128
N_KEYS = 128
PEER_HEADS = 8
D_KEY = 256
TOPK = 16
ALPHA = (2.0 * DEPTH) ** 0.25
NEG_INF = -1e30

VMEM_LIMIT = 56 * 1024 * 1024


def _matmul_kernel(a_ref, b_ref, o_ref, a16_ref):
    @pl.when(pl.program_id(1) == 0)
    def _():
        a16_ref[...] = a_ref[...].astype(jnp.bfloat16)

    o_ref[...] = jnp.dot(a16_ref[...], b_ref[...], preferred_element_type=jnp.float32)


def _pick(n, cands):
    for c in cands:
        if n % c == 0:
            return c
    return n


def matmul(a, b16):
    m, k = a.shape
    n = b16.shape[1]
    tm = _pick(m, (512, 256, 128))
    tn = _pick(n, (2048, 1664, 1536, 1024, 512, 256, 128))
    return pl.pallas_call(
        _matmul_kernel,
        out_shape=jax.ShapeDtypeStruct((m, n), jnp.float32),
        grid=(m // tm, n // tn),
        in_specs=[pl.BlockSpec((tm, k), lambda i, j: (i, 0)),
                  pl.BlockSpec((k, tn), lambda i, j: (0, j))],
        out_specs=pl.BlockSpec((tm, tn), lambda i, j: (i, j)),
        scratch_shapes=[pltpu.VMEM((tm, k), jnp.bfloat16)],
        compiler_params=pltpu.CompilerParams(
            dimension_semantics=("arbitrary", "arbitrary"),
            vmem_limit_bytes=VMEM_LIMIT),
        name="matmul",
    )(a, b16)


def mm(x, w16):
    lead = x.shape[:-1]
    return matmul(x.reshape(-1, x.shape[-1]), w16).reshape(*lead, w16.shape[1])


N_PAIR = N_HEADS_A // 2
PAIR_W = 2 * HEAD_A


def _split2(x):
    hi = x.astype(jnp.bfloat16)
    lo = (x - hi.astype(jnp.float32)).astype(jnp.bfloat16)
    return jnp.concatenate([hi, lo], axis=1)


def _hl(x):
    hi = x.astype(jnp.bfloat16)
    return hi, (x - hi.astype(jnp.float32)).astype(jnp.bfloat16)


def _dot3(a, b):
    ah, al = _hl(a)
    bh, bl = _hl(b)
    return jnp.dot(jnp.concatenate([ah, ah, al], axis=1), jnp.concatenate([bh, bl, bh], axis=0),
                   preferred_element_type=jnp.float32)


def _dot3_nt(a, b):
    ah, al = _hl(a)
    bh, bl = _hl(b)
    return lax.dot_general(jnp.concatenate([ah, ah, al], axis=1), jnp.concatenate([bh, bl, bh], axis=1),
                           (((1,), (1,)), ((), ())), preferred_element_type=jnp.float32)


def _rwkv_chunk_kernel(r_ref, w_ref, k_ref, v_ref, kk_ref, a_ref, s0_ref, ltri_ref, ones_ref, msk_ref,
                       y_ref, sT_ref, p_ref, *, n_chunks):
    tblk = pl.program_id(1)

    @pl.when(tblk == 0)
    def _():
        p_ref[...] = s0_ref[...]

    eye2, m0, m1, strict, incl = (msk_ref[i] for i in range(5))
    bd = lambda y: jnp.concatenate([y * m0, y * m1], axis=0)

    def split3(x, axis):
        t1 = x.astype(jnp.bfloat16)
        d = x - t1.astype(jnp.float32)
        t2 = d.astype(jnp.bfloat16)
        t3 = (d - t2.astype(jnp.float32)).astype(jnp.bfloat16)
        return jnp.concatenate([t1, t2, t3], axis=axis)

    def chunk(idx, carry):
        b, c = idx // n_chunks, idx % n_chunks
        rows = pl.ds(pl.multiple_of(c * CHUNK, CHUNK), CHUNK)
        pairs = range(N_PAIR)
        sls = [pl.ds(p * PAIR_W, PAIR_W) for p in pairs]
        lw = [w_ref[b, rows,sl] for sl in sls]
        cum = [jnp.dot(ltri_ref[...], split3(lw[p], 0), preferred_element_type=jnp.float32) for p in pairs]
        g = [jnp.exp(cum[p]) for p in pairs]
        ginv = [jnp.exp(-cum[p]) for p in pairs]
        kk = [kk_ref[b, rows,sl] for sl in sls]
        kh = [kk[p] * jnp.exp(cum[p] - lw[p]) for p in pairs]
        bh = [kk[p] * a_ref[b, rows,sls[p]] * ginv[p] for p in pairs]
        kf = [k_ref[b, rows,sls[p]] * ginv[p] for p in pairs]
        rh = [r_ref[b, rows,sls[p]] * g[p] for p in pairs]
        v = [v_ref[b, rows,sl] for sl in sls]
        g_last = [g[p][CHUNK - 1:CHUNK, :] for p in pairs]
        kr = [jnp.concatenate([kh[p], rh[p]], axis=0) for p in pairs]
        gram = [_dot3_nt(kr[p], jnp.concatenate([bd(bh[p]), bd(kf[p])], axis=0)) for p in pairs]
        a_b = [gram[p][:CHUNK, :PAIR_W] * strict for p in pairs]
        a_k = [gram[p][:CHUNK, PAIR_W:] * strict for p in pairs]
        a_r = [jnp.concatenate([gram[p][CHUNK:, :PAIR_W] * incl, gram[p][CHUNK:, PAIR_W:] * incl], axis=1)
               for p in pairs]
        t_inv = [eye2 - a_b[p] * msk_ref[5] for p in pairs]
        for lvl in range(1, 6):
            half = [_dot3(t_inv[p], bd(a_b[p] * msk_ref[5 + lvl])) for p in pairs]
            t_inv = [t_inv[p] - _dot3(half[p], bd(t_inv[p])) for p in pairs]
        akv = [_dot3(a_k[p], bd(v[p])) for p in pairs]
        g_col = [jnp.dot(split3(eye2 * g_last[p], 1), ones_ref[...], preferred_element_type=jnp.float32)
                 for p in pairs]
        xt = [jnp.concatenate([bh[p] * g_last[p], kf[p] * g_last[p]], axis=0).T for p in pairs]
        p0 = [p_ref[b, p] for p in pairs]
        zy = [_dot3(kr[p], bd(p0[p])) for p in pairs]
        u = [-_dot3(t_inv[p], bd(zy[p][:CHUNK] + akv[p])) for p in pairs]
        y = [zy[p][CHUNK:] + _dot3(a_r[p], jnp.concatenate([bd(u[p]), bd(v[p])], axis=0)) for p in pairs]
        delta = [_dot3(xt[p], jnp.concatenate([u[p], v[p]], axis=0)) for p in pairs]
        for p in pairs:
            p_ref[b, p] = g_col[p] * p0[p] + delta[p][:CHUNK] * m0 + delta[p][CHUNK:] * m1
            y_ref[b, rows, sls[p]] = y[p]
        return carry

    lax.fori_loop(0, r_ref.shape[0] * n_chunks, chunk, 0)

    @pl.when(tblk == pl.num_programs(1) - 1)
    def _():
        sT_ref[...] = p_ref[...]


def rwkv7_chunked(r, logw, k, v, kk, a, state0, tb=256):
    B, T, _ = r.shape
    tb = min(tb, T)
    s0 = state0.astype(jnp.float32).reshape(B, N_PAIR, 2, HEAD_A, HEAD_A)
    s0 = s0.transpose(0, 1, 4, 2, 3).reshape(B, N_PAIR, HEAD_A, PAIR_W)
    t_i = jnp.arange(CHUNK)[:, None]
    s_i = (jnp.arange(PAIR_W) % HEAD_A)[None, :]
    lane_head = (jnp.arange(PAIR_W) // HEAD_A)[None, :]
    ones_row = jnp.ones((CHUNK, 1), jnp.int32)
    masks = [t_i == s_i, (lane_head == 0) * ones_row, (lane_head == 1) * ones_row, s_i < t_i, s_i <= t_i]
    for m in (1, 2, 4, 8, 16, 32):
        masks.append((t_i // (2 * m) == s_i // (2 * m)) & (t_i % (2 * m) >= m) & (s_i % (2 * m) < m))
    masks = jnp.stack([mk.astype(jnp.float32) for mk in masks])
    ltri = jnp.tile((jnp.arange(CHUNK)[None, :] <= jnp.arange(CHUNK)[:, None]), (1, 3)).astype(jnp.bfloat16)
    ones3 = jnp.tile((lane_head.T == lane_head), (3, 1)).astype(jnp.bfloat16)
    bb = B if T == tb else 1
    seq = pl.BlockSpec((bb, tb, WIDTH_A), lambda b, t: (b, t, 0))
    st = pl.BlockSpec((bb, N_PAIR, HEAD_A, PAIR_W), lambda b, t: (b, 0, 0, 0))
    y, sT = pl.pallas_call(
        functools.partial(_rwkv_chunk_kernel, n_chunks=tb // CHUNK),
        out_shape=(jax.ShapeDtypeStruct((B, T, WIDTH_A), jnp.float32),
                   jax.ShapeDtypeStruct((B, N_PAIR, HEAD_A, PAIR_W), jnp.float32)),
        grid=(B // bb, T // tb),
        in_specs=[seq] * 6 + [st, _const_spec((CHUNK, 3 * CHUNK)), _const_spec((3 * PAIR_W, PAIR_W)),
                              _const_spec((11, CHUNK, PAIR_W))],
        out_specs=(seq, st),
        scratch_shapes=[pltpu.VMEM((bb, N_PAIR, HEAD_A, PAIR_W), jnp.float32)],
        compiler_params=pltpu.CompilerParams(
            dimension_semantics=("arbitrary", "arbitrary"), vmem_limit_bytes=VMEM_LIMIT),
        name="rwkv7_chunked",
    )(r, logw, k, v, kk, a, s0, ltri, ones3, masks)
    sT = sT.reshape(B, N_PAIR, HEAD_A, 2, HEAD_A).transpose(0, 1, 3, 4, 2)
    return y, sT.reshape(B, N_HEADS_A, HEAD_A, HEAD_A)


def _head_sums(x, ones2):
    tiles = [jnp.dot(_split2(x[:, c:c + PAIR_W]), ones2, preferred_element_type=jnp.float32)
             for c in range(0, WIDTH_A, PAIR_W)]
    return jnp.concatenate(tiles, axis=1)


def _softplus(z):
    return jnp.maximum(z, 0.0) + jnp.log1p(jnp.exp(-jnp.abs(z)))


def _rwkv_pre_kernel(seg_ref, prev_ref, shift_ref, mu_ref, vec_ref, wa_ref, gup_ref, ones_ref,
                     r_ref, w_ref, k_ref, v_ref, kk_ref, a_ref, g_ref, bonus_ref):
    seg = seg_ref[0]
    tb = seg.shape[0]
    before = jnp.where(pl.program_id(1) == 0, shift_ref[0], prev_ref[0, 7:8, :])
    row = lax.broadcasted_iota(jnp.int32, seg.shape, 0)
    shifted = jnp.where(row == 0, before, pltpu.roll(seg, 1, axis=0))
    xm = seg + mu_ref[...] * (shifted - seg)
    r = xm[:, :WIDTH_A]
    k = xm[:, WIDTH_A:2 * WIDTH_A]
    v = xm[:, 2 * WIDTH_A:3 * WIDTH_A]
    wa = xm[:, 3 * WIDTH_A:3 * WIDTH_A + LORA_W + LORA_A]
    gl = xm[:, 3 * WIDTH_A + LORA_W + LORA_A:]
    lane = lax.broadcasted_iota(jnp.int32, wa.shape, 1)
    wa = jnp.where(lane < LORA_W, jnp.tanh(wa), wa).astype(jnp.bfloat16)
    lora = jnp.dot(wa, wa_ref[...], preferred_element_type=jnp.float32)
    w0, a0, k_k, k_a, r_k = (vec_ref[i:i + 1, :] for i in range(5))
    w_log = -_softplus(-(w0 + lora[:, :WIDTH_A])) - 0.5
    log_decay = -jnp.exp(w_log)
    a = jax.nn.sigmoid(a0 + lora[:, WIDTH_A:])
    g = jnp.dot(jax.nn.sigmoid(gl).astype(jnp.bfloat16), gup_ref[...], preferred_element_type=jnp.float32)
    ones2 = ones_ref[...]
    kk = k * k_k
    kk = kk * lax.rsqrt(jnp.maximum(_head_sums(kk * kk, ones2), 1e-24))
    kf = k * (1.0 + (a - 1.0) * k_a)
    r_ref[0], w_ref[0], k_ref[0], v_ref[0], kk_ref[0], a_ref[0], g_ref[0] = r, log_decay, kf, v, kk, a, g
    bonus_ref[0] = _head_sums(r * kf * r_k, ones2) * v


def _block_ones2():
    lane_head = jnp.arange(PAIR_W) // HEAD_A
    return (jnp.tile(lane_head, 2)[:, None] == lane_head[None, :]).astype(jnp.bfloat16)


def rwkv7_pre(seg, prev_row, p, tb=256):
    B, T, _ = seg.shape
    tb = min(tb, T)
    zeros = jnp.zeros((LORA_W, WIDTH_A), jnp.float32)
    wa_up = jnp.concatenate([jnp.concatenate([p['w_lora_up'], zeros], 1),
                             jnp.concatenate([zeros, p['a_lora_up']], 1)], 0).astype(jnp.bfloat16)
    vecs = jnp.stack([p['w0'], p['a0'], p['k_k'], p['k_a'], p['r_k'].reshape(WIDTH_A)])
    out = jax.ShapeDtypeStruct((B, T, WIDTH_A), jnp.float32)
    ospec = pl.BlockSpec((1, tb, WIDTH_A), lambda b, t: (b, t, 0))
    full = lambda shape: pl.BlockSpec(shape, lambda b, t: (0,) * len(shape))
    return pl.pallas_call(
        _rwkv_pre_kernel,
        out_shape=(out,) * 8,
        grid=(B, T // tb),
        in_specs=[pl.BlockSpec((1, tb, C_RWKV), lambda b, t: (b, t, 0)),
                  pl.BlockSpec((1, 8, C_RWKV), lambda b, t: (b, jnp.maximum(t * (tb // 8) - 1, 0), 0)),
                  pl.BlockSpec((1, 1, C_RWKV), lambda b, t: (b, 0, 0)),
                  full((1, C_RWKV)), full((5, WIDTH_A)), full((LORA_W + LORA_A, 2 * WIDTH_A)),
                  full((LORA_G, WIDTH_A)), full((2 * PAIR_W, PAIR_W))],
        out_specs=(ospec,) * 8,
        compiler_params=pltpu.CompilerParams(
            dimension_semantics=("arbitrary", "arbitrary"), vmem_limit_bytes=VMEM_LIMIT),
        name="rwkv7_pre",
    )(seg, seg, prev_row, p['shift_mu'].reshape(1, C_RWKV), vecs, wa_up,
      p['g_lora_up'].astype(jnp.bfloat16), _block_ones2())


SWA_BAND = WINDOW + CHUNK


def _swa_kernel(q_ref, k_ref, v_ref, kp_ref, vp_ref, bias_ref, sink_ref, o_ref, *, mask_start):
    n_chunks = q_ref.shape[1] // CHUNK
    k_all = jnp.concatenate([kp_ref[0], k_ref[0]], axis=0).astype(jnp.bfloat16)
    v_all = jnp.concatenate([vp_ref[0], v_ref[0]], axis=0).astype(jnp.bfloat16)
    first = pl.program_id(1) == 0
    key_chunk = lax.broadcasted_iota(jnp.int32, (GROUP_B * CHUNK, SWA_BAND), 1) // CHUNK
    for c in range(n_chunks):
        q_c = q_ref[0, c * CHUNK:(c + 1) * CHUNK, :].astype(jnp.bfloat16)
        k_c = k_all[c * CHUNK:c * CHUNK + SWA_BAND]
        v_c = v_all[c * CHUNK:c * CHUNK + SWA_BAND]
        dead = jnp.logical_and(first, key_chunk + (c - WIN_CHUNKS) < 0) if (mask_start and c < WIN_CHUNKS) else None
        kvs = range(N_KV_B)
        qg = [jnp.concatenate([q_c[:, (kv * GROUP_B + g) * HEAD_B:(kv * GROUP_B + g + 1) * HEAD_B]
                               for g in range(GROUP_B)], axis=0) for kv in kvs]
        s = [lax.dot_general(qg[kv], k_c[:, kv * HEAD_B:(kv + 1) * HEAD_B], (((1,), (1,)), ((), ())),
                             preferred_element_type=jnp.float32) * (HEAD_B ** -0.5) + bias_ref[kv] for kv in kvs]
        if dead is not None:
            s = [jnp.where(dead, NEG_INF, s[kv]) for kv in kvs]
        sink = [sink_ref[kv * GROUP_B * CHUNK:(kv + 1) * GROUP_B * CHUNK, 0:1] for kv in kvs]
        m = [jnp.maximum(jnp.max(s[kv], axis=-1, keepdims=True), sink[kv]) for kv in kvs]
        e = [jnp.exp(s[kv] - m[kv]) for kv in kvs]
        pr = [e[kv] / (jnp.sum(e[kv], axis=-1, keepdims=True) + jnp.exp(sink[kv] - m[kv])) for kv in kvs]
        og = [jnp.dot(pr[kv].astype(jnp.bfloat16), v_c[:, kv * HEAD_B:(kv + 1) * HEAD_B],
                      preferred_element_type=jnp.float32) for kv in kvs]
        o_ref[0, c * CHUNK:(c + 1) * CHUNK, :] = jnp.concatenate(
            [og[kv][g * CHUNK:(g + 1) * CHUNK] for kv in kvs for g in range(GROUP_B)], axis=1)


def swa_attention(seg_b, prev_k, prev_v, sinks, *, prev_is_seq, qb=512):
    B, T, _ = seg_b.shape
    qb = min(qb, T)
    slopes = 2.0 ** (-8.0 * jnp.arange(1, N_HEADS_B + 1, dtype=jnp.float32) / N_HEADS_B)
    dist = jnp.abs(jnp.arange(CHUNK)[:, None] - (jnp.arange(SWA_BAND) - WINDOW)[None, :]).astype(jnp.float32)
    bias = (-slopes[:, None, None] * dist).reshape(N_KV_B, GROUP_B * CHUNK, SWA_BAND)
    sink_tab = jnp.broadcast_to(jnp.repeat(sinks.astype(jnp.float32), CHUNK)[:, None], (N_HEADS_B * CHUNK, 128))
    kcol, vcol = WIDTH_B // KV_WIDTH_B, WIDTH_B // KV_WIDTH_B + 1
    if prev_is_seq:
        per = qb // WINDOW
        kp_spec = pl.BlockSpec((1, WINDOW, KV_WIDTH_B), lambda b, i: (b, jnp.maximum(i * per - 1, 0), kcol))
        vp_spec = pl.BlockSpec((1, WINDOW, KV_WIDTH_B), lambda b, i: (b, jnp.maximum(i * per - 1, 0), vcol))
    else:
        kp_spec = vp_spec = pl.BlockSpec((1, WINDOW, KV_WIDTH_B), lambda b, i: (b, 0, 0))
    return pl.pallas_call(
        functools.partial(_swa_kernel, mask_start=prev_is_seq),
        out_shape=jax.ShapeDtypeStruct((B, T, WIDTH_B), jnp.float32),
        grid=(B, T // qb),
        in_specs=[pl.BlockSpec((1, qb, WIDTH_B), lambda b, i: (b, i, 0)),
                  pl.BlockSpec((1, qb, KV_WIDTH_B), lambda b, i: (b, i, kcol)),
                  pl.BlockSpec((1, qb, KV_WIDTH_B), lambda b, i: (b, i, vcol)),
                  kp_spec, vp_spec,
                  pl.BlockSpec((N_KV_B, GROUP_B * CHUNK, SWA_BAND), lambda b, i: (0, 0, 0)),
                  pl.BlockSpec((N_HEADS_B * CHUNK, 128), lambda b, i: (0, 0))],
        out_specs=pl.BlockSpec((1, qb, WIDTH_B), lambda b, i: (b, i, 0)),
        compiler_params=pltpu.CompilerParams(
            dimension_semantics=("arbitrary", "arbitrary"), vmem_limit_bytes=VMEM_LIMIT),
        name="swa_attention",
    )(seg_b, seg_b, seg_b, prev_k, prev_v, bias, sink_tab)


def _layer_norm_rows(h, g, b, eps=1e-5):
    mu = jnp.mean(h, axis=-1, keepdims=True)
    d = h - mu
    var = jnp.mean(d * d, axis=-1, keepdims=True)
    return d * lax.rsqrt(var + eps) * g + b


def _const_spec(shape):
    return pl.BlockSpec(shape, lambda *_: (0,) * len(shape), pipeline_mode=pl.Buffered(1))


def _merge_kernel(x_ref, y_ref, bonus_ref, g_ref, ob_ref, gates_ref, lnx_ref, ln1_ref, ones_ref,
                  pa_ref, pb_ref, wout_ref, o_ref):
    ones2 = ones_ref[...]
    y = y_ref[...]
    mean = _head_sums(y, ones2) * (1.0 / HEAD_A)
    d = y - mean
    var = _head_sums(d * d, ones2) * (1.0 / HEAD_A)
    yn = d * lax.rsqrt(var + GN_EPS) * lnx_ref[0:1, :] + lnx_ref[1:2, :]
    o_a = ((yn + bonus_ref[...]) * g_ref[...]).astype(jnp.bfloat16)
    br_a = jnp.dot(o_a, pa_ref[...], preferred_element_type=jnp.float32)
    br_b = jnp.dot(ob_ref[...].astype(jnp.bfloat16), pb_ref[...], preferred_element_type=jnp.float32)
    gates = jax.nn.sigmoid(gates_ref[...])
    merged = gates[:, :D_MODEL] * br_a + gates[:, D_MODEL:] * br_b
    h = ALPHA * x_ref[...] + jnp.dot(merged.astype(jnp.bfloat16), wout_ref[...], preferred_element_type=jnp.float32)
    o_ref[...] = _layer_norm_rows(h, ln1_ref[0:1, :], ln1_ref[1:2, :])


def branch_merge(x, y, bonus, g, o_b, gates, p, tm=256):
    n = x.shape[0]
    tm = _pick(n, (tm, 128, 64))
    rows = lambda w: pl.BlockSpec((tm, w), lambda i: (i, 0))
    return pl.pallas_call(
        _merge_kernel,
        out_shape=jax.ShapeDtypeStruct((n, D_MODEL), jnp.float32),
        grid=(n // tm,),
        in_specs=[rows(D_MODEL), rows(WIDTH_A), rows(WIDTH_A), rows(WIDTH_A), rows(WIDTH_B), rows(C_GATE),
                  _const_spec((2, WIDTH_A)), _const_spec((2, D_MODEL)), _const_spec((2 * PAIR_W, PAIR_W)),
                  _const_spec((WIDTH_A, D_MODEL)), _const_spec((WIDTH_B, D_MODEL)), _const_spec((D_MODEL, D_MODEL))],
        out_specs=rows(D_MODEL),
        compiler_params=pltpu.CompilerParams(dimension_semantics=("arbitrary",), vmem_limit_bytes=VMEM_LIMIT),
        name="branch_merge",
    )(x, y, bonus, g, o_b, gates, jnp.stack([p['lnx_g'], p['lnx_b']]), jnp.stack([p['ln1_g'], p['ln1_b']]),
      _block_ones2(), p['w_branch_a16'], p['w_branch_b16'], p['w_out16'])


def _mem_kernel(x_ref, mk_ref, mv_ref, ln2_ref, wq_ref, wo_ref, pwq_ref, o_ref, q_ref):
    x = x_ref[0]
    qm = jnp.dot(x.astype(jnp.bfloat16), wq_ref[...], preferred_element_type=jnp.float32).astype(jnp.bfloat16)
    mk = mk_ref[0].astype(jnp.bfloat16)
    mv = mv_ref[0].astype(jnp.bfloat16)
    outs = []
    for h in range(N_HEADS_M):
        cols = slice(h * HEAD_M, (h + 1) * HEAD_M)
        s = lax.dot_general(qm[:, cols], mk[:, cols], (((1,), (1,)), ((), ())),
                            preferred_element_type=jnp.float32) * (HEAD_M ** -0.5)
        e = jnp.exp(s - jnp.max(s, axis=-1, keepdims=True))
        pr = e / jnp.sum(e, axis=-1, keepdims=True)
        outs.append(jnp.dot(pr.astype(jnp.bfloat16), mv[:, cols], preferred_element_type=jnp.float32))
    o = jnp.concatenate(outs, axis=1).astype(jnp.bfloat16)
    h2 = ALPHA * x + jnp.dot(o, wo_ref[...], preferred_element_type=jnp.float32)
    x2 = _layer_norm_rows(h2, ln2_ref[0:1, :], ln2_ref[1:2, :])
    o_ref[0] = x2
    q_ref[0] = jnp.dot(x2.astype(jnp.bfloat16), pwq_ref[...], preferred_element_type=jnp.float32)


def mem_block(x, mk, mv, p, tm=256):
    B, T, _ = x.shape
    tm = _pick(T, (tm, 128, 64))
    wm = N_HEADS_M * HEAD_M
    rows = lambda w: pl.BlockSpec((1, tm, w), lambda b, i: (b, i, 0))
    mem = pl.BlockSpec((1, N_MEM, wm), lambda b, i: (b, 0, 0))
    out = jax.ShapeDtypeStruct((B, T, D_MODEL), jnp.float32)
    return pl.pallas_call(
        _mem_kernel,
        out_shape=(out, jax.ShapeDtypeStruct((B, T, PEER_HEADS * D_KEY), jnp.float32)),
        grid=(B, T // tm),
        in_specs=[rows(D_MODEL), mem, mem, _const_spec((2, D_MODEL)), _const_spec((D_MODEL, wm)),
                  _const_spec((wm, D_MODEL)), _const_spec((D_MODEL, PEER_HEADS * D_KEY))],
        out_specs=(rows(D_MODEL), rows(PEER_HEADS * D_KEY)),
        compiler_params=pltpu.CompilerParams(
            dimension_semantics=("arbitrary", "arbitrary"), vmem_limit_bytes=VMEM_LIMIT),
        name="mem_block",
    )(x, mk, mv, jnp.stack([p['ln2_g'], p['ln2_b']]), p['wq_mem16'], p['wo_mem16'], p['peer_wq16'])


ROUTE_TQ = 256


def _top_values(s, k, exact):
    n_rows = s.shape[0]
    iota = lax.broadcasted_iota(jnp.int32, s.shape, 0).astype(jnp.float32)
    rank = jnp.full(s.shape, float(k), jnp.float32)
    masked0 = jnp.sum((s == -jnp.inf).astype(jnp.float32), axis=0, keepdims=True)
    out = []
    for step in range(k):
        m = jnp.max(s, axis=0, keepdims=True)
        if exact:
            first = jnp.min(jnp.where(s == m, iota, float(n_rows)), axis=0, keepdims=True)
            taken = iota == first
        else:
            taken = s == m
        s = jnp.where(taken, -jnp.inf, s)
        rank = jnp.where(taken, float(step), rank)
        out.append(m)
    surplus = jnp.sum((s == -jnp.inf).astype(jnp.float32), axis=0, keepdims=True) - masked0 - float(k)
    return out, rank, surplus


def _peer_route_kernel(q_ref, keys_ref, n1_ref, c1_ref, rank2_ref, e2_ref):
    refs = (q_ref, keys_ref, n1_ref, c1_ref, rank2_ref, e2_ref)
    for h in range(PEER_HEADS):
        surplus = _peer_route_pass(*refs, exact=False, heads=(h,))

        @pl.when(jnp.max(surplus) > 0.0)
        def _(h=h):
            _peer_route_pass(*refs, exact=True, heads=(h,))


def _peer_route_pass(q_ref, keys_ref, n1_ref, c1_ref, rank2_ref, e2_ref, *, exact, heads):
    half = D_KEY // 2
    surplus = None
    for h in heads:
        tops, scores = [], []
        for p in range(2):
            c0 = (2 * h + p) * half
            qs = q_ref[:, c0:c0 + half].astype(jnp.bfloat16)
            s = lax.dot_general(keys_ref[2 * h + p], qs, (((1,), (1,)), ((), ())),
                                preferred_element_type=jnp.float32)
            scores.append(s)
            tops.append(_top_values(s, TOPK, exact))
        (t1, _, sur1), (t2, rank2, sur2) = tops
        t2all = jnp.concatenate(t2, axis=0)
        t1all = jnp.concatenate(t1, axis=0)
        rank8 = lax.broadcasted_iota(jnp.int32, (8, t1all.shape[1]), 0)
        cand_rows = [t1all + t2[0], t1all[:8] + t2[1]]
        for b in range(2, 8):
            cand_rows.append(jnp.where(rank8 < TOPK // (b + 1), t1all[:8] + t2[b], -jnp.inf))
        cand_rows.append(t1[0] + t2all[8:])
        sc, _, sur3 = _top_values(jnp.concatenate(cand_rows, axis=0), TOPK, exact)
        sur = sur1 + sur2 + sur3
        surplus = sur if surplus is None else surplus + sur
        z = jnp.zeros_like(sc[0])
        for kq in range(TOPK):
            z = z + jnp.exp(sc[kq] - sc[0])
        theta = sc[TOPK - 1]
        n1 = jnp.zeros_like(scores[0])
        for a in range(TOPK):
            n_a = jnp.sum((t1[a] + t2all >= theta).astype(jnp.float32), axis=0, keepdims=True)
            n1 = jnp.where(scores[0] == t1[a], n_a, n1)
        n1_ref[h] = n1
        rank2_ref[h] = rank2.astype(jnp.bfloat16)
        c1_ref[h] = jnp.exp(scores[0] - t1[0]) / z
        e2_ref[h] = jnp.exp(scores[1] - t2[0]).astype(jnp.bfloat16)
    return surplus


def peer_route(q, keys16):
    n = q.shape[0]
    tq = _pick(n, (ROUTE_TQ, 128))
    big = jax.ShapeDtypeStruct((PEER_HEADS, N_KEYS, n), jnp.float32)
    big16 = jax.ShapeDtypeStruct((PEER_HEADS, N_KEYS, n), jnp.bfloat16)
    bspec = pl.BlockSpec((PEER_HEADS, N_KEYS, tq), lambda i: (0, 0, i))
    return pl.pallas_call(
        _peer_route_kernel,
        out_shape=(big, big, big16, big16),
        grid=(n // tq,),
        in_specs=[pl.BlockSpec((tq, PEER_HEADS * D_KEY), lambda i: (i, 0)),
                  pl.BlockSpec((2 * PEER_HEADS, N_KEYS, D_KEY // 2), lambda i: (0, 0, 0))],
        out_specs=(bspec, bspec, bspec, bspec),
        compiler_params=pltpu.CompilerParams(
            dimension_semantics=("arbitrary",), vmem_limit_bytes=VMEM_LIMIT),
        name="peer_route",
    )(q, keys16)


PEER_TM = 512
PEER_ROWS = 8
PEER_TE = PEER_ROWS * N_KEYS


def _gelu(x):
    return 0.5 * x * (1.0 + lax.erf(x * (2.0 ** -0.5)))


def _peer_mix_kernel(x_ref, u_ref, vt_ref, n1_ref, c1_ref, rank2_ref, e2_ref, ln3_ref, o_ref,
                     x16_ref, h_ref, acc_ref):
    j = pl.program_id(1)

    @pl.when(j == 0)
    def _():
        x16_ref[...] = x_ref[...].T.astype(jnp.bfloat16)
        acc_ref[...] = jnp.zeros_like(acc_ref)

    half_te = PEER_TE // 2
    a_halves = [jnp.dot(u_ref[k * half_te:(k + 1) * half_te, :], x16_ref[...],
                        preferred_element_type=jnp.float32) for k in range(2)]

    def sublane_bcast16(row):
        return jnp.broadcast_to(row, (N_KEYS, row.shape[1])).astype(jnp.bfloat16)

    for r in range(PEER_ROWS):
        rows = slice(r * N_KEYS, (r + 1) * N_KEYS)
        for c0 in range(0, x_ref.shape[0], 128):
            cols = slice(c0, c0 + 128)
            gate = None
            for h in range(PEER_HEADS):
                keep = rank2_ref[h, :, cols] < sublane_bcast16(n1_ref[h, r:r + 1, cols])
                w = jnp.where(keep, e2_ref[h, :, cols], jnp.zeros((), jnp.bfloat16))
                w = w * sublane_bcast16(c1_ref[h, r:r + 1, cols])
                gate = w if gate is None else gate + w
            a_rows = a_halves[r // (PEER_ROWS // 2)][(r % (PEER_ROWS // 2)) * N_KEYS:(r % (PEER_ROWS // 2) + 1) * N_KEYS]
            h_ref[rows, cols] = gate * _gelu(a_rows[:, cols]).astype(jnp.bfloat16)
    acc_ref[...] += jnp.dot(vt_ref[0], h_ref[...], preferred_element_type=jnp.float32)

    @pl.when(j == pl.num_programs(1) - 1)
    def _():
        o_ref[...] = _layer_norm_rows(ALPHA * x_ref[...] + acc_ref[...].T, ln3_ref[0:1, :], ln3_ref[1:2, :])


def peer_mix(x, u16, vt16, n1, c1, rank2, e2, ln3):
    n, d = x.shape
    tm = _pick(n, (PEER_TM, 256, 128))
    n_exp = u16.shape[0]
    row_spec = pl.BlockSpec((PEER_HEADS, PEER_ROWS, tm), lambda i, j: (0, j, i))
    all_spec = pl.BlockSpec((PEER_HEADS, N_KEYS, tm), lambda i, j: (0, 0, i), pipeline_mode=pl.Buffered(1))
    return pl.pallas_call(
        _peer_mix_kernel,
        out_shape=jax.ShapeDtypeStruct((n, d), jnp.float32),
        grid=(n // tm, n_exp // PEER_TE),
        in_specs=[pl.BlockSpec((tm, d), lambda i, j: (i, 0), pipeline_mode=pl.Buffered(1)),
                  pl.BlockSpec((PEER_TE, d), lambda i, j: (j, 0)),
                  pl.BlockSpec((1, d, PEER_TE), lambda i, j: (j, 0, 0)),
                  row_spec, row_spec, all_spec, all_spec, _const_spec((2, d))],
        out_specs=pl.BlockSpec((tm, d), lambda i, j: (i, 0)),
        scratch_shapes=[pltpu.VMEM((d, tm), jnp.bfloat16), pltpu.VMEM((PEER_TE, tm), jnp.bfloat16),
                        pltpu.VMEM((d, tm), jnp.float32)],
        compiler_params=pltpu.CompilerParams(
            dimension_semantics=("arbitrary", "arbitrary"), vmem_limit_bytes=VMEM_LIMIT),
        name="peer_mix",
    )(x, u16, vt16, n1, c1, rank2, e2, ln3)


def peer_block(x, q, p):
    stats = peer_route(q, p['peer_keys16'])
    return peer_mix(x, p['peer_u16'], p['peer_vt16'], *stats, jnp.stack([p['ln3_g'], p['ln3_b']]))


def trunk_layer(x, p, mem_k, mem_v, rwkv_state, shift_row, swa_k_cache, swa_v_cache):
    B, T, _ = x.shape
    n = B * T
    x2d = x.reshape(n, D_MODEL)
    seg_a = matmul(x2d, p['w_in_a16']).reshape(B, T, C_RWKV)
    seg_b = matmul(x2d, p['w_in_b16']).reshape(B, T, C_SWA)
    gates = matmul(x2d, p['w_in_g16'])
    r, w, k, v, kk, a, g, bonus = rwkv7_pre(seg_a, shift_row.astype(jnp.float32), p)
    y, rwkv_new = rwkv7_chunked(r, w, k, v, kk, a, rwkv_state)
    shift_new = seg_a[:, -1:]
    k_new = seg_b[:, :, WIDTH_B:WIDTH_B + KV_WIDTH_B].reshape(B, T, N_KV_B, HEAD_B)
    v_new = seg_b[:, :, WIDTH_B + KV_WIDTH_B:].reshape(B, T, N_KV_B, HEAD_B)
    if swa_k_cache is None:
        o_b = swa_attention(seg_b, seg_b, seg_b, p['attn_sinks'], prev_is_seq=True)
        swa_k_new, swa_v_new = k_new[:, -WINDOW:], v_new[:, -WINDOW:]
    else:
        o_b = swa_attention(seg_b, swa_k_cache.reshape(B, WINDOW, KV_WIDTH_B),
                            swa_v_cache.reshape(B, WINDOW, KV_WIDTH_B), p['attn_sinks'], prev_is_seq=False)
        swa_k_new = jnp.concatenate([swa_k_cache, k_new], axis=1)[:, -WINDOW:]
        swa_v_new = jnp.concatenate([swa_v_cache, v_new], axis=1)[:, -WINDOW:]
    flat = lambda t: t.reshape(n, t.shape[-1])
    x1 = branch_merge(x2d, flat(y), flat(bonus), flat(g), flat(o_b), gates, p)
    wm = N_HEADS_M * HEAD_M
    x2, q = mem_block(x1.reshape(B, T, D_MODEL), mem_k.reshape(-1, N_MEM, wm), mem_v.reshape(-1, N_MEM, wm), p)
    x3 = peer_block(flat(x2), flat(q), p)
    return x3.reshape(B, T, D_MODEL), rwkv_new, shift_new, swa_k_new, swa_v_new


_MM_WEIGHTS = ('w_branch_a', 'w_branch_b', 'w_out', 'wq_mem', 'wk_mem', 'wv_mem', 'wo_mem', 'peer_wq')


def kernel(x_prompt, x_sample, state_rwkv, state_shift, cache_swa_k, cache_swa_v, cache_mem_k, cache_mem_v, mem_prompt, w_in, shift_mu, w0, w_lora_up, a0, a_lora_up, g_lora_up, k_k, k_a, r_k, lnx_g, lnx_b, attn_sinks, w_branch_a, w_branch_b, w_out, ln1_g, ln1_b, wq_mem, wk_mem, wv_mem, wo_mem, ln2_g, ln2_b, peer_wq, peer_sub_keys, peer_u, peer_v, ln3_g, ln3_b):
    params = {
        'w_in': w_in, 'shift_mu': shift_mu, 'w0': w0, 'w_lora_up': w_lora_up, 'a0': a0,
        'a_lora_up': a_lora_up, 'g_lora_up': g_lora_up, 'k_k': k_k, 'k_a': k_a, 'r_k': r_k,
        'lnx_g': lnx_g, 'lnx_b': lnx_b, 'attn_sinks': attn_sinks, 'w_branch_a': w_branch_a,
        'w_branch_b': w_branch_b, 'w_out': w_out, 'ln1_g': ln1_g, 'ln1_b': ln1_b, 'wq_mem': wq_mem,
        'wk_mem': wk_mem, 'wv_mem': wv_mem, 'wo_mem': wo_mem, 'ln2_g': ln2_g, 'ln2_b': ln2_b,
        'peer_wq': peer_wq, 'peer_sub_keys': peer_sub_keys, 'peer_u': peer_u, 'peer_v': peer_v,
        'ln3_g': ln3_g, 'ln3_b': ln3_b,
    }
    B = x_prompt.shape[0]
    rwkv0 = jnp.zeros((B, N_HEADS_A, HEAD_A, HEAD_A), jnp.float32)
    shift0 = jnp.zeros((B, 1, C_RWKV), x_prompt.dtype)
    xp, xs = x_prompt, x_sample
    p_rw, p_sh, p_k, p_v, p_mk, p_mv = [], [], [], [], [], []
    s_rw, s_sh, s_k, s_v = [], [], [], []
    for l in range(DEPTH):
        p = {name: arr[l] for name, arr in params.items()}
        for name in _MM_WEIGHTS:
            p[name + '16'] = p[name].astype(jnp.bfloat16)
        w_in16 = p['w_in'].astype(jnp.bfloat16)
        p['w_in_a16'] = w_in16[:, :C_RWKV]
        p['w_in_b16'] = w_in16[:, C_RWKV:C_RWKV + C_SWA]
        p['w_in_g16'] = w_in16[:, C_RWKV + C_SWA:]
        p['peer_keys16'] = p['peer_sub_keys'].reshape(2 * PEER_HEADS, N_KEYS, D_KEY // 2).astype(jnp.bfloat16)
        p['peer_u16'] = p['peer_u'].astype(jnp.bfloat16)
        p['peer_vt16'] = (p['peer_v'].astype(jnp.bfloat16)
                          .reshape(p['peer_v'].shape[0] // PEER_TE, PEER_TE, D_MODEL).transpose(0, 2, 1))
        mk = mm(mem_prompt, p['wk_mem16']).reshape(B, N_MEM, N_HEADS_M, HEAD_M)
        mv = mm(mem_prompt, p['wv_mem16']).reshape(B, N_MEM, N_HEADS_M, HEAD_M)
        xp, rw, sh, kn, vn = trunk_layer(xp, p, mk, mv, rwkv0, shift0, None, None)
        p_rw.append(rw); p_sh.append(sh); p_k.append(kn); p_v.append(vn); p_mk.append(mk); p_mv.append(mv)
        xs, rw, sh, kn, vn = trunk_layer(xs, p, cache_mem_k[l], cache_mem_v[l], state_rwkv[l], state_shift[l],
                                         cache_swa_k[l], cache_swa_v[l])
        s_rw.append(rw); s_sh.append(sh); s_k.append(kn); s_v.append(vn)
    return (xp, xs,
            jnp.stack(p_rw), jnp.stack(p_sh), jnp.stack(p_k), jnp.stack(p_v), jnp.stack(p_mk), jnp.stack(p_mv),
            jnp.stack(s_rw), jnp.stack(s_sh), jnp.stack(s_k), jnp.stack(s_v))
```

```python
import functools

import jax
import jax.numpy as jnp
from jax import lax
from jax.experimental import pallas as pl
from jax.experimental.pallas import tpu as pltpu

D_MODEL = 2048
DEPTH = 2
CHUNK = 64
HEAD_A = 64
N_HEADS_A = 16
WIDTH_A = N_HEADS_A * HEAD_A
LORA_W = 64
LORA_A = 64
LORA_G = 128
GN_EPS = 64e-5
C_RWKV = 3 * WIDTH_A + LORA_W + LORA_A + LORA_G
HEAD_B = 64
N_HEADS_B = 16
N_KV_B = 4
GROUP_B = N_HEADS_B // N_KV_B
WIDTH_B = N_HEADS_B * HEAD_B
KV_WIDTH_B = N_KV_B * HEAD_B
WINDOW = 128
WIN_CHUNKS = WINDOW // CHUNK
C_SWA = WIDTH_B + 2 * KV_WIDTH_B
C_GATE = 2 * D_MODEL
N_MEM = 256
N_HEADS_M = 4
HEAD_M = 128
N_KEYS = 128
PEER_HEADS = 8
D_KEY = 256
TOPK = 16
ALPHA = (2.0 * DEPTH) ** 0.25
NEG_INF = -1e30

VMEM_LIMIT = 56 * 1024 * 1024


def _matmul_kernel(a_ref, b_ref, o_ref, a16_ref):
    @pl.when(pl.program_id(1) == 0)
    def _():
        a16_ref[...] = a_ref[...].astype(jnp.bfloat16)

    o_ref[...] = jnp.dot(a16_ref[...], b_ref[...], preferred_element_type=jnp.float32)


def _pick(n, cands):
    for c in cands:
        if n % c == 0:
            return c
    return n


def matmul(a, b16):
    m, k = a.shape
    n = b16.shape[1]
    tm = _pick(m, (1024, 512, 256, 128))
    tn = _pick(n, (2048, 1664, 1536, 1024, 512, 256, 128))
    return pl.pallas_call(
        _matmul_kernel,
        out_shape=jax.ShapeDtypeStruct((m, n), jnp.float32),
        grid=(m // tm, n // tn),
        in_specs=[pl.BlockSpec((tm, k), lambda i, j: (i, 0)),
                  pl.BlockSpec((k, tn), lambda i, j: (0, j))],
        out_specs=pl.BlockSpec((tm, tn), lambda i, j: (i, j)),
        scratch_shapes=[pltpu.VMEM((tm, k), jnp.bfloat16)],
        compiler_params=pltpu.CompilerParams(
            dimension_semantics=("arbitrary", "arbitrary"),
            vmem_limit_bytes=VMEM_LIMIT),
        name="matmul",
    )(a, b16)


def mm(x, w16):
    lead = x.shape[:-1]
    return matmul(x.reshape(-1, x.shape[-1]), w16).reshape(*lead, w16.shape[1])


N_PAIR = N_HEADS_A // 2
PAIR_W = 2 * HEAD_A


def _split2(x):
    hi = x.astype(jnp.bfloat16)
    lo = (x - hi.astype(jnp.float32)).astype(jnp.bfloat16)
    return jnp.concatenate([hi, lo], axis=1)


def _hl(x):
    hi = x.astype(jnp.bfloat16)
    return hi, (x - hi.astype(jnp.float32)).astype(jnp.bfloat16)


def _dot3(a, b):
    ah, al = _hl(a)
    bh, bl = _hl(b)
    return jnp.dot(jnp.concatenate([ah, ah, al], axis=1), jnp.concatenate([bh, bl, bh], axis=0),
                   preferred_element_type=jnp.float32)


def _dot3_nt(a, b):
    ah, al = _hl(a)
    bh, bl = _hl(b)
    return lax.dot_general(jnp.concatenate([ah, ah, al], axis=1), jnp.concatenate([bh, bl, bh], axis=1),
                           (((1,), (1,)), ((), ())), preferred_element_type=jnp.float32)


def _rwkv_chunk_kernel(r_ref, w_ref, k_ref, v_ref, kk_ref, a_ref, s0_ref, ltri_ref, ones_ref, msk_ref,
                       y_ref, sT_ref, p_ref, *, n_chunks):
    tblk = pl.program_id(1)

    @pl.when(tblk == 0)
    def _():
        p_ref[...] = s0_ref[0]

    eye2, m0, m1, strict, incl = (msk_ref[i] for i in range(5))
    bd = lambda y: jnp.concatenate([y * m0, y * m1], axis=0)

    def split3(x, axis):
        t1 = x.astype(jnp.bfloat16)
        d = x - t1.astype(jnp.float32)
        t2 = d.astype(jnp.bfloat16)
        t3 = (d - t2.astype(jnp.float32)).astype(jnp.bfloat16)
        return jnp.concatenate([t1, t2, t3], axis=axis)

    def chunk(c, carry):
        rows = pl.ds(pl.multiple_of(c * CHUNK, CHUNK), CHUNK)
        pairs = range(N_PAIR)
        sls = [pl.ds(p * PAIR_W, PAIR_W) for p in pairs]
        lw = [w_ref[0, rows, sl] for sl in sls]
        cum = [jnp.dot(ltri_ref[...], split3(lw[p], 0), preferred_element_type=jnp.float32) for p in pairs]
        g = [jnp.exp(cum[p]) for p in pairs]
        ginv = [jnp.exp(-cum[p]) for p in pairs]
        kk = [kk_ref[0, rows, sl] for sl in sls]
        kh = [kk[p] * jnp.exp(cum[p] - lw[p]) for p in pairs]
        bh = [kk[p] * a_ref[0, rows, sls[p]] * ginv[p] for p in pairs]
        kf = [k_ref[0, rows, sls[p]] * ginv[p] for p in pairs]
        rh = [r_ref[0, rows, sls[p]] * g[p] for p in pairs]
        v = [v_ref[0, rows, sl] for sl in sls]
        g_last = [g[p][CHUNK - 1:CHUNK, :] for p in pairs]
        kr = [jnp.concatenate([kh[p], rh[p]], axis=0) for p in pairs]
        gram = [_dot3_nt(kr[p], jnp.concatenate([bd(bh[p]), bd(kf[p])], axis=0)) for p in pairs]
        a_b = [gram[p][:CHUNK, :PAIR_W] * strict for p in pairs]
        a_k = [gram[p][:CHUNK, PAIR_W:] * strict for p in pairs]
        a_r = [jnp.concatenate([gram[p][CHUNK:, :PAIR_W] * incl, gram[p][CHUNK:, PAIR_W:] * incl], axis=1)
               for p in pairs]
        t_inv = [eye2 - a_b[p] * msk_ref[5] for p in pairs]
        for lvl in range(1, 6):
            half = [_dot3(t_inv[p], bd(a_b[p] * msk_ref[5 + lvl])) for p in pairs]
            t_inv = [t_inv[p] - _dot3(half[p], bd(t_inv[p])) for p in pairs]
        akv = [_dot3(a_k[p], bd(v[p])) for p in pairs]
        g_col = [jnp.dot(split3(eye2 * g_last[p], 1), ones_ref[...], preferred_element_type=jnp.float32)
                 for p in pairs]
        xt = [jnp.concatenate([bh[p] * g_last[p], kf[p] * g_last[p]], axis=0).T for p in pairs]
        p0 = [p_ref[p] for p in pairs]
        zy = [_dot3(kr[p], bd(p0[p])) for p in pairs]
        u = [-_dot3(t_inv[p], bd(zy[p][:CHUNK] + akv[p])) for p in pairs]
        y = [zy[p][CHUNK:] + _dot3(a_r[p], jnp.concatenate([bd(u[p]), bd(v[p])], axis=0)) for p in pairs]
        delta = [_dot3(xt[p], jnp.concatenate([u[p], v[p]], axis=0)) for p in pairs]
        for p in pairs:
            p_ref[p] = g_col[p] * p0[p] + delta[p][:CHUNK] * m0 + delta[p][CHUNK:] * m1
            y_ref[0, rows, sls[p]] = y[p]
        return carry

    lax.fori_loop(0, n_chunks, chunk, 0)

    @pl.when(tblk == pl.num_programs(1) - 1)
    def _():
        sT_ref[0] = p_ref[...]


def rwkv7_chunked(r, logw, k, v, kk, a, state0, tb=256):
    B, T, _ = r.shape
    tb = min(tb, T)
    s0 = state0.astype(jnp.float32).reshape(B, N_PAIR, 2, HEAD_A, HEAD_A)
    s0 = s0.transpose(0, 1, 4, 2, 3).reshape(B, N_PAIR, HEAD_A, PAIR_W)
    t_i = jnp.arange(CHUNK)[:, None]
    s_i = (jnp.arange(PAIR_W) % HEAD_A)[None, :]
    lane_head = (jnp.arange(PAIR_W) // HEAD_A)[None, :]
    ones_row = jnp.ones((CHUNK, 1), jnp.int32)
    masks = [t_i == s_i, (lane_head == 0) * ones_row, (lane_head == 1) * ones_row, s_i < t_i, s_i <= t_i]
    for m in (1, 2, 4, 8, 16, 32):
        masks.append((t_i // (2 * m) == s_i // (2 * m)) & (t_i % (2 * m) >= m) & (s_i % (2 * m) < m))
    masks = jnp.stack([mk.astype(jnp.float32) for mk in masks])
    ltri = jnp.tile((jnp.arange(CHUNK)[None, :] <= jnp.arange(CHUNK)[:, None]), (1, 3)).astype(jnp.bfloat16)
    ones3 = jnp.tile((lane_head.T == lane_head), (3, 1)).astype(jnp.bfloat16)
    seq = pl.BlockSpec((1, tb, WIDTH_A), lambda b, t: (b, t, 0))
    st = pl.BlockSpec((1, N_PAIR, HEAD_A, PAIR_W), lambda b, t: (b, 0, 0, 0))
    y, sT = pl.pallas_call(
        functools.partial(_rwkv_chunk_kernel, n_chunks=tb // CHUNK),
        out_shape=(jax.ShapeDtypeStruct((B, T, WIDTH_A), jnp.float32),
                   jax.ShapeDtypeStruct((B, N_PAIR, HEAD_A, PAIR_W), jnp.float32)),
        grid=(B, T // tb),
        in_specs=[seq] * 6 + [st, _const_spec((CHUNK, 3 * CHUNK)), _const_spec((3 * PAIR_W, PAIR_W)),
                              _const_spec((11, CHUNK, PAIR_W))],
        out_specs=(seq, st),
        scratch_shapes=[pltpu.VMEM((N_PAIR, HEAD_A, PAIR_W), jnp.float32)],
        compiler_params=pltpu.CompilerParams(
            dimension_semantics=("arbitrary", "arbitrary"), vmem_limit_bytes=VMEM_LIMIT),
        name="rwkv7_chunked",
    )(r, logw, k, v, kk, a, s0, ltri, ones3, masks)
    sT = sT.reshape(B, N_PAIR, HEAD_A, 2, HEAD_A).transpose(0, 1, 3, 4, 2)
    return y, sT.reshape(B, N_HEADS_A, HEAD_A, HEAD_A)


def _head_sums(x, ones2):
    tiles = [jnp.dot(_split2(x[:, c:c + PAIR_W]), ones2, preferred_element_type=jnp.float32)
             for c in range(0, WIDTH_A, PAIR_W)]
    return jnp.concatenate(tiles, axis=1)


def _softplus(z):
    return jnp.maximum(z, 0.0) + jnp.log1p(jnp.exp(-jnp.abs(z)))


def _rwkv_pre_kernel(seg_ref, prev_ref, shift_ref, mu_ref, vec_ref, wa_ref, gup_ref, ones_ref,
                     r_ref, w_ref, k_ref, v_ref, kk_ref, a_ref, g_ref, bonus_ref):
    seg = seg_ref[0]
    tb = seg.shape[0]
    before = jnp.where(pl.program_id(1) == 0, shift_ref[0], prev_ref[0, 7:8, :])
    row = lax.broadcasted_iota(jnp.int32, seg.shape, 0)
    shifted = jnp.where(row == 0, before, pltpu.roll(seg, 1, axis=0))
    xm = seg + mu_ref[...] * (shifted - seg)
    r = xm[:, :WIDTH_A]
    k = xm[:, WIDTH_A:2 * WIDTH_A]
    v = xm[:, 2 * WIDTH_A:3 * WIDTH_A]
    wa = xm[:, 3 * WIDTH_A:3 * WIDTH_A + LORA_W + LORA_A]
    gl = xm[:, 3 * WIDTH_A + LORA_W + LORA_A:]
    lane = lax.broadcasted_iota(jnp.int32, wa.shape, 1)
    wa = jnp.where(lane < LORA_W, jnp.tanh(wa), wa).astype(jnp.bfloat16)
    lora = jnp.dot(wa, wa_ref[...], preferred_element_type=jnp.float32)
    w0, a0, k_k, k_a, r_k = (vec_ref[i:i + 1, :] for i in range(5))
    w_log = -_softplus(-(w0 + lora[:, :WIDTH_A])) - 0.5
    log_decay = -jnp.exp(w_log)
    a = jax.nn.sigmoid(a0 + lora[:, WIDTH_A:])
    g = jnp.dot(jax.nn.sigmoid(gl).astype(jnp.bfloat16), gup_ref[...], preferred_element_type=jnp.float32)
    ones2 = ones_ref[...]
    kk = k * k_k
    kk = kk * lax.rsqrt(jnp.maximum(_head_sums(kk * kk, ones2), 1e-24))
    kf = k * (1.0 + (a - 1.0) * k_a)
    r_ref[0], w_ref[0], k_ref[0], v_ref[0], kk_ref[0], a_ref[0], g_ref[0] = r, log_decay, kf, v, kk, a, g
    bonus_ref[0] = _head_sums(r * kf * r_k, ones2) * v


def _block_ones2():
    lane_head = jnp.arange(PAIR_W) // HEAD_A
    return (jnp.tile(lane_head, 2)[:, None] == lane_head[None, :]).astype(jnp.bfloat16)


def rwkv7_pre(seg, prev_row, p, tb=256):
    B, T, _ = seg.shape
    tb = min(tb, T)
    zeros = jnp.zeros((LORA_W, WIDTH_A), jnp.float32)
    wa_up = jnp.concatenate([jnp.concatenate([p['w_lora_up'], zeros], 1),
                             jnp.concatenate([zeros, p['a_lora_up']], 1)], 0).astype(jnp.bfloat16)
    vecs = jnp.stack([p['w0'], p['a0'], p['k_k'], p['k_a'], p['r_k'].reshape(WIDTH_A)])
    out = jax.ShapeDtypeStruct((B, T, WIDTH_A), jnp.float32)
    ospec = pl.BlockSpec((1, tb, WIDTH_A), lambda b, t: (b, t, 0))
    full = lambda shape: pl.BlockSpec(shape, lambda b, t: (0,) * len(shape))
    return pl.pallas_call(
        _rwkv_pre_kernel,
        out_shape=(out,) * 8,
        grid=(B, T // tb),
        in_specs=[pl.BlockSpec((1, tb, C_RWKV), lambda b, t: (b, t, 0)),
                  pl.BlockSpec((1, 8, C_RWKV), lambda b, t: (b, jnp.maximum(t * (tb // 8) - 1, 0), 0)),
                  pl.BlockSpec((1, 1, C_RWKV), lambda b, t: (b, 0, 0)),
                  full((1, C_RWKV)), full((5, WIDTH_A)), full((LORA_W + LORA_A, 2 * WIDTH_A)),
                  full((LORA_G, WIDTH_A)), full((2 * PAIR_W, PAIR_W))],
        out_specs=(ospec,) * 8,
        compiler_params=pltpu.CompilerParams(
            dimension_semantics=("arbitrary", "arbitrary"), vmem_limit_bytes=VMEM_LIMIT),
        name="rwkv7_pre",
    )(seg, seg, prev_row, p['shift_mu'].reshape(1, C_RWKV), vecs, wa_up,
      p['g_lora_up'].astype(jnp.bfloat16), _block_ones2())


SWA_BAND = WINDOW + CHUNK


def _swa_kernel(q_ref, k_ref, v_ref, kp_ref, vp_ref, bias_ref, sink_ref, o_ref, *, mask_start):
    n_chunks = q_ref.shape[1] // CHUNK
    k_all = jnp.concatenate([kp_ref[0], k_ref[0]], axis=0).astype(jnp.bfloat16)
    v_all = jnp.concatenate([vp_ref[0], v_ref[0]], axis=0).astype(jnp.bfloat16)
    first = pl.program_id(1) == 0
    key_chunk = lax.broadcasted_iota(jnp.int32, (GROUP_B * CHUNK, SWA_BAND), 1) // CHUNK
    for c in range(n_chunks):
        q_c = q_ref[0, c * CHUNK:(c + 1) * CHUNK, :].astype(jnp.bfloat16)
        k_c = k_all[c * CHUNK:c * CHUNK + SWA_BAND]
        v_c = v_all[c * CHUNK:c * CHUNK + SWA_BAND]
        dead = jnp.logical_and(first, key_chunk + (c - WIN_CHUNKS) < 0) if (mask_start and c < WIN_CHUNKS) else None
        kvs = range(N_KV_B)
        qg = [jnp.concatenate([q_c[:, (kv * GROUP_B + g) * HEAD_B:(kv * GROUP_B + g + 1) * HEAD_B]
                               for g in range(GROUP_B)], axis=0) for kv in kvs]
        s = [lax.dot_general(qg[kv], k_c[:, kv * HEAD_B:(kv + 1) * HEAD_B], (((1,), (1,)), ((), ())),
                             preferred_element_type=jnp.float32) * (HEAD_B ** -0.5) + bias_ref[kv] for kv in kvs]
        if dead is not None:
            s = [jnp.where(dead, NEG_INF, s[kv]) for kv in kvs]
        sink = [sink_ref[kv * GROUP_B * CHUNK:(kv + 1) * GROUP_B * CHUNK, 0:1] for kv in kvs]
        m = [jnp.maximum(jnp.max(s[kv], axis=-1, keepdims=True), sink[kv]) for kv in kvs]
        e = [jnp.exp(s[kv] - m[kv]) for kv in kvs]
        pr = [e[kv] / (jnp.sum(e[kv], axis=-1, keepdims=True) + jnp.exp(sink[kv] - m[kv])) for kv in kvs]
        og = [jnp.dot(pr[kv].astype(jnp.bfloat16), v_c[:, kv * HEAD_B:(kv + 1) * HEAD_B],
                      preferred_element_type=jnp.float32) for kv in kvs]
        o_ref[0, c * CHUNK:(c + 1) * CHUNK, :] = jnp.concatenate(
            [og[kv][g * CHUNK:(g + 1) * CHUNK] for kv in kvs for g in range(GROUP_B)], axis=1)


def swa_attention(seg_b, prev_k, prev_v, sinks, *, prev_is_seq, qb=512):
    B, T, _ = seg_b.shape
    qb = min(qb, T)
    slopes = 2.0 ** (-8.0 * jnp.arange(1, N_HEADS_B + 1, dtype=jnp.float32) / N_HEADS_B)
    dist = jnp.abs(jnp.arange(CHUNK)[:, None] - (jnp.arange(SWA_BAND) - WINDOW)[None, :]).astype(jnp.float32)
    bias = (-slopes[:, None, None] * dist).reshape(N_KV_B, GROUP_B * CHUNK, SWA_BAND)
    sink_tab = jnp.broadcast_to(jnp.repeat(sinks.astype(jnp.float32), CHUNK)[:, None], (N_HEADS_B * CHUNK, 128))
    kcol, vcol = WIDTH_B // KV_WIDTH_B, WIDTH_B // KV_WIDTH_B + 1
    if prev_is_seq:
        per = qb // WINDOW
        kp_spec = pl.BlockSpec((1, WINDOW, KV_WIDTH_B), lambda b, i: (b, jnp.maximum(i * per - 1, 0), kcol))
        vp_spec = pl.BlockSpec((1, WINDOW, KV_WIDTH_B), lambda b, i: (b, jnp.maximum(i * per - 1, 0), vcol))
    else:
        kp_spec = vp_spec = pl.BlockSpec((1, WINDOW, KV_WIDTH_B), lambda b, i: (b, 0, 0))
    return pl.pallas_call(
        functools.partial(_swa_kernel, mask_start=prev_is_seq),
        out_shape=jax.ShapeDtypeStruct((B, T, WIDTH_B), jnp.float32),
        grid=(B, T // qb),
        in_specs=[pl.BlockSpec((1, qb, WIDTH_B), lambda b, i: (b, i, 0)),
                  pl.BlockSpec((1, qb, KV_WIDTH_B), lambda b, i: (b, i, kcol)),
                  pl.BlockSpec((1, qb, KV_WIDTH_B), lambda b, i: (b, i, vcol)),
                  kp_spec, vp_spec,
                  pl.BlockSpec((N_KV_B, GROUP_B * CHUNK, SWA_BAND), lambda b, i: (0, 0, 0)),
                  pl.BlockSpec((N_HEADS_B * CHUNK, 128), lambda b, i: (0, 0))],
        out_specs=pl.BlockSpec((1, qb, WIDTH_B), lambda b, i: (b, i, 0)),
        compiler_params=pltpu.CompilerParams(
            dimension_semantics=("arbitrary", "arbitrary"), vmem_limit_bytes=VMEM_LIMIT),
        name="swa_attention",
    )(seg_b, seg_b, seg_b, prev_k, prev_v, bias, sink_tab)


def _layer_norm_rows(h, g, b, eps=1e-5):
    mu = jnp.mean(h, axis=-1, keepdims=True)
    d = h - mu
    var = jnp.mean(d * d, axis=-1, keepdims=True)
    return d * lax.rsqrt(var + eps) * g + b


def _const_spec(shape):
    return pl.BlockSpec(shape, lambda *_: (0,) * len(shape), pipeline_mode=pl.Buffered(1))


def _merge_kernel(x_ref, y_ref, bonus_ref, g_ref, ob_ref, gates_ref, lnx_ref, ln1_ref, ones_ref,
                  pa_ref, pb_ref, wout_ref, o_ref):
    ones2 = ones_ref[...]
    y = y_ref[...]
    mean = _head_sums(y, ones2) * (1.0 / HEAD_A)
    d = y - mean
    var = _head_sums(d * d, ones2) * (1.0 / HEAD_A)
    yn = d * lax.rsqrt(var + GN_EPS) * lnx_ref[0:1, :] + lnx_ref[1:2, :]
    o_a = ((yn + bonus_ref[...]) * g_ref[...]).astype(jnp.bfloat16)
    br_a = jnp.dot(o_a, pa_ref[...], preferred_element_type=jnp.float32)
    br_b = jnp.dot(ob_ref[...].astype(jnp.bfloat16), pb_ref[...], preferred_element_type=jnp.float32)
    gates = jax.nn.sigmoid(gates_ref[...])
    merged = gates[:, :D_MODEL] * br_a + gates[:, D_MODEL:] * br_b
    h = ALPHA * x_ref[...] + jnp.dot(merged.astype(jnp.bfloat16), wout_ref[...], preferred_element_type=jnp.float32)
    o_ref[...] = _layer_norm_rows(h, ln1_ref[0:1, :], ln1_ref[1:2, :])


def branch_merge(x, y, bonus, g, o_b, gates, p, tm=256):
    n = x.shape[0]
    tm = _pick(n, (tm, 128, 64))
    rows = lambda w: pl.BlockSpec((tm, w), lambda i: (i, 0))
    return pl.pallas_call(
        _merge_kernel,
        out_shape=jax.ShapeDtypeStruct((n, D_MODEL), jnp.float32),
        grid=(n // tm,),
        in_specs=[rows(D_MODEL), rows(WIDTH_A), rows(WIDTH_A), rows(WIDTH_A), rows(WIDTH_B), rows(C_GATE),
                  _const_spec((2, WIDTH_A)), _const_spec((2, D_MODEL)), _const_spec((2 * PAIR_W, PAIR_W)),
                  _const_spec((WIDTH_A, D_MODEL)), _const_spec((WIDTH_B, D_MODEL)), _const_spec((D_MODEL, D_MODEL))],
        out_specs=rows(D_MODEL),
        compiler_params=pltpu.CompilerParams(dimension_semantics=("arbitrary",), vmem_limit_bytes=VMEM_LIMIT),
        name="branch_merge",
    )(x, y, bonus, g, o_b, gates, jnp.stack([p['lnx_g'], p['lnx_b']]), jnp.stack([p['ln1_g'], p['ln1_b']]),
      _block_ones2(), p['w_branch_a16'], p['w_branch_b16'], p['w_out16'])


def _mem_kernel(x_ref, mk_ref, mv_ref, ln2_ref, wq_ref, wo_ref, pwq_ref, o_ref, q_ref):
    x = x_ref[0]
    qm = jnp.dot(x.astype(jnp.bfloat16), wq_ref[...], preferred_element_type=jnp.float32).astype(jnp.bfloat16)
    mk = mk_ref[0].astype(jnp.bfloat16)
    mv = mv_ref[0].astype(jnp.bfloat16)
    outs = []
    for h in range(N_HEADS_M):
        cols = slice(h * HEAD_M, (h + 1) * HEAD_M)
        s = lax.dot_general(qm[:, cols], mk[:, cols], (((1,), (1,)), ((), ())),
                            preferred_element_type=jnp.float32) * (HEAD_M ** -0.5)
        e = jnp.exp(s - jnp.max(s, axis=-1, keepdims=True))
        pr = e / jnp.sum(e, axis=-1, keepdims=True)
        outs.append(jnp.dot(pr.astype(jnp.bfloat16), mv[:, cols], preferred_element_type=jnp.float32))
    o = jnp.concatenate(outs, axis=1).astype(jnp.bfloat16)
    h2 = ALPHA * x + jnp.dot(o, wo_ref[...], preferred_element_type=jnp.float32)
    x2 = _layer_norm_rows(h2, ln2_ref[0:1, :], ln2_ref[1:2, :])
    o_ref[0] = x2
    q_ref[0] = jnp.dot(x2.astype(jnp.bfloat16), pwq_ref[...], preferred_element_type=jnp.float32)


def mem_block(x, mk, mv, p, tm=256):
    B, T, _ = x.shape
    tm = _pick(T, (tm, 128, 64))
    wm = N_HEADS_M * HEAD_M
    rows = lambda w: pl.BlockSpec((1, tm, w), lambda b, i: (b, i, 0))
    mem = pl.BlockSpec((1, N_MEM, wm), lambda b, i: (b, 0, 0))
    out = jax.ShapeDtypeStruct((B, T, D_MODEL), jnp.float32)
    return pl.pallas_call(
        _mem_kernel,
        out_shape=(out, jax.ShapeDtypeStruct((B, T, PEER_HEADS * D_KEY), jnp.float32)),
        grid=(B, T // tm),
        in_specs=[rows(D_MODEL), mem, mem, _const_spec((2, D_MODEL)), _const_spec((D_MODEL, wm)),
                  _const_spec((wm, D_MODEL)), _const_spec((D_MODEL, PEER_HEADS * D_KEY))],
        out_specs=(rows(D_MODEL), rows(PEER_HEADS * D_KEY)),
        compiler_params=pltpu.CompilerParams(
            dimension_semantics=("arbitrary", "arbitrary"), vmem_limit_bytes=VMEM_LIMIT),
        name="mem_block",
    )(x, mk, mv, jnp.stack([p['ln2_g'], p['ln2_b']]), p['wq_mem16'], p['wo_mem16'], p['peer_wq16'])


ROUTE_TQ = 256


def _top_values(s, k, exact):
    n_rows = s.shape[0]
    iota = lax.broadcasted_iota(jnp.int32, s.shape, 0).astype(jnp.float32)
    rank = jnp.full(s.shape, float(k), jnp.float32)
    masked0 = jnp.sum((s == -jnp.inf).astype(jnp.float32), axis=0, keepdims=True)
    out = []
    for step in range(k):
        m = jnp.max(s, axis=0, keepdims=True)
        if exact:
            first = jnp.min(jnp.where(s == m, iota, float(n_rows)), axis=0, keepdims=True)
            taken = iota == first
        else:
            taken = s == m
        s = jnp.where(taken, -jnp.inf, s)
        rank = jnp.where(taken, float(step), rank)
        out.append(m)
    surplus = jnp.sum((s == -jnp.inf).astype(jnp.float32), axis=0, keepdims=True) - masked0 - float(k)
    return out, rank, surplus


def _peer_route_kernel(q_ref, keys_ref, n1_ref, c1_ref, rank2_ref, e2_ref):
    refs = (q_ref, keys_ref, n1_ref, c1_ref, rank2_ref, e2_ref)
    for h in range(PEER_HEADS):
        surplus = _peer_route_pass(*refs, exact=False, heads=(h,))

        @pl.when(jnp.max(surplus) > 0.0)
        def _(h=h):
            _peer_route_pass(*refs, exact=True, heads=(h,))


def _peer_route_pass(q_ref, keys_ref, n1_ref, c1_ref, rank2_ref, e2_ref, *, exact, heads):
    half = D_KEY // 2
    surplus = None
    for h in heads:
        tops, scores = [], []
        for p in range(2):
            c0 = (2 * h + p) * half
            qs = q_ref[:, c0:c0 + half].astype(jnp.bfloat16)
            s = lax.dot_general(keys_ref[2 * h + p], qs, (((1,), (1,)), ((), ())),
                                preferred_element_type=jnp.float32)
            scores.append(s)
            tops.append(_top_values(s, TOPK, exact))
        (t1, _, sur1), (t2, rank2, sur2) = tops
        t2all = jnp.concatenate(t2, axis=0)
        t1all = jnp.concatenate(t1, axis=0)
        rank8 = lax.broadcasted_iota(jnp.int32, (8, t1all.shape[1]), 0)
        cand_rows = [t1all + t2[0], t1all[:8] + t2[1]]
        for b in range(2, 8):
            cand_rows.append(jnp.where(rank8 < TOPK // (b + 1), t1all[:8] + t2[b], -jnp.inf))
        cand_rows.append(t1[0] + t2all[8:])
        sc, _, sur3 = _top_values(jnp.concatenate(cand_rows, axis=0), TOPK, exact)
        sur = sur1 + sur2 + sur3
        surplus = sur if surplus is None else surplus + sur
        z = jnp.zeros_like(sc[0])
        for kq in range(TOPK):
            z = z + jnp.exp(sc[kq] - sc[0])
        theta = sc[TOPK - 1]
        n1 = jnp.zeros_like(scores[0])
        for a in range(TOPK):
            n_a = jnp.sum((t1[a] + t2all >= theta).astype(jnp.float32), axis=0, keepdims=True)
            n1 = jnp.where(scores[0] == t1[a], n_a, n1)
        n1_ref[h] = n1
        rank2_ref[h] = rank2.astype(jnp.bfloat16)
        c1_ref[h] = jnp.exp(scores[0] - t1[0]) / z
        e2_ref[h] = jnp.exp(scores[1] - t2[0]).astype(jnp.bfloat16)
    return surplus


def peer_route(q, keys16):
    n = q.shape[0]
    tq = _pick(n, (ROUTE_TQ, 128))
    big = jax.ShapeDtypeStruct((PEER_HEADS, N_KEYS, n), jnp.float32)
    big16 = jax.ShapeDtypeStruct((PEER_HEADS, N_KEYS, n), jnp.bfloat16)
    bspec = pl.BlockSpec((PEER_HEADS, N_KEYS, tq), lambda i: (0, 0, i))
    return pl.pallas_call(
        _peer_route_kernel,
        out_shape=(big, big, big16, big16),
        grid=(n // tq,),
        in_specs=[pl.BlockSpec((tq, PEER_HEADS * D_KEY), lambda i: (i, 0)),
                  pl.BlockSpec((2 * PEER_HEADS, N_KEYS, D_KEY // 2), lambda i: (0, 0, 0))],
        out_specs=(bspec, bspec, bspec, bspec),
        compiler_params=pltpu.CompilerParams(
            dimension_semantics=("arbitrary",), vmem_limit_bytes=VMEM_LIMIT),
        name="peer_route",
    )(q, keys16)


PEER_TM = 512
PEER_ROWS = 8
PEER_TE = PEER_ROWS * N_KEYS


def _gelu(x):
    return 0.5 * x * (1.0 + lax.erf(x * (2.0 ** -0.5)))


def _peer_mix_kernel(x_ref, u_ref, vt_ref, n1_ref, c1_ref, rank2_ref, e2_ref, ln3_ref, o_ref,
                     x16_ref, h_ref, acc_ref):
    j = pl.program_id(1)

    @pl.when(j == 0)
    def _():
        x16_ref[...] = x_ref[...].T.astype(jnp.bfloat16)
        acc_ref[...] = jnp.zeros_like(acc_ref)

    half_te = PEER_TE // 2
    a_halves = [jnp.dot(u_ref[k * half_te:(k + 1) * half_te, :], x16_ref[...],
                        preferred_element_type=jnp.float32) for k in range(2)]

    def sublane_bcast16(row):
        return jnp.broadcast_to(row, (N_KEYS, row.shape[1])).astype(jnp.bfloat16)

    for r in range(PEER_ROWS):
        rows = slice(r * N_KEYS, (r + 1) * N_KEYS)
        for c0 in range(0, x_ref.shape[0], 128):
            cols = slice(c0, c0 + 128)
            gate = None
            for h in range(PEER_HEADS):
                keep = rank2_ref[h, :, cols] < sublane_bcast16(n1_ref[h, r:r + 1, cols])
                w = jnp.where(keep, e2_ref[h, :, cols], jnp.zeros((), jnp.bfloat16))
                w = w * sublane_bcast16(c1_ref[h, r:r + 1, cols])
                gate = w if gate is None else gate + w
            a_rows = a_halves[r // (PEER_ROWS // 2)][(r % (PEER_ROWS // 2)) * N_KEYS:(r % (PEER_ROWS // 2) + 1) * N_KEYS]
            h_ref[rows, cols] = gate * _gelu(a_rows[:, cols]).astype(jnp.bfloat16)
    acc_ref[...] += jnp.dot(vt_ref[0], h_ref[...], preferred_element_type=jnp.float32)

    @pl.when(j == pl.num_programs(1) - 1)
    def _():
        o_ref[...] = _layer_norm_rows(ALPHA * x_ref[...] + acc_ref[...].T, ln3_ref[0:1, :], ln3_ref[1:2, :])


def peer_mix(x, u16, vt16, n1, c1, rank2, e2, ln3):
    n, d = x.shape
    tm = _pick(n, (PEER_TM, 256, 128))
    n_exp = u16.shape[0]
    row_spec = pl.BlockSpec((PEER_HEADS, PEER_ROWS, tm), lambda i, j: (0, j, i))
    all_spec = pl.BlockSpec((PEER_HEADS, N_KEYS, tm), lambda i, j: (0, 0, i), pipeline_mode=pl.Buffered(1))
    return pl.pallas_call(
        _peer_mix_kernel,
        out_shape=jax.ShapeDtypeStruct((n, d), jnp.float32),
        grid=(n // tm, n_exp // PEER_TE),
        in_specs=[pl.BlockSpec((tm, d), lambda i, j: (i, 0), pipeline_mode=pl.Buffered(1)),
                  pl.BlockSpec((PEER_TE, d), lambda i, j: (j, 0)),
                  pl.BlockSpec((1, d, PEER_TE), lambda i, j: (j, 0, 0)),
                  row_spec, row_spec, all_spec, all_spec, _const_spec((2, d))],
        out_specs=pl.BlockSpec((tm, d), lambda i, j: (i, 0)),
        scratch_shapes=[pltpu.VMEM((d, tm), jnp.bfloat16), pltpu.VMEM((PEER_TE, tm), jnp.bfloat16),
                        pltpu.VMEM((d, tm), jnp.float32)],
        compiler_params=pltpu.CompilerParams(
            dimension_semantics=("arbitrary", "arbitrary"), vmem_limit_bytes=VMEM_LIMIT),
        name="peer_mix",
    )(x, u16, vt16, n1, c1, rank2, e2, ln3)


def peer_block(x, q, p):
    stats = peer_route(q, p['peer_keys16'])
    return peer_mix(x, p['peer_u16'], p['peer_vt16'], *stats, jnp.stack([p['ln3_g'], p['ln3_b']]))


def trunk_layer(x, p, mem_k, mem_v, rwkv_state, shift_row, swa_k_cache, swa_v_cache):
    B, T, _ = x.shape
    n = B * T
    x2d = x.reshape(n, D_MODEL)
    seg_a = matmul(x2d, p['w_in_a16']).reshape(B, T, C_RWKV)
    seg_b = matmul(x2d, p['w_in_b16']).reshape(B, T, C_SWA)
    gates = matmul(x2d, p['w_in_g16'])
    r, w, k, v, kk, a, g, bonus = rwkv7_pre(seg_a, shift_row.astype(jnp.float32), p)
    y, rwkv_new = rwkv7_chunked(r, w, k, v, kk, a, rwkv_state)
    shift_new = seg_a[:, -1:]
    k_new = seg_b[:, :, WIDTH_B:WIDTH_B + KV_WIDTH_B].reshape(B, T, N_KV_B, HEAD_B)
    v_new = seg_b[:, :, WIDTH_B + KV_WIDTH_B:].reshape(B, T, N_KV_B, HEAD_B)
    if swa_k_cache is None:
        o_b = swa_attention(seg_b, seg_b, seg_b, p['attn_sinks'], prev_is_seq=True)
        swa_k_new, swa_v_new = k_new[:, -WINDOW:], v_new[:, -WINDOW:]
    else:
        o_b = swa_attention(seg_b, swa_k_cache.reshape(B, WINDOW, KV_WIDTH_B),
                            swa_v_cache.reshape(B, WINDOW, KV_WIDTH_B), p['attn_sinks'], prev_is_seq=False)
        swa_k_new = jnp.concatenate([swa_k_cache, k_new], axis=1)[:, -WINDOW:]
        swa_v_new = jnp.concatenate([swa_v_cache, v_new], axis=1)[:, -WINDOW:]
    flat = lambda t: t.reshape(n, t.shape[-1])
    x1 = branch_merge(x2d, flat(y), flat(bonus), flat(g), flat(o_b), gates, p)
    wm = N_HEADS_M * HEAD_M
    x2, q = mem_block(x1.reshape(B, T, D_MODEL), mem_k.reshape(-1, N_MEM, wm), mem_v.reshape(-1, N_MEM, wm), p)
    x3 = peer_block(flat(x2), flat(q), p)
    return x3.reshape(B, T, D_MODEL), rwkv_new, shift_new, swa_k_new, swa_v_new


_MM_WEIGHTS = ('w_branch_a', 'w_branch_b', 'w_out', 'wq_mem', 'wk_mem', 'wv_mem', 'wo_mem', 'peer_wq')


def kernel(x_prompt, x_sample, state_rwkv, state_shift, cache_swa_k, cache_swa_v, cache_mem_k, cache_mem_v, mem_prompt, w_in, shift_mu, w0, w_lora_up, a0, a_lora_up, g_lora_up, k_k, k_a, r_k, lnx_g, lnx_b, attn_sinks, w_branch_a, w_branch_b, w_out, ln1_g, ln1_b, wq_mem, wk_mem, wv_mem, wo_mem, ln2_g, ln2_b, peer_wq, peer_sub_keys, peer_u, peer_v, ln3_g, ln3_b):
    params = {
        'w_in': w_in, 'shift_mu': shift_mu, 'w0': w0, 'w_lora_up': w_lora_up, 'a0': a0,
        'a_lora_up': a_lora_up, 'g_lora_up': g_lora_up, 'k_k': k_k, 'k_a': k_a, 'r_k': r_k,
        'lnx_g': lnx_g, 'lnx_b': lnx_b, 'attn_sinks': attn_sinks, 'w_branch_a': w_branch_a,
        'w_branch_b': w_branch_b, 'w_out': w_out, 'ln1_g': ln1_g, 'ln1_b': ln1_b, 'wq_mem': wq_mem,
        'wk_mem': wk_mem, 'wv_mem': wv_mem, 'wo_mem': wo_mem, 'ln2_g': ln2_g, 'ln2_b': ln2_b,
        'peer_wq': peer_wq, 'peer_sub_keys': peer_sub_keys, 'peer_u': peer_u, 'peer_v': peer_v,
        'ln3_g': ln3_g, 'ln3_b': ln3_b,
    }
    B = x_prompt.shape[0]
    rwkv0 = jnp.zeros((B, N_HEADS_A, HEAD_A, HEAD_A), jnp.float32)
    shift0 = jnp.zeros((B, 1, C_RWKV), x_prompt.dtype)
    xp, xs = x_prompt, x_sample
    p_rw, p_sh, p_k, p_v, p_mk, p_mv = [], [], [], [], [], []
    s_rw, s_sh, s_k, s_v = [], [], [], []
    for l in range(DEPTH):
        p = {name: arr[l] for name, arr in params.items()}
        for name in _MM_WEIGHTS:
            p[name + '16'] = p[name].astype(jnp.bfloat16)
        w_in16 = p['w_in'].astype(jnp.bfloat16)
        p['w_in_a16'] = w_in16[:, :C_RWKV]
        p['w_in_b16'] = w_in16[:, C_RWKV:C_RWKV + C_SWA]
        p['w_in_g16'] = w_in16[:, C_RWKV + C_SWA:]
        p['peer_keys16'] = p['peer_sub_keys'].reshape(2 * PEER_HEADS, N_KEYS, D_KEY // 2).astype(jnp.bfloat16)
        p['peer_u16'] = p['peer_u'].astype(jnp.bfloat16)
        p['peer_vt16'] = (p['peer_v'].astype(jnp.bfloat16)
                          .reshape(p['peer_v'].shape[0] // PEER_TE, PEER_TE, D_MODEL).transpose(0, 2, 1))
        mk = mm(mem_prompt, p['wk_mem16']).reshape(B, N_MEM, N_HEADS_M, HEAD_M)
        mv = mm(mem_prompt, p['wv_mem16']).reshape(B, N_MEM, N_HEADS_M, HEAD_M)
        xp, rw, sh, kn, vn = trunk_layer(xp, p, mk, mv, rwkv0, shift0, None, None)
        p_rw.append(rw); p_sh.append(sh); p_k.append(kn); p_v.append(vn); p_mk.append(mk); p_mv.append(mv)
        xs, rw, sh, kn, vn = trunk_layer(xs, p, cache_mem_k[l], cache_mem_v[l], state_rwkv[l], state_shift[l],
                                         cache_swa_k[l], cache_swa_v[l])
        s_rw.append(rw); s_sh.append(sh); s_k.append(kn); s_v.append(vn)
    return (xp, xs,
            jnp.stack(p_rw), jnp.stack(p_sh), jnp.stack(p_k), jnp.stack(p_v), jnp.stack(p_mk), jnp.stack(p_mv),
            jnp.stack(s_rw), jnp.stack(s_sh), jnp.stack(s_k), jnp.stack(s_v))
```

```python
import functools

import jax
import jax.numpy as jnp
from jax import lax
from jax.experimental import pallas as pl
from jax.experimental.pallas import tpu as pltpu

D_MODEL = 2048
DEPTH = 2
CHUNK = 64
HEAD_A = 64
N_HEADS_A = 16
WIDTH_A = N_HEADS_A * HEAD_A
LORA_W = 64
LORA_A = 64
LORA_G = 128
GN_EPS = 64e-5
C_RWKV = 3 * WIDTH_A + LORA_W + LORA_A + LORA_G
HEAD_B = 64
N_HEADS_B = 16
N_KV_B = 4
GROUP_B = N_HEADS_B // N_KV_B
WIDTH_B = N_HEADS_B * HEAD_B
KV_WIDTH_B = N_KV_B * HEAD_B
WINDOW = 128
WIN_CHUNKS = WINDOW // CHUNK
C_SWA = WIDTH_B + 2 * KV_WIDTH_B
C_GATE = 2 * D_MODEL
N_MEM = 256
N_HEADS_M = 4
HEAD_M = 128
N_KEYS = 128
PEER_HEADS = 8
D_KEY = 256
TOPK = 16
ALPHA = (2.0 * DEPTH) ** 0.25
NEG_INF = -1e30

VMEM_LIMIT = 56 * 1024 * 1024


def _matmul_kernel(a_ref, b_ref, o_ref, a16_ref):
    @pl.when(pl.program_id(1) == 0)
    def _():
        a16_ref[...] = a_ref[...].astype(jnp.bfloat16)

    o_ref[...] = jnp.dot(a16_ref[...], b_ref[...], preferred_element_type=jnp.float32)


def _pick(n, cands):
    for c in cands:
        if n % c == 0:
            return c
    return n


def matmul(a, b16):
    m, k = a.shape
    n = b16.shape[1]
    tm = _pick(m, (1024, 512, 256, 128))
    tn = _pick(n, (2048, 1664, 1536, 1024, 512, 256, 128))
    return pl.pallas_call(
        _matmul_kernel,
        out_shape=jax.ShapeDtypeStruct((m, n), jnp.float32),
        grid=(m // tm, n // tn),
        in_specs=[pl.BlockSpec((tm, k), lambda i, j: (i, 0)),
                  pl.BlockSpec((k, tn), lambda i, j: (0, j))],
        out_specs=pl.BlockSpec((tm, tn), lambda i, j: (i, j)),
        scratch_shapes=[pltpu.VMEM((tm, k), jnp.bfloat16)],
        compiler_params=pltpu.CompilerParams(
            dimension_semantics=("arbitrary", "arbitrary"),
            vmem_limit_bytes=VMEM_LIMIT),
        name="matmul",
    )(a, b16)


def mm(x, w16):
    lead = x.shape[:-1]
    return matmul(x.reshape(-1, x.shape[-1]), w16).reshape(*lead, w16.shape[1])


N_PAIR = N_HEADS_A // 2
PAIR_W = 2 * HEAD_A


def _split2(x):
    hi = x.astype(jnp.bfloat16)
    lo = (x - hi.astype(jnp.float32)).astype(jnp.bfloat16)
    return jnp.concatenate([hi, lo], axis=1)


def _hl(x):
    hi = x.astype(jnp.bfloat16)
    return hi, (x - hi.astype(jnp.float32)).astype(jnp.bfloat16)


def _dot3(a, b):
    ah, al = _hl(a)
    bh, bl = _hl(b)
    return jnp.dot(jnp.concatenate([ah, ah, al], axis=1), jnp.concatenate([bh, bl, bh], axis=0),
                   preferred_element_type=jnp.float32)


def _dot3_nt(a, b):
    ah, al = _hl(a)
    bh, bl = _hl(b)
    return lax.dot_general(jnp.concatenate([ah, ah, al], axis=1), jnp.concatenate([bh, bl, bh], axis=1),
                           (((1,), (1,)), ((), ())), preferred_element_type=jnp.float32)


def _rwkv_chunk_kernel(r_ref, w_ref, k_ref, v_ref, kk_ref, a_ref, s0_ref, ltri_ref, ones_ref, msk_ref,
                       y_ref, sT_ref, p_ref, *, n_chunks):
    tblk = pl.program_id(1)

    @pl.when(tblk == 0)
    def _():
        p_ref[...] = s0_ref[0]

    eye2, m0, m1, strict, incl = (msk_ref[i] for i in range(5))
    bd = lambda y: jnp.concatenate([y * m0, y * m1], axis=0)

    def split3(x, axis):
        t1 = x.astype(jnp.bfloat16)
        d = x - t1.astype(jnp.float32)
        t2 = d.astype(jnp.bfloat16)
        t3 = (d - t2.astype(jnp.float32)).astype(jnp.bfloat16)
        return jnp.concatenate([t1, t2, t3], axis=axis)

    def chunk(c, carry):
        rows = pl.ds(pl.multiple_of(c * CHUNK, CHUNK), CHUNK)
        pairs = range(N_PAIR)
        sls = [pl.ds(p * PAIR_W, PAIR_W) for p in pairs]
        lw = [w_ref[0, rows, sl] for sl in sls]
        cum = [jnp.dot(ltri_ref[...], split3(lw[p], 0), preferred_element_type=jnp.float32) for p in pairs]
        g = [jnp.exp(cum[p]) for p in pairs]
        ginv = [jnp.exp(-cum[p]) for p in pairs]
        kk = [kk_ref[0, rows, sl] for sl in sls]
        kh = [kk[p] * jnp.exp(cum[p] - lw[p]) for p in pairs]
        bh = [kk[p] * a_ref[0, rows, sls[p]] * ginv[p] for p in pairs]
        kf = [k_ref[0, rows, sls[p]] * ginv[p] for p in pairs]
        rh = [r_ref[0, rows, sls[p]] * g[p] for p in pairs]
        v = [v_ref[0, rows, sl] for sl in sls]
        g_last = [g[p][CHUNK - 1:CHUNK, :] for p in pairs]
        kr = [jnp.concatenate([kh[p], rh[p]], axis=0) for p in pairs]
        gram = [_dot3_nt(kr[p], jnp.concatenate([bd(bh[p]), bd(kf[p])], axis=0)) for p in pairs]
        a_b = [gram[p][:CHUNK, :PAIR_W] * strict for p in pairs]
        a_k = [gram[p][:CHUNK, PAIR_W:] * strict for p in pairs]
        a_r = [jnp.concatenate([gram[p][CHUNK:, :PAIR_W] * incl, gram[p][CHUNK:, PAIR_W:] * incl], axis=1)
               for p in pairs]
        t_inv = [eye2 - a_b[p] * msk_ref[5] for p in pairs]
        for lvl in range(1, 6):
            half = [_dot3(t_inv[p], bd(a_b[p] * msk_ref[5 + lvl])) for p in pairs]
            t_inv = [t_inv[p] - _dot3(half[p], bd(t_inv[p])) for p in pairs]
        akv = [_dot3(a_k[p], bd(v[p])) for p in pairs]
        g_col = [jnp.dot(split3(eye2 * g_last[p], 1), ones_ref[...], preferred_element_type=jnp.float32)
                 for p in pairs]
        xt = [jnp.concatenate([bh[p] * g_last[p], kf[p] * g_last[p]], axis=0).T for p in pairs]
        p0 = [p_ref[p] for p in pairs]
        zy = [_dot3(kr[p], bd(p0[p])) for p in pairs]
        u = [-_dot3(t_inv[p], bd(zy[p][:CHUNK] + akv[p])) for p in pairs]
        y = [zy[p][CHUNK:] + _dot3(a_r[p], jnp.concatenate([bd(u[p]), bd(v[p])], axis=0)) for p in pairs]
        delta = [_dot3(xt[p], jnp.concatenate([u[p], v[p]], axis=0)) for p in pairs]
        for p in pairs:
            p_ref[p] = g_col[p] * p0[p] + delta[p][:CHUNK] * m0 + delta[p][CHUNK:] * m1
            y_ref[0, rows, sls[p]] = y[p]
        return carry

    lax.fori_loop(0, n_chunks, chunk, 0)

    @pl.when(tblk == pl.num_programs(1) - 1)
    def _():
        sT_ref[0] = p_ref[...]


def rwkv7_chunked(r, logw, k, v, kk, a, state0, tb=256):
    B, T, _ = r.shape
    tb = min(tb, T)
    s0 = state0.astype(jnp.float32).reshape(B, N_PAIR, 2, HEAD_A, HEAD_A)
    s0 = s0.transpose(0, 1, 4, 2, 3).reshape(B, N_PAIR, HEAD_A, PAIR_W)
    t_i = jnp.arange(CHUNK)[:, None]
    s_i = (jnp.arange(PAIR_W) % HEAD_A)[None, :]
    lane_head = (jnp.arange(PAIR_W) // HEAD_A)[None, :]
    ones_row = jnp.ones((CHUNK, 1), jnp.int32)
    masks = [t_i == s_i, (lane_head == 0) * ones_row, (lane_head == 1) * ones_row, s_i < t_i, s_i <= t_i]
    for m in (1, 2, 4, 8, 16, 32):
        masks.append((t_i // (2 * m) == s_i // (2 * m)) & (t_i % (2 * m) >= m) & (s_i % (2 * m) < m))
    masks = jnp.stack([mk.astype(jnp.float32) for mk in masks])
    ltri = jnp.tile((jnp.arange(CHUNK)[None, :] <= jnp.arange(CHUNK)[:, None]), (1, 3)).astype(jnp.bfloat16)
    ones3 = jnp.tile((lane_head.T == lane_head), (3, 1)).astype(jnp.bfloat16)
    seq = pl.BlockSpec((1, tb, WIDTH_A), lambda b, t: (b, t, 0))
    st = pl.BlockSpec((1, N_PAIR, HEAD_A, PAIR_W), lambda b, t: (b, 0, 0, 0))
    y, sT = pl.pallas_call(
        functools.partial(_rwkv_chunk_kernel, n_chunks=tb // CHUNK),
        out_shape=(jax.ShapeDtypeStruct((B, T, WIDTH_A), jnp.float32),
                   jax.ShapeDtypeStruct((B, N_PAIR, HEAD_A, PAIR_W), jnp.float32)),
        grid=(B, T // tb),
        in_specs=[seq] * 6 + [st, _const_spec((CHUNK, 3 * CHUNK)), _const_spec((3 * PAIR_W, PAIR_W)),
                              _const_spec((11, CHUNK, PAIR_W))],
        out_specs=(seq, st),
        scratch_shapes=[pltpu.VMEM((N_PAIR, HEAD_A, PAIR_W), jnp.float32)],
        compiler_params=pltpu.CompilerParams(
            dimension_semantics=("arbitrary", "arbitrary"), vmem_limit_bytes=VMEM_LIMIT),
        name="rwkv7_chunked",
    )(r, logw, k, v, kk, a, s0, ltri, ones3, masks)
    sT = sT.reshape(B, N_PAIR, HEAD_A, 2, HEAD_A).transpose(0, 1, 3, 4, 2)
    return y, sT.reshape(B, N_HEADS_A, HEAD_A, HEAD_A)


def _head_sums(x, ones2):
    tiles = [jnp.dot(_split2(x[:, c:c + PAIR_W]), ones2, preferred_element_type=jnp.float32)
             for c in range(0, WIDTH_A, PAIR_W)]
    return jnp.concatenate(tiles, axis=1)


def _softplus(z):
    return jnp.maximum(z, 0.0) + jnp.log1p(jnp.exp(-jnp.abs(z)))


def _rwkv_pre_kernel(seg_ref, prev_ref, shift_ref, mu_ref, vec_ref, wa_ref, gup_ref, ones_ref,
                     r_ref, w_ref, k_ref, v_ref, kk_ref, a_ref, g_ref, bonus_ref):
    seg = seg_ref[0]
    tb = seg.shape[0]
    before = jnp.where(pl.program_id(1) == 0, shift_ref[0], prev_ref[0, 7:8, :])
    row = lax.broadcasted_iota(jnp.int32, seg.shape, 0)
    shifted = jnp.where(row == 0, before, pltpu.roll(seg, 1, axis=0))
    xm = seg + mu_ref[...] * (shifted - seg)
    r = xm[:, :WIDTH_A]
    k = xm[:, WIDTH_A:2 * WIDTH_A]
    v = xm[:, 2 * WIDTH_A:3 * WIDTH_A]
    wa = xm[:, 3 * WIDTH_A:3 * WIDTH_A + LORA_W + LORA_A]
    gl = xm[:, 3 * WIDTH_A + LORA_W + LORA_A:]
    lane = lax.broadcasted_iota(jnp.int32, wa.shape, 1)
    wa = jnp.where(lane < LORA_W, jnp.tanh(wa), wa).astype(jnp.bfloat16)
    lora = jnp.dot(wa, wa_ref[...], preferred_element_type=jnp.float32)
    w0, a0, k_k, k_a, r_k = (vec_ref[i:i + 1, :] for i in range(5))
    w_log = -_softplus(-(w0 + lora[:, :WIDTH_A])) - 0.5
    log_decay = -jnp.exp(w_log)
    a = jax.nn.sigmoid(a0 + lora[:, WIDTH_A:])
    g = jnp.dot(jax.nn.sigmoid(gl).astype(jnp.bfloat16), gup_ref[...], preferred_element_type=jnp.float32)
    ones2 = ones_ref[...]
    kk = k * k_k
    kk = kk * lax.rsqrt(jnp.maximum(_head_sums(kk * kk, ones2), 1e-24))
    kf = k * (1.0 + (a - 1.0) * k_a)
    r_ref[0], w_ref[0], k_ref[0], v_ref[0], kk_ref[0], a_ref[0], g_ref[0] = r, log_decay, kf, v, kk, a, g
    bonus_ref[0] = _head_sums(r * kf * r_k, ones2) * v


def _block_ones2():
    lane_head = jnp.arange(PAIR_W) // HEAD_A
    return (jnp.tile(lane_head, 2)[:, None] == lane_head[None, :]).astype(jnp.bfloat16)


def rwkv7_pre(seg, prev_row, p, tb=256):
    B, T, _ = seg.shape
    tb = min(tb, T)
    zeros = jnp.zeros((LORA_W, WIDTH_A), jnp.float32)
    wa_up = jnp.concatenate([jnp.concatenate([p['w_lora_up'], zeros], 1),
                             jnp.concatenate([zeros, p['a_lora_up']], 1)], 0).astype(jnp.bfloat16)
    vecs = jnp.stack([p['w0'], p['a0'], p['k_k'], p['k_a'], p['r_k'].reshape(WIDTH_A)])
    out = jax.ShapeDtypeStruct((B, T, WIDTH_A), jnp.float32)
    ospec = pl.BlockSpec((1, tb, WIDTH_A), lambda b, t: (b, t, 0))
    full = lambda shape: pl.BlockSpec(shape, lambda b, t: (0,) * len(shape))
    return pl.pallas_call(
        _rwkv_pre_kernel,
        out_shape=(out,) * 8,
        grid=(B, T // tb),
        in_specs=[pl.BlockSpec((1, tb, C_RWKV), lambda b, t: (b, t, 0)),
                  pl.BlockSpec((1, 8, C_RWKV), lambda b, t: (b, jnp.maximum(t * (tb // 8) - 1, 0), 0)),
                  pl.BlockSpec((1, 1, C_RWKV), lambda b, t: (b, 0, 0)),
                  full((1, C_RWKV)), full((5, WIDTH_A)), full((LORA_W + LORA_A, 2 * WIDTH_A)),
                  full((LORA_G, WIDTH_A)), full((2 * PAIR_W, PAIR_W))],
        out_specs=(ospec,) * 8,
        compiler_params=pltpu.CompilerParams(
            dimension_semantics=("arbitrary", "arbitrary"), vmem_limit_bytes=VMEM_LIMIT),
        name="rwkv7_pre",
    )(seg, seg, prev_row, p['shift_mu'].reshape(1, C_RWKV), vecs, wa_up,
      p['g_lora_up'].astype(jnp.bfloat16), _block_ones2())


SWA_BAND = WINDOW + CHUNK


def _swa_kernel(q_ref, k_ref, v_ref, kp_ref, vp_ref, bias_ref, sink_ref, o_ref, *, mask_start):
    n_chunks = q_ref.shape[1] // CHUNK
    k_all = jnp.concatenate([kp_ref[0], k_ref[0]], axis=0).astype(jnp.bfloat16)
    v_all = jnp.concatenate([vp_ref[0], v_ref[0]], axis=0).astype(jnp.bfloat16)
    first = pl.program_id(1) == 0
    key_chunk = lax.broadcasted_iota(jnp.int32, (GROUP_B * CHUNK, SWA_BAND), 1) // CHUNK
    group = 8 if n_chunks % 8 == 0 else 1
    for c0 in range(0, n_chunks, group):
        units = [(c, kv) for c in range(c0, c0 + group) for kv in range(N_KV_B)]
        ids = range(len(units))
        q_c = {c: q_ref[0, c * CHUNK:(c + 1) * CHUNK, :].astype(jnp.bfloat16) for c in range(c0, c0 + group)}
        k_c = {c: k_all[c * CHUNK:c * CHUNK + SWA_BAND] for c in range(c0, c0 + group)}
        v_c = {c: v_all[c * CHUNK:c * CHUNK + SWA_BAND] for c in range(c0, c0 + group)}
        qg = [jnp.concatenate([q_c[c][:, (kv * GROUP_B + g) * HEAD_B:(kv * GROUP_B + g + 1) * HEAD_B]
                               for g in range(GROUP_B)], axis=0) for c, kv in units]
        s = [lax.dot_general(qg[i], k_c[c][:, kv * HEAD_B:(kv + 1) * HEAD_B], (((1,), (1,)), ((), ())),
                             preferred_element_type=jnp.float32) * (HEAD_B ** -0.5) + bias_ref[kv]
             for i, (c, kv) in enumerate(units)]
        for i, (c, kv) in enumerate(units):
            if mask_start and c < WIN_CHUNKS:
                s[i] = jnp.where(jnp.logical_and(first, key_chunk + (c - WIN_CHUNKS) < 0), NEG_INF, s[i])
        sink = [sink_ref[kv * GROUP_B * CHUNK:(kv + 1) * GROUP_B * CHUNK, 0:1] for c, kv in units]
        m = [jnp.maximum(jnp.max(s[i], axis=-1, keepdims=True), sink[i]) for i in ids]
        e = [jnp.exp(s[i] - m[i]) for i in ids]
        pr = [e[i] / (jnp.sum(e[i], axis=-1, keepdims=True) + jnp.exp(sink[i] - m[i])) for i in ids]
        og = [jnp.dot(pr[i].astype(jnp.bfloat16), v_c[c][:, kv * HEAD_B:(kv + 1) * HEAD_B],
                      preferred_element_type=jnp.float32) for i, (c, kv) in enumerate(units)]
        for j, c in enumerate(range(c0, c0 + group)):
            o_ref[0, c * CHUNK:(c + 1) * CHUNK, :] = jnp.concatenate(
                [og[j * N_KV_B + kv][g * CHUNK:(g + 1) * CHUNK] for kv in range(N_KV_B) for g in range(GROUP_B)],
                axis=1)


def swa_attention(seg_b, prev_k, prev_v, sinks, *, prev_is_seq, qb=512):
    B, T, _ = seg_b.shape
    qb = min(qb, T)
    slopes = 2.0 ** (-8.0 * jnp.arange(1, N_HEADS_B + 1, dtype=jnp.float32) / N_HEADS_B)
    dist = jnp.abs(jnp.arange(CHUNK)[:, None] - (jnp.arange(SWA_BAND) - WINDOW)[None, :]).astype(jnp.float32)
    bias = (-slopes[:, None, None] * dist).reshape(N_KV_B, GROUP_B * CHUNK, SWA_BAND)
    sink_tab = jnp.broadcast_to(jnp.repeat(sinks.astype(jnp.float32), CHUNK)[:, None], (N_HEADS_B * CHUNK, 128))
    kcol, vcol = WIDTH_B // KV_WIDTH_B, WIDTH_B // KV_WIDTH_B + 1
    if prev_is_seq:
        per = qb // WINDOW
        kp_spec = pl.BlockSpec((1, WINDOW, KV_WIDTH_B), lambda b, i: (b, jnp.maximum(i * per - 1, 0), kcol))
        vp_spec = pl.BlockSpec((1, WINDOW, KV_WIDTH_B), lambda b, i: (b, jnp.maximum(i * per - 1, 0), vcol))
    else:
        kp_spec = vp_spec = pl.BlockSpec((1, WINDOW, KV_WIDTH_B), lambda b, i: (b, 0, 0))
    return pl.pallas_call(
        functools.partial(_swa_kernel, mask_start=prev_is_seq),
        out_shape=jax.ShapeDtypeStruct((B, T, WIDTH_B), jnp.float32),
        grid=(B, T // qb),
        in_specs=[pl.BlockSpec((1, qb, WIDTH_B), lambda b, i: (b, i, 0)),
                  pl.BlockSpec((1, qb, KV_WIDTH_B), lambda b, i: (b, i, kcol)),
                  pl.BlockSpec((1, qb, KV_WIDTH_B), lambda b, i: (b, i, vcol)),
                  kp_spec, vp_spec,
                  pl.BlockSpec((N_KV_B, GROUP_B * CHUNK, SWA_BAND), lambda b, i: (0, 0, 0)),
                  pl.BlockSpec((N_HEADS_B * CHUNK, 128), lambda b, i: (0, 0))],
        out_specs=pl.BlockSpec((1, qb, WIDTH_B), lambda b, i: (b, i, 0)),
        compiler_params=pltpu.CompilerParams(
            dimension_semantics=("arbitrary", "arbitrary"), vmem_limit_bytes=VMEM_LIMIT),
        name="swa_attention",
    )(seg_b, seg_b, seg_b, prev_k, prev_v, bias, sink_tab)


def _layer_norm_rows(h, g, b, eps=1e-5):
    mu = jnp.mean(h, axis=-1, keepdims=True)
    d = h - mu
    var = jnp.mean(d * d, axis=-1, keepdims=True)
    return d * lax.rsqrt(var + eps) * g + b


def _const_spec(shape):
    return pl.BlockSpec(shape, lambda *_: (0,) * len(shape), pipeline_mode=pl.Buffered(1))


def _merge_kernel(x_ref, y_ref, bonus_ref, g_ref, ob_ref, gates_ref, lnx_ref, ln1_ref, ones_ref,
                  pa_ref, pb_ref, wout_ref, o_ref):
    ones2 = ones_ref[...]
    y = y_ref[...]
    mean = _head_sums(y, ones2) * (1.0 / HEAD_A)
    d = y - mean
    var = _head_sums(d * d, ones2) * (1.0 / HEAD_A)
    yn = d * lax.rsqrt(var + GN_EPS) * lnx_ref[0:1, :] + lnx_ref[1:2, :]
    o_a = ((yn + bonus_ref[...]) * g_ref[...]).astype(jnp.bfloat16)
    br_a = jnp.dot(o_a, pa_ref[...], preferred_element_type=jnp.float32)
    br_b = jnp.dot(ob_ref[...].astype(jnp.bfloat16), pb_ref[...], preferred_element_type=jnp.float32)
    gates = jax.nn.sigmoid(gates_ref[...])
    merged = gates[:, :D_MODEL] * br_a + gates[:, D_MODEL:] * br_b
    h = ALPHA * x_ref[...] + jnp.dot(merged.astype(jnp.bfloat16), wout_ref[...], preferred_element_type=jnp.float32)
    o_ref[...] = _layer_norm_rows(h, ln1_ref[0:1, :], ln1_ref[1:2, :])


def branch_merge(x, y, bonus, g, o_b, gates, p, tm=256):
    n = x.shape[0]
    tm = _pick(n, (tm, 128, 64))
    rows = lambda w: pl.BlockSpec((tm, w), lambda i: (i, 0))
    return pl.pallas_call(
        _merge_kernel,
        out_shape=jax.ShapeDtypeStruct((n, D_MODEL), jnp.float32),
        grid=(n // tm,),
        in_specs=[rows(D_MODEL), rows(WIDTH_A), rows(WIDTH_A), rows(WIDTH_A), rows(WIDTH_B), rows(C_GATE),
                  _const_spec((2, WIDTH_A)), _const_spec((2, D_MODEL)), _const_spec((2 * PAIR_W, PAIR_W)),
                  _const_spec((WIDTH_A, D_MODEL)), _const_spec((WIDTH_B, D_MODEL)), _const_spec((D_MODEL, D_MODEL))],
        out_specs=rows(D_MODEL),
        compiler_params=pltpu.CompilerParams(dimension_semantics=("arbitrary",), vmem_limit_bytes=VMEM_LIMIT),
        name="branch_merge",
    )(x, y, bonus, g, o_b, gates, jnp.stack([p['lnx_g'], p['lnx_b']]), jnp.stack([p['ln1_g'], p['ln1_b']]),
      _block_ones2(), p['w_branch_a16'], p['w_branch_b16'], p['w_out16'])


def _mem_kernel(x_ref, mk_ref, mv_ref, ln2_ref, wq_ref, wo_ref, pwq_ref, o_ref, q_ref):
    x = x_ref[0]
    qm = jnp.dot(x.astype(jnp.bfloat16), wq_ref[...], preferred_element_type=jnp.float32).astype(jnp.bfloat16)
    mk = mk_ref[0].astype(jnp.bfloat16)
    mv = mv_ref[0].astype(jnp.bfloat16)
    outs = []
    for h in range(N_HEADS_M):
        cols = slice(h * HEAD_M, (h + 1) * HEAD_M)
        s = lax.dot_general(qm[:, cols], mk[:, cols], (((1,), (1,)), ((), ())),
                            preferred_element_type=jnp.float32) * (HEAD_M ** -0.5)
        e = jnp.exp(s - jnp.max(s, axis=-1, keepdims=True))
        pr = e / jnp.sum(e, axis=-1, keepdims=True)
        outs.append(jnp.dot(pr.astype(jnp.bfloat16), mv[:, cols], preferred_element_type=jnp.float32))
    o = jnp.concatenate(outs, axis=1).astype(jnp.bfloat16)
    h2 = ALPHA * x + jnp.dot(o, wo_ref[...], preferred_element_type=jnp.float32)
    x2 = _layer_norm_rows(h2, ln2_ref[0:1, :], ln2_ref[1:2, :])
    o_ref[0] = x2
    q_ref[0] = jnp.dot(x2.astype(jnp.bfloat16), pwq_ref[...], preferred_element_type=jnp.float32)


def mem_block(x, mk, mv, p, tm=256):
    B, T, _ = x.shape
    tm = _pick(T, (tm, 128, 64))
    wm = N_HEADS_M * HEAD_M
    rows = lambda w: pl.BlockSpec((1, tm, w), lambda b, i: (b, i, 0))
    mem = pl.BlockSpec((1, N_MEM, wm), lambda b, i: (b, 0, 0))
    out = jax.ShapeDtypeStruct((B, T, D_MODEL), jnp.float32)
    return pl.pallas_call(
        _mem_kernel,
        out_shape=(out, jax.ShapeDtypeStruct((B, T, PEER_HEADS * D_KEY), jnp.float32)),
        grid=(B, T // tm),
        in_specs=[rows(D_MODEL), mem, mem, _const_spec((2, D_MODEL)), _const_spec((D_MODEL, wm)),
                  _const_spec((wm, D_MODEL)), _const_spec((D_MODEL, PEER_HEADS * D_KEY))],
        out_specs=(rows(D_MODEL), rows(PEER_HEADS * D_KEY)),
        compiler_params=pltpu.CompilerParams(
            dimension_semantics=("arbitrary", "arbitrary"), vmem_limit_bytes=VMEM_LIMIT),
        name="mem_block",
    )(x, mk, mv, jnp.stack([p['ln2_g'], p['ln2_b']]), p['wq_mem16'], p['wo_mem16'], p['peer_wq16'])


ROUTE_TQ = 256


def _top_values(s, k, exact):
    n_rows = s.shape[0]
    iota = lax.broadcasted_iota(jnp.int32, s.shape, 0).astype(jnp.float32)
    rank = jnp.full(s.shape, float(k), jnp.float32)
    masked0 = jnp.sum((s == -jnp.inf).astype(jnp.float32), axis=0, keepdims=True)
    out = []
    for step in range(k):
        m = jnp.max(s, axis=0, keepdims=True)
        if exact:
            first = jnp.min(jnp.where(s == m, iota, float(n_rows)), axis=0, keepdims=True)
            taken = iota == first
        else:
            taken = s == m
        s = jnp.where(taken, -jnp.inf, s)
        rank = jnp.where(taken, float(step), rank)
        out.append(m)
    surplus = jnp.sum((s == -jnp.inf).astype(jnp.float32), axis=0, keepdims=True) - masked0 - float(k)
    return out, rank, surplus


def _peer_route_kernel(q_ref, keys_ref, n1_ref, c1_ref, rank2_ref, e2_ref):
    refs = (q_ref, keys_ref, n1_ref, c1_ref, rank2_ref, e2_ref)
    for h in range(PEER_HEADS):
        surplus = _peer_route_pass(*refs, exact=False, heads=(h,))

        @pl.when(jnp.max(surplus) > 0.0)
        def _(h=h):
            _peer_route_pass(*refs, exact=True, heads=(h,))


def _peer_route_pass(q_ref, keys_ref, n1_ref, c1_ref, rank2_ref, e2_ref, *, exact, heads):
    half = D_KEY // 2
    surplus = None
    for h in heads:
        tops, scores = [], []
        for p in range(2):
            c0 = (2 * h + p) * half
            qs = q_ref[:, c0:c0 + half].astype(jnp.bfloat16)
            s = lax.dot_general(keys_ref[2 * h + p], qs, (((1,), (1,)), ((), ())),
                                preferred_element_type=jnp.float32)
            scores.append(s)
            tops.append(_top_values(s, TOPK, exact))
        (t1, _, sur1), (t2, rank2, sur2) = tops
        t2all = jnp.concatenate(t2, axis=0)
        t1all = jnp.concatenate(t1, axis=0)
        rank8 = lax.broadcasted_iota(jnp.int32, (8, t1all.shape[1]), 0)
        cand_rows = [t1all + t2[0], t1all[:8] + t2[1]]
        for b in range(2, 8):
            cand_rows.append(jnp.where(rank8 < TOPK // (b + 1), t1all[:8] + t2[b], -jnp.inf))
        cand_rows.append(t1[0] + t2all[8:])
        sc, _, sur3 = _top_values(jnp.concatenate(cand_rows, axis=0), TOPK, exact)
        sur = sur1 + sur2 + sur3
        surplus = sur if surplus is None else surplus + sur
        z = jnp.zeros_like(sc[0])
        for kq in range(TOPK):
            z = z + jnp.exp(sc[kq] - sc[0])
        theta = sc[TOPK - 1]
        n1 = jnp.zeros_like(scores[0])
        for a in range(TOPK):
            n_a = jnp.sum((t1[a] + t2all >= theta).astype(jnp.float32), axis=0, keepdims=True)
            n1 = jnp.where(scores[0] == t1[a], n_a, n1)
        n1_ref[h] = n1
        rank2_ref[h] = rank2.astype(jnp.bfloat16)
        c1_ref[h] = jnp.exp(scores[0] - t1[0]) / z
        e2_ref[h] = jnp.exp(scores[1] - t2[0]).astype(jnp.bfloat16)
    return surplus


def peer_route(q, keys16):
    n = q.shape[0]
    tq = _pick(n, (ROUTE_TQ, 128))
    big = jax.ShapeDtypeStruct((PEER_HEADS, N_KEYS, n), jnp.float32)
    big16 = jax.ShapeDtypeStruct((PEER_HEADS, N_KEYS, n), jnp.bfloat16)
    bspec = pl.BlockSpec((PEER_HEADS, N_KEYS, tq), lambda i: (0, 0, i))
    return pl.pallas_call(
        _peer_route_kernel,
        out_shape=(big, big, big16, big16),
        grid=(n // tq,),
        in_specs=[pl.BlockSpec((tq, PEER_HEADS * D_KEY), lambda i: (i, 0)),
                  pl.BlockSpec((2 * PEER_HEADS, N_KEYS, D_KEY // 2), lambda i: (0, 0, 0))],
        out_specs=(bspec, bspec, bspec, bspec),
        compiler_params=pltpu.CompilerParams(
            dimension_semantics=("arbitrary",), vmem_limit_bytes=VMEM_LIMIT),
        name="peer_route",
    )(q, keys16)


PEER_TM = 512
PEER_ROWS = 8
PEER_TE = PEER_ROWS * N_KEYS


def _gelu(x):
    return 0.5 * x * (1.0 + lax.erf(x * (2.0 ** -0.5)))


def _peer_mix_kernel(x_ref, u_ref, vt_ref, n1_ref, c1_ref, rank2_ref, e2_ref, ln3_ref, o_ref,
                     x16_ref, h_ref, acc_ref):
    j = pl.program_id(1)

    @pl.when(j == 0)
    def _():
        x16_ref[...] = x_ref[...].T.astype(jnp.bfloat16)
        acc_ref[...] = jnp.zeros_like(acc_ref)

    half_te = PEER_TE // 2
    a_halves = [jnp.dot(u_ref[k * half_te:(k + 1) * half_te, :], x16_ref[...],
                        preferred_element_type=jnp.float32) for k in range(2)]

    def sublane_bcast16(row):
        return jnp.broadcast_to(row, (N_KEYS, row.shape[1])).astype(jnp.bfloat16)

    for r in range(PEER_ROWS):
        rows = slice(r * N_KEYS, (r + 1) * N_KEYS)
        for c0 in range(0, x_ref.shape[0], 128):
            cols = slice(c0, c0 + 128)
            gate = None
            for h in range(PEER_HEADS):
                keep = rank2_ref[h, :, cols] < sublane_bcast16(n1_ref[h, r:r + 1, cols])
                w = jnp.where(keep, e2_ref[h, :, cols], jnp.zeros((), jnp.bfloat16))
                w = w * sublane_bcast16(c1_ref[h, r:r + 1, cols])
                gate = w if gate is None else gate + w
            a_rows = a_halves[r // (PEER_ROWS // 2)][(r % (PEER_ROWS // 2)) * N_KEYS:(r % (PEER_ROWS // 2) + 1) * N_KEYS]
            h_ref[rows, cols] = gate * _gelu(a_rows[:, cols]).astype(jnp.bfloat16)
    acc_ref[...] += jnp.dot(vt_ref[0], h_ref[...], preferred_element_type=jnp.float32)

    @pl.when(j == pl.num_programs(1) - 1)
    def _():
        o_ref[...] = _layer_norm_rows(ALPHA * x_ref[...] + acc_ref[...].T, ln3_ref[0:1, :], ln3_ref[1:2, :])


def peer_mix(x, u16, vt16, n1, c1, rank2, e2, ln3):
    n, d = x.shape
    tm = _pick(n, (PEER_TM, 256, 128))
    n_exp = u16.shape[0]
    row_spec = pl.BlockSpec((PEER_HEADS, PEER_ROWS, tm), lambda i, j: (0, j, i))
    all_spec = pl.BlockSpec((PEER_HEADS, N_KEYS, tm), lambda i, j: (0, 0, i), pipeline_mode=pl.Buffered(1))
    return pl.pallas_call(
        _peer_mix_kernel,
        out_shape=jax.ShapeDtypeStruct((n, d), jnp.float32),
        grid=(n // tm, n_exp // PEER_TE),
        in_specs=[pl.BlockSpec((tm, d), lambda i, j: (i, 0), pipeline_mode=pl.Buffered(1)),
                  pl.BlockSpec((PEER_TE, d), lambda i, j: (j, 0)),
                  pl.BlockSpec((1, d, PEER_TE), lambda i, j: (j, 0, 0)),
                  row_spec, row_spec, all_spec, all_spec, _const_spec((2, d))],
        out_specs=pl.BlockSpec((tm, d), lambda i, j: (i, 0)),
        scratch_shapes=[pltpu.VMEM((d, tm), jnp.bfloat16), pltpu.VMEM((PEER_TE, tm), jnp.bfloat16),
                        pltpu.VMEM((d, tm), jnp.float32)],
        compiler_params=pltpu.CompilerParams(
            dimension_semantics=("arbitrary", "arbitrary"), vmem_limit_bytes=VMEM_LIMIT),
        name="peer_mix",
    )(x, u16, vt16, n1, c1, rank2, e2, ln3)


def peer_block(x, q, p):
    stats = peer_route(q, p['peer_keys16'])
    return peer_mix(x, p['peer_u16'], p['peer_vt16'], *stats, jnp.stack([p['ln3_g'], p['ln3_b']]))


def trunk_layer(x, p, mem_k, mem_v, rwkv_state, shift_row, swa_k_cache, swa_v_cache):
    B, T, _ = x.shape
    n = B * T
    x2d = x.reshape(n, D_MODEL)
    seg_a = matmul(x2d, p['w_in_a16']).reshape(B, T, C_RWKV)
    seg_b = matmul(x2d, p['w_in_b16']).reshape(B, T, C_SWA)
    gates = matmul(x2d, p['w_in_g16'])
    r, w, k, v, kk, a, g, bonus = rwkv7_pre(seg_a, shift_row.astype(jnp.float32), p)
    y, rwkv_new = rwkv7_chunked(r, w, k, v, kk, a, rwkv_state)
    shift_new = seg_a[:, -1:]
    k_new = seg_b[:, :, WIDTH_B:WIDTH_B + KV_WIDTH_B].reshape(B, T, N_KV_B, HEAD_B)
    v_new = seg_b[:, :, WIDTH_B + KV_WIDTH_B:].reshape(B, T, N_KV_B, HEAD_B)
    if swa_k_cache is None:
        o_b = swa_attention(seg_b, seg_b, seg_b, p['attn_sinks'], prev_is_seq=True)
        swa_k_new, swa_v_new = k_new[:, -WINDOW:], v_new[:, -WINDOW:]
    else:
        o_b = swa_attention(seg_b, swa_k_cache.reshape(B, WINDOW, KV_WIDTH_B),
                            swa_v_cache.reshape(B, WINDOW, KV_WIDTH_B), p['attn_sinks'], prev_is_seq=False)
        swa_k_new = jnp.concatenate([swa_k_cache, k_new], axis=1)[:, -WINDOW:]
        swa_v_new = jnp.concatenate([swa_v_cache, v_new], axis=1)[:, -WINDOW:]
    flat = lambda t: t.reshape(n, t.shape[-1])
    x1 = branch_merge(x2d, flat(y), flat(bonus), flat(g), flat(o_b), gates, p)
    wm = N_HEADS_M * HEAD_M
    x2, q = mem_block(x1.reshape(B, T, D_MODEL), mem_k.reshape(-1, N_MEM, wm), mem_v.reshape(-1, N_MEM, wm), p)
    x3 = peer_block(flat(x2), flat(q), p)
    return x3.reshape(B, T, D_MODEL), rwkv_new, shift_new, swa_k_new, swa_v_new


_MM_WEIGHTS = ('w_branch_a', 'w_branch_b', 'w_out', 'wq_mem', 'wk_mem', 'wv_mem', 'wo_mem', 'peer_wq')


def kernel(x_prompt, x_sample, state_rwkv, state_shift, cache_swa_k, cache_swa_v, cache_mem_k, cache_mem_v, mem_prompt, w_in, shift_mu, w0, w_lora_up, a0, a_lora_up, g_lora_up, k_k, k_a, r_k, lnx_g, lnx_b, attn_sinks, w_branch_a, w_branch_b, w_out, ln1_g, ln1_b, wq_mem, wk_mem, wv_mem, wo_mem, ln2_g, ln2_b, peer_wq, peer_sub_keys, peer_u, peer_v, ln3_g, ln3_b):
    params = {
        'w_in': w_in, 'shift_mu': shift_mu, 'w0': w0, 'w_lora_up': w_lora_up, 'a0': a0,
        'a_lora_up': a_lora_up, 'g_lora_up': g_lora_up, 'k_k': k_k, 'k_a': k_a, 'r_k': r_k,
        'lnx_g': lnx_g, 'lnx_b': lnx_b, 'attn_sinks': attn_sinks, 'w_branch_a': w_branch_a,
        'w_branch_b': w_branch_b, 'w_out': w_out, 'ln1_g': ln1_g, 'ln1_b': ln1_b, 'wq_mem': wq_mem,
        'wk_mem': wk_mem, 'wv_mem': wv_mem, 'wo_mem': wo_mem, 'ln2_g': ln2_g, 'ln2_b': ln2_b,
        'peer_wq': peer_wq, 'peer_sub_keys': peer_sub_keys, 'peer_u': peer_u, 'peer_v': peer_v,
        'ln3_g': ln3_g, 'ln3_b': ln3_b,
    }
    B = x_prompt.shape[0]
    rwkv0 = jnp.zeros((B, N_HEADS_A, HEAD_A, HEAD_A), jnp.float32)
    shift0 = jnp.zeros((B, 1, C_RWKV), x_prompt.dtype)
    xp, xs = x_prompt, x_sample
    p_rw, p_sh, p_k, p_v, p_mk, p_mv = [], [], [], [], [], []
    s_rw, s_sh, s_k, s_v = [], [], [], []
    for l in range(DEPTH):
        p = {name: arr[l] for name, arr in params.items()}
        for name in _MM_WEIGHTS:
            p[name + '16'] = p[name].astype(jnp.bfloat16)
        w_in16 = p['w_in'].astype(jnp.bfloat16)
        p['w_in_a16'] = w_in16[:, :C_RWKV]
        p['w_in_b16'] = w_in16[:, C_RWKV:C_RWKV + C_SWA]
        p['w_in_g16'] = w_in16[:, C_RWKV + C_SWA:]
        p['peer_keys16'] = p['peer_sub_keys'].reshape(2 * PEER_HEADS, N_KEYS, D_KEY // 2).astype(jnp.bfloat16)
        p['peer_u16'] = p['peer_u'].astype(jnp.bfloat16)
        p['peer_vt16'] = (p['peer_v'].astype(jnp.bfloat16)
                          .reshape(p['peer_v'].shape[0] // PEER_TE, PEER_TE, D_MODEL).transpose(0, 2, 1))
        mk = mm(mem_prompt, p['wk_mem16']).reshape(B, N_MEM, N_HEADS_M, HEAD_M)
        mv = mm(mem_prompt, p['wv_mem16']).reshape(B, N_MEM, N_HEADS_M, HEAD_M)
        xp, rw, sh, kn, vn = trunk_layer(xp, p, mk, mv, rwkv0, shift0, None, None)
        p_rw.append(rw); p_sh.append(sh); p_k.append(kn); p_v.append(vn); p_mk.append(mk); p_mv.append(mv)
        xs, rw, sh, kn, vn = trunk_layer(xs, p, cache_mem_k[l], cache_mem_v[l], state_rwkv[l], state_shift[l],
                                         cache_swa_k[l], cache_swa_v[l])
        s_rw.append(rw); s_sh.append(sh); s_k.append(kn); s_v.append(vn)
    return (xp, xs,
            jnp.stack(p_rw), jnp.stack(p_sh), jnp.stack(p_k), jnp.stack(p_v), jnp.stack(p_mk), jnp.stack(p_mv),
            jnp.stack(s_rw), jnp.stack(s_sh), jnp.stack(s_k), jnp.stack(s_v))
```

```python
import functools

import jax
import jax.numpy as jnp
from jax import lax
from jax.experimental import pallas as pl
from jax.experimental.pallas import tpu as pltpu

D_MODEL = 2048
DEPTH = 2
CHUNK = 64
HEAD_A = 64
N_HEADS_A = 16
WIDTH_A = N_HEADS_A * HEAD_A
LORA_W = 64
LORA_A = 64
LORA_G = 128
GN_EPS = 64e-5
C_RWKV = 3 * WIDTH_A + LORA_W + LORA_A + LORA_G
HEAD_B = 64
N_HEADS_B = 16
N_KV_B = 4
GROUP_B = N_HEADS_B // N_KV_B
WIDTH_B = N_HEADS_B * HEAD_B
KV_WIDTH_B = N_KV_B * HEAD_B
WINDOW = 128
WIN_CHUNKS = WINDOW // CHUNK
C_SWA = WIDTH_B + 2 * KV_WIDTH_B
C_GATE = 2 * D_MODEL
N_MEM = 256
N_HEADS_M = 4
HEAD_M = 128
N_KEYS = 128
PEER_HEADS = 8
D_KEY = 256
TOPK = 16
ALPHA = (2.0 * DEPTH) ** 0.25
NEG_INF = -1e30

VMEM_LIMIT = 56 * 1024 * 1024


def _matmul_kernel(a_ref, b_ref, o_ref, a16_ref):
    @pl.when(pl.program_id(1) == 0)
    def _():
        a16_ref[...] = a_ref[...].astype(jnp.bfloat16)

    o_ref[...] = jnp.dot(a16_ref[...], b_ref[...], preferred_element_type=jnp.float32)


def _pick(n, cands):
    for c in cands:
        if n % c == 0:
            return c
    return n


def matmul(a, b16):
    m, k = a.shape
    n = b16.shape[1]
    tm = _pick(m, (1024, 512, 256, 128))
    tn = _pick(n, (2048, 1664, 1536, 1024, 512, 256, 128))
    return pl.pallas_call(
        _matmul_kernel,
        out_shape=jax.ShapeDtypeStruct((m, n), jnp.float32),
        grid=(m // tm, n // tn),
        in_specs=[pl.BlockSpec((tm, k), lambda i, j: (i, 0)),
                  pl.BlockSpec((k, tn), lambda i, j: (0, j))],
        out_specs=pl.BlockSpec((tm, tn), lambda i, j: (i, j)),
        scratch_shapes=[pltpu.VMEM((tm, k), jnp.bfloat16)],
        compiler_params=pltpu.CompilerParams(
            dimension_semantics=("arbitrary", "arbitrary"),
            allow_input_fusion=(False, True),
            vmem_limit_bytes=VMEM_LIMIT),
        name="matmul",
    )(a, b16)


def mm(x, w16):
    lead = x.shape[:-1]
    return matmul(x.reshape(-1, x.shape[-1]), w16).reshape(*lead, w16.shape[1])


N_PAIR = N_HEADS_A // 2
PAIR_W = 2 * HEAD_A


def _split2(x):
    hi = x.astype(jnp.bfloat16)
    lo = (x - hi.astype(jnp.float32)).astype(jnp.bfloat16)
    return jnp.concatenate([hi, lo], axis=1)


def _hl(x):
    hi = x.astype(jnp.bfloat16)
    return hi, (x - hi.astype(jnp.float32)).astype(jnp.bfloat16)


def _dot3(a, b):
    ah, al = _hl(a)
    bh, bl = _hl(b)
    return jnp.dot(jnp.concatenate([ah, ah, al], axis=1), jnp.concatenate([bh, bl, bh], axis=0),
                   preferred_element_type=jnp.float32)


def _dot3_nt(a, b):
    ah, al = _hl(a)
    bh, bl = _hl(b)
    return lax.dot_general(jnp.concatenate([ah, ah, al], axis=1), jnp.concatenate([bh, bl, bh], axis=1),
                           (((1,), (1,)), ((), ())), preferred_element_type=jnp.float32)


def _rwkv_chunk_kernel(r_ref, w_ref, k_ref, v_ref, kk_ref, a_ref, s0_ref, ltri_ref, ones_ref, msk_ref,
                       y_ref, sT_ref, p_ref, *, n_chunks):
    tblk = pl.program_id(1)

    @pl.when(tblk == 0)
    def _():
        p_ref[...] = s0_ref[0]

    eye2, m0, m1, strict, incl = (msk_ref[i] for i in range(5))
    bd = lambda y: jnp.concatenate([y * m0, y * m1], axis=0)

    def split3(x, axis):
        t1 = x.astype(jnp.bfloat16)
        d = x - t1.astype(jnp.float32)
        t2 = d.astype(jnp.bfloat16)
        t3 = (d - t2.astype(jnp.float32)).astype(jnp.bfloat16)
        return jnp.concatenate([t1, t2, t3], axis=axis)

    def chunk(c, carry):
        rows = pl.ds(pl.multiple_of(c * CHUNK, CHUNK), CHUNK)
        pairs = range(N_PAIR)
        sls = [pl.ds(p * PAIR_W, PAIR_W) for p in pairs]
        lw = [w_ref[0, rows, sl] for sl in sls]
        cum = [jnp.dot(ltri_ref[...], split3(lw[p], 0), preferred_element_type=jnp.float32) for p in pairs]
        g = [jnp.exp(cum[p]) for p in pairs]
        ginv = [jnp.exp(-cum[p]) for p in pairs]
        kk = [kk_ref[0, rows, sl] for sl in sls]
        kh = [kk[p] * jnp.exp(cum[p] - lw[p]) for p in pairs]
        bh = [kk[p] * a_ref[0, rows, sls[p]] * ginv[p] for p in pairs]
        kf = [k_ref[0, rows, sls[p]] * ginv[p] for p in pairs]
        rh = [r_ref[0, rows, sls[p]] * g[p] for p in pairs]
        v = [v_ref[0, rows, sl] for sl in sls]
        g_last = [g[p][CHUNK - 1:CHUNK, :] for p in pairs]
        kr = [jnp.concatenate([kh[p], rh[p]], axis=0) for p in pairs]
        gram = [_dot3_nt(kr[p], jnp.concatenate([bd(bh[p]), bd(kf[p])], axis=0)) for p in pairs]
        a_b = [gram[p][:CHUNK, :PAIR_W] * strict for p in pairs]
        a_k = [gram[p][:CHUNK, PAIR_W:] * strict for p in pairs]
        a_r = [jnp.concatenate([gram[p][CHUNK:, :PAIR_W] * incl, gram[p][CHUNK:, PAIR_W:] * incl], axis=1)
               for p in pairs]
        t_inv = [eye2 - a_b[p] * msk_ref[5] for p in pairs]
        for lvl in range(1, 6):
            half = [_dot3(t_inv[p], bd(a_b[p] * msk_ref[5 + lvl])) for p in pairs]
            t_inv = [t_inv[p] - _dot3(half[p], bd(t_inv[p])) for p in pairs]
        akv = [_dot3(a_k[p], bd(v[p])) for p in pairs]
        g_col = [jnp.dot(split3(eye2 * g_last[p], 1), ones_ref[...], preferred_element_type=jnp.float32)
                 for p in pairs]
        xt = [jnp.concatenate([bh[p] * g_last[p], kf[p] * g_last[p]], axis=0).T for p in pairs]
        p0 = [p_ref[p] for p in pairs]
        zy = [_dot3(kr[p], bd(p0[p])) for p in pairs]
        u = [-_dot3(t_inv[p], bd(zy[p][:CHUNK] + akv[p])) for p in pairs]
        y = [zy[p][CHUNK:] + _dot3(a_r[p], jnp.concatenate([bd(u[p]), bd(v[p])], axis=0)) for p in pairs]
        delta = [_dot3(xt[p], jnp.concatenate([u[p], v[p]], axis=0)) for p in pairs]
        for p in pairs:
            p_ref[p] = g_col[p] * p0[p] + delta[p][:CHUNK] * m0 + delta[p][CHUNK:] * m1
            y_ref[0, rows, sls[p]] = y[p]
        return carry

    lax.fori_loop(0, n_chunks, chunk, 0)

    @pl.when(tblk == pl.num_programs(1) - 1)
    def _():
        sT_ref[0] = p_ref[...]


def rwkv7_chunked(r, logw, k, v, kk, a, state0, tb=256):
    B, T, _ = r.shape
    tb = min(tb, T)
    s0 = state0.astype(jnp.float32).reshape(B, N_PAIR, 2, HEAD_A, HEAD_A)
    s0 = s0.transpose(0, 1, 4, 2, 3).reshape(B, N_PAIR, HEAD_A, PAIR_W)
    t_i = jnp.arange(CHUNK)[:, None]
    s_i = (jnp.arange(PAIR_W) % HEAD_A)[None, :]
    lane_head = (jnp.arange(PAIR_W) // HEAD_A)[None, :]
    ones_row = jnp.ones((CHUNK, 1), jnp.int32)
    masks = [t_i == s_i, (lane_head == 0) * ones_row, (lane_head == 1) * ones_row, s_i < t_i, s_i <= t_i]
    for m in (1, 2, 4, 8, 16, 32):
        masks.append((t_i // (2 * m) == s_i // (2 * m)) & (t_i % (2 * m) >= m) & (s_i % (2 * m) < m))
    masks = jnp.stack([mk.astype(jnp.float32) for mk in masks])
    ltri = jnp.tile((jnp.arange(CHUNK)[None, :] <= jnp.arange(CHUNK)[:, None]), (1, 3)).astype(jnp.bfloat16)
    ones3 = jnp.tile((lane_head.T == lane_head), (3, 1)).astype(jnp.bfloat16)
    seq = pl.BlockSpec((1, tb, WIDTH_A), lambda b, t: (b, t, 0))
    st = pl.BlockSpec((1, N_PAIR, HEAD_A, PAIR_W), lambda b, t: (b, 0, 0, 0))
    y, sT = pl.pallas_call(
        functools.partial(_rwkv_chunk_kernel, n_chunks=tb // CHUNK),
        out_shape=(jax.ShapeDtypeStruct((B, T, WIDTH_A), jnp.float32),
                   jax.ShapeDtypeStruct((B, N_PAIR, HEAD_A, PAIR_W), jnp.float32)),
        grid=(B, T // tb),
        in_specs=[seq] * 6 + [st, _const_spec((CHUNK, 3 * CHUNK)), _const_spec((3 * PAIR_W, PAIR_W)),
                              _const_spec((11, CHUNK, PAIR_W))],
        out_specs=(seq, st),
        scratch_shapes=[pltpu.VMEM((N_PAIR, HEAD_A, PAIR_W), jnp.float32)],
        compiler_params=pltpu.CompilerParams(
            dimension_semantics=("arbitrary", "arbitrary"), vmem_limit_bytes=VMEM_LIMIT),
        name="rwkv7_chunked",
    )(r, logw, k, v, kk, a, s0, ltri, ones3, masks)
    sT = sT.reshape(B, N_PAIR, HEAD_A, 2, HEAD_A).transpose(0, 1, 3, 4, 2)
    return y, sT.reshape(B, N_HEADS_A, HEAD_A, HEAD_A)


def _head_sums(x, ones2):
    tiles = [jnp.dot(_split2(x[:, c:c + PAIR_W]), ones2, preferred_element_type=jnp.float32)
             for c in range(0, WIDTH_A, PAIR_W)]
    return jnp.concatenate(tiles, axis=1)


def _softplus(z):
    return jnp.maximum(z, 0.0) + jnp.log1p(jnp.exp(-jnp.abs(z)))


def _rwkv_pre_kernel(seg_ref, prev_ref, shift_ref, mu_ref, vec_ref, wa_ref, gup_ref, ones_ref,
                     r_ref, w_ref, k_ref, v_ref, kk_ref, a_ref, g_ref, bonus_ref):
    seg = seg_ref[0]
    tb = seg.shape[0]
    before = jnp.where(pl.program_id(1) == 0, shift_ref[0], prev_ref[0, 7:8, :])
    row = lax.broadcasted_iota(jnp.int32, seg.shape, 0)
    shifted = jnp.where(row == 0, before, pltpu.roll(seg, 1, axis=0))
    xm = seg + mu_ref[...] * (shifted - seg)
    r = xm[:, :WIDTH_A]
    k = xm[:, WIDTH_A:2 * WIDTH_A]
    v = xm[:, 2 * WIDTH_A:3 * WIDTH_A]
    wa = xm[:, 3 * WIDTH_A:3 * WIDTH_A + LORA_W + LORA_A]
    gl = xm[:, 3 * WIDTH_A + LORA_W + LORA_A:]
    lane = lax.broadcasted_iota(jnp.int32, wa.shape, 1)
    wa = jnp.where(lane < LORA_W, jnp.tanh(wa), wa).astype(jnp.bfloat16)
    lora = jnp.dot(wa, wa_ref[...], preferred_element_type=jnp.float32)
    w0, a0, k_k, k_a, r_k = (vec_ref[i:i + 1, :] for i in range(5))
    w_log = -_softplus(-(w0 + lora[:, :WIDTH_A])) - 0.5
    log_decay = -jnp.exp(w_log)
    a = jax.nn.sigmoid(a0 + lora[:, WIDTH_A:])
    g = jnp.dot(jax.nn.sigmoid(gl).astype(jnp.bfloat16), gup_ref[...], preferred_element_type=jnp.float32)
    ones2 = ones_ref[...]
    kk = k * k_k
    kk = kk * lax.rsqrt(jnp.maximum(_head_sums(kk * kk, ones2), 1e-24))
    kf = k * (1.0 + (a - 1.0) * k_a)
    r_ref[0], w_ref[0], k_ref[0], v_ref[0], kk_ref[0], a_ref[0], g_ref[0] = r, log_decay, kf, v, kk, a, g
    bonus_ref[0] = _head_sums(r * kf * r_k, ones2) * v


def _block_ones2():
    lane_head = jnp.arange(PAIR_W) // HEAD_A
    return (jnp.tile(lane_head, 2)[:, None] == lane_head[None, :]).astype(jnp.bfloat16)


def rwkv7_pre(seg, prev_row, p, tb=256):
    B, T, _ = seg.shape
    tb = min(tb, T)
    zeros = jnp.zeros((LORA_W, WIDTH_A), jnp.float32)
    wa_up = jnp.concatenate([jnp.concatenate([p['w_lora_up'], zeros], 1),
                             jnp.concatenate([zeros, p['a_lora_up']], 1)], 0).astype(jnp.bfloat16)
    vecs = jnp.stack([p['w0'], p['a0'], p['k_k'], p['k_a'], p['r_k'].reshape(WIDTH_A)])
    out = jax.ShapeDtypeStruct((B, T, WIDTH_A), jnp.float32)
    ospec = pl.BlockSpec((1, tb, WIDTH_A), lambda b, t: (b, t, 0))
    full = lambda shape: pl.BlockSpec(shape, lambda b, t: (0,) * len(shape))
    return pl.pallas_call(
        _rwkv_pre_kernel,
        out_shape=(out,) * 8,
        grid=(B, T // tb),
        in_specs=[pl.BlockSpec((1, tb, C_RWKV), lambda b, t: (b, t, 0)),
                  pl.BlockSpec((1, 8, C_RWKV), lambda b, t: (b, jnp.maximum(t * (tb // 8) - 1, 0), 0)),
                  pl.BlockSpec((1, 1, C_RWKV), lambda b, t: (b, 0, 0)),
                  full((1, C_RWKV)), full((5, WIDTH_A)), full((LORA_W + LORA_A, 2 * WIDTH_A)),
                  full((LORA_G, WIDTH_A)), full((2 * PAIR_W, PAIR_W))],
        out_specs=(ospec,) * 8,
        compiler_params=pltpu.CompilerParams(
            dimension_semantics=("arbitrary", "arbitrary"), vmem_limit_bytes=VMEM_LIMIT),
        name="rwkv7_pre",
    )(seg, seg, prev_row, p['shift_mu'].reshape(1, C_RWKV), vecs, wa_up,
      p['g_lora_up'].astype(jnp.bfloat16), _block_ones2())


SWA_BAND = WINDOW + CHUNK


def _swa_kernel(q_ref, k_ref, v_ref, kp_ref, vp_ref, bias_ref, sink_ref, o_ref, *, mask_start):
    n_chunks = q_ref.shape[1] // CHUNK
    k_all = jnp.concatenate([kp_ref[0], k_ref[0]], axis=0).astype(jnp.bfloat16)
    v_all = jnp.concatenate([vp_ref[0], v_ref[0]], axis=0).astype(jnp.bfloat16)
    first = pl.program_id(1) == 0
    key_chunk = lax.broadcasted_iota(jnp.int32, (GROUP_B * CHUNK, SWA_BAND), 1) // CHUNK
    group = 8 if n_chunks % 8 == 0 else 1
    for c0 in range(0, n_chunks, group):
        units = [(c, kv) for c in range(c0, c0 + group) for kv in range(N_KV_B)]
        ids = range(len(units))
        q_c = {c: q_ref[0, c * CHUNK:(c + 1) * CHUNK, :].astype(jnp.bfloat16) for c in range(c0, c0 + group)}
        k_c = {c: k_all[c * CHUNK:c * CHUNK + SWA_BAND] for c in range(c0, c0 + group)}
        v_c = {c: v_all[c * CHUNK:c * CHUNK + SWA_BAND] for c in range(c0, c0 + group)}
        qg = [jnp.concatenate([q_c[c][:, (kv * GROUP_B + g) * HEAD_B:(kv * GROUP_B + g + 1) * HEAD_B]
                               for g in range(GROUP_B)], axis=0) for c, kv in units]
        s = [lax.dot_general(qg[i], k_c[c][:, kv * HEAD_B:(kv + 1) * HEAD_B], (((1,), (1,)), ((), ())),
                             preferred_element_type=jnp.float32) * (HEAD_B ** -0.5) + bias_ref[kv]
             for i, (c, kv) in enumerate(units)]
        for i, (c, kv) in enumerate(units):
            if mask_start and c < WIN_CHUNKS:
                s[i] = jnp.where(jnp.logical_and(first, key_chunk + (c - WIN_CHUNKS) < 0), NEG_INF, s[i])
        sink = [sink_ref[kv * GROUP_B * CHUNK:(kv + 1) * GROUP_B * CHUNK, 0:1] for c, kv in units]
        m = [jnp.maximum(jnp.max(s[i], axis=-1, keepdims=True), sink[i]) for i in ids]
        e = [jnp.exp(s[i] - m[i]) for i in ids]
        pr = [e[i] / (jnp.sum(e[i], axis=-1, keepdims=True) + jnp.exp(sink[i] - m[i])) for i in ids]
        og = [jnp.dot(pr[i].astype(jnp.bfloat16), v_c[c][:, kv * HEAD_B:(kv + 1) * HEAD_B],
                      preferred_element_type=jnp.float32) for i, (c, kv) in enumerate(units)]
        for j, c in enumerate(range(c0, c0 + group)):
            o_ref[0, c * CHUNK:(c + 1) * CHUNK, :] = jnp.concatenate(
                [og[j * N_KV_B + kv][g * CHUNK:(g + 1) * CHUNK] for kv in range(N_KV_B) for g in range(GROUP_B)],
                axis=1)


def swa_attention(seg_b, prev_k, prev_v, sinks, *, prev_is_seq, qb=512):
    B, T, _ = seg_b.shape
    qb = min(qb, T)
    slopes = 2.0 ** (-8.0 * jnp.arange(1, N_HEADS_B + 1, dtype=jnp.float32) / N_HEADS_B)
    dist = jnp.abs(jnp.arange(CHUNK)[:, None] - (jnp.arange(SWA_BAND) - WINDOW)[None, :]).astype(jnp.float32)
    bias = (-slopes[:, None, None] * dist).reshape(N_KV_B, GROUP_B * CHUNK, SWA_BAND)
    sink_tab = jnp.broadcast_to(jnp.repeat(sinks.astype(jnp.float32), CHUNK)[:, None], (N_HEADS_B * CHUNK, 128))
    kcol, vcol = WIDTH_B // KV_WIDTH_B, WIDTH_B // KV_WIDTH_B + 1
    if prev_is_seq:
        per = qb // WINDOW
        kp_spec = pl.BlockSpec((1, WINDOW, KV_WIDTH_B), lambda b, i: (b, jnp.maximum(i * per - 1, 0), kcol))
        vp_spec = pl.BlockSpec((1, WINDOW, KV_WIDTH_B), lambda b, i: (b, jnp.maximum(i * per - 1, 0), vcol))
    else:
        kp_spec = vp_spec = pl.BlockSpec((1, WINDOW, KV_WIDTH_B), lambda b, i: (b, 0, 0))
    return pl.pallas_call(
        functools.partial(_swa_kernel, mask_start=prev_is_seq),
        out_shape=jax.ShapeDtypeStruct((B, T, WIDTH_B), jnp.float32),
        grid=(B, T // qb),
        in_specs=[pl.BlockSpec((1, qb, WIDTH_B), lambda b, i: (b, i, 0)),
                  pl.BlockSpec((1, qb, KV_WIDTH_B), lambda b, i: (b, i, kcol)),
                  pl.BlockSpec((1, qb, KV_WIDTH_B), lambda b, i: (b, i, vcol)),
                  kp_spec, vp_spec,
                  pl.BlockSpec((N_KV_B, GROUP_B * CHUNK, SWA_BAND), lambda b, i: (0, 0, 0)),
                  pl.BlockSpec((N_HEADS_B * CHUNK, 128), lambda b, i: (0, 0))],
        out_specs=pl.BlockSpec((1, qb, WIDTH_B), lambda b, i: (b, i, 0)),
        compiler_params=pltpu.CompilerParams(
            dimension_semantics=("arbitrary", "arbitrary"), vmem_limit_bytes=VMEM_LIMIT),
        name="swa_attention",
    )(seg_b, seg_b, seg_b, prev_k, prev_v, bias, sink_tab)


def _layer_norm_rows(h, g, b, eps=1e-5):
    mu = jnp.mean(h, axis=-1, keepdims=True)
    d = h - mu
    var = jnp.mean(d * d, axis=-1, keepdims=True)
    return d * lax.rsqrt(var + eps) * g + b


def _const_spec(shape):
    return pl.BlockSpec(shape, lambda *_: (0,) * len(shape), pipeline_mode=pl.Buffered(1))


def _merge_kernel(x_ref, y_ref, bonus_ref, g_ref, ob_ref, gates_ref, lnx_ref, ln1_ref, ones_ref,
                  pa_ref, pb_ref, wout_ref, o_ref):
    ones2 = ones_ref[...]
    y = y_ref[...]
    mean = _head_sums(y, ones2) * (1.0 / HEAD_A)
    d = y - mean
    var = _head_sums(d * d, ones2) * (1.0 / HEAD_A)
    yn = d * lax.rsqrt(var + GN_EPS) * lnx_ref[0:1, :] + lnx_ref[1:2, :]
    o_a = ((yn + bonus_ref[...]) * g_ref[...]).astype(jnp.bfloat16)
    br_a = jnp.dot(o_a, pa_ref[...], preferred_element_type=jnp.float32)
    br_b = jnp.dot(ob_ref[...].astype(jnp.bfloat16), pb_ref[...], preferred_element_type=jnp.float32)
    gates = jax.nn.sigmoid(gates_ref[...])
    merged = gates[:, :D_MODEL] * br_a + gates[:, D_MODEL:] * br_b
    h = ALPHA * x_ref[...] + jnp.dot(merged.astype(jnp.bfloat16), wout_ref[...], preferred_element_type=jnp.float32)
    o_ref[...] = _layer_norm_rows(h, ln1_ref[0:1, :], ln1_ref[1:2, :])


def branch_merge(x, y, bonus, g, o_b, gates, p, tm=256):
    n = x.shape[0]
    tm = _pick(n, (tm, 128, 64))
    rows = lambda w: pl.BlockSpec((tm, w), lambda i: (i, 0))
    return pl.pallas_call(
        _merge_kernel,
        out_shape=jax.ShapeDtypeStruct((n, D_MODEL), jnp.float32),
        grid=(n // tm,),
        in_specs=[rows(D_MODEL), rows(WIDTH_A), rows(WIDTH_A), rows(WIDTH_A), rows(WIDTH_B), rows(C_GATE),
                  _const_spec((2, WIDTH_A)), _const_spec((2, D_MODEL)), _const_spec((2 * PAIR_W, PAIR_W)),
                  _const_spec((WIDTH_A, D_MODEL)), _const_spec((WIDTH_B, D_MODEL)), _const_spec((D_MODEL, D_MODEL))],
        out_specs=rows(D_MODEL),
        compiler_params=pltpu.CompilerParams(dimension_semantics=("arbitrary",), vmem_limit_bytes=VMEM_LIMIT),
        name="branch_merge",
    )(x, y, bonus, g, o_b, gates, jnp.stack([p['lnx_g'], p['lnx_b']]), jnp.stack([p['ln1_g'], p['ln1_b']]),
      _block_ones2(), p['w_branch_a16'], p['w_branch_b16'], p['w_out16'])


def _mem_kernel(x_ref, mk_ref, mv_ref, ln2_ref, wq_ref, wo_ref, pwq_ref, o_ref, q_ref):
    x = x_ref[0]
    qm = jnp.dot(x.astype(jnp.bfloat16), wq_ref[...], preferred_element_type=jnp.float32).astype(jnp.bfloat16)
    mk = mk_ref[0].astype(jnp.bfloat16)
    mv = mv_ref[0].astype(jnp.bfloat16)
    outs = []
    for h in range(N_HEADS_M):
        cols = slice(h * HEAD_M, (h + 1) * HEAD_M)
        s = lax.dot_general(qm[:, cols], mk[:, cols], (((1,), (1,)), ((), ())),
                            preferred_element_type=jnp.float32) * (HEAD_M ** -0.5)
        e = jnp.exp(s - jnp.max(s, axis=-1, keepdims=True))
        pr = e / jnp.sum(e, axis=-1, keepdims=True)
        outs.append(jnp.dot(pr.astype(jnp.bfloat16), mv[:, cols], preferred_element_type=jnp.float32))
    o = jnp.concatenate(outs, axis=1).astype(jnp.bfloat16)
    h2 = ALPHA * x + jnp.dot(o, wo_ref[...], preferred_element_type=jnp.float32)
    x2 = _layer_norm_rows(h2, ln2_ref[0:1, :], ln2_ref[1:2, :])
    o_ref[0] = x2
    q_ref[0] = jnp.dot(x2.astype(jnp.bfloat16), pwq_ref[...], preferred_element_type=jnp.float32)


def mem_block(x, mk, mv, p, tm=256):
    B, T, _ = x.shape
    tm = _pick(T, (tm, 128, 64))
    wm = N_HEADS_M * HEAD_M
    rows = lambda w: pl.BlockSpec((1, tm, w), lambda b, i: (b, i, 0))
    mem = pl.BlockSpec((1, N_MEM, wm), lambda b, i: (b, 0, 0))
    out = jax.ShapeDtypeStruct((B, T, D_MODEL), jnp.float32)
    return pl.pallas_call(
        _mem_kernel,
        out_shape=(out, jax.ShapeDtypeStruct((B, T, PEER_HEADS * D_KEY), jnp.float32)),
        grid=(B, T // tm),
        in_specs=[rows(D_MODEL), mem, mem, _const_spec((2, D_MODEL)), _const_spec((D_MODEL, wm)),
                  _const_spec((wm, D_MODEL)), _const_spec((D_MODEL, PEER_HEADS * D_KEY))],
        out_specs=(rows(D_MODEL), rows(PEER_HEADS * D_KEY)),
        compiler_params=pltpu.CompilerParams(
            dimension_semantics=("arbitrary", "arbitrary"), vmem_limit_bytes=VMEM_LIMIT),
        name="mem_block",
    )(x, mk, mv, jnp.stack([p['ln2_g'], p['ln2_b']]), p['wq_mem16'], p['wo_mem16'], p['peer_wq16'])


ROUTE_TQ = 256


def _top_values(s, k, exact):
    n_rows = s.shape[0]
    iota = lax.broadcasted_iota(jnp.int32, s.shape, 0).astype(jnp.float32)
    rank = jnp.full(s.shape, float(k), jnp.float32)
    masked0 = jnp.sum((s == -jnp.inf).astype(jnp.float32), axis=0, keepdims=True)
    out = []
    for step in range(k):
        m = jnp.max(s, axis=0, keepdims=True)
        if exact:
            first = jnp.min(jnp.where(s == m, iota, float(n_rows)), axis=0, keepdims=True)
            taken = iota == first
        else:
            taken = s == m
        s = jnp.where(taken, -jnp.inf, s)
        rank = jnp.where(taken, float(step), rank)
        out.append(m)
    surplus = jnp.sum((s == -jnp.inf).astype(jnp.float32), axis=0, keepdims=True) - masked0 - float(k)
    return out, rank, surplus


def _peer_route_kernel(q_ref, keys_ref, n1_ref, c1_ref, rank2_ref, e2_ref):
    refs = (q_ref, keys_ref, n1_ref, c1_ref, rank2_ref, e2_ref)
    for h in range(PEER_HEADS):
        surplus = _peer_route_pass(*refs, exact=False, heads=(h,))

        @pl.when(jnp.max(surplus) > 0.0)
        def _(h=h):
            _peer_route_pass(*refs, exact=True, heads=(h,))


def _peer_route_pass(q_ref, keys_ref, n1_ref, c1_ref, rank2_ref, e2_ref, *, exact, heads):
    half = D_KEY // 2
    surplus = None
    for h in heads:
        tops, scores = [], []
        for p in range(2):
            c0 = (2 * h + p) * half
            qs = q_ref[:, c0:c0 + half].astype(jnp.bfloat16)
            s = lax.dot_general(keys_ref[2 * h + p], qs, (((1,), (1,)), ((), ())),
                                preferred_element_type=jnp.float32)
            scores.append(s)
            tops.append(_top_values(s, TOPK, exact))
        (t1, _, sur1), (t2, rank2, sur2) = tops
        t2all = jnp.concatenate(t2, axis=0)
        t1all = jnp.concatenate(t1, axis=0)
        rank8 = lax.broadcasted_iota(jnp.int32, (8, t1all.shape[1]), 0)
        cand_rows = [t1all + t2[0], t1all[:8] + t2[1]]
        for b in range(2, 8):
            cand_rows.append(jnp.where(rank8 < TOPK // (b + 1), t1all[:8] + t2[b], -jnp.inf))
        cand_rows.append(t1[0] + t2all[8:])
        sc, _, sur3 = _top_values(jnp.concatenate(cand_rows, axis=0), TOPK, exact)
        sur = sur1 + sur2 + sur3
        surplus = sur if surplus is None else surplus + sur
        z = jnp.zeros_like(sc[0])
        for kq in range(TOPK):
            z = z + jnp.exp(sc[kq] - sc[0])
        theta = sc[TOPK - 1]
        n1 = jnp.zeros_like(scores[0])
        for a in range(TOPK):
            n_a = jnp.sum((t1[a] + t2all >= theta).astype(jnp.float32), axis=0, keepdims=True)
            n1 = jnp.where(scores[0] == t1[a], n_a, n1)
        n1_ref[h] = n1
        rank2_ref[h] = rank2.astype(jnp.bfloat16)
        c1_ref[h] = jnp.exp(scores[0] - t1[0]) / z
        e2_ref[h] = jnp.exp(scores[1] - t2[0]).astype(jnp.bfloat16)
    return surplus


def peer_route(q, keys16):
    n = q.shape[0]
    tq = _pick(n, (ROUTE_TQ, 128))
    big = jax.ShapeDtypeStruct((PEER_HEADS, N_KEYS, n), jnp.float32)
    big16 = jax.ShapeDtypeStruct((PEER_HEADS, N_KEYS, n), jnp.bfloat16)
    bspec = pl.BlockSpec((PEER_HEADS, N_KEYS, tq), lambda i: (0, 0, i))
    return pl.pallas_call(
        _peer_route_kernel,
        out_shape=(big, big, big16, big16),
        grid=(n // tq,),
        in_specs=[pl.BlockSpec((tq, PEER_HEADS * D_KEY), lambda i: (i, 0)),
                  pl.BlockSpec((2 * PEER_HEADS, N_KEYS, D_KEY // 2), lambda i: (0, 0, 0))],
        out_specs=(bspec, bspec, bspec, bspec),
        compiler_params=pltpu.CompilerParams(
            dimension_semantics=("arbitrary",), vmem_limit_bytes=VMEM_LIMIT),
        name="peer_route",
    )(q, keys16)


PEER_TM = 512
PEER_ROWS = 8
PEER_TE = PEER_ROWS * N_KEYS


def _gelu(x):
    return 0.5 * x * (1.0 + lax.erf(x * (2.0 ** -0.5)))


def _peer_mix_kernel(x_ref, u_ref, vt_ref, n1_ref, c1_ref, rank2_ref, e2_ref, ln3_ref, o_ref,
                     x16_ref, h_ref, acc_ref):
    j = pl.program_id(1)

    @pl.when(j == 0)
    def _():
        x16_ref[...] = x_ref[...].T.astype(jnp.bfloat16)
        acc_ref[...] = jnp.zeros_like(acc_ref)

    half_te = PEER_TE // 2
    a_halves = [jnp.dot(u_ref[k * half_te:(k + 1) * half_te, :], x16_ref[...],
                        preferred_element_type=jnp.float32) for k in range(2)]

    def sublane_bcast16(row):
        return jnp.broadcast_to(row, (N_KEYS, row.shape[1])).astype(jnp.bfloat16)

    for r in range(PEER_ROWS):
        rows = slice(r * N_KEYS, (r + 1) * N_KEYS)
        for c0 in range(0, x_ref.shape[0], 128):
            cols = slice(c0, c0 + 128)
            gate = None
            for h in range(PEER_HEADS):
                keep = rank2_ref[h, :, cols] < sublane_bcast16(n1_ref[h, r:r + 1, cols])
                w = jnp.where(keep, e2_ref[h, :, cols], jnp.zeros((), jnp.bfloat16))
                w = w * sublane_bcast16(c1_ref[h, r:r + 1, cols])
                gate = w if gate is None else gate + w
            a_rows = a_halves[r // (PEER_ROWS // 2)][(r % (PEER_ROWS // 2)) * N_KEYS:(r % (PEER_ROWS // 2) + 1) * N_KEYS]
            h_ref[rows, cols] = gate * _gelu(a_rows[:, cols]).astype(jnp.bfloat16)
    acc_ref[...] += jnp.dot(vt_ref[0], h_ref[...], preferred_element_type=jnp.float32)

    @pl.when(j == pl.num_programs(1) - 1)
    def _():
        o_ref[...] = _layer_norm_rows(ALPHA * x_ref[...] + acc_ref[...].T, ln3_ref[0:1, :], ln3_ref[1:2, :])


def peer_mix(x, u16, vt16, n1, c1, rank2, e2, ln3):
    n, d = x.shape
    tm = _pick(n, (PEER_TM, 256, 128))
    n_exp = u16.shape[0]
    row_spec = pl.BlockSpec((PEER_HEADS, PEER_ROWS, tm), lambda i, j: (0, j, i))
    all_spec = pl.BlockSpec((PEER_HEADS, N_KEYS, tm), lambda i, j: (0, 0, i), pipeline_mode=pl.Buffered(1))
    return pl.pallas_call(
        _peer_mix_kernel,
        out_shape=jax.ShapeDtypeStruct((n, d), jnp.float32),
        grid=(n // tm, n_exp // PEER_TE),
        in_specs=[pl.BlockSpec((tm, d), lambda i, j: (i, 0), pipeline_mode=pl.Buffered(1)),
                  pl.BlockSpec((PEER_TE, d), lambda i, j: (j, 0)),
                  pl.BlockSpec((1, d, PEER_TE), lambda i, j: (j, 0, 0)),
                  row_spec, row_spec, all_spec, all_spec, _const_spec((2, d))],
        out_specs=pl.BlockSpec((tm, d), lambda i, j: (i, 0)),
        scratch_shapes=[pltpu.VMEM((d, tm), jnp.bfloat16), pltpu.VMEM((PEER_TE, tm), jnp.bfloat16),
                        pltpu.VMEM((d, tm), jnp.float32)],
        compiler_params=pltpu.CompilerParams(
            dimension_semantics=("arbitrary", "arbitrary"), vmem_limit_bytes=VMEM_LIMIT),
        name="peer_mix",
    )(x, u16, vt16, n1, c1, rank2, e2, ln3)


def peer_block(x, q, p):
    stats = peer_route(q, p['peer_keys16'])
    return peer_mix(x, p['peer_u16'], p['peer_vt16'], *stats, jnp.stack([p['ln3_g'], p['ln3_b']]))


def trunk_layer(x, p, mem_k, mem_v, rwkv_state, shift_row, swa_k_cache, swa_v_cache):
    B, T, _ = x.shape
    n = B * T
    x2d = x.reshape(n, D_MODEL)
    seg_a = matmul(x2d, p['w_in_a16']).reshape(B, T, C_RWKV)
    seg_b = matmul(x2d, p['w_in_b16']).reshape(B, T, C_SWA)
    gates = matmul(x2d, p['w_in_g16'])
    r, w, k, v, kk, a, g, bonus = rwkv7_pre(seg_a, shift_row.astype(jnp.float32), p)
    y, rwkv_new = rwkv7_chunked(r, w, k, v, kk, a, rwkv_state)
    shift_new = seg_a[:, -1:]
    k_new = seg_b[:, :, WIDTH_B:WIDTH_B + KV_WIDTH_B].reshape(B, T, N_KV_B, HEAD_B)
    v_new = seg_b[:, :, WIDTH_B + KV_WIDTH_B:].reshape(B, T, N_KV_B, HEAD_B)
    if swa_k_cache is None:
        o_b = swa_attention(seg_b, seg_b, seg_b, p['attn_sinks'], prev_is_seq=True)
        swa_k_new, swa_v_new = k_new[:, -WINDOW:], v_new[:, -WINDOW:]
    else:
        o_b = swa_attention(seg_b, swa_k_cache.reshape(B, WINDOW, KV_WIDTH_B),
                            swa_v_cache.reshape(B, WINDOW, KV_WIDTH_B), p['attn_sinks'], prev_is_seq=False)
        swa_k_new = jnp.concatenate([swa_k_cache, k_new], axis=1)[:, -WINDOW:]
        swa_v_new = jnp.concatenate([swa_v_cache, v_new], axis=1)[:, -WINDOW:]
    flat = lambda t: t.reshape(n, t.shape[-1])
    x1 = branch_merge(x2d, flat(y), flat(bonus), flat(g), flat(o_b), gates, p)
    wm = N_HEADS_M * HEAD_M
    x2, q = mem_block(x1.reshape(B, T, D_MODEL), mem_k.reshape(-1, N_MEM, wm), mem_v.reshape(-1, N_MEM, wm), p)
    x3 = peer_block(flat(x2), flat(q), p)
    return x3.reshape(B, T, D_MODEL), rwkv_new, shift_new, swa_k_new, swa_v_new


_MM_WEIGHTS = ('w_branch_a', 'w_branch_b', 'w_out', 'wq_mem', 'wk_mem', 'wv_mem', 'wo_mem', 'peer_wq')


def kernel(x_prompt, x_sample, state_rwkv, state_shift, cache_swa_k, cache_swa_v, cache_mem_k, cache_mem_v, mem_prompt, w_in, shift_mu, w0, w_lora_up, a0, a_lora_up, g_lora_up, k_k, k_a, r_k, lnx_g, lnx_b, attn_sinks, w_branch_a, w_branch_b, w_out, ln1_g, ln1_b, wq_mem, wk_mem, wv_mem, wo_mem, ln2_g, ln2_b, peer_wq, peer_sub_keys, peer_u, peer_v, ln3_g, ln3_b):
    params = {
        'w_in': w_in, 'shift_mu': shift_mu, 'w0': w0, 'w_lora_up': w_lora_up, 'a0': a0,
        'a_lora_up': a_lora_up, 'g_lora_up': g_lora_up, 'k_k': k_k, 'k_a': k_a, 'r_k': r_k,
        'lnx_g': lnx_g, 'lnx_b': lnx_b, 'attn_sinks': attn_sinks, 'w_branch_a': w_branch_a,
        'w_branch_b': w_branch_b, 'w_out': w_out, 'ln1_g': ln1_g, 'ln1_b': ln1_b, 'wq_mem': wq_mem,
        'wk_mem': wk_mem, 'wv_mem': wv_mem, 'wo_mem': wo_mem, 'ln2_g': ln2_g, 'ln2_b': ln2_b,
        'peer_wq': peer_wq, 'peer_sub_keys': peer_sub_keys, 'peer_u': peer_u, 'peer_v': peer_v,
        'ln3_g': ln3_g, 'ln3_b': ln3_b,
    }
    B = x_prompt.shape[0]
    rwkv0 = jnp.zeros((B, N_HEADS_A, HEAD_A, HEAD_A), jnp.float32)
    shift0 = jnp.zeros((B, 1, C_RWKV), x_prompt.dtype)
    xp, xs = x_prompt, x_sample
    p_rw, p_sh, p_k, p_v, p_mk, p_mv = [], [], [], [], [], []
    s_rw, s_sh, s_k, s_v = [], [], [], []
    for l in range(DEPTH):
        p = {name: arr[l] for name, arr in params.items()}
        for name in _MM_WEIGHTS:
            p[name + '16'] = p[name].astype(jnp.bfloat16)
        w_in16 = p['w_in'].astype(jnp.bfloat16)
        p['w_in_a16'] = w_in16[:, :C_RWKV]
        p['w_in_b16'] = w_in16[:, C_RWKV:C_RWKV + C_SWA]
        p['w_in_g16'] = w_in16[:, C_RWKV + C_SWA:]
        p['peer_keys16'] = p['peer_sub_keys'].reshape(2 * PEER_HEADS, N_KEYS, D_KEY // 2).astype(jnp.bfloat16)
        p['peer_u16'] = p['peer_u'].astype(jnp.bfloat16)
        p['peer_vt16'] = (p['peer_v'].astype(jnp.bfloat16)
                          .reshape(p['peer_v'].shape[0] // PEER_TE, PEER_TE, D_MODEL).transpose(0, 2, 1))
        mk = mm(mem_prompt, p['wk_mem16']).reshape(B, N_MEM, N_HEADS_M, HEAD_M)
        mv = mm(mem_prompt, p['wv_mem16']).reshape(B, N_MEM, N_HEADS_M, HEAD_M)
        xp, rw, sh, kn, vn = trunk_layer(xp, p, mk, mv, rwkv0, shift0, None, None)
        p_rw.append(rw); p_sh.append(sh); p_k.append(kn); p_v.append(vn); p_mk.append(mk); p_mv.append(mv)
        xs, rw, sh, kn, vn = trunk_layer(xs, p, cache_mem_k[l], cache_mem_v[l], state_rwkv[l], state_shift[l],
                                         cache_swa_k[l], cache_swa_v[l])
        s_rw.append(rw); s_sh.append(sh); s_k.append(kn); s_v.append(vn)
    return (xp, xs,
            jnp.stack(p_rw), jnp.stack(p_sh), jnp.stack(p_k), jnp.stack(p_v), jnp.stack(p_mk), jnp.stack(p_mv),
            jnp.stack(s_rw), jnp.stack(s_sh), jnp.stack(s_k), jnp.stack(s_v))
```
